```python
import jax
import jax.numpy as jnp
from jax import lax
import numpy as np

D_MODEL = 1024
BATCH = 16
SEQ = 256
DEPTH = 2
DEC_BATCH = 2
DEC_SEQ = 4096
PAST_LEN = 512

GRID_W = 64
HEAD_DIM = 64
N_RET_HEADS = 8
N_NA_HEADS = 8
RET_WIDTH = N_RET_HEADS * HEAD_DIM
NA_WIDTH = N_NA_HEADS * HEAD_DIM
MIX_WIDTH = RET_WIDTH + NA_WIDTH
IN_WIDTH = 4 * RET_WIDTH + 3 * NA_WIDTH
D_FF = 2816
CONV_W = 3
CHUNK = 128
NA_KH_MAX = 8
NA_KW = 16
NA_QB = 16
NA_KB = 2 * NA_KW
ATTN_QBLOCK = 128
ROPE_BASE = 10000.0
EPS = 1e-6
NEG_INF = -1e9

kernel_name = "hymba_retention_natten_prefix_dit"


def _rmsnorm(x, g):
    xf = x.astype(jnp.float32)
    y = xf * lax.rsqrt(jnp.mean(xf * xf, axis=-1, keepdims=True) + EPS)
    return (y * g.astype(jnp.float32)).astype(x.dtype)


def _modulation(cond, ada_w, ada_b):
    m = jax.nn.silu(cond) @ ada_w + ada_b
    return jnp.split(m[:, None, :], 6, axis=-1)


def _split_proj(h, w_in):
    B, T, _ = h.shape
    p = h @ w_in
    cuts = [RET_WIDTH, 2 * RET_WIDTH, 3 * RET_WIDTH, 4 * RET_WIDTH,
            4 * RET_WIDTH + NA_WIDTH, 4 * RET_WIDTH + 2 * NA_WIDTH]
    rq, rk, rv, rg, nq, nk, nv = jnp.split(p, cuts, axis=-1)
    hr = lambda a: a.reshape(B, T, N_RET_HEADS, HEAD_DIM)
    hn = lambda a: a.reshape(B, T, N_NA_HEADS, HEAD_DIM)
    return hr(rq), hr(rk), hr(rv), rg, hn(nq), hn(nk), hn(nv)


def _rotate(x, pos):
    d = x.shape[-1]
    inv = jnp.power(ROPE_BASE, -jnp.arange(0, d, 2, dtype=jnp.float32) / d)
    ang = pos[:, None] * inv[None, :]
    cos = jnp.cos(ang)[None, :, None, :]
    sin = jnp.sin(ang)[None, :, None, :]
    xf = x.astype(jnp.float32)
    x1, x2 = xf[..., : d // 2], xf[..., d // 2:]
    return jnp.concatenate([x1 * cos - x2 * sin, x1 * sin + x2 * cos], axis=-1).astype(x.dtype)


def _rope_2d(x):
    T = x.shape[1]
    pos = jnp.arange(T)
    rows = (pos // GRID_W).astype(jnp.float32)
    cols = (pos % GRID_W).astype(jnp.float32)
    half = x.shape[-1] // 2
    return jnp.concatenate([_rotate(x[..., :half], rows), _rotate(x[..., half:], cols)], axis=-1)


def _retention_scan(q, k, v, log_gamma, s0):
    B, T, H, Dh = q.shape
    nc = T // CHUNK
    lg = log_gamma.astype(jnp.float32)
    pos = jnp.arange(CHUNK, dtype=jnp.float32)
    diff = pos[:, None] - pos[None, :]
    decay_in = jnp.where(diff >= 0, jnp.exp(lg[:, None, None] * jnp.maximum(diff, 0.0)), 0.0)
    q_dec = jnp.exp(lg[None, :] * (pos[:, None] + 1.0))
    k_dec = jnp.exp(lg[None, :] * (CHUNK - 1.0 - pos[:, None]))
    c_dec = jnp.exp(lg * CHUNK)

    def to_chunks(a):
        return a.astype(jnp.float32).reshape(B, nc, CHUNK, H, Dh).transpose(1, 0, 2, 3, 4)

    def step(s, qkv):
        qc, kc, vc = qkv
        scores = jnp.einsum('bihd,bjhd->bhij', qc, kc) * decay_in[None]
        inner = jnp.einsum('bhij,bjhe->bihe', scores, vc)
        cross = jnp.einsum('bihd,bhde->bihe', qc, s) * q_dec[None, :, :, None]
        s_new = s * c_dec[None, :, None, None] + jnp.einsum(
            'bjhd,bjhe->bhde', kc * k_dec[None, :, :, None], vc)
        return s_new, inner + cross

    s_fin, out = lax.scan(step, s0.astype(jnp.float32), (to_chunks(q), to_chunks(k), to_chunks(v)))
    out = out.transpose(1, 0, 2, 3, 4).reshape(B, T, H, Dh)
    return out, s_fin


def _bi_retention(q, k, v, g, lg_f, lg_b, s_f0, s_b0):
    B, T = q.shape[0], q.shape[1]
    q = q * (HEAD_DIM ** -0.5)
    o_f, s_f = _retention_scan(q, k, v, lg_f, s_f0)
    o_b, s_b = _retention_scan(jnp.flip(q, 1), jnp.flip(k, 1), jnp.flip(v, 1), lg_b, s_b0)
    o = o_f + jnp.flip(o_b, 1)
    mu = jnp.mean(o, axis=-1, keepdims=True)
    var = jnp.mean(jnp.square(o - mu), axis=-1, keepdims=True)
    o = ((o - mu) * lax.rsqrt(var + EPS)).reshape(B, T, RET_WIDTH).astype(g.dtype)
    return o * jax.nn.silu(g), s_f, s_b


def _context_attention(q, k, v):
    B, N, H, Dh = q.shape
    nb = N // ATTN_QBLOCK
    qb = q.reshape(B, nb, ATTN_QBLOCK, H, Dh).transpose(1, 0, 2, 3, 4)
    scale = HEAD_DIM ** -0.5

    def blk(qi):
        s = jnp.einsum('bqhd,bkhd->bhqk', qi, k).astype(jnp.float32) * scale
        p = jax.nn.softmax(s, axis=-1)
        return jnp.einsum('bhqk,bkhd->bqhd', p.astype(v.dtype), v)

    out = lax.map(blk, qb)
    return out.transpose(1, 0, 2, 3, 4).reshape(B, N, H * Dh)


def _neighbourhood_attention(q, k, v, ctx_k, ctx_v, rpb):
    B, T, H, Dh = q.shape
    rows = T // GRID_W
    kh = min(NA_KH_MAX, rows)
    ncb = GRID_W // NA_QB
    r = np.arange(rows)
    rs = np.clip(r - kh // 2, 0, rows - kh)
    row_idx = rs[:, None] + np.arange(kh)[None, :]
    dr_idx = row_idx - r[:, None] + NA_KH_MAX - 1
    jb = np.arange(ncb)
    kb = np.clip(jb * NA_QB - NA_KW // 2, 0, GRID_W - NA_KB)
    col_idx = kb[:, None] + np.arange(NA_KB)[None, :]
    qcol = jb[:, None] * NA_QB + np.arange(NA_QB)[None, :]
    cs = np.clip(qcol - NA_KW // 2, 0, GRID_W - NA_KW)
    col_ok = (col_idx[:, None, :] >= cs[:, :, None]) & (col_idx[:, None, :] < cs[:, :, None] + NA_KW)
    dc_idx = np.clip(col_idx[:, None, :] - qcol[:, :, None] + NA_KW - 1, 0, 2 * NA_KW - 2)
    L = kh * NA_KB
    scale = HEAD_DIM ** -0.5

    k_grid = k.reshape(B, rows, GRID_W, H, Dh)
    v_grid = v.reshape(B, rows, GRID_W, H, Dh)
    q_rows = q.reshape(B, rows, ncb, NA_QB, H, Dh).transpose(1, 0, 2, 3, 4, 5)
    rpb_f = rpb.astype(jnp.float32)

    def gather_rows(grid, ridx):
        g = grid[:, ridx][:, :, col_idx]
        return g.transpose(0, 2, 1, 3, 4, 5).reshape(B, ncb, L, H, Dh)

    def row_block(args):
        qr, ridx, dri = args
        kr = gather_rows(k_grid, ridx)
        vr = gather_rows(v_grid, ridx)
        bias = rpb_f[:, dri[None, None, :, None], dc_idx[:, :, None, :]]
        bias = jnp.where(col_ok[None, :, :, None, :], bias, NEG_INF).reshape(H, ncb, NA_QB, L)
        s_loc = jnp.einsum('bjqhd,bjlhd->bhjql', qr, kr).astype(jnp.float32) * scale + bias[None]
        s_ctx = jnp.einsum('bjqhd,bnhd->bhjqn', qr, ctx_k).astype(jnp.float32) * scale
        p = jax.nn.softmax(jnp.concatenate([s_loc, s_ctx], axis=-1), axis=-1)
        o = jnp.einsum('bhjql,bjlhd->bjqhd', p[..., :L].astype(vr.dtype), vr)
        o = o + jnp.einsum('bhjqn,bnhd->bjqhd', p[..., L:].astype(ctx_v.dtype), ctx_v)
        return o

    out = lax.map(row_block, (q_rows, jnp.asarray(row_idx, jnp.int32), jnp.asarray(dr_idx, jnp.int32)))
    return out.transpose(1, 0, 2, 3, 4, 5).reshape(B, T, H * Dh)


def _conv_ffn(h, w_up, conv_w, conv_b, w_down):
    T = h.shape[1]
    u = h @ w_up
    pad = CONV_W // 2
    up = jnp.pad(u, ((0, 0), (pad, pad), (0, 0)))
    u = sum(up[:, i:i + T] * conv_w[i] for i in range(CONV_W)) + conv_b
    a, gate = jnp.split(u, 2, axis=-1)
    return (jax.nn.silu(gate) * a) @ w_down


def _context_mixer(parts, lg_f, lg_b):
    rq, rk, rv, rg, nq, nk, nv = parts
    B = rq.shape[0]
    z = jnp.zeros((B, N_RET_HEADS, HEAD_DIM, HEAD_DIM), jnp.float32)
    ret, s_f, s_b = _bi_retention(rq, rk, rv, rg, lg_f, lg_b, z, z)
    na = _context_attention(nq, nk, nv)
    return jnp.concatenate([ret, na], axis=-1), (nk, nv, s_f, s_b)


def _latent_mixer(parts, lg_f, lg_b, rpb, ctx_k, ctx_v, s_f0, s_b0):
    rq, rk, rv, rg, nq, nk, nv = parts
    rq = _rope_2d(rq)
    rk = _rope_2d(rk)
    ret, _, _ = _bi_retention(rq, rk, rv, rg, lg_f, lg_b, s_f0, s_b0)
    na = _neighbourhood_attention(nq, nk, nv, ctx_k, ctx_v, rpb)
    return jnp.concatenate([ret, na], axis=-1), None


def _layer(x, cond, mixer, ada_w, ada_b, g_pre_mix, g_post_mix, g_pre_ffn, g_post_ffn,
           w_in, w_out, w_up, conv_w, conv_b, w_down):
    sh1, sc1, ga1, sh2, sc2, ga2 = _modulation(cond, ada_w, ada_b)
    h = _rmsnorm(x, g_pre_mix) * (1.0 + sc1) + sh1
    mixed, side = mixer(_split_proj(h, w_in))
    x = x + ga1 * _rmsnorm(mixed @ w_out, g_post_mix)
    h = _rmsnorm(x, g_pre_ffn) * (1.0 + sc2) + sh2
    x = x + ga2 * _rmsnorm(_conv_ffn(h, w_up, conv_w, conv_b, w_down), g_post_ffn)
    return x, side


def setup_inputs(seed: int = 0) -> dict:
    key = jax.random.key(seed)
    ks = jax.random.split(key, 24)
    f32 = jnp.float32

    def nrm(k, shape, scale):
        return scale * jax.random.normal(k, shape, f32)

    decay_base = jnp.log(jnp.power(2.0, 5.0 + jnp.arange(N_RET_HEADS, dtype=f32)) - 1.0)
    return {
        'x_prompt': nrm(ks[0], (BATCH, SEQ, D_MODEL), 1.0),
        'x_sample': nrm(ks[1], (DEC_BATCH, DEC_SEQ, D_MODEL), 1.0),
        'c': nrm(ks[2], (DEC_BATCH, D_MODEL), 1.0),
        'cache_na_k': nrm(ks[3], (DEC_BATCH, DEPTH, PAST_LEN, N_NA_HEADS, HEAD_DIM), 1.0),
        'cache_na_v': nrm(ks[4], (DEC_BATCH, DEPTH, PAST_LEN, N_NA_HEADS, HEAD_DIM), 1.0),
        'state_ret_fwd': nrm(ks[5], (DEC_BATCH, DEPTH, N_RET_HEADS, HEAD_DIM, HEAD_DIM), 0.5),
        'state_ret_bwd': nrm(ks[6], (DEC_BATCH, DEPTH, N_RET_HEADS, HEAD_DIM, HEAD_DIM), 0.5),
        'c_ctx': nrm(ks[7], (D_MODEL,), 1.0),
        'ada_w': nrm(ks[8], (DEPTH, D_MODEL, 6 * D_MODEL), 0.5 * D_MODEL ** -0.5),
        'ada_b': nrm(ks[9], (DEPTH, 6 * D_MODEL), 0.02),
        'g_pre_mix': 1.0 + nrm(ks[10], (DEPTH, D_MODEL), 0.05),
        'g_post_mix': 1.0 + nrm(ks[11], (DEPTH, D_MODEL), 0.05),
        'g_pre_ffn': 1.0 + nrm(ks[12], (DEPTH, D_MODEL), 0.05),
        'g_post_ffn': 1.0 + nrm(ks[13], (DEPTH, D_MODEL), 0.05),
        'w_in': nrm(ks[14], (DEPTH, D_MODEL, IN_WIDTH), D_MODEL ** -0.5),
        'ret_decay_fwd': decay_base + nrm(ks[15], (DEPTH, N_RET_HEADS), 0.1),
        'ret_decay_bwd': decay_base + nrm(ks[16], (DEPTH, N_RET_HEADS), 0.1),
        'na_rpb': nrm(ks[17], (DEPTH, N_NA_HEADS, 2 * NA_KH_MAX - 1, 2 * NA_KW - 1), 0.5),
        'w_out': nrm(ks[18], (DEPTH, MIX_WIDTH, D_MODEL), MIX_WIDTH ** -0.5),
        'w_up': nrm(ks[19], (DEPTH, D_MODEL, 2 * D_FF), D_MODEL ** -0.5),
        'conv_w': nrm(ks[20], (DEPTH, CONV_W, 2 * D_FF), CONV_W ** -0.5),
        'conv_b': nrm(ks[21], (DEPTH, 2 * D_FF), 0.02),
        'w_down': nrm(ks[22], (DEPTH, D_FF, D_MODEL), D_FF ** -0.5),
    }


def reference(x_prompt, x_sample, c, cache_na_k, cache_na_v, state_ret_fwd, state_ret_bwd, c_ctx,
              ada_w, ada_b, g_pre_mix, g_post_mix, g_pre_ffn, g_post_ffn, w_in,
              ret_decay_fwd, ret_decay_bwd, na_rpb, w_out, w_up, conv_w, conv_b, w_down):
    y_p = x_prompt
    ks_, vs_, sfs_, sbs_ = [], [], [], []
    for l in range(DEPTH):
        lg_f = jax.nn.log_sigmoid(ret_decay_fwd[l].astype(jnp.float32))
        lg_b = jax.nn.log_sigmoid(ret_decay_bwd[l].astype(jnp.float32))
        mixer = lambda parts, lg_f=lg_f, lg_b=lg_b: _context_mixer(parts, lg_f, lg_b)
        y_p, (k_l, v_l, sf_l, sb_l) = _layer(
            y_p, c_ctx[None, :], mixer, ada_w[l], ada_b[l], g_pre_mix[l], g_post_mix[l],
            g_pre_ffn[l], g_post_ffn[l], w_in[l], w_out[l], w_up[l], conv_w[l], conv_b[l], w_down[l])
        ks_.append(k_l)
        vs_.append(v_l)
        sfs_.append(sf_l.astype(x_prompt.dtype))
        sbs_.append(sb_l.astype(x_prompt.dtype))

    y_s = x_sample
    for l in range(DEPTH):
        lg_f = jax.nn.log_sigmoid(ret_decay_fwd[l].astype(jnp.float32))
        lg_b = jax.nn.log_sigmoid(ret_decay_bwd[l].astype(jnp.float32))
        mixer = (lambda parts, lg_f=lg_f, lg_b=lg_b, rpb=na_rpb[l], ck=cache_na_k[:, l], cv=cache_na_v[:, l],
                 sf0=state_ret_fwd[:, l], sb0=state_ret_bwd[:, l]:
                 _latent_mixer(parts, lg_f, lg_b, rpb, ck, cv, sf0, sb0))
        y_s, _ = _layer(
            y_s, c, mixer, ada_w[l], ada_b[l], g_pre_mix[l], g_post_mix[l],
            g_pre_ffn[l], g_post_ffn[l], w_in[l], w_out[l], w_up[l], conv_w[l], conv_b[l], w_down[l])

    new_na_k = jnp.stack(ks_, axis=1)
    new_na_v = jnp.stack(vs_, axis=1)
    new_ret_fwd = jnp.stack(sfs_, axis=1)
    new_ret_bwd = jnp.stack(sbs_, axis=1)
    return (y_p, y_s, new_na_k, new_na_v, new_ret_fwd, new_ret_bwd)
```

```python
import functools

import numpy as np
import jax
import jax.numpy as jnp
from jax import lax
from jax.experimental import pallas as pl
from jax.experimental.pallas import tpu as pltpu

F32 = jnp.float32
BF16 = jnp.bfloat16

D_MODEL = 1024
HEAD_DIM = 64
N_HEADS = 8
HEAD_PAIRS = N_HEADS // 2
LANES = 128
WIDTH = N_HEADS * HEAD_DIM
N_GROUPS = 7
IN_WIDTH = N_GROUPS * WIDTH
D_FF = 2816
FF_CHUNK = 256
CHUNK = 128
GRID_W = 64
NA_KH = 8
NA_KW = 16
ROPE_BASE = 10000.0
EPS = 1e-6
NEG_INF = -1e9
HALO = 8
VMEM_LIMIT = 56 * 1024 * 1024


def _cparams(n_grid):
    return pltpu.CompilerParams(
        dimension_semantics=("arbitrary",) * n_grid, vmem_limit_bytes=VMEM_LIMIT)


def _rms(x, g):
    ms = jnp.mean(x * x, axis=-1, keepdims=True)
    return x * lax.rsqrt(ms + EPS) * g


def _silu(x):
    return x * jax.nn.sigmoid(x)


def _log_sigmoid(x):
    return jnp.minimum(x, 0.0) - jnp.log1p(jnp.exp(-jnp.abs(x)))


def _dot(a, b):
    return jnp.dot(a, b, preferred_element_type=F32)


def _dot_nt(a, b):
    return lax.dot_general(a, b, (((1,), (1,)), ((), ())), preferred_element_type=F32)


def _dot_tn(a, b):
    return lax.dot_general(a, b, (((0,), (0,)), ((), ())), preferred_element_type=F32)


def _mod_kernel(cond_ref, w_ref, b_ref, o_ref):
    s = _silu(cond_ref[...]).astype(BF16)
    o_ref[0] = _dot(s, w_ref[0].astype(BF16)) + b_ref[0]


def _modulation(cond, ada_w, ada_b):
    depth = ada_w.shape[0]
    nb = 6 * D_MODEL // D_MODEL
    return pl.pallas_call(
        _mod_kernel,
        grid=(depth, nb),
        in_specs=[
            pl.BlockSpec((8, D_MODEL), lambda l, j: (0, 0)),
            pl.BlockSpec((1, D_MODEL, D_MODEL), lambda l, j: (l, 0, j)),
            pl.BlockSpec((1, 1, D_MODEL), lambda l, j: (l, 0, j)),
        ],
        out_specs=pl.BlockSpec((1, 8, D_MODEL), lambda l, j: (l, 0, j)),
        out_shape=jax.ShapeDtypeStruct((depth, 8, 6 * D_MODEL), F32),
        compiler_params=_cparams(2),
        name="modulation",
    )(cond, ada_w, ada_b.reshape(depth, 1, 6 * D_MODEL))


def _inproj_kernel(*refs, rope, emit_kv):
    x_ref, mod_ref, g_ref, w_ref = refs[:4]
    refs = refs[4:]
    if rope:
        cos_ref, sin_up_ref, sin_dn_ref = refs[:3]
        refs = refs[3:]
    p_ref = refs[0]
    mod = mod_ref[0]
    h = (_rms(x_ref[...], g_ref[...]) * (1.0 + mod[1:2]) + mod[0:1]).astype(BF16)
    for g in range(N_GROUPS):
        cols = slice(g * WIDTH, (g + 1) * WIDTH)
        pg = _dot(h, w_ref[:, cols])
        if rope and g < 2:
            parts = []
            for j in range(WIDTH // LANES):
                xj = pg[:, j * LANES:(j + 1) * LANES]
                parts.append(xj * cos_ref[...]
                             + pltpu.roll(xj, 16, 1) * sin_up_ref[...]
                             + pltpu.roll(xj, LANES - 16, 1) * sin_dn_ref[...])
            pg = jnp.concatenate(parts, axis=1)
        if g in (0, 4):
            pg = pg * (HEAD_DIM ** -0.5)
        p_ref[:, cols] = pg.astype(BF16)
        if emit_kv and g >= 5:
            refs[1][:, (g - 5) * WIDTH:(g - 4) * WIDTH] = pg


def _inproj(x, mod, g, w, seq_len, tm, rope_tabs=None, emit_kv=False):
    n = x.shape[0]
    tiles_per_seq = max(seq_len // tm, 1)
    cond_row = (lambda i: 0) if rope_tabs is None else (lambda i: 1 + i // tiles_per_seq)
    in_specs = [
        pl.BlockSpec((tm, D_MODEL), lambda i: (i, 0)),
        pl.BlockSpec((1, 6, D_MODEL), lambda i: (cond_row(i), 0, 0)),
        pl.BlockSpec((1, D_MODEL), lambda i: (0, 0)),
        pl.BlockSpec((D_MODEL, IN_WIDTH), lambda i: (0, 0), pipeline_mode=pl.Buffered(1)),
    ]
    args = [x, mod, g.reshape(1, D_MODEL), w]
    if rope_tabs is not None:
        in_specs += [pl.BlockSpec((tm, LANES), lambda i: (i % tiles_per_seq, 0))] * 3
        args += list(rope_tabs)
    out_specs = [pl.BlockSpec((tm, IN_WIDTH), lambda i: (i, 0))]
    out_shape = [jax.ShapeDtypeStruct((n, IN_WIDTH), BF16)]
    if emit_kv:
        out_specs.append(pl.BlockSpec((tm, 2 * WIDTH), lambda i: (i, 0)))
        out_shape.append(jax.ShapeDtypeStruct((n, 2 * WIDTH), F32))
    return pl.pallas_call(
        functools.partial(_inproj_kernel, rope=rope_tabs is not None, emit_kv=emit_kv),
        grid=(n // tm,),
        in_specs=in_specs,
        out_specs=out_specs,
        out_shape=out_shape,
        compiler_params=_cparams(1),
        name="inproj_latent" if rope_tabs is not None else "inproj_context",
    )(*args)


def _rope_tables(seq_len):
    t = np.arange(seq_len)
    lane = np.arange(LANES)
    d = lane % HEAD_DIM
    pos = np.where(d[None, :] < HEAD_DIM // 2, (t // GRID_W)[:, None], (t % GRID_W)[:, None])
    pos = pos.astype(np.float32)
    half = HEAD_DIM // 2
    inv = np.power(np.float32(ROPE_BASE), -np.arange(0, half, 2, dtype=np.float32) / half)
    ang = pos * inv[d % (half // 2)][None, :]
    cos, sin = np.cos(ang), np.sin(ang)
    upper = (d % half) >= half // 2
    sin_up = np.where(upper[None, :], sin, 0.0)
    sin_dn = np.where(upper[None, :], 0.0, -sin)
    return (jnp.asarray(cos, F32), jnp.asarray(sin_up, F32), jnp.asarray(sin_dn, F32))


def _retention_tables(dec_f, dec_b, dec_f2, dec_b2):
    lgf, lgb = _log_sigmoid(dec_f), _log_sigmoid(dec_b)
    pos = lax.broadcasted_iota(jnp.int32, (CHUNK, LANES), 0).astype(F32)
    tabs = dict(
        qdf=jnp.exp(lgf * (pos + 1.0)), kdf=jnp.exp(lgf * (CHUNK - 1.0 - pos)),
        cdf=jnp.exp(lgf * float(CHUNK)),
        qdb=jnp.exp(lgb * (CHUNK - pos)), kdb=jnp.exp(lgb * pos),
        cdb=jnp.exp(lgb * float(CHUNK)),
    )
    lgf2, lgb2 = _log_sigmoid(dec_f2), _log_sigmoid(dec_b2)
    i = lax.broadcasted_iota(jnp.int32, (CHUNK, 2 * CHUNK), 0)
    j = lax.broadcasted_iota(jnp.int32, (CHUNK, 2 * CHUNK), 1) & (CHUNK - 1)
    diff = (i - j).astype(F32)
    tabs["decay"] = (jnp.where(diff >= 0, jnp.exp(lgf2 * jnp.maximum(diff, 0.0)), 0.0)
                     + jnp.where(diff <= 0, jnp.exp(lgb2 * jnp.maximum(-diff, 0.0)), 0.0))
    lane = lax.broadcasted_iota(jnp.int32, (1, LANES), 1)
    tabs["head_a"] = lane < HEAD_DIM
    r = lax.broadcasted_iota(jnp.int32, (2 * LANES, LANES), 0) & (LANES - 1)
    c = lax.broadcasted_iota(jnp.int32, (2 * LANES, LANES), 1)
    tabs["same_head"] = (r < HEAD_DIM) == (c < HEAD_DIM)
    return tabs


def _split_heads(x, head_a, axis):
    zero = jnp.zeros_like(x)
    return jnp.concatenate([jnp.where(head_a, x, zero), jnp.where(head_a, zero, x)], axis=axis)


def _chunk_kv(k2, v2, tabs):
    kf = k2.astype(F32)
    kk = jnp.concatenate([kf * tabs["kdf"], kf * tabs["kdb"]], axis=1).astype(BF16)
    return jnp.where(tabs["same_head"], _dot_tn(kk, v2), 0.0)


def _chunk_out(q2, k2, v2, g2, state, tabs):
    head_a = tabs["head_a"]
    s = _dot_nt(q2, _split_heads(k2, head_a, 0))
    p = (s * tabs["decay"]).astype(BF16)
    qf = q2.astype(F32)
    lhs = jnp.concatenate(
        [p, (qf * tabs["qdf"]).astype(BF16), (qf * tabs["qdb"]).astype(BF16)], axis=1)
    rhs = jnp.concatenate([_split_heads(v2, head_a, 0), state], axis=0)
    o = _dot(lhs, rhs)
    inv = 1.0 / HEAD_DIM
    sum_a = jnp.sum(jnp.where(head_a, o, 0.0), axis=-1, keepdims=True)
    sum_b = jnp.sum(jnp.where(head_a, 0.0, o), axis=-1, keepdims=True)
    d = o - jnp.where(head_a, sum_a, sum_b) * inv
    d2 = d * d
    var_a = jnp.sum(jnp.where(head_a, d2, 0.0), axis=-1, keepdims=True)
    var_b = jnp.sum(jnp.where(head_a, 0.0, d2), axis=-1, keepdims=True)
    o = d * lax.rsqrt(jnp.where(head_a, var_a, var_b) * inv + EPS)
    return o * _silu(g2.astype(F32))


def _ctx_mixer_kernel(rq, rk, rv, rg, nq, nk, nv, df, db, df2, db2, mix_ref, st_ref):
    seq = rq.shape[0]
    nc = seq // CHUNK
    for hp in range(HEAD_PAIRS):
        cols = slice(hp * LANES, (hp + 1) * LANES)
        cols2 = slice(hp * 2 * LANES, (hp + 1) * 2 * LANES)
        tabs = _retention_tables(df[:, cols], db[:, cols], df2[:, cols2], db2[:, cols2])
        rows = [slice(c * CHUNK, (c + 1) * CHUNK) for c in range(nc)]
        kv = [_chunk_kv(rk[r, cols], rv[r, cols], tabs) for r in rows]
        sf = [jnp.zeros((LANES, LANES), F32)]
        for c in range(nc):
            sf.append(sf[-1] * tabs["cdf"] + kv[c][:LANES])
        sb = [jnp.zeros((LANES, LANES), F32)]
        for c in reversed(range(nc)):
            sb.append(sb[-1] * tabs["cdb"] + kv[c][LANES:])
        for c in range(nc):
            state = jnp.concatenate([sf[c], sb[nc - 1 - c]], axis=0).astype(BF16)
            o = _chunk_out(rq[rows[c], cols], rk[rows[c], cols], rv[rows[c], cols],
                           rg[rows[c], cols], state, tabs)
            mix_ref[rows[c], cols] = o.astype(BF16)
        st_ref[0, hp, 0] = sf[nc]
        st_ref[0, hp, 1] = sb[nc]

        head_a = tabs["head_a"]
        s = _dot_nt(nq[:, cols], _split_heads(nk[:, cols], head_a, 0))
        es, rinv = [], []
        for h in range(2):
            sh = s[:, h * seq:(h + 1) * seq]
            e = jnp.exp(sh - jnp.max(sh, axis=-1, keepdims=True))
            rinv.append(1.0 / jnp.sum(e, axis=-1, keepdims=True))
            es.append(e.astype(BF16))
        o = _dot(jnp.concatenate(es, axis=1), _split_heads(nv[:, cols], head_a, 0))
        o = o * jnp.where(head_a, rinv[0], rinv[1])
        mix_ref[:, WIDTH + hp * LANES:WIDTH + (hp + 1) * LANES] = o.astype(BF16)


def _ctx_mixer(p, dec, batch, seq):
    group = lambda g: pl.BlockSpec((seq, WIDTH), lambda b, g=g: (b, g))
    vec = lambda w: pl.BlockSpec((1, w), lambda b: (0, 0))
    return pl.pallas_call(
        _ctx_mixer_kernel,
        grid=(batch,),
        in_specs=[group(g) for g in range(N_GROUPS)] + [vec(WIDTH), vec(WIDTH),
                                                         vec(2 * WIDTH), vec(2 * WIDTH)],
        out_specs=[
            pl.BlockSpec((seq, 2 * WIDTH), lambda b: (b, 0)),
            pl.BlockSpec((1, HEAD_PAIRS, 2, LANES, LANES), lambda b: (b, 0, 0, 0, 0)),
        ],
        out_shape=[
            jax.ShapeDtypeStruct((batch * seq, 2 * WIDTH), BF16),
            jax.ShapeDtypeStruct((batch, HEAD_PAIRS, 2, LANES, LANES), F32),
        ],
        compiler_params=_cparams(1),
        name="context_mixer",
    )(*([p] * N_GROUPS), *dec)


def _lat_retention_kernel(q_ref, k_ref, v_ref, g_ref, df, db, df2, db2, s0_ref, o_ref,
                          kv_ref, st_ref):
    nc = q_ref.shape[0] // CHUNK
    tabs = _retention_tables(df[...], db[...], df2[...], db2[...])

    def rows(c):
        return pl.ds(pl.multiple_of(c * CHUNK, CHUNK), CHUNK)

    def kv_body(c, carry):
        kv_ref[c] = _chunk_kv(k_ref[rows(c), :], v_ref[rows(c), :], tabs)
        return carry
    lax.fori_loop(0, nc, kv_body, 0)

    def fwd_body(c, s):
        st_ref[c, :LANES, :] = s.astype(BF16)
        return s * tabs["cdf"] + kv_ref[c, :LANES, :]
    lax.fori_loop(0, nc, fwd_body, s0_ref[0, 0, 0])

    def bwd_body(i, s):
        c = nc - 1 - i
        st_ref[c, LANES:, :] = s.astype(BF16)
        return s * tabs["cdb"] + kv_ref[c, LANES:, :]
    lax.fori_loop(0, nc, bwd_body, s0_ref[0, 0, 1])

    def out_body(c, carry):
        r = rows(c)
        o = _chunk_out(q_ref[r, :], k_ref[r, :], v_ref[r, :], g_ref[r, :], st_ref[c], tabs)
        o_ref[r, :] = o.astype(BF16)
        return carry
    lax.fori_loop(0, nc, out_body, 0)


def _lat_retention(p, dec, s0, batch, seq):
    nc = seq // CHUNK
    group = lambda g: pl.BlockSpec((seq, LANES), lambda b, h, g=g: (b, g * HEAD_PAIRS + h))
    vec = lambda w: pl.BlockSpec((1, w), lambda b, h: (0, h))
    return pl.pallas_call(
        _lat_retention_kernel,
        grid=(batch, HEAD_PAIRS),
        in_specs=[group(g) for g in range(4)] + [vec(LANES), vec(LANES), vec(2 * LANES),
                                                 vec(2 * LANES)]
        + [pl.BlockSpec((1, 1, 2, LANES, LANES), lambda b, h: (b, h, 0, 0, 0))],
        out_specs=pl.BlockSpec((seq, LANES), lambda b, h: (b, h)),
        out_shape=jax.ShapeDtypeStruct((batch * seq, WIDTH), BF16),
        scratch_shapes=[pltpu.VMEM((nc, 2 * LANES, LANES), F32),
                        pltpu.VMEM((nc, 2 * LANES, LANES), BF16)],
        compiler_params=_cparams(2),
        name="latent_retention",
    )(*([p] * 4), *dec, s0)


def _block_diag_states(s_f, s_b):
    def bd(s):
        b = s.shape[0]
        s = s.reshape(b, HEAD_PAIRS, 2, HEAD_DIM, HEAD_DIM)
        z = jnp.zeros_like(s[:, :, 0])
        top = jnp.concatenate([s[:, :, 0], z], axis=-1)
        bot = jnp.concatenate([z, s[:, :, 1]], axis=-1)
        return jnp.concatenate([top, bot], axis=-2)
    return jnp.stack([bd(s_f), bd(s_b)], axis=2)


def _diag_states(st):
    b = st.shape[0]
    a = st[:, :, :HEAD_DIM, :HEAD_DIM]
    c = st[:, :, HEAD_DIM:, HEAD_DIM:]
    return jnp.stack([a, c], axis=2).reshape(b, N_HEADS, HEAD_DIM, HEAD_DIM)


def _na_kernel(q_ref, k_ref, v_ref, ck_ref, cv_ref, bias_ref, o_ref):
    rows_total = k_ref.shape[0] // GRID_W
    r = pl.program_id(1)
    rs = jnp.clip(r - NA_KH // 2, 0, rows_total - NA_KH)
    win = pl.ds(pl.multiple_of(rs * GRID_W, GRID_W), NA_KH * GRID_W)
    lane = lax.broadcasted_iota(jnp.int32, (1, LANES), 1)
    head_a = lane < HEAD_DIM
    for hp in range(HEAD_PAIRS):
        cols = slice(hp * LANES, (hp + 1) * LANES)
        qq = _split_heads(q_ref[:, cols], head_a, 0)
        s_loc = _dot_nt(qq, k_ref[win, cols]) + bias_ref[0, hp]
        s_ctx = _dot_nt(qq, ck_ref[0, :, cols])
        m = jnp.maximum(jnp.max(s_loc, axis=-1, keepdims=True),
                        jnp.max(s_ctx, axis=-1, keepdims=True))
        e_loc = jnp.exp(s_loc - m)
        e_ctx = jnp.exp(s_ctx - m)
        den = jnp.sum(e_loc, axis=-1, keepdims=True) + jnp.sum(e_ctx, axis=-1, keepdims=True)
        o = _dot(e_loc.astype(BF16), v_ref[win, cols]) + _dot(e_ctx.astype(BF16), cv_ref[0, :, cols])
        o = o / den
        o_ref[:, cols] = jnp.where(head_a, o[:GRID_W], o[GRID_W:]).astype(BF16)


def _na_bias_tables(rpb, rows_total):
    kh = NA_KH
    lo, hi = kh // 2, rows_total - kh // 2 - 1
    reps = list(range(lo)) + [lo] + list(range(hi + 1, rows_total))
    r = np.asarray(reps)
    rs = np.clip(r - kh // 2, 0, rows_total - kh)
    dr = rs[:, None] + np.arange(kh)[None, :] - r[:, None] + kh - 1
    qc = np.arange(GRID_W)
    kc = np.arange(GRID_W)
    cs = np.clip(qc - NA_KW // 2, 0, GRID_W - NA_KW)
    ok = (kc[None, :] >= cs[:, None]) & (kc[None, :] < cs[:, None] + NA_KW)
    dc = np.clip(kc[None, :] - qc[:, None] + NA_KW - 1, 0, 2 * NA_KW - 2)
    b = rpb.astype(F32)[:, dr[:, None, :, None], dc[None, :, None, :]]
    b = jnp.where(ok[None, None, :, None, :], b, NEG_INF)
    ncls = len(reps)
    b = b.reshape(HEAD_PAIRS, 2, ncls, GRID_W, kh * GRID_W)
    b = b.transpose(2, 0, 1, 3, 4).reshape(ncls, HEAD_PAIRS, 2 * GRID_W, kh * GRID_W)
    return b, lo, hi


def _neighbourhood_attention(p, ctx_k, ctx_v, bias, lo, hi, batch, seq):
    rows_total = seq // GRID_W
    past = ctx_k.shape[1]

    def cls(r):
        return jnp.where(r < lo, r, jnp.where(r > hi, r - hi + lo, lo))

    return pl.pallas_call(
        _na_kernel,
        grid=(batch, rows_total),
        in_specs=[
            pl.BlockSpec((GRID_W, WIDTH), lambda b, r: (b * rows_total + r, 4)),
            pl.BlockSpec((seq, WIDTH), lambda b, r: (b, 5)),
            pl.BlockSpec((seq, WIDTH), lambda b, r: (b, 6)),
            pl.BlockSpec((1, past, WIDTH), lambda b, r: (b, 0, 0)),
            pl.BlockSpec((1, past, WIDTH), lambda b, r: (b, 0, 0)),
            pl.BlockSpec((1, HEAD_PAIRS, 2 * GRID_W, NA_KH * GRID_W),
                         lambda b, r: (cls(r), 0, 0, 0)),
        ],
        out_specs=pl.BlockSpec((GRID_W, WIDTH), lambda b, r: (b * rows_total + r, 0)),
        out_shape=jax.ShapeDtypeStruct((batch * seq, WIDTH), BF16),
        compiler_params=_cparams(2),
        name="neighbourhood_attention",
    )(p, p, p, ctx_k, ctx_v, bias)


def _post_kernel(xm, xp, xn, am, ap, an, bm, bp, bn, mod_ref, g_ref, wo, wu, cw, cb, wd, o_ref,
                 acc_ref, *, seq_len):
    tm = xm.shape[0]
    mod = mod_ref[0]
    g = g_ref[...]

    def ext(main, prev, nxt):
        lo = prev[...].astype(F32)[prev.shape[0] - HALO:]
        hi = nxt[...].astype(F32)[:HALO]
        return jnp.concatenate([lo, main[...].astype(F32), hi], axis=0)

    x = ext(xm, xp, xn)
    a = ext(am, ap, an).astype(BF16)
    b = ext(bm, bp, bn).astype(BF16)
    half = wo.shape[0] // 2
    y = _dot(a, wo[:half, :]) + _dot(b, wo[half:, :])
    x1 = x + mod[2:3] * _rms(y, g[0:1])
    h = (_rms(x1, g[1:2]) * (1.0 + mod[4:5]) + mod[3:4]).astype(BF16)

    row = pl.program_id(0) * tm + lax.broadcasted_iota(jnp.int32, (tm, FF_CHUNK), 0)
    pos = row & (seq_len - 1)
    has_prev = pos != 0
    has_next = pos != seq_len - 1

    def conv(u, cols):
        w = cw[:, cols]
        prev = jnp.where(has_prev, u[HALO - 1:HALO - 1 + tm], 0.0)
        nxt = jnp.where(has_next, u[HALO + 1:HALO + 1 + tm], 0.0)
        return prev * w[0:1] + u[HALO:HALO + tm] * w[1:2] + nxt * w[2:3] + cb[:, cols]

    for ch in range(D_FF // FF_CHUNK):
        ca = slice(ch * FF_CHUNK, (ch + 1) * FF_CHUNK)
        cg = slice(D_FF + ch * FF_CHUNK, D_FF + (ch + 1) * FF_CHUNK)
        act = (_silu(conv(_dot(h, wu[:, cg]), cg)) * conv(_dot(h, wu[:, ca]), ca)).astype(BF16)
        part = _dot(act, wd[ca, :])
        if ch == 0:
            acc_ref[...] = part
        else:
            acc_ref[...] += part
    o_ref[...] = x1[HALO:HALO + tm] + mod[5:6] * _rms(acc_ref[...], g[2:3])


def _post(x, mix_a, mix_b, mod, gains, wo, wu, cw, cb, wd, seq_len, tm, latent):
    n = x.shape[0]
    nt = n // tm
    assert seq_len & (seq_len - 1) == 0
    (mix_a, col_a), (mix_b, col_b) = mix_a, mix_b
    tiles_per_seq = max(seq_len // tm, 1)
    cond_row = (lambda i: 1 + i // tiles_per_seq) if latent else (lambda i: 0)

    def triple(width, halo_rows, col=0):
        per = tm // halo_rows
        last = n // halo_rows - 1
        return [
            pl.BlockSpec((tm, width), lambda i: (i, col)),
            pl.BlockSpec((halo_rows, width), lambda i: (jnp.maximum(i * per - 1, 0), col)),
            pl.BlockSpec((halo_rows, width), lambda i: (jnp.minimum((i + 1) * per, last), col)),
        ]

    const = lambda shape: pl.BlockSpec(shape, lambda i: (0,) * len(shape),
                                       pipeline_mode=pl.Buffered(1))
    in_specs = (triple(D_MODEL, HALO) + triple(WIDTH, 2 * HALO, col_a)
                + triple(WIDTH, 2 * HALO, col_b) + [
        pl.BlockSpec((1, 6, D_MODEL), lambda i: (cond_row(i), 0, 0)),
        const((3, D_MODEL)),
        const((2 * WIDTH, D_MODEL)),
        const((D_MODEL, 2 * D_FF)),
        const((3, 2 * D_FF)),
        const((1, 2 * D_FF)),
        const((D_FF, D_MODEL)),
    ])
    return pl.pallas_call(
        functools.partial(_post_kernel, seq_len=seq_len),
        grid=(nt,),
        in_specs=in_specs,
        out_specs=pl.BlockSpec((tm, D_MODEL), lambda i: (i, 0)),
        out_shape=jax.ShapeDtypeStruct((n, D_MODEL), F32),
        scratch_shapes=[pltpu.VMEM((tm, D_MODEL), F32)],
        compiler_params=_cparams(1),
        name="post_latent" if latent else "post_context",
    )(x, x, x, mix_a, mix_a, mix_a, mix_b, mix_b, mix_b, mod, gains, wo, wu, cw, cb, wd)


def kernel(x_prompt, x_sample, c, cache_na_k, cache_na_v, state_ret_fwd, state_ret_bwd, c_ctx,
           ada_w, ada_b, g_pre_mix, g_post_mix, g_pre_ffn, g_post_ffn, w_in,
           ret_decay_fwd, ret_decay_bwd, na_rpb, w_out, w_up, conv_w, conv_b, w_down):
    depth = w_in.shape[0]
    batch, seq, _ = x_prompt.shape
    dec_batch, dec_seq, _ = x_sample.shape
    past = cache_na_k.shape[2]
    tm = 512

    cond = jnp.concatenate(
        [c_ctx[None, :], c, jnp.zeros((8 - 1 - dec_batch, D_MODEL), F32)], axis=0)
    mods = _modulation(cond, ada_w, ada_b).reshape(depth, 8, 6, D_MODEL)
    rope_tabs = _rope_tables(dec_seq)

    y_p = x_prompt.reshape(batch * seq, D_MODEL)
    y_s = x_sample.reshape(dec_batch * dec_seq, D_MODEL)
    ks, vs, sfs, sbs = [], [], [], []
    for l in range(depth):
        w_in_l = w_in[l].astype(BF16)
        wo_l = w_out[l].astype(BF16)
        wu_l = w_up[l].astype(BF16)
        wd_l = w_down[l].astype(BF16)
        gains = jnp.stack([g_post_mix[l], g_pre_ffn[l], g_post_ffn[l]], axis=0)
        cb_l = conv_b[l].reshape(1, 2 * D_FF)
        dec = (jnp.repeat(ret_decay_fwd[l], HEAD_DIM)[None, :],
               jnp.repeat(ret_decay_bwd[l], HEAD_DIM)[None, :],
               jnp.repeat(ret_decay_fwd[l], LANES)[None, :],
               jnp.repeat(ret_decay_bwd[l], LANES)[None, :])

        p_c, kv_c = _inproj(y_p, mods[l], g_pre_mix[l], w_in_l, seq, tm, emit_kv=True)
        mix_c, st_c = _ctx_mixer(p_c, dec, batch, seq)
        y_p = _post(y_p, (mix_c, 0), (mix_c, 1), mods[l], gains, wo_l, wu_l,
                    conv_w[l], cb_l, wd_l, seq, tm, latent=False)
        ks.append(kv_c[:, :WIDTH].reshape(batch, seq, N_HEADS, HEAD_DIM))
        vs.append(kv_c[:, WIDTH:].reshape(batch, seq, N_HEADS, HEAD_DIM))
        sfs.append(_diag_states(st_c[:, :, 0]))
        sbs.append(_diag_states(st_c[:, :, 1]))

        (p_s,) = _inproj(y_s, mods[l], g_pre_mix[l], w_in_l, dec_seq, tm, rope_tabs=rope_tabs)
        s0 = _block_diag_states(state_ret_fwd[:, l], state_ret_bwd[:, l])
        ret_s = _lat_retention(p_s, dec, s0, dec_batch, dec_seq)
        bias, lo, hi = _na_bias_tables(na_rpb[l], dec_seq // GRID_W)
        ck = cache_na_k[:, l].reshape(dec_batch, past, WIDTH).astype(BF16)
        cv = cache_na_v[:, l].reshape(dec_batch, past, WIDTH).astype(BF16)
        na_s = _neighbourhood_attention(p_s, ck, cv, bias, lo, hi, dec_batch, dec_seq)
        y_s = _post(y_s, (ret_s, 0), (na_s, 0), mods[l], gains, wo_l, wu_l, conv_w[l], cb_l, wd_l,
                    dec_seq, tm, latent=True)

    return (y_p.reshape(batch, seq, D_MODEL),
            y_s.reshape(dec_batch, dec_seq, D_MODEL),
            jnp.stack(ks, axis=1), jnp.stack(vs, axis=1),
            jnp.stack(sfs, axis=1), jnp.stack(sbs, axis=1))
```

```python
import functools

import numpy as np
import jax
import jax.numpy as jnp
from jax import lax
from jax.experimental import pallas as pl
from jax.experimental.pallas import tpu as pltpu

F32 = jnp.float32
BF16 = jnp.bfloat16

D_MODEL = 1024
HEAD_DIM = 64
N_HEADS = 8
HEAD_PAIRS = N_HEADS // 2
LANES = 128
WIDTH = N_HEADS * HEAD_DIM
N_GROUPS = 7
IN_WIDTH = N_GROUPS * WIDTH
D_FF = 2816
FF_CHUNK = 256
CHUNK = 128
GRID_W = 64
NA_KH = 8
NA_KW = 16
ROPE_BASE = 10000.0
EPS = 1e-6
NEG_INF = -1e9
HALO = 8
VMEM_LIMIT = 56 * 1024 * 1024


def _cparams(n_grid):
    return pltpu.CompilerParams(
        dimension_semantics=("arbitrary",) * n_grid, vmem_limit_bytes=VMEM_LIMIT)


def _rms(x, g):
    ms = jnp.mean(x * x, axis=-1, keepdims=True)
    return x * lax.rsqrt(ms + EPS) * g


def _silu(x):
    return x * jax.nn.sigmoid(x)


def _log_sigmoid(x):
    return jnp.minimum(x, 0.0) - jnp.log1p(jnp.exp(-jnp.abs(x)))


def _dot(a, b):
    return jnp.dot(a, b, preferred_element_type=F32)


def _dot_nt(a, b):
    return lax.dot_general(a, b, (((1,), (1,)), ((), ())), preferred_element_type=F32)


def _dot_tn(a, b):
    return lax.dot_general(a, b, (((0,), (0,)), ((), ())), preferred_element_type=F32)


def _mod_kernel(cond_ref, w_ref, b_ref, o_ref):
    s = _silu(cond_ref[...]).astype(BF16)
    o_ref[0] = _dot(s, w_ref[0].astype(BF16)) + b_ref[0]


def _modulation(cond, ada_w, ada_b):
    depth = ada_w.shape[0]
    nb = 6 * D_MODEL // D_MODEL
    return pl.pallas_call(
        _mod_kernel,
        grid=(depth, nb),
        in_specs=[
            pl.BlockSpec((8, D_MODEL), lambda l, j: (0, 0)),
            pl.BlockSpec((1, D_MODEL, D_MODEL), lambda l, j: (l, 0, j)),
            pl.BlockSpec((1, 1, D_MODEL), lambda l, j: (l, 0, j)),
        ],
        out_specs=pl.BlockSpec((1, 8, D_MODEL), lambda l, j: (l, 0, j)),
        out_shape=jax.ShapeDtypeStruct((depth, 8, 6 * D_MODEL), F32),
        compiler_params=_cparams(2),
        name="modulation",
    )(cond, ada_w, ada_b.reshape(depth, 1, 6 * D_MODEL))


def _inproj_kernel(*refs, rope, emit_kv):
    x_ref, mod_ref, g_ref, w_ref = refs[:4]
    refs = refs[4:]
    if rope:
        cos_ref, sin_up_ref, sin_dn_ref = refs[:3]
        refs = refs[3:]
    p_ref = refs[0]
    mod = mod_ref[0]
    h = (_rms(x_ref[...], g_ref[...]) * (1.0 + mod[1:2]) + mod[0:1]).astype(BF16)
    for g in range(N_GROUPS):
        cols = slice(g * WIDTH, (g + 1) * WIDTH)
        pg = _dot(h, w_ref[:, cols])
        if rope and g < 2:
            parts = []
            for j in range(WIDTH // LANES):
                xj = pg[:, j * LANES:(j + 1) * LANES]
                parts.append(xj * cos_ref[...]
                             + pltpu.roll(xj, 16, 1) * sin_up_ref[...]
                             + pltpu.roll(xj, LANES - 16, 1) * sin_dn_ref[...])
            pg = jnp.concatenate(parts, axis=1)
        if g in (0, 4):
            pg = pg * (HEAD_DIM ** -0.5)
        p_ref[:, cols] = pg.astype(BF16)
        if emit_kv and g >= 5:
            refs[1][:, (g - 5) * WIDTH:(g - 4) * WIDTH] = pg


def _inproj(x, mod, g, w, seq_len, tm, rope_tabs=None, emit_kv=False):
    n = x.shape[0]
    tiles_per_seq = max(seq_len // tm, 1)
    cond_row = (lambda i: 0) if rope_tabs is None else (lambda i: 1 + i // tiles_per_seq)
    in_specs = [
        pl.BlockSpec((tm, D_MODEL), lambda i: (i, 0)),
        pl.BlockSpec((1, 6, D_MODEL), lambda i: (cond_row(i), 0, 0)),
        pl.BlockSpec((1, D_MODEL), lambda i: (0, 0)),
        pl.BlockSpec((D_MODEL, IN_WIDTH), lambda i: (0, 0), pipeline_mode=pl.Buffered(1)),
    ]
    args = [x, mod, g.reshape(1, D_MODEL), w]
    if rope_tabs is not None:
        in_specs += [pl.BlockSpec((tm, LANES), lambda i: (i % tiles_per_seq, 0))] * 3
        args += list(rope_tabs)
    out_specs = [pl.BlockSpec((tm, IN_WIDTH), lambda i: (i, 0))]
    out_shape = [jax.ShapeDtypeStruct((n, IN_WIDTH), BF16)]
    if emit_kv:
        out_specs.append(pl.BlockSpec((tm, 2 * WIDTH), lambda i: (i, 0)))
        out_shape.append(jax.ShapeDtypeStruct((n, 2 * WIDTH), F32))
    return pl.pallas_call(
        functools.partial(_inproj_kernel, rope=rope_tabs is not None, emit_kv=emit_kv),
        grid=(n // tm,),
        in_specs=in_specs,
        out_specs=out_specs,
        out_shape=out_shape,
        compiler_params=_cparams(1),
        name="inproj_latent" if rope_tabs is not None else "inproj_context",
    )(*args)


def _rope_tables(seq_len):
    t = np.arange(seq_len)
    lane = np.arange(LANES)
    d = lane % HEAD_DIM
    pos = np.where(d[None, :] < HEAD_DIM // 2, (t // GRID_W)[:, None], (t % GRID_W)[:, None])
    pos = pos.astype(np.float32)
    half = HEAD_DIM // 2
    inv = np.power(np.float32(ROPE_BASE), -np.arange(0, half, 2, dtype=np.float32) / half)
    ang = pos * inv[d % (half // 2)][None, :]
    cos, sin = np.cos(ang), np.sin(ang)
    upper = (d % half) >= half // 2
    sin_up = np.where(upper[None, :], sin, 0.0)
    sin_dn = np.where(upper[None, :], 0.0, -sin)
    return (jnp.asarray(cos, F32), jnp.asarray(sin_up, F32), jnp.asarray(sin_dn, F32))


def _retention_tables(dec_f, dec_b, dec_f2, dec_b2):
    lgf, lgb = _log_sigmoid(dec_f), _log_sigmoid(dec_b)
    pos = lax.broadcasted_iota(jnp.int32, (CHUNK, LANES), 0).astype(F32)
    tabs = dict(
        qdf=jnp.exp(lgf * (pos + 1.0)), kdf=jnp.exp(lgf * (CHUNK - 1.0 - pos)),
        cdf=jnp.exp(lgf * float(CHUNK)),
        qdb=jnp.exp(lgb * (CHUNK - pos)), kdb=jnp.exp(lgb * pos),
        cdb=jnp.exp(lgb * float(CHUNK)),
    )
    lgf2, lgb2 = _log_sigmoid(dec_f2), _log_sigmoid(dec_b2)
    i = lax.broadcasted_iota(jnp.int32, (CHUNK, 2 * CHUNK), 0)
    j = lax.broadcasted_iota(jnp.int32, (CHUNK, 2 * CHUNK), 1) & (CHUNK - 1)
    diff = (i - j).astype(F32)
    tabs["decay"] = (jnp.where(diff >= 0, jnp.exp(lgf2 * jnp.maximum(diff, 0.0)), 0.0)
                     + jnp.where(diff <= 0, jnp.exp(lgb2 * jnp.maximum(-diff, 0.0)), 0.0))
    lane = lax.broadcasted_iota(jnp.int32, (1, LANES), 1)
    tabs["head_a"] = lane < HEAD_DIM
    r = lax.broadcasted_iota(jnp.int32, (2 * LANES, LANES), 0) & (LANES - 1)
    c = lax.broadcasted_iota(jnp.int32, (2 * LANES, LANES), 1)
    tabs["same_head"] = (r < HEAD_DIM) == (c < HEAD_DIM)
    return tabs


def _split_heads(x, head_a, axis):
    zero = jnp.zeros_like(x)
    return jnp.concatenate([jnp.where(head_a, x, zero), jnp.where(head_a, zero, x)], axis=axis)


def _chunk_kv(k2, v2, tabs):
    kf = k2.astype(F32)
    kk = jnp.concatenate([kf * tabs["kdf"], kf * tabs["kdb"]], axis=1).astype(BF16)
    return jnp.where(tabs["same_head"], _dot_tn(kk, v2), 0.0)


def _chunk_out(q2, k2, v2, g2, state, tabs):
    head_a = tabs["head_a"]
    s = _dot_nt(q2, _split_heads(k2, head_a, 0))
    p = (s * tabs["decay"]).astype(BF16)
    qf = q2.astype(F32)
    lhs = jnp.concatenate(
        [p, (qf * tabs["qdf"]).astype(BF16), (qf * tabs["qdb"]).astype(BF16)], axis=1)
    rhs = jnp.concatenate([_split_heads(v2, head_a, 0), state], axis=0)
    o = _dot(lhs, rhs)
    inv = 1.0 / HEAD_DIM
    sum_a = jnp.sum(jnp.where(head_a, o, 0.0), axis=-1, keepdims=True)
    sum_b = jnp.sum(jnp.where(head_a, 0.0, o), axis=-1, keepdims=True)
    d = o - jnp.where(head_a, sum_a, sum_b) * inv
    d2 = d * d
    var_a = jnp.sum(jnp.where(head_a, d2, 0.0), axis=-1, keepdims=True)
    var_b = jnp.sum(jnp.where(head_a, 0.0, d2), axis=-1, keepdims=True)
    o = d * lax.rsqrt(jnp.where(head_a, var_a, var_b) * inv + EPS)
    return o * _silu(g2.astype(F32))


def _ctx_mixer_kernel(rq, rk, rv, rg, nq, nk, nv, df, db, df2, db2, mix_ref, st_ref):
    seq = rq.shape[0]
    nc = seq // CHUNK
    for hp in range(HEAD_PAIRS):
        cols = slice(hp * LANES, (hp + 1) * LANES)
        cols2 = slice(hp * 2 * LANES, (hp + 1) * 2 * LANES)
        tabs = _retention_tables(df[:, cols], db[:, cols], df2[:, cols2], db2[:, cols2])
        rows = [slice(c * CHUNK, (c + 1) * CHUNK) for c in range(nc)]
        kv = [_chunk_kv(rk[r, cols], rv[r, cols], tabs) for r in rows]
        sf = [jnp.zeros((LANES, LANES), F32)]
        for c in range(nc):
            sf.append(sf[-1] * tabs["cdf"] + kv[c][:LANES])
        sb = [jnp.zeros((LANES, LANES), F32)]
        for c in reversed(range(nc)):
            sb.append(sb[-1] * tabs["cdb"] + kv[c][LANES:])
        for c in range(nc):
            state = jnp.concatenate([sf[c], sb[nc - 1 - c]], axis=0).astype(BF16)
            o = _chunk_out(rq[rows[c], cols], rk[rows[c], cols], rv[rows[c], cols],
                           rg[rows[c], cols], state, tabs)
            mix_ref[rows[c], cols] = o.astype(BF16)
        st_ref[0, hp, 0] = sf[nc]
        st_ref[0, hp, 1] = sb[nc]

        head_a = tabs["head_a"]
        s = _dot_nt(nq[:, cols], _split_heads(nk[:, cols], head_a, 0))
        es, rinv = [], []
        for h in range(2):
            sh = s[:, h * seq:(h + 1) * seq]
            e = jnp.exp(sh - jnp.max(sh, axis=-1, keepdims=True))
            rinv.append(1.0 / jnp.sum(e, axis=-1, keepdims=True))
            es.append(e.astype(BF16))
        o = _dot(jnp.concatenate(es, axis=1), _split_heads(nv[:, cols], head_a, 0))
        o = o * jnp.where(head_a, rinv[0], rinv[1])
        mix_ref[:, WIDTH + hp * LANES:WIDTH + (hp + 1) * LANES] = o.astype(BF16)


def _ctx_mixer(p, dec, batch, seq):
    group = lambda g: pl.BlockSpec((seq, WIDTH), lambda b, g=g: (b, g))
    vec = lambda w: pl.BlockSpec((1, w), lambda b: (0, 0))
    return pl.pallas_call(
        _ctx_mixer_kernel,
        grid=(batch,),
        in_specs=[group(g) for g in range(N_GROUPS)] + [vec(WIDTH), vec(WIDTH),
                                                         vec(2 * WIDTH), vec(2 * WIDTH)],
        out_specs=[
            pl.BlockSpec((seq, 2 * WIDTH), lambda b: (b, 0)),
            pl.BlockSpec((1, HEAD_PAIRS, 2, LANES, LANES), lambda b: (b, 0, 0, 0, 0)),
        ],
        out_shape=[
            jax.ShapeDtypeStruct((batch * seq, 2 * WIDTH), BF16),
            jax.ShapeDtypeStruct((batch, HEAD_PAIRS, 2, LANES, LANES), F32),
        ],
        compiler_params=_cparams(1),
        name="context_mixer",
    )(*([p] * N_GROUPS), *dec)


def _lat_retention_kernel(q_ref, k_ref, v_ref, g_ref, df, db, df2, db2, s0_ref, o_ref,
                          kv_ref, st_ref):
    nc = q_ref.shape[0] // CHUNK
    tabs = _retention_tables(df[...], db[...], df2[...], db2[...])

    def rows(c):
        return pl.ds(pl.multiple_of(c * CHUNK, CHUNK), CHUNK)

    def kv_body(c, carry):
        kv_ref[c] = _chunk_kv(k_ref[rows(c), :], v_ref[rows(c), :], tabs)
        return carry
    lax.fori_loop(0, nc, kv_body, 0)

    def fwd_body(c, s):
        st_ref[c, :LANES, :] = s.astype(BF16)
        return s * tabs["cdf"] + kv_ref[c, :LANES, :]
    lax.fori_loop(0, nc, fwd_body, s0_ref[0, 0, 0])

    def bwd_body(i, s):
        c = nc - 1 - i
        st_ref[c, LANES:, :] = s.astype(BF16)
        return s * tabs["cdb"] + kv_ref[c, LANES:, :]
    lax.fori_loop(0, nc, bwd_body, s0_ref[0, 0, 1])

    def out_body(c, carry):
        r = rows(c)
        o = _chunk_out(q_ref[r, :], k_ref[r, :], v_ref[r, :], g_ref[r, :], st_ref[c], tabs)
        o_ref[r, :] = o.astype(BF16)
        return carry
    lax.fori_loop(0, nc, out_body, 0)


def _lat_retention(p, dec, s0, batch, seq):
    nc = seq // CHUNK
    group = lambda g: pl.BlockSpec((seq, LANES), lambda b, h, g=g: (b, g * HEAD_PAIRS + h))
    vec = lambda w: pl.BlockSpec((1, w), lambda b, h: (0, h))
    return pl.pallas_call(
        _lat_retention_kernel,
        grid=(batch, HEAD_PAIRS),
        in_specs=[group(g) for g in range(4)] + [vec(LANES), vec(LANES), vec(2 * LANES),
                                                 vec(2 * LANES)]
        + [pl.BlockSpec((1, 1, 2, LANES, LANES), lambda b, h: (b, h, 0, 0, 0))],
        out_specs=pl.BlockSpec((seq, LANES), lambda b, h: (b, h)),
        out_shape=jax.ShapeDtypeStruct((batch * seq, WIDTH), BF16),
        scratch_shapes=[pltpu.VMEM((nc, 2 * LANES, LANES), F32),
                        pltpu.VMEM((nc, 2 * LANES, LANES), BF16)],
        compiler_params=_cparams(2),
        name="latent_retention",
    )(*([p] * 4), *dec, s0)


def _block_diag_states(s_f, s_b):
    def bd(s):
        b = s.shape[0]
        s = s.reshape(b, HEAD_PAIRS, 2, HEAD_DIM, HEAD_DIM)
        z = jnp.zeros_like(s[:, :, 0])
        top = jnp.concatenate([s[:, :, 0], z], axis=-1)
        bot = jnp.concatenate([z, s[:, :, 1]], axis=-1)
        return jnp.concatenate([top, bot], axis=-2)
    return jnp.stack([bd(s_f), bd(s_b)], axis=2)


def _diag_states(st):
    b = st.shape[0]
    a = st[:, :, :HEAD_DIM, :HEAD_DIM]
    c = st[:, :, HEAD_DIM:, HEAD_DIM:]
    return jnp.stack([a, c], axis=2).reshape(b, N_HEADS, HEAD_DIM, HEAD_DIM)


def _na_kernel(q_ref, k_ref, v_ref, ck_ref, cv_ref, bias_ref, o_ref):
    rows_total = k_ref.shape[0] // GRID_W
    r = pl.program_id(1)
    rs = jnp.clip(r - NA_KH // 2, 0, rows_total - NA_KH)
    win = pl.ds(pl.multiple_of(rs * GRID_W, GRID_W), NA_KH * GRID_W)
    lane = lax.broadcasted_iota(jnp.int32, (1, LANES), 1)
    head_a = lane < HEAD_DIM
    for hp in range(HEAD_PAIRS):
        cols = slice(hp * LANES, (hp + 1) * LANES)
        qq = _split_heads(q_ref[:, cols], head_a, 0)
        s_loc = _dot_nt(qq, k_ref[win, cols]) + bias_ref[0, hp]
        s_ctx = _dot_nt(qq, ck_ref[0, :, cols])
        m = jnp.maximum(jnp.max(s_loc, axis=-1, keepdims=True),
                        jnp.max(s_ctx, axis=-1, keepdims=True))
        e_loc = jnp.exp(s_loc - m)
        e_ctx = jnp.exp(s_ctx - m)
        den = jnp.sum(e_loc, axis=-1, keepdims=True) + jnp.sum(e_ctx, axis=-1, keepdims=True)
        o = _dot(e_loc.astype(BF16), v_ref[win, cols]) + _dot(e_ctx.astype(BF16), cv_ref[0, :, cols])
        o = o / den
        o_ref[:, cols] = jnp.where(head_a, o[:GRID_W], o[GRID_W:]).astype(BF16)


def _bias_expand_kernel(rpb_ref, onehot_ref, outside_ref, o_ref):
    x = rpb_ref[...]
    hi = x.astype(BF16)
    r1 = x - hi.astype(F32)
    mid = r1.astype(BF16)
    lo = (r1 - mid.astype(F32)).astype(BF16)
    oh = onehot_ref[...]
    o_ref[...] = _dot(hi, oh) + _dot(mid, oh) + _dot(lo, oh) + outside_ref[...]


def _na_bias_tables(na_rpb, rows_total):
    depth, heads, n_dr, n_dc = na_rpb.shape
    kh = NA_KH
    qc = np.arange(GRID_W)
    kc = np.arange(GRID_W)
    cs = np.clip(qc - NA_KW // 2, 0, GRID_W - NA_KW)
    ok = (kc[None, :] >= cs[:, None]) & (kc[None, :] < cs[:, None] + NA_KW)
    dc = np.clip(kc[None, :] - qc[:, None] + NA_KW - 1, 0, 2 * NA_KW - 2)
    n_dc_pad = 32
    onehot = (np.arange(n_dc_pad)[:, None, None] == dc[None]) & ok[None]
    onehot = jnp.asarray(onehot.reshape(n_dc_pad, GRID_W * GRID_W), BF16)
    outside = jnp.asarray(np.where(ok, 0.0, NEG_INF).reshape(1, GRID_W * GRID_W), F32)
    n_rows = depth * heads * n_dr
    n_rows_pad = -(-n_rows // 8) * 8
    rows = jnp.pad(na_rpb.reshape(n_rows, n_dc), ((0, n_rows_pad - n_rows), (0, n_dc_pad - n_dc)))
    t = pl.pallas_call(
        _bias_expand_kernel,
        out_shape=jax.ShapeDtypeStruct((n_rows_pad, GRID_W * GRID_W), F32),
        name="na_bias_expand",
    )(rows, onehot, outside)
    t = t[:n_rows].reshape(depth, heads, n_dr, GRID_W, GRID_W)

    lo, hi = kh // 2, rows_total - kh // 2 - 1
    reps = list(range(lo)) + [lo] + list(range(hi + 1, rows_total))
    blocks = []
    for r in reps:
        dr0 = int(np.clip(r - kh // 2, 0, rows_total - kh)) - r + kh - 1
        b = t[:, :, dr0:dr0 + kh].transpose(0, 1, 3, 2, 4)
        blocks.append(b.reshape(depth, HEAD_PAIRS, 2 * GRID_W, kh * GRID_W))
    return jnp.stack(blocks, axis=1), lo, hi


def _neighbourhood_attention(p, ctx_k, ctx_v, bias, lo, hi, batch, seq):
    rows_total = seq // GRID_W
    past = ctx_k.shape[1]

    def cls(r):
        return jnp.where(r < lo, r, jnp.where(r > hi, r - hi + lo, lo))

    return pl.pallas_call(
        _na_kernel,
        grid=(batch, rows_total),
        in_specs=[
            pl.BlockSpec((GRID_W, WIDTH), lambda b, r: (b * rows_total + r, 4)),
            pl.BlockSpec((seq, WIDTH), lambda b, r: (b, 5)),
            pl.BlockSpec((seq, WIDTH), lambda b, r: (b, 6)),
            pl.BlockSpec((1, past, WIDTH), lambda b, r: (b, 0, 0)),
            pl.BlockSpec((1, past, WIDTH), lambda b, r: (b, 0, 0)),
            pl.BlockSpec((1, HEAD_PAIRS, 2 * GRID_W, NA_KH * GRID_W),
                         lambda b, r: (cls(r), 0, 0, 0)),
        ],
        out_specs=pl.BlockSpec((GRID_W, WIDTH), lambda b, r: (b * rows_total + r, 0)),
        out_shape=jax.ShapeDtypeStruct((batch * seq, WIDTH), BF16),
        compiler_params=_cparams(2),
        name="neighbourhood_attention",
    )(p, p, p, ctx_k, ctx_v, bias)


def _post_kernel(xm, xp, xn, am, ap, an, bm, bp, bn, mod_ref, g_ref, wo, wu, cw, cb, wd, o_ref,
                 acc_ref, *, seq_len):
    tm = xm.shape[0]
    mod = mod_ref[0]
    g = g_ref[...]

    def ext(main, prev, nxt):
        lo = prev[...].astype(F32)[prev.shape[0] - HALO:]
        hi = nxt[...].astype(F32)[:HALO]
        return jnp.concatenate([lo, main[...].astype(F32), hi], axis=0)

    x = ext(xm, xp, xn)
    a = ext(am, ap, an).astype(BF16)
    b = ext(bm, bp, bn).astype(BF16)
    half = wo.shape[0] // 2
    y = _dot(a, wo[:half, :]) + _dot(b, wo[half:, :])
    x1 = x + mod[2:3] * _rms(y, g[0:1])
    h = (_rms(x1, g[1:2]) * (1.0 + mod[4:5]) + mod[3:4]).astype(BF16)

    row = pl.program_id(0) * tm + lax.broadcasted_iota(jnp.int32, (tm, FF_CHUNK), 0)
    pos = row & (seq_len - 1)
    has_prev = pos != 0
    has_next = pos != seq_len - 1

    def conv(u, cols):
        w = cw[:, cols]
        prev = jnp.where(has_prev, u[HALO - 1:HALO - 1 + tm], 0.0)
        nxt = jnp.where(has_next, u[HALO + 1:HALO + 1 + tm], 0.0)
        return prev * w[0:1] + u[HALO:HALO + tm] * w[1:2] + nxt * w[2:3] + cb[:, cols]

    for ch in range(D_FF // FF_CHUNK):
        ca = slice(ch * FF_CHUNK, (ch + 1) * FF_CHUNK)
        cg = slice(D_FF + ch * FF_CHUNK, D_FF + (ch + 1) * FF_CHUNK)
        act = (_silu(conv(_dot(h, wu[:, cg]), cg)) * conv(_dot(h, wu[:, ca]), ca)).astype(BF16)
        part = _dot(act, wd[ca, :])
        if ch == 0:
            acc_ref[...] = part
        else:
            acc_ref[...] += part
    o_ref[...] = x1[HALO:HALO + tm] + mod[5:6] * _rms(acc_ref[...], g[2:3])


def _post(x, mix_a, mix_b, mod, gains, wo, wu, cw, cb, wd, seq_len, tm, latent):
    n = x.shape[0]
    nt = n // tm
    assert seq_len & (seq_len - 1) == 0
    (mix_a, col_a), (mix_b, col_b) = mix_a, mix_b
    tiles_per_seq = max(seq_len // tm, 1)
    cond_row = (lambda i: 1 + i // tiles_per_seq) if latent else (lambda i: 0)

    def triple(width, halo_rows, col=0):
        per = tm // halo_rows
        last = n // halo_rows - 1
        return [
            pl.BlockSpec((tm, width), lambda i: (i, col)),
            pl.BlockSpec((halo_rows, width), lambda i: (jnp.maximum(i * per - 1, 0), col)),
            pl.BlockSpec((halo_rows, width), lambda i: (jnp.minimum((i + 1) * per, last), col)),
        ]

    const = lambda shape: pl.BlockSpec(shape, lambda i: (0,) * len(shape),
                                       pipeline_mode=pl.Buffered(1))
    in_specs = (triple(D_MODEL, HALO) + triple(WIDTH, 2 * HALO, col_a)
                + triple(WIDTH, 2 * HALO, col_b) + [
        pl.BlockSpec((1, 6, D_MODEL), lambda i: (cond_row(i), 0, 0)),
        const((3, D_MODEL)),
        const((2 * WIDTH, D_MODEL)),
        const((D_MODEL, 2 * D_FF)),
        const((3, 2 * D_FF)),
        const((1, 2 * D_FF)),
        const((D_FF, D_MODEL)),
    ])
    return pl.pallas_call(
        functools.partial(_post_kernel, seq_len=seq_len),
        grid=(nt,),
        in_specs=in_specs,
        out_specs=pl.BlockSpec((tm, D_MODEL), lambda i: (i, 0)),
        out_shape=jax.ShapeDtypeStruct((n, D_MODEL), F32),
        scratch_shapes=[pltpu.VMEM((tm, D_MODEL), F32)],
        compiler_params=_cparams(1),
        name="post_latent" if latent else "post_context",
    )(x, x, x, mix_a, mix_a, mix_a, mix_b, mix_b, mix_b, mod, gains, wo, wu, cw, cb, wd)


def kernel(x_prompt, x_sample, c, cache_na_k, cache_na_v, state_ret_fwd, state_ret_bwd, c_ctx,
           ada_w, ada_b, g_pre_mix, g_post_mix, g_pre_ffn, g_post_ffn, w_in,
           ret_decay_fwd, ret_decay_bwd, na_rpb, w_out, w_up, conv_w, conv_b, w_down):
    depth = w_in.shape[0]
    batch, seq, _ = x_prompt.shape
    dec_batch, dec_seq, _ = x_sample.shape
    past = cache_na_k.shape[2]
    tm = 512

    cond = jnp.concatenate(
        [c_ctx[None, :], c, jnp.zeros((8 - 1 - dec_batch, D_MODEL), F32)], axis=0)
    mods = _modulation(cond, ada_w, ada_b).reshape(depth, 8, 6, D_MODEL)
    rope_tabs = _rope_tables(dec_seq)
    bias, lo, hi = _na_bias_tables(na_rpb, dec_seq // GRID_W)

    y_p = x_prompt.reshape(batch * seq, D_MODEL)
    y_s = x_sample.reshape(dec_batch * dec_seq, D_MODEL)
    ks, vs, sfs, sbs = [], [], [], []
    for l in range(depth):
        w_in_l = w_in[l].astype(BF16)
        wo_l = w_out[l].astype(BF16)
        wu_l = w_up[l].astype(BF16)
        wd_l = w_down[l].astype(BF16)
        gains = jnp.stack([g_post_mix[l], g_pre_ffn[l], g_post_ffn[l]], axis=0)
        cb_l = conv_b[l].reshape(1, 2 * D_FF)
        dec = (jnp.repeat(ret_decay_fwd[l], HEAD_DIM)[None, :],
               jnp.repeat(ret_decay_bwd[l], HEAD_DIM)[None, :],
               jnp.repeat(ret_decay_fwd[l], LANES)[None, :],
               jnp.repeat(ret_decay_bwd[l], LANES)[None, :])

        p_c, kv_c = _inproj(y_p, mods[l], g_pre_mix[l], w_in_l, seq, tm, emit_kv=True)
        mix_c, st_c = _ctx_mixer(p_c, dec, batch, seq)
        y_p = _post(y_p, (mix_c, 0), (mix_c, 1), mods[l], gains, wo_l, wu_l,
                    conv_w[l], cb_l, wd_l, seq, tm, latent=False)
        ks.append(kv_c[:, :WIDTH].reshape(batch, seq, N_HEADS, HEAD_DIM))
        vs.append(kv_c[:, WIDTH:].reshape(batch, seq, N_HEADS, HEAD_DIM))
        sfs.append(_diag_states(st_c[:, :, 0]))
        sbs.append(_diag_states(st_c[:, :, 1]))

        (p_s,) = _inproj(y_s, mods[l], g_pre_mix[l], w_in_l, dec_seq, tm, rope_tabs=rope_tabs)
        s0 = _block_diag_states(state_ret_fwd[:, l], state_ret_bwd[:, l])
        ret_s = _lat_retention(p_s, dec, s0, dec_batch, dec_seq)
        ck = cache_na_k[:, l].reshape(dec_batch, past, WIDTH).astype(BF16)
        cv = cache_na_v[:, l].reshape(dec_batch, past, WIDTH).astype(BF16)
        na_s = _neighbourhood_attention(p_s, ck, cv, bias[l], lo, hi, dec_batch, dec_seq)
        y_s = _post(y_s, (ret_s, 0), (na_s, 0), mods[l], gains, wo_l, wu_l, conv_w[l], cb_l, wd_l,
                    dec_seq, tm, latent=True)

    return (y_p.reshape(batch, seq, D_MODEL),
            y_s.reshape(dec_batch, dec_seq, D_MODEL),
            jnp.stack(ks, axis=1), jnp.stack(vs, axis=1),
            jnp.stack(sfs, axis=1), jnp.stack(sbs, axis=1))
```

```python
import functools

import numpy as np
import jax
import jax.numpy as jnp
from jax import lax
from jax.experimental import pallas as pl
from jax.experimental.pallas import tpu as pltpu

F32 = jnp.float32
BF16 = jnp.bfloat16

D_MODEL = 1024
HEAD_DIM = 64
N_HEADS = 8
HEAD_PAIRS = N_HEADS // 2
LANES = 128
WIDTH = N_HEADS * HEAD_DIM
N_GROUPS = 7
IN_WIDTH = N_GROUPS * WIDTH
D_FF = 2816
FF_CHUNK = 256
CHUNK = 128
RET_UNROLL = 4
GRID_W = 64
NA_KH = 8
NA_KW = 16
ROPE_BASE = 10000.0
EPS = 1e-6
NEG_INF = -1e9
HALO = 8
VMEM_LIMIT = 56 * 1024 * 1024


def _cparams(n_grid):
    return pltpu.CompilerParams(
        dimension_semantics=("arbitrary",) * n_grid, vmem_limit_bytes=VMEM_LIMIT)


def _rms(x, g):
    ms = jnp.mean(x * x, axis=-1, keepdims=True)
    return x * lax.rsqrt(ms + EPS) * g


def _silu(x):
    return x * jax.nn.sigmoid(x)


def _log_sigmoid(x):
    return jnp.minimum(x, 0.0) - jnp.log1p(jnp.exp(-jnp.abs(x)))


def _dot(a, b):
    return jnp.dot(a, b, preferred_element_type=F32)


def _dot_nt(a, b):
    return lax.dot_general(a, b, (((1,), (1,)), ((), ())), preferred_element_type=F32)


def _dot_tn(a, b):
    return lax.dot_general(a, b, (((0,), (0,)), ((), ())), preferred_element_type=F32)


def _mod_kernel(cond_ref, w_ref, b_ref, o_ref):
    s = _silu(cond_ref[...]).astype(BF16)
    o_ref[0] = _dot(s, w_ref[0].astype(BF16)) + b_ref[0]


def _modulation(cond, ada_w, ada_b):
    depth = ada_w.shape[0]
    nb = 6 * D_MODEL // D_MODEL
    return pl.pallas_call(
        _mod_kernel,
        grid=(depth, nb),
        in_specs=[
            pl.BlockSpec((8, D_MODEL), lambda l, j: (0, 0)),
            pl.BlockSpec((1, D_MODEL, D_MODEL), lambda l, j: (l, 0, j)),
            pl.BlockSpec((1, 1, D_MODEL), lambda l, j: (l, 0, j)),
        ],
        out_specs=pl.BlockSpec((1, 8, D_MODEL), lambda l, j: (l, 0, j)),
        out_shape=jax.ShapeDtypeStruct((depth, 8, 6 * D_MODEL), F32),
        compiler_params=_cparams(2),
        name="modulation",
    )(cond, ada_w, ada_b.reshape(depth, 1, 6 * D_MODEL))


def _inproj_kernel(*refs, rope, emit_kv, n_alias):
    x_ref, mod_ref, g_ref, w_ref = refs[:4]
    refs = refs[4:]
    if rope:
        cos_ref, sin_up_ref, sin_dn_ref = refs[:3]
        refs = refs[3:]
    refs = refs[n_alias:]
    p_ref = refs[0]
    mod = mod_ref[0]
    h = (_rms(x_ref[...], g_ref[...]) * (1.0 + mod[1:2]) + mod[0:1]).astype(BF16)
    for g in range(N_GROUPS):
        cols = slice(g * WIDTH, (g + 1) * WIDTH)
        pg = _dot(h, w_ref[:, cols])
        if rope and g < 2:
            parts = []
            for j in range(WIDTH // LANES):
                xj = pg[:, j * LANES:(j + 1) * LANES]
                parts.append(xj * cos_ref[...]
                             + pltpu.roll(xj, 16, 1) * sin_up_ref[...]
                             + pltpu.roll(xj, LANES - 16, 1) * sin_dn_ref[...])
            pg = jnp.concatenate(parts, axis=1)
        if g in (0, 4):
            pg = pg * (HEAD_DIM ** -0.5)
        p_ref[:, cols] = pg.astype(BF16)
        if emit_kv and g >= 5:
            seq = refs[g - 4].shape[1]
            for j in range(refs[g - 4].shape[0]):
                refs[g - 4][j] = pg[j * seq:(j + 1) * seq]


def _inproj(x, mods, g_pre, w, layer, seq_len, tm, rope_tabs=None, kv_out=None):
    n = x.shape[0]
    depth = w.shape[0]
    tiles_per_seq = max(seq_len // tm, 1)
    cond_row = (lambda i: 0) if rope_tabs is None else (lambda i: 1 + i // tiles_per_seq)
    in_specs = [
        pl.BlockSpec((tm, D_MODEL), lambda i: (i, 0)),
        pl.BlockSpec((None, 1, 6, D_MODEL), lambda i: (layer, cond_row(i), 0, 0)),
        pl.BlockSpec((None, 1, D_MODEL), lambda i: (layer, 0, 0)),
        pl.BlockSpec((None, D_MODEL, IN_WIDTH), lambda i: (layer, 0, 0),
                     pipeline_mode=pl.Buffered(1)),
    ]
    args = [x, mods, g_pre, w]
    if rope_tabs is not None:
        in_specs += [pl.BlockSpec((tm, LANES), lambda i: (i % tiles_per_seq, 0))] * 3
        args += list(rope_tabs)
    out_specs = [pl.BlockSpec((tm, IN_WIDTH), lambda i: (i, 0))]
    out_shape = [jax.ShapeDtypeStruct((n, IN_WIDTH), BF16)]
    aliases = {}
    if kv_out is not None:
        seqs_per_tile = tm // seq_len
        kv_shape = jax.ShapeDtypeStruct((n // seq_len, depth, seq_len, WIDTH), F32)
        for j, prev in enumerate(kv_out):
            if prev is not None:
                aliases[len(args)] = 1 + j
                in_specs.append(pl.BlockSpec(memory_space=pl.ANY))
                args.append(prev)
            out_specs.append(pl.BlockSpec((seqs_per_tile, None, seq_len, WIDTH),
                                          lambda i: (i, layer, 0, 0)))
            out_shape.append(kv_shape)
    return pl.pallas_call(
        functools.partial(_inproj_kernel, rope=rope_tabs is not None,
                          emit_kv=kv_out is not None, n_alias=len(aliases)),
        grid=(n // tm,),
        in_specs=in_specs,
        out_specs=out_specs,
        out_shape=out_shape,
        input_output_aliases=aliases,
        compiler_params=_cparams(1),
        name="inproj_latent" if rope_tabs is not None else "inproj_context",
    )(*args)


def _rope_tables(seq_len):
    t = np.arange(seq_len)
    lane = np.arange(LANES)
    d = lane % HEAD_DIM
    pos = np.where(d[None, :] < HEAD_DIM // 2, (t // GRID_W)[:, None], (t % GRID_W)[:, None])
    pos = pos.astype(np.float32)
    half = HEAD_DIM // 2
    inv = np.power(np.float32(ROPE_BASE), -np.arange(0, half, 2, dtype=np.float32) / half)
    ang = pos * inv[d % (half // 2)][None, :]
    cos, sin = np.cos(ang), np.sin(ang)
    upper = (d % half) >= half // 2
    sin_up = np.where(upper[None, :], sin, 0.0)
    sin_dn = np.where(upper[None, :], 0.0, -sin)
    return (jnp.asarray(cos, F32), jnp.asarray(sin_up, F32), jnp.asarray(sin_dn, F32))


def _retention_tables(dec_f, dec_b, dec_f2, dec_b2):
    lgf, lgb = _log_sigmoid(dec_f), _log_sigmoid(dec_b)
    pos = lax.broadcasted_iota(jnp.int32, (CHUNK, LANES), 0).astype(F32)
    tabs = dict(
        qdf=jnp.exp(lgf * (pos + 1.0)), kdf=jnp.exp(lgf * (CHUNK - 1.0 - pos)),
        cdf=jnp.exp(lgf * float(CHUNK)),
        qdb=jnp.exp(lgb * (CHUNK - pos)), kdb=jnp.exp(lgb * pos),
        cdb=jnp.exp(lgb * float(CHUNK)),
    )
    lgf2, lgb2 = _log_sigmoid(dec_f2), _log_sigmoid(dec_b2)
    i = lax.broadcasted_iota(jnp.int32, (CHUNK, 2 * CHUNK), 0)
    j = lax.broadcasted_iota(jnp.int32, (CHUNK, 2 * CHUNK), 1) & (CHUNK - 1)
    diff = (i - j).astype(F32)
    tabs["decay"] = (jnp.where(diff >= 0, jnp.exp(lgf2 * jnp.maximum(diff, 0.0)), 0.0)
                     + jnp.where(diff <= 0, jnp.exp(lgb2 * jnp.maximum(-diff, 0.0)), 0.0))
    lane = lax.broadcasted_iota(jnp.int32, (1, LANES), 1)
    tabs["head_a"] = lane < HEAD_DIM
    r = lax.broadcasted_iota(jnp.int32, (2 * LANES, LANES), 0) & (LANES - 1)
    c = lax.broadcasted_iota(jnp.int32, (2 * LANES, LANES), 1)
    tabs["same_head"] = (r < HEAD_DIM) == (c < HEAD_DIM)
    return tabs


def _split_heads(x, head_a, axis):
    zero = jnp.zeros_like(x)
    return jnp.concatenate([jnp.where(head_a, x, zero), jnp.where(head_a, zero, x)], axis=axis)


def _chunk_kv(k2, v2, tabs):
    kf = k2.astype(F32)
    kk = jnp.concatenate([kf * tabs["kdf"], kf * tabs["kdb"]], axis=1).astype(BF16)
    return jnp.where(tabs["same_head"], _dot_tn(kk, v2), 0.0)


def _chunk_out(q2, k2, v2, g2, state, tabs):
    head_a = tabs["head_a"]
    s = _dot_nt(q2, _split_heads(k2, head_a, 0))
    p = (s * tabs["decay"]).astype(BF16)
    qf = q2.astype(F32)
    lhs = jnp.concatenate(
        [p, (qf * tabs["qdf"]).astype(BF16), (qf * tabs["qdb"]).astype(BF16)], axis=1)
    rhs = jnp.concatenate([_split_heads(v2, head_a, 0), state], axis=0)
    o = _dot(lhs, rhs)
    inv = 1.0 / HEAD_DIM
    sum_a = jnp.sum(jnp.where(head_a, o, 0.0), axis=-1, keepdims=True)
    sum_b = jnp.sum(jnp.where(head_a, 0.0, o), axis=-1, keepdims=True)
    d = o - jnp.where(head_a, sum_a, sum_b) * inv
    d2 = d * d
    var_a = jnp.sum(jnp.where(head_a, d2, 0.0), axis=-1, keepdims=True)
    var_b = jnp.sum(jnp.where(head_a, 0.0, d2), axis=-1, keepdims=True)
    o = d * lax.rsqrt(jnp.where(head_a, var_a, var_b) * inv + EPS)
    return o * _silu(g2.astype(F32))


def _ctx_mixer_kernel(rq, rk, rv, rg, nq, nk, nv, df, db, df2, db2, mix_ref, st_ref):
    seq = rq.shape[0]
    nc = seq // CHUNK
    for hp in range(HEAD_PAIRS):
        cols = slice(hp * LANES, (hp + 1) * LANES)
        cols2 = slice(hp * 2 * LANES, (hp + 1) * 2 * LANES)
        tabs = _retention_tables(df[:, cols], db[:, cols], df2[:, cols2], db2[:, cols2])
        rows = [slice(c * CHUNK, (c + 1) * CHUNK) for c in range(nc)]
        kv = [_chunk_kv(rk[r, cols], rv[r, cols], tabs) for r in rows]
        sf = [jnp.zeros((LANES, LANES), F32)]
        for c in range(nc):
            sf.append(sf[-1] * tabs["cdf"] + kv[c][:LANES])
        sb = [jnp.zeros((LANES, LANES), F32)]
        for c in reversed(range(nc)):
            sb.append(sb[-1] * tabs["cdb"] + kv[c][LANES:])
        for c in range(nc):
            state = jnp.concatenate([sf[c], sb[nc - 1 - c]], axis=0).astype(BF16)
            o = _chunk_out(rq[rows[c], cols], rk[rows[c], cols], rv[rows[c], cols],
                           rg[rows[c], cols], state, tabs)
            mix_ref[rows[c], cols] = o.astype(BF16)
        st_ref[0, hp, 0] = sf[nc]
        st_ref[0, hp, 1] = sb[nc]

        head_a = tabs["head_a"]
        s = _dot_nt(nq[:, cols], _split_heads(nk[:, cols], head_a, 0))
        es, rinv = [], []
        for h in range(2):
            sh = s[:, h * seq:(h + 1) * seq]
            e = jnp.exp(sh - jnp.max(sh, axis=-1, keepdims=True))
            rinv.append(1.0 / jnp.sum(e, axis=-1, keepdims=True))
            es.append(e.astype(BF16))
        o = _dot(jnp.concatenate(es, axis=1), _split_heads(nv[:, cols], head_a, 0))
        o = o * jnp.where(head_a, rinv[0], rinv[1])
        mix_ref[:, WIDTH + hp * LANES:WIDTH + (hp + 1) * LANES] = o.astype(BF16)


def _ctx_mixer(p, dec, batch, seq):
    group = lambda g: pl.BlockSpec((seq, WIDTH), lambda b, g=g: (b, g))
    vec = lambda w: pl.BlockSpec((1, w), lambda b: (0, 0))
    return pl.pallas_call(
        _ctx_mixer_kernel,
        grid=(batch,),
        in_specs=[group(g) for g in range(N_GROUPS)] + [vec(WIDTH), vec(WIDTH),
                                                         vec(2 * WIDTH), vec(2 * WIDTH)],
        out_specs=[
            pl.BlockSpec((seq, 2 * WIDTH), lambda b: (b, 0)),
            pl.BlockSpec((1, HEAD_PAIRS, 2, LANES, LANES), lambda b: (b, 0, 0, 0, 0)),
        ],
        out_shape=[
            jax.ShapeDtypeStruct((batch * seq, 2 * WIDTH), BF16),
            jax.ShapeDtypeStruct((batch, HEAD_PAIRS, 2, LANES, LANES), F32),
        ],
        compiler_params=_cparams(1),
        name="context_mixer",
    )(*([p] * N_GROUPS), *dec)


def _lat_retention_kernel(q_ref, k_ref, v_ref, g_ref, df, db, df2, db2, s0_ref, o_ref,
                          kv_ref, st_ref):
    nc = q_ref.shape[0] // CHUNK
    tabs = _retention_tables(df[...], db[...], df2[...], db2[...])

    def rows(c):
        return pl.ds(pl.multiple_of(c * CHUNK, CHUNK), CHUNK)

    def kv_body(c, carry):
        kv_ref[c] = _chunk_kv(k_ref[rows(c), :], v_ref[rows(c), :], tabs)
        return carry
    lax.fori_loop(0, nc, kv_body, 0, unroll=RET_UNROLL)

    def fwd_body(c, s):
        st_ref[c, :LANES, :] = s.astype(BF16)
        return s * tabs["cdf"] + kv_ref[c, :LANES, :]
    lax.fori_loop(0, nc, fwd_body, s0_ref[0, 0, 0])

    def bwd_body(i, s):
        c = nc - 1 - i
        st_ref[c, LANES:, :] = s.astype(BF16)
        return s * tabs["cdb"] + kv_ref[c, LANES:, :]
    lax.fori_loop(0, nc, bwd_body, s0_ref[0, 0, 1])

    def out_body(c, carry):
        r = rows(c)
        o = _chunk_out(q_ref[r, :], k_ref[r, :], v_ref[r, :], g_ref[r, :], st_ref[c], tabs)
        o_ref[r, :] = o.astype(BF16)
        return carry
    lax.fori_loop(0, nc, out_body, 0, unroll=RET_UNROLL)


def _lat_retention(p, dec, s0, batch, seq):
    nc = seq // CHUNK
    group = lambda g: pl.BlockSpec((seq, LANES), lambda b, h, g=g: (b, g * HEAD_PAIRS + h))
    vec = lambda w: pl.BlockSpec((1, w), lambda b, h: (0, h))
    return pl.pallas_call(
        _lat_retention_kernel,
        grid=(batch, HEAD_PAIRS),
        in_specs=[group(g) for g in range(4)] + [vec(LANES), vec(LANES), vec(2 * LANES),
                                                 vec(2 * LANES)]
        + [pl.BlockSpec((1, 1, 2, LANES, LANES), lambda b, h: (b, h, 0, 0, 0))],
        out_specs=pl.BlockSpec((seq, LANES), lambda b, h: (b, h)),
        out_shape=jax.ShapeDtypeStruct((batch * seq, WIDTH), BF16),
        scratch_shapes=[pltpu.VMEM((nc, 2 * LANES, LANES), F32),
                        pltpu.VMEM((nc, 2 * LANES, LANES), BF16)],
        compiler_params=_cparams(2),
        name="latent_retention",
    )(*([p] * 4), *dec, s0)


def _block_diag_states(s_f, s_b):
    def bd(s):
        b = s.shape[0]
        s = s.reshape(b, HEAD_PAIRS, 2, HEAD_DIM, HEAD_DIM)
        z = jnp.zeros_like(s[:, :, 0])
        top = jnp.concatenate([s[:, :, 0], z], axis=-1)
        bot = jnp.concatenate([z, s[:, :, 1]], axis=-1)
        return jnp.concatenate([top, bot], axis=-2)
    return jnp.stack([bd(s_f), bd(s_b)], axis=2)


def _diag_states(st):
    b = st.shape[0]
    a = st[:, :, :HEAD_DIM, :HEAD_DIM]
    c = st[:, :, HEAD_DIM:, HEAD_DIM:]
    return jnp.stack([a, c], axis=2).reshape(b, N_HEADS, HEAD_DIM, HEAD_DIM)


def _na_kernel(q_ref, k_ref, v_ref, ck_ref, cv_ref, bias_ref, o_ref):
    rows_total = k_ref.shape[0] // GRID_W
    r = pl.program_id(1)
    rs = jnp.clip(r - NA_KH // 2, 0, rows_total - NA_KH)
    win = pl.ds(pl.multiple_of(rs * GRID_W, GRID_W), NA_KH * GRID_W)
    lane = lax.broadcasted_iota(jnp.int32, (1, LANES), 1)
    head_a = lane < HEAD_DIM
    for hp in range(HEAD_PAIRS):
        cols = slice(hp * LANES, (hp + 1) * LANES)
        qq = _split_heads(q_ref[:, cols], head_a, 0)
        s_loc = _dot_nt(qq, k_ref[win, cols]) + bias_ref[0, hp]
        s_ctx = _dot_nt(qq, ck_ref[0, :, cols])
        m = jnp.maximum(jnp.max(s_loc, axis=-1, keepdims=True),
                        jnp.max(s_ctx, axis=-1, keepdims=True))
        e_loc = jnp.exp(s_loc - m)
        e_ctx = jnp.exp(s_ctx - m)
        den = jnp.sum(e_loc, axis=-1, keepdims=True) + jnp.sum(e_ctx, axis=-1, keepdims=True)
        o = _dot(e_loc.astype(BF16), v_ref[win, cols]) + _dot(e_ctx.astype(BF16), cv_ref[0, :, cols])
        o = o / den
        o_ref[:, cols] = jnp.where(head_a, o[:GRID_W], o[GRID_W:]).astype(BF16)


def _bias_build_kernel(rpb_ref, o_ref, *, dr_first, n_dr):
    qc = lax.broadcasted_iota(jnp.int32, (GRID_W, LANES), 0)
    lane = lax.broadcasted_iota(jnp.int32, (GRID_W, LANES), 1)
    kc = lane & (GRID_W - 1)
    cs = jnp.clip(qc - NA_KW // 2, 0, GRID_W - NA_KW)
    inside = (kc >= cs) & (kc < cs + NA_KW)
    first = lane < GRID_W
    for h in range(N_HEADS):
        lo_half, hi_half = [], []
        for dr in range(n_dr):
            line = jnp.broadcast_to(rpb_ref[pl.ds(h * n_dr + dr, 1), :], (GRID_W, LANES))
            lo_half.append(pltpu.roll(line, LANES - (NA_KW - 1), 1, stride=1, stride_axis=0))
            hi_half.append(pltpu.roll(line, GRID_W - (NA_KW - 1), 1, stride=1, stride_axis=0))
        for cls, dr0 in enumerate(dr_first):
            for jp in range(NA_KH // 2):
                dr = dr0 + 2 * jp
                tile = jnp.where(inside, jnp.where(first, lo_half[dr], hi_half[dr + 1]), NEG_INF)
                o_ref[cls, h // 2, (h % 2) * GRID_W:(h % 2 + 1) * GRID_W,
                      jp * LANES:(jp + 1) * LANES] = tile


def _na_bias_tables(na_rpb, rows_total):
    depth, heads, n_dr, n_dc = na_rpb.shape
    kh = NA_KH
    lo, hi = kh // 2, rows_total - kh // 2 - 1
    reps = list(range(lo)) + [lo] + list(range(hi + 1, rows_total))
    dr_first = tuple(int(np.clip(r - kh // 2, 0, rows_total - kh)) - r + kh - 1 for r in reps)
    lines = jnp.pad(na_rpb.reshape(depth, heads * n_dr, n_dc), ((0, 0), (0, 0), (0, LANES - n_dc)))
    bias = pl.pallas_call(
        functools.partial(_bias_build_kernel, dr_first=dr_first, n_dr=n_dr),
        grid=(depth,),
        in_specs=[pl.BlockSpec((None, heads * n_dr, LANES), lambda l: (l, 0, 0))],
        out_specs=pl.BlockSpec((None, len(reps), HEAD_PAIRS, 2 * GRID_W, kh * GRID_W),
                               lambda l: (l, 0, 0, 0, 0)),
        out_shape=jax.ShapeDtypeStruct(
            (depth, len(reps), HEAD_PAIRS, 2 * GRID_W, kh * GRID_W), F32),
        compiler_params=_cparams(1),
        name="na_bias_build",
    )(lines)
    return bias, lo, hi


def _neighbourhood_attention(p, ctx_k, ctx_v, bias, layer, lo, hi, batch, seq):
    rows_total = seq // GRID_W
    past = ctx_k.shape[1]

    def cls(r):
        return jnp.where(r < lo, r, jnp.where(r > hi, r - hi + lo, lo))

    return pl.pallas_call(
        _na_kernel,
        grid=(batch, rows_total),
        in_specs=[
            pl.BlockSpec((GRID_W, WIDTH), lambda b, r: (b * rows_total + r, 4)),
            pl.BlockSpec((seq, WIDTH), lambda b, r: (b, 5)),
            pl.BlockSpec((seq, WIDTH), lambda b, r: (b, 6)),
            pl.BlockSpec((1, past, WIDTH), lambda b, r: (b, 0, 0)),
            pl.BlockSpec((1, past, WIDTH), lambda b, r: (b, 0, 0)),
            pl.BlockSpec((None, 1, HEAD_PAIRS, 2 * GRID_W, NA_KH * GRID_W),
                         lambda b, r: (layer, cls(r), 0, 0, 0)),
        ],
        out_specs=pl.BlockSpec((GRID_W, WIDTH), lambda b, r: (b * rows_total + r, 0)),
        out_shape=jax.ShapeDtypeStruct((batch * seq, WIDTH), BF16),
        compiler_params=_cparams(2),
        name="neighbourhood_attention",
    )(p, p, p, ctx_k, ctx_v, bias)


def _post_kernel(xm, xp, xn, am, ap, an, bm, bp, bn, mod_ref, g_ref, wo, wu, cw, cb, wd, o_ref,
                 acc_ref, h_ref, *, tiles_per_seq):
    tm = xm.shape[0]
    nj = tm // HALO
    mod = mod_ref[0]
    g = g_ref[...]

    def ext(main, prev, nxt):
        lo = prev[...].astype(F32)[prev.shape[0] - HALO:]
        hi = nxt[...].astype(F32)[:HALO]
        return jnp.concatenate([lo, main[...].astype(F32), hi], axis=0)

    x = ext(xm, xp, xn)
    a = ext(am, ap, an).astype(BF16)
    b = ext(bm, bp, bn).astype(BF16)
    half = wo.shape[0] // 2
    y = _dot(a, wo[:half, :]) + _dot(b, wo[half:, :])
    x1 = x + mod[2:3] * _rms(y, g[0:1])
    h = _rms(x1, g[1:2]) * (1.0 + mod[4:5]) + mod[3:4]

    t = pl.program_id(0) % tiles_per_seq
    ridx = lax.broadcasted_iota(jnp.int32, h.shape, 0)
    keep = (((ridx >= HALO) | (t != 0)) & ((ridx < HALO + tm) | (t != tiles_per_seq - 1)))
    h = jnp.where(keep, h, 0.0)
    n_ct = D_MODEL // LANES
    for c in range(n_ct):
        h_ref[c] = h[:, c * LANES:(c + 1) * LANES]

    def group(rows):
        return jnp.concatenate([h_ref[c, rows, :] for c in range(n_ct)], axis=1)
    groups = [group(pl.ds(HALO + j, HALO, stride=nj)) for j in range(nj)]
    groups += [group(pl.ds(0, HALO)), group(pl.ds(HALO + tm, HALO))]
    hp = jnp.concatenate(groups, axis=0).astype(BF16)
    sub = lax.broadcasted_iota(jnp.int32, (HALO, FF_CHUNK), 0)

    def conv(u, cols):
        w = cw[:, cols]
        before = jnp.where(sub == 0, u[tm + HALO - 1:tm + HALO], pltpu.roll(u[tm - HALO:tm], 1, 0))
        after = jnp.where(sub == HALO - 1, u[tm + HALO:tm + HALO + 1], pltpu.roll(u[0:HALO], HALO - 1, 0))
        prev = jnp.concatenate([before, u[0:tm - HALO]], axis=0)
        nxt = jnp.concatenate([u[HALO:tm], after], axis=0)
        return prev * w[0:1] + u[0:tm] * w[1:2] + nxt * w[2:3] + cb[:, cols]

    n_chunks = D_FF // FF_CHUNK

    def up(ch):
        ca = slice(ch * FF_CHUNK, (ch + 1) * FF_CHUNK)
        cg = slice(D_FF + ch * FF_CHUNK, D_FF + (ch + 1) * FF_CHUNK)
        return _dot(hp, wu[:, cg]), _dot(hp, wu[:, ca])

    nxt_u = up(0)
    for ch in range(n_chunks):
        ug, ua = nxt_u
        if ch + 1 < n_chunks:
            nxt_u = up(ch + 1)
        ca = slice(ch * FF_CHUNK, (ch + 1) * FF_CHUNK)
        cg = slice(D_FF + ch * FF_CHUNK, D_FF + (ch + 1) * FF_CHUNK)
        part = _dot((_silu(conv(ug, cg)) * conv(ua, ca)).astype(BF16), wd[ca, :])
        if ch == 0:
            acc = part
        elif ch + 1 < n_chunks:
            acc = acc + part
        else:
            ffn = mod[5:6] * _rms(acc + part, g[2:3])
            for c in range(n_ct):
                acc_ref[c] = ffn[:, c * LANES:(c + 1) * LANES]

    for s in range(HALO):
        for jb in range(nj // HALO):
            t0 = s * nj + HALO * jb
            rows = pl.ds(HALO * HALO * jb + s, HALO, stride=HALO)
            back = jnp.concatenate([acc_ref[c, rows, :] for c in range(n_ct)], axis=1)
            o_ref[t0:t0 + HALO, :] = x1[HALO + t0:2 * HALO + t0] + back


def _post(x, mix_a, mix_b, mods, gains, wo, wu, cw, cb, wd, layer, seq_len, tm, latent):
    n = x.shape[0]
    tm = min(tm, seq_len)
    nt = n // tm
    (mix_a, col_a), (mix_b, col_b) = mix_a, mix_b
    tiles_per_seq = seq_len // tm
    cond_row = (lambda i: 1 + i // tiles_per_seq) if latent else (lambda i: 0)

    def triple(width, halo_rows, col=0):
        per = tm // halo_rows
        last = n // halo_rows - 1
        return [
            pl.BlockSpec((tm, width), lambda i: (i, col)),
            pl.BlockSpec((halo_rows, width), lambda i: (jnp.maximum(i * per - 1, 0), col)),
            pl.BlockSpec((halo_rows, width), lambda i: (jnp.minimum((i + 1) * per, last), col)),
        ]

    const = lambda *shape: pl.BlockSpec((None,) + shape, lambda i: (layer,) + (0,) * len(shape),
                                        pipeline_mode=pl.Buffered(1))
    in_specs = (triple(D_MODEL, HALO) + triple(WIDTH, 2 * HALO, col_a)
                + triple(WIDTH, 2 * HALO, col_b) + [
        pl.BlockSpec((None, 1, 6, D_MODEL), lambda i: (layer, cond_row(i), 0, 0)),
        const(3, D_MODEL),
        const(2 * WIDTH, D_MODEL),
        const(D_MODEL, 2 * D_FF),
        const(3, 2 * D_FF),
        const(1, 2 * D_FF),
        const(D_FF, D_MODEL),
    ])
    return pl.pallas_call(
        functools.partial(_post_kernel, tiles_per_seq=tiles_per_seq),
        grid=(nt,),
        in_specs=in_specs,
        out_specs=pl.BlockSpec((tm, D_MODEL), lambda i: (i, 0)),
        out_shape=jax.ShapeDtypeStruct((n, D_MODEL), F32),
        scratch_shapes=[pltpu.VMEM((D_MODEL // LANES, tm, LANES), F32),
                        pltpu.VMEM((D_MODEL // LANES, tm + 2 * HALO, LANES), F32)],
        compiler_params=_cparams(1),
        name="post_latent" if latent else "post_context",
    )(x, x, x, mix_a, mix_a, mix_a, mix_b, mix_b, mix_b, mods, gains, wo, wu, cw, cb, wd)


def kernel(x_prompt, x_sample, c, cache_na_k, cache_na_v, state_ret_fwd, state_ret_bwd, c_ctx,
           ada_w, ada_b, g_pre_mix, g_post_mix, g_pre_ffn, g_post_ffn, w_in,
           ret_decay_fwd, ret_decay_bwd, na_rpb, w_out, w_up, conv_w, conv_b, w_down):
    depth = w_in.shape[0]
    batch, seq, _ = x_prompt.shape
    dec_batch, dec_seq, _ = x_sample.shape
    past = cache_na_k.shape[2]
    tm = 512

    cond = jnp.concatenate(
        [c_ctx[None, :], c, jnp.zeros((8 - 1 - dec_batch, D_MODEL), F32)], axis=0)
    mods = _modulation(cond, ada_w, ada_b).reshape(depth, 8, 6, D_MODEL)
    rope_tabs = _rope_tables(dec_seq)
    bias, lo, hi = _na_bias_tables(na_rpb, dec_seq // GRID_W)

    w_in_b, wo_b, wu_b, wd_b = (w.astype(BF16) for w in (w_in, w_out, w_up, w_down))
    g_pre = g_pre_mix.reshape(depth, 1, D_MODEL)
    gains = jnp.stack([g_post_mix, g_pre_ffn, g_post_ffn], axis=1)
    cb = conv_b.reshape(depth, 1, 2 * D_FF)
    post_params = (mods, gains, wo_b, wu_b, conv_w, cb, wd_b)

    y_p = x_prompt.reshape(batch * seq, D_MODEL)
    y_s = x_sample.reshape(dec_batch * dec_seq, D_MODEL)
    new_k = new_v = None
    sfs, sbs = [], []
    for l in range(depth):
        dec = (jnp.repeat(ret_decay_fwd[l], HEAD_DIM)[None, :],
               jnp.repeat(ret_decay_bwd[l], HEAD_DIM)[None, :],
               jnp.repeat(ret_decay_fwd[l], LANES)[None, :],
               jnp.repeat(ret_decay_bwd[l], LANES)[None, :])

        p_c, new_k, new_v = _inproj(y_p, mods, g_pre, w_in_b, l, seq, tm, kv_out=(new_k, new_v))
        mix_c, st_c = _ctx_mixer(p_c, dec, batch, seq)
        y_p = _post(y_p, (mix_c, 0), (mix_c, 1), *post_params, l, seq, tm, latent=False)
        sfs.append(_diag_states(st_c[:, :, 0]))
        sbs.append(_diag_states(st_c[:, :, 1]))

        (p_s,) = _inproj(y_s, mods, g_pre, w_in_b, l, dec_seq, tm, rope_tabs=rope_tabs)
        s0 = _block_diag_states(state_ret_fwd[:, l], state_ret_bwd[:, l])
        ret_s = _lat_retention(p_s, dec, s0, dec_batch, dec_seq)
        ck = cache_na_k[:, l].reshape(dec_batch, past, WIDTH).astype(BF16)
        cv = cache_na_v[:, l].reshape(dec_batch, past, WIDTH).astype(BF16)
        na_s = _neighbourhood_attention(p_s, ck, cv, bias, l, lo, hi, dec_batch, dec_seq)
        y_s = _post(y_s, (ret_s, 0), (na_s, 0), *post_params, l, dec_seq, tm, latent=True)

    return (y_p.reshape(batch, seq, D_MODEL),
            y_s.reshape(dec_batch, dec_seq, D_MODEL),
            new_k.reshape(batch, depth, seq, N_HEADS, HEAD_DIM),
            new_v.reshape(batch, depth, seq, N_HEADS, HEAD_DIM),
            jnp.stack(sfs, axis=1), jnp.stack(sbs, axis=1))
```

```python
import functools

import numpy as np
import jax
import jax.numpy as jnp
from jax import lax
from jax.experimental import pallas as pl
from jax.experimental.pallas import tpu as pltpu

F32 = jnp.float32
BF16 = jnp.bfloat16

D_MODEL = 1024
HEAD_DIM = 64
N_HEADS = 8
HEAD_PAIRS = N_HEADS // 2
LANES = 128
WIDTH = N_HEADS * HEAD_DIM
N_GROUPS = 7
IN_WIDTH = N_GROUPS * WIDTH
D_FF = 2816
FF_CHUNK = 256
CHUNK = 128
RET_UNROLL = 4
GRID_W = 64
NA_KH = 8
NA_KW = 16
NA_ROWS = 4
ROPE_BASE = 10000.0
EPS = 1e-6
NEG_INF = -1e9
HALO = 8
VMEM_LIMIT = 56 * 1024 * 1024


def _cparams(n_grid):
    return pltpu.CompilerParams(
        dimension_semantics=("arbitrary",) * n_grid, vmem_limit_bytes=VMEM_LIMIT)


def _rms(x, g):
    ms = jnp.mean(x * x, axis=-1, keepdims=True)
    return x * lax.rsqrt(ms + EPS) * g


def _silu(x):
    return x * jax.nn.sigmoid(x)


def _log_sigmoid(x):
    return jnp.minimum(x, 0.0) - jnp.log1p(jnp.exp(-jnp.abs(x)))


def _dot(a, b):
    return jnp.dot(a, b, preferred_element_type=F32)


def _dot_nt(a, b):
    return lax.dot_general(a, b, (((1,), (1,)), ((), ())), preferred_element_type=F32)


def _dot_tn(a, b):
    return lax.dot_general(a, b, (((0,), (0,)), ((), ())), preferred_element_type=F32)


def _mod_kernel(cond_ref, w_ref, b_ref, o_ref):
    s = _silu(cond_ref[...]).astype(BF16)
    o_ref[0] = _dot(s, w_ref[0].astype(BF16)) + b_ref[0]


def _modulation(cond, ada_w, ada_b):
    depth = ada_w.shape[0]
    nb = 6 * D_MODEL // D_MODEL
    return pl.pallas_call(
        _mod_kernel,
        grid=(depth, nb),
        in_specs=[
            pl.BlockSpec((8, D_MODEL), lambda l, j: (0, 0)),
            pl.BlockSpec((1, D_MODEL, D_MODEL), lambda l, j: (l, 0, j)),
            pl.BlockSpec((1, 1, D_MODEL), lambda l, j: (l, 0, j)),
        ],
        out_specs=pl.BlockSpec((1, 8, D_MODEL), lambda l, j: (l, 0, j)),
        out_shape=jax.ShapeDtypeStruct((depth, 8, 6 * D_MODEL), F32),
        compiler_params=_cparams(2),
        name="modulation",
    )(cond, ada_w, ada_b.reshape(depth, 1, 6 * D_MODEL))


def _inproj_kernel(*refs, rope, emit_kv, n_alias):
    x_ref, mod_ref, g_ref, w_ref = refs[:4]
    refs = refs[4:]
    if rope:
        cos_ref, sin_up_ref, sin_dn_ref = refs[:3]
        refs = refs[3:]
    refs = refs[n_alias:]
    p_ref = refs[0]
    mod = mod_ref[0]
    h = (_rms(x_ref[...], g_ref[...]) * (1.0 + mod[1:2]) + mod[0:1]).astype(BF16)
    for g in range(N_GROUPS):
        cols = slice(g * WIDTH, (g + 1) * WIDTH)
        pg = _dot(h, w_ref[:, cols])
        if rope and g < 2:
            parts = []
            for j in range(WIDTH // LANES):
                xj = pg[:, j * LANES:(j + 1) * LANES]
                parts.append(xj * cos_ref[...]
                             + pltpu.roll(xj, 16, 1) * sin_up_ref[...]
                             + pltpu.roll(xj, LANES - 16, 1) * sin_dn_ref[...])
            pg = jnp.concatenate(parts, axis=1)
        if g in (0, 4):
            pg = pg * (HEAD_DIM ** -0.5)
        p_ref[:, cols] = pg.astype(BF16)
        if emit_kv and g >= 5:
            seq = refs[g - 4].shape[1]
            for j in range(refs[g - 4].shape[0]):
                refs[g - 4][j] = pg[j * seq:(j + 1) * seq]


def _inproj(x, mods, g_pre, w, layer, seq_len, tm, rope_tabs=None, kv_out=None):
    n = x.shape[0]
    depth = w.shape[0]
    tiles_per_seq = max(seq_len // tm, 1)
    cond_row = (lambda i: 0) if rope_tabs is None else (lambda i: 1 + i // tiles_per_seq)
    in_specs = [
        pl.BlockSpec((tm, D_MODEL), lambda i: (i, 0)),
        pl.BlockSpec((None, 1, 6, D_MODEL), lambda i: (layer, cond_row(i), 0, 0)),
        pl.BlockSpec((None, 1, D_MODEL), lambda i: (layer, 0, 0)),
        pl.BlockSpec((None, D_MODEL, IN_WIDTH), lambda i: (layer, 0, 0),
                     pipeline_mode=pl.Buffered(1)),
    ]
    args = [x, mods, g_pre, w]
    if rope_tabs is not None:
        in_specs += [pl.BlockSpec((tm, LANES), lambda i: (i % tiles_per_seq, 0))] * 3
        args += list(rope_tabs)
    out_specs = [pl.BlockSpec((tm, IN_WIDTH), lambda i: (i, 0))]
    out_shape = [jax.ShapeDtypeStruct((n, IN_WIDTH), BF16)]
    aliases = {}
    if kv_out is not None:
        seqs_per_tile = tm // seq_len
        kv_shape = jax.ShapeDtypeStruct((n // seq_len, depth, seq_len, WIDTH), F32)
        for j, prev in enumerate(kv_out):
            if prev is not None:
                aliases[len(args)] = 1 + j
                in_specs.append(pl.BlockSpec(memory_space=pl.ANY))
                args.append(prev)
            out_specs.append(pl.BlockSpec((seqs_per_tile, None, seq_len, WIDTH),
                                          lambda i: (i, layer, 0, 0)))
            out_shape.append(kv_shape)
    return pl.pallas_call(
        functools.partial(_inproj_kernel, rope=rope_tabs is not None,
                          emit_kv=kv_out is not None, n_alias=len(aliases)),
        grid=(n // tm,),
        in_specs=in_specs,
        out_specs=out_specs,
        out_shape=out_shape,
        input_output_aliases=aliases,
        compiler_params=_cparams(1),
        name="inproj_latent" if rope_tabs is not None else "inproj_context",
    )(*args)


def _rope_tables(seq_len):
    t = np.arange(seq_len)
    lane = np.arange(LANES)
    d = lane % HEAD_DIM
    pos = np.where(d[None, :] < HEAD_DIM // 2, (t // GRID_W)[:, None], (t % GRID_W)[:, None])
    pos = pos.astype(np.float32)
    half = HEAD_DIM // 2
    inv = np.power(np.float32(ROPE_BASE), -np.arange(0, half, 2, dtype=np.float32) / half)
    ang = pos * inv[d % (half // 2)][None, :]
    cos, sin = np.cos(ang), np.sin(ang)
    upper = (d % half) >= half // 2
    sin_up = np.where(upper[None, :], sin, 0.0)
    sin_dn = np.where(upper[None, :], 0.0, -sin)
    return (jnp.asarray(cos, F32), jnp.asarray(sin_up, F32), jnp.asarray(sin_dn, F32))


def _retention_tables(dec_f, dec_b, dec_f2, dec_b2):
    lgf, lgb = _log_sigmoid(dec_f), _log_sigmoid(dec_b)
    pos = lax.broadcasted_iota(jnp.int32, (CHUNK, LANES), 0).astype(F32)
    tabs = dict(
        qdf=jnp.exp(lgf * (pos + 1.0)), kdf=jnp.exp(lgf * (CHUNK - 1.0 - pos)),
        cdf=jnp.exp(lgf * float(CHUNK)),
        qdb=jnp.exp(lgb * (CHUNK - pos)), kdb=jnp.exp(lgb * pos),
        cdb=jnp.exp(lgb * float(CHUNK)),
    )
    lgf2, lgb2 = _log_sigmoid(dec_f2), _log_sigmoid(dec_b2)
    i = lax.broadcasted_iota(jnp.int32, (CHUNK, 2 * CHUNK), 0)
    j = lax.broadcasted_iota(jnp.int32, (CHUNK, 2 * CHUNK), 1) & (CHUNK - 1)
    diff = (i - j).astype(F32)
    tabs["decay"] = (jnp.where(diff >= 0, jnp.exp(lgf2 * jnp.maximum(diff, 0.0)), 0.0)
                     + jnp.where(diff <= 0, jnp.exp(lgb2 * jnp.maximum(-diff, 0.0)), 0.0))
    lane = lax.broadcasted_iota(jnp.int32, (1, LANES), 1)
    tabs["head_a"] = lane < HEAD_DIM
    r = lax.broadcasted_iota(jnp.int32, (2 * LANES, LANES), 0) & (LANES - 1)
    c = lax.broadcasted_iota(jnp.int32, (2 * LANES, LANES), 1)
    tabs["same_head"] = (r < HEAD_DIM) == (c < HEAD_DIM)
    return tabs


def _split_heads(x, head_a, axis):
    zero = jnp.zeros_like(x)
    return jnp.concatenate([jnp.where(head_a, x, zero), jnp.where(head_a, zero, x)], axis=axis)


def _chunk_kv(k2, v2, tabs):
    kf = k2.astype(F32)
    kk = jnp.concatenate([kf * tabs["kdf"], kf * tabs["kdb"]], axis=1).astype(BF16)
    return jnp.where(tabs["same_head"], _dot_tn(kk, v2), 0.0)


def _chunk_out(q2, k2, v2, g2, state, tabs):
    head_a = tabs["head_a"]
    s = _dot_nt(q2, _split_heads(k2, head_a, 0))
    p = (s * tabs["decay"]).astype(BF16)
    qf = q2.astype(F32)
    lhs = jnp.concatenate(
        [p, (qf * tabs["qdf"]).astype(BF16), (qf * tabs["qdb"]).astype(BF16)], axis=1)
    rhs = jnp.concatenate([_split_heads(v2, head_a, 0), state], axis=0)
    o = _dot(lhs, rhs)
    inv = 1.0 / HEAD_DIM
    sum_a = jnp.sum(jnp.where(head_a, o, 0.0), axis=-1, keepdims=True)
    sum_b = jnp.sum(jnp.where(head_a, 0.0, o), axis=-1, keepdims=True)
    d = o - jnp.where(head_a, sum_a, sum_b) * inv
    d2 = d * d
    var_a = jnp.sum(jnp.where(head_a, d2, 0.0), axis=-1, keepdims=True)
    var_b = jnp.sum(jnp.where(head_a, 0.0, d2), axis=-1, keepdims=True)
    o = d * lax.rsqrt(jnp.where(head_a, var_a, var_b) * inv + EPS)
    return o * _silu(g2.astype(F32))


def _ctx_mixer_kernel(rq, rk, rv, rg, nq, nk, nv, df, db, df2, db2, mix_ref, st_ref):
    seq = rq.shape[0]
    nc = seq // CHUNK
    for hp in range(HEAD_PAIRS):
        cols = slice(hp * LANES, (hp + 1) * LANES)
        cols2 = slice(hp * 2 * LANES, (hp + 1) * 2 * LANES)
        tabs = _retention_tables(df[:, cols], db[:, cols], df2[:, cols2], db2[:, cols2])
        rows = [slice(c * CHUNK, (c + 1) * CHUNK) for c in range(nc)]
        kv = [_chunk_kv(rk[r, cols], rv[r, cols], tabs) for r in rows]
        sf = [jnp.zeros((LANES, LANES), F32)]
        for c in range(nc):
            sf.append(sf[-1] * tabs["cdf"] + kv[c][:LANES])
        sb = [jnp.zeros((LANES, LANES), F32)]
        for c in reversed(range(nc)):
            sb.append(sb[-1] * tabs["cdb"] + kv[c][LANES:])
        for c in range(nc):
            state = jnp.concatenate([sf[c], sb[nc - 1 - c]], axis=0).astype(BF16)
            o = _chunk_out(rq[rows[c], cols], rk[rows[c], cols], rv[rows[c], cols],
                           rg[rows[c], cols], state, tabs)
            mix_ref[rows[c], cols] = o.astype(BF16)
        st_ref[0, hp, 0] = sf[nc]
        st_ref[0, hp, 1] = sb[nc]

        head_a = tabs["head_a"]
        s = _dot_nt(nq[:, cols], _split_heads(nk[:, cols], head_a, 0))
        es, rinv = [], []
        for h in range(2):
            sh = s[:, h * seq:(h + 1) * seq]
            e = jnp.exp(sh - jnp.max(sh, axis=-1, keepdims=True))
            rinv.append(1.0 / jnp.sum(e, axis=-1, keepdims=True))
            es.append(e.astype(BF16))
        o = _dot(jnp.concatenate(es, axis=1), _split_heads(nv[:, cols], head_a, 0))
        o = o * jnp.where(head_a, rinv[0], rinv[1])
        mix_ref[:, WIDTH + hp * LANES:WIDTH + (hp + 1) * LANES] = o.astype(BF16)


def _ctx_mixer(p, dec, batch, seq):
    group = lambda g: pl.BlockSpec((seq, WIDTH), lambda b, g=g: (b, g))
    vec = lambda w: pl.BlockSpec((1, w), lambda b: (0, 0))
    return pl.pallas_call(
        _ctx_mixer_kernel,
        grid=(batch,),
        in_specs=[group(g) for g in range(N_GROUPS)] + [vec(WIDTH), vec(WIDTH),
                                                         vec(2 * WIDTH), vec(2 * WIDTH)],
        out_specs=[
            pl.BlockSpec((seq, 2 * WIDTH), lambda b: (b, 0)),
            pl.BlockSpec((1, HEAD_PAIRS, 2, LANES, LANES), lambda b: (b, 0, 0, 0, 0)),
        ],
        out_shape=[
            jax.ShapeDtypeStruct((batch * seq, 2 * WIDTH), BF16),
            jax.ShapeDtypeStruct((batch, HEAD_PAIRS, 2, LANES, LANES), F32),
        ],
        compiler_params=_cparams(1),
        name="context_mixer",
    )(*([p] * N_GROUPS), *dec)


def _lat_retention_kernel(q_ref, k_ref, v_ref, g_ref, df, db, df2, db2, s0_ref, o_ref,
                          kv_ref, st_ref):
    nc = q_ref.shape[0] // CHUNK
    tabs = _retention_tables(df[...], db[...], df2[...], db2[...])

    def rows(c):
        return pl.ds(pl.multiple_of(c * CHUNK, CHUNK), CHUNK)

    def kv_body(c, carry):
        kv_ref[c] = _chunk_kv(k_ref[rows(c), :], v_ref[rows(c), :], tabs)
        return carry
    lax.fori_loop(0, nc, kv_body, 0, unroll=RET_UNROLL)

    def fwd_body(c, s):
        st_ref[c, :LANES, :] = s.astype(BF16)
        return s * tabs["cdf"] + kv_ref[c, :LANES, :]
    lax.fori_loop(0, nc, fwd_body, s0_ref[0, 0, 0])

    def bwd_body(i, s):
        c = nc - 1 - i
        st_ref[c, LANES:, :] = s.astype(BF16)
        return s * tabs["cdb"] + kv_ref[c, LANES:, :]
    lax.fori_loop(0, nc, bwd_body, s0_ref[0, 0, 1])

    def out_body(c, carry):
        r = rows(c)
        o = _chunk_out(q_ref[r, :], k_ref[r, :], v_ref[r, :], g_ref[r, :], st_ref[c], tabs)
        o_ref[r, :] = o.astype(BF16)
        return carry
    lax.fori_loop(0, nc, out_body, 0, unroll=RET_UNROLL)


def _lat_retention(p, dec, s0, batch, seq):
    nc = seq // CHUNK
    group = lambda g: pl.BlockSpec((seq, LANES), lambda b, h, g=g: (b, g * HEAD_PAIRS + h))
    vec = lambda w: pl.BlockSpec((1, w), lambda b, h: (0, h))
    return pl.pallas_call(
        _lat_retention_kernel,
        grid=(batch, HEAD_PAIRS),
        in_specs=[group(g) for g in range(4)] + [vec(LANES), vec(LANES), vec(2 * LANES),
                                                 vec(2 * LANES)]
        + [pl.BlockSpec((1, 1, 2, LANES, LANES), lambda b, h: (b, h, 0, 0, 0))],
        out_specs=pl.BlockSpec((seq, LANES), lambda b, h: (b, h)),
        out_shape=jax.ShapeDtypeStruct((batch * seq, WIDTH), BF16),
        scratch_shapes=[pltpu.VMEM((nc, 2 * LANES, LANES), F32),
                        pltpu.VMEM((nc, 2 * LANES, LANES), BF16)],
        compiler_params=_cparams(2),
        name="latent_retention",
    )(*([p] * 4), *dec, s0)


def _block_diag_states(s_f, s_b):
    def bd(s):
        b = s.shape[0]
        s = s.reshape(b, HEAD_PAIRS, 2, HEAD_DIM, HEAD_DIM)
        z = jnp.zeros_like(s[:, :, 0])
        top = jnp.concatenate([s[:, :, 0], z], axis=-1)
        bot = jnp.concatenate([z, s[:, :, 1]], axis=-1)
        return jnp.concatenate([top, bot], axis=-2)
    return jnp.stack([bd(s_f), bd(s_b)], axis=2)


def _diag_states(st):
    b = st.shape[0]
    a = st[:, :, :HEAD_DIM, :HEAD_DIM]
    c = st[:, :, HEAD_DIM:, HEAD_DIM:]
    return jnp.stack([a, c], axis=2).reshape(b, N_HEADS, HEAD_DIM, HEAD_DIM)


def _na_kernel(q_ref, k_ref, v_ref, ck_ref, cv_ref, *refs):
    bias_refs, o_ref = refs[:NA_ROWS], refs[NA_ROWS]
    rows_total = k_ref.shape[0] // GRID_W
    lane = lax.broadcasted_iota(jnp.int32, (1, LANES), 1)
    head_a = lane < HEAD_DIM
    wins = []
    for i in range(NA_ROWS):
        r = pl.program_id(1) * NA_ROWS + i
        rs = jnp.clip(r - NA_KH // 2, 0, rows_total - NA_KH)
        wins.append(pl.ds(pl.multiple_of(rs * GRID_W, GRID_W), NA_KH * GRID_W))
    for hp in range(HEAD_PAIRS):
        cols = slice(hp * LANES, (hp + 1) * LANES)
        qq = jnp.concatenate(
            [_split_heads(q_ref[i * GRID_W:(i + 1) * GRID_W, cols], head_a, 0)
             for i in range(NA_ROWS)], axis=0)
        s_ctx = _dot_nt(qq, ck_ref[0, :, cols])
        m_ctx = jnp.max(s_ctx, axis=-1, keepdims=True)
        s_loc, m = [], []
        for i in range(NA_ROWS):
            blk = slice(i * 2 * GRID_W, (i + 1) * 2 * GRID_W)
            s = _dot_nt(qq[blk], k_ref[wins[i], cols]) + bias_refs[i][0, hp]
            s_loc.append(s)
            m.append(jnp.maximum(jnp.max(s, axis=-1, keepdims=True), m_ctx[blk]))
        e_ctx = jnp.exp(s_ctx - jnp.concatenate(m, axis=0))
        den_ctx = jnp.sum(e_ctx, axis=-1, keepdims=True)
        o_ctx = _dot(e_ctx.astype(BF16), cv_ref[0, :, cols])
        for i in range(NA_ROWS):
            blk = slice(i * 2 * GRID_W, (i + 1) * 2 * GRID_W)
            e = jnp.exp(s_loc[i] - m[i])
            den = jnp.sum(e, axis=-1, keepdims=True) + den_ctx[blk]
            o = (_dot(e.astype(BF16), v_ref[wins[i], cols]) + o_ctx[blk]) / den
            o_ref[i * GRID_W:(i + 1) * GRID_W, cols] = jnp.where(
                head_a, o[:GRID_W], o[GRID_W:]).astype(BF16)


def _bias_build_kernel(rpb_ref, o_ref, *, dr_first, n_dr):
    qc = lax.broadcasted_iota(jnp.int32, (GRID_W, LANES), 0)
    lane = lax.broadcasted_iota(jnp.int32, (GRID_W, LANES), 1)
    kc = lane & (GRID_W - 1)
    cs = jnp.clip(qc - NA_KW // 2, 0, GRID_W - NA_KW)
    inside = (kc >= cs) & (kc < cs + NA_KW)
    first = lane < GRID_W
    for h in range(N_HEADS):
        lo_half, hi_half = [], []
        for dr in range(n_dr):
            line = jnp.broadcast_to(rpb_ref[pl.ds(h * n_dr + dr, 1), :], (GRID_W, LANES))
            lo_half.append(pltpu.roll(line, LANES - (NA_KW - 1), 1, stride=1, stride_axis=0))
            hi_half.append(pltpu.roll(line, GRID_W - (NA_KW - 1), 1, stride=1, stride_axis=0))
        for cls, dr0 in enumerate(dr_first):
            for jp in range(NA_KH // 2):
                dr = dr0 + 2 * jp
                tile = jnp.where(inside, jnp.where(first, lo_half[dr], hi_half[dr + 1]), NEG_INF)
                o_ref[cls, h // 2, (h % 2) * GRID_W:(h % 2 + 1) * GRID_W,
                      jp * LANES:(jp + 1) * LANES] = tile


def _na_bias_tables(na_rpb, rows_total):
    depth, heads, n_dr, n_dc = na_rpb.shape
    kh = NA_KH
    lo, hi = kh // 2, rows_total - kh // 2 - 1
    reps = list(range(lo)) + [lo] + list(range(hi + 1, rows_total))
    dr_first = tuple(int(np.clip(r - kh // 2, 0, rows_total - kh)) - r + kh - 1 for r in reps)
    lines = jnp.pad(na_rpb.reshape(depth, heads * n_dr, n_dc), ((0, 0), (0, 0), (0, LANES - n_dc)))
    bias = pl.pallas_call(
        functools.partial(_bias_build_kernel, dr_first=dr_first, n_dr=n_dr),
        grid=(depth,),
        in_specs=[pl.BlockSpec((None, heads * n_dr, LANES), lambda l: (l, 0, 0))],
        out_specs=pl.BlockSpec((None, len(reps), HEAD_PAIRS, 2 * GRID_W, kh * GRID_W),
                               lambda l: (l, 0, 0, 0, 0)),
        out_shape=jax.ShapeDtypeStruct(
            (depth, len(reps), HEAD_PAIRS, 2 * GRID_W, kh * GRID_W), F32),
        compiler_params=_cparams(1),
        name="na_bias_build",
    )(lines)
    return bias, lo, hi


def _neighbourhood_attention(p, ctx_k, ctx_v, bias, layer, lo, hi, batch, seq):
    rows_total = seq // GRID_W
    past = ctx_k.shape[1]

    def cls(r):
        return jnp.where(r < lo, r, jnp.where(r > hi, r - hi + lo, lo))

    groups = rows_total // NA_ROWS
    bias_spec = lambda i: pl.BlockSpec(
        (None, 1, HEAD_PAIRS, 2 * GRID_W, NA_KH * GRID_W),
        lambda b, g: (layer, cls(g * NA_ROWS + i), 0, 0, 0))
    return pl.pallas_call(
        _na_kernel,
        grid=(batch, groups),
        in_specs=[
            pl.BlockSpec((NA_ROWS * GRID_W, WIDTH), lambda b, g: (b * groups + g, 4)),
            pl.BlockSpec((seq, WIDTH), lambda b, g: (b, 5)),
            pl.BlockSpec((seq, WIDTH), lambda b, g: (b, 6)),
            pl.BlockSpec((1, past, WIDTH), lambda b, g: (b, 0, 0)),
            pl.BlockSpec((1, past, WIDTH), lambda b, g: (b, 0, 0)),
        ] + [bias_spec(i) for i in range(NA_ROWS)],
        out_specs=pl.BlockSpec((NA_ROWS * GRID_W, WIDTH), lambda b, g: (b * groups + g, 0)),
        out_shape=jax.ShapeDtypeStruct((batch * seq, WIDTH), BF16),
        compiler_params=_cparams(2),
        name="neighbourhood_attention",
    )(p, p, p, ctx_k, ctx_v, *([bias] * NA_ROWS))


def _post_kernel(xm, xp, xn, am, ap, an, bm, bp, bn, mod_ref, g_ref, wo, wu, cw, cb, wd, o_ref,
                 acc_ref, h_ref, *, tiles_per_seq):
    tm = xm.shape[0]
    nj = tm // HALO
    mod = mod_ref[0]
    g = g_ref[...]

    def ext(main, prev, nxt):
        lo = prev[...].astype(F32)[prev.shape[0] - HALO:]
        hi = nxt[...].astype(F32)[:HALO]
        return jnp.concatenate([lo, main[...].astype(F32), hi], axis=0)

    x = ext(xm, xp, xn)
    a = ext(am, ap, an).astype(BF16)
    b = ext(bm, bp, bn).astype(BF16)
    half = wo.shape[0] // 2
    y = _dot(a, wo[:half, :]) + _dot(b, wo[half:, :])
    x1 = x + mod[2:3] * _rms(y, g[0:1])
    h = _rms(x1, g[1:2]) * (1.0 + mod[4:5]) + mod[3:4]

    t = pl.program_id(0) % tiles_per_seq
    ridx = lax.broadcasted_iota(jnp.int32, h.shape, 0)
    keep = (((ridx >= HALO) | (t != 0)) & ((ridx < HALO + tm) | (t != tiles_per_seq - 1)))
    h = jnp.where(keep, h, 0.0)
    n_ct = D_MODEL // LANES
    for c in range(n_ct):
        h_ref[c] = h[:, c * LANES:(c + 1) * LANES]

    def group(rows):
        return jnp.concatenate([h_ref[c, rows, :] for c in range(n_ct)], axis=1)
    groups = [group(pl.ds(HALO + j, HALO, stride=nj)) for j in range(nj)]
    groups += [group(pl.ds(0, HALO)), group(pl.ds(HALO + tm, HALO))]
    hp = jnp.concatenate(groups, axis=0).astype(BF16)
    sub = lax.broadcasted_iota(jnp.int32, (HALO, FF_CHUNK), 0)

    def conv(u, cols):
        w = cw[:, cols]
        before = jnp.where(sub == 0, u[tm + HALO - 1:tm + HALO], pltpu.roll(u[tm - HALO:tm], 1, 0))
        after = jnp.where(sub == HALO - 1, u[tm + HALO:tm + HALO + 1], pltpu.roll(u[0:HALO], HALO - 1, 0))
        prev = jnp.concatenate([before, u[0:tm - HALO]], axis=0)
        nxt = jnp.concatenate([u[HALO:tm], after], axis=0)
        return prev * w[0:1] + u[0:tm] * w[1:2] + nxt * w[2:3] + cb[:, cols]

    n_chunks = D_FF // FF_CHUNK

    def up(ch):
        ca = slice(ch * FF_CHUNK, (ch + 1) * FF_CHUNK)
        cg = slice(D_FF + ch * FF_CHUNK, D_FF + (ch + 1) * FF_CHUNK)
        return _dot(hp, wu[:, cg]), _dot(hp, wu[:, ca])

    nxt_u = up(0)
    for ch in range(n_chunks):
        ug, ua = nxt_u
        if ch + 1 < n_chunks:
            nxt_u = up(ch + 1)
        ca = slice(ch * FF_CHUNK, (ch + 1) * FF_CHUNK)
        cg = slice(D_FF + ch * FF_CHUNK, D_FF + (ch + 1) * FF_CHUNK)
        part = _dot((_silu(conv(ug, cg)) * conv(ua, ca)).astype(BF16), wd[ca, :])
        if ch == 0:
            acc = part
        elif ch + 1 < n_chunks:
            acc = acc + part
        else:
            ffn = mod[5:6] * _rms(acc + part, g[2:3])
            for c in range(n_ct):
                acc_ref[c] = ffn[:, c * LANES:(c + 1) * LANES]

    for s in range(HALO):
        for jb in range(nj // HALO):
            t0 = s * nj + HALO * jb
            rows = pl.ds(HALO * HALO * jb + s, HALO, stride=HALO)
            back = jnp.concatenate([acc_ref[c, rows, :] for c in range(n_ct)], axis=1)
            o_ref[t0:t0 + HALO, :] = x1[HALO + t0:2 * HALO + t0] + back


def _post(x, mix_a, mix_b, mods, gains, wo, wu, cw, cb, wd, layer, seq_len, tm, latent):
    n = x.shape[0]
    tm = min(tm, seq_len)
    nt = n // tm
    (mix_a, col_a), (mix_b, col_b) = mix_a, mix_b
    tiles_per_seq = seq_len // tm
    cond_row = (lambda i: 1 + i // tiles_per_seq) if latent else (lambda i: 0)

    def triple(width, halo_rows, col=0):
        per = tm // halo_rows
        last = n // halo_rows - 1
        return [
            pl.BlockSpec((tm, width), lambda i: (i, col)),
            pl.BlockSpec((halo_rows, width), lambda i: (jnp.maximum(i * per - 1, 0), col)),
            pl.BlockSpec((halo_rows, width), lambda i: (jnp.minimum((i + 1) * per, last), col)),
        ]

    const = lambda *shape: pl.BlockSpec((None,) + shape, lambda i: (layer,) + (0,) * len(shape),
                                        pipeline_mode=pl.Buffered(1))
    in_specs = (triple(D_MODEL, HALO) + triple(WIDTH, 2 * HALO, col_a)
                + triple(WIDTH, 2 * HALO, col_b) + [
        pl.BlockSpec((None, 1, 6, D_MODEL), lambda i: (layer, cond_row(i), 0, 0)),
        const(3, D_MODEL),
        const(2 * WIDTH, D_MODEL),
        const(D_MODEL, 2 * D_FF),
        const(3, 2 * D_FF),
        const(1, 2 * D_FF),
        const(D_FF, D_MODEL),
    ])
    return pl.pallas_call(
        functools.partial(_post_kernel, tiles_per_seq=tiles_per_seq),
        grid=(nt,),
        in_specs=in_specs,
        out_specs=pl.BlockSpec((tm, D_MODEL), lambda i: (i, 0)),
        out_shape=jax.ShapeDtypeStruct((n, D_MODEL), F32),
        scratch_shapes=[pltpu.VMEM((D_MODEL // LANES, tm, LANES), F32),
                        pltpu.VMEM((D_MODEL // LANES, tm + 2 * HALO, LANES), F32)],
        compiler_params=_cparams(1),
        name="post_latent" if latent else "post_context",
    )(x, x, x, mix_a, mix_a, mix_a, mix_b, mix_b, mix_b, mods, gains, wo, wu, cw, cb, wd)


def kernel(x_prompt, x_sample, c, cache_na_k, cache_na_v, state_ret_fwd, state_ret_bwd, c_ctx,
           ada_w, ada_b, g_pre_mix, g_post_mix, g_pre_ffn, g_post_ffn, w_in,
           ret_decay_fwd, ret_decay_bwd, na_rpb, w_out, w_up, conv_w, conv_b, w_down):
    depth = w_in.shape[0]
    batch, seq, _ = x_prompt.shape
    dec_batch, dec_seq, _ = x_sample.shape
    past = cache_na_k.shape[2]
    tm = 512

    cond = jnp.concatenate(
        [c_ctx[None, :], c, jnp.zeros((8 - 1 - dec_batch, D_MODEL), F32)], axis=0)
    mods = _modulation(cond, ada_w, ada_b).reshape(depth, 8, 6, D_MODEL)
    rope_tabs = _rope_tables(dec_seq)
    bias, lo, hi = _na_bias_tables(na_rpb, dec_seq // GRID_W)

    w_in_b, wo_b, wu_b, wd_b = (w.astype(BF16) for w in (w_in, w_out, w_up, w_down))
    g_pre = g_pre_mix.reshape(depth, 1, D_MODEL)
    gains = jnp.stack([g_post_mix, g_pre_ffn, g_post_ffn], axis=1)
    cb = conv_b.reshape(depth, 1, 2 * D_FF)
    post_params = (mods, gains, wo_b, wu_b, conv_w, cb, wd_b)

    y_p = x_prompt.reshape(batch * seq, D_MODEL)
    y_s = x_sample.reshape(dec_batch * dec_seq, D_MODEL)
    new_k = new_v = None
    sfs, sbs = [], []
    for l in range(depth):
        dec = (jnp.repeat(ret_decay_fwd[l], HEAD_DIM)[None, :],
               jnp.repeat(ret_decay_bwd[l], HEAD_DIM)[None, :],
               jnp.repeat(ret_decay_fwd[l], LANES)[None, :],
               jnp.repeat(ret_decay_bwd[l], LANES)[None, :])

        p_c, new_k, new_v = _inproj(y_p, mods, g_pre, w_in_b, l, seq, tm, kv_out=(new_k, new_v))
        mix_c, st_c = _ctx_mixer(p_c, dec, batch, seq)
        y_p = _post(y_p, (mix_c, 0), (mix_c, 1), *post_params, l, seq, tm, latent=False)
        sfs.append(_diag_states(st_c[:, :, 0]))
        sbs.append(_diag_states(st_c[:, :, 1]))

        (p_s,) = _inproj(y_s, mods, g_pre, w_in_b, l, dec_seq, tm, rope_tabs=rope_tabs)
        s0 = _block_diag_states(state_ret_fwd[:, l], state_ret_bwd[:, l])
        ret_s = _lat_retention(p_s, dec, s0, dec_batch, dec_seq)
        ck = cache_na_k[:, l].reshape(dec_batch, past, WIDTH).astype(BF16)
        cv = cache_na_v[:, l].reshape(dec_batch, past, WIDTH).astype(BF16)
        na_s = _neighbourhood_attention(p_s, ck, cv, bias, l, lo, hi, dec_batch, dec_seq)
        y_s = _post(y_s, (ret_s, 0), (na_s, 0), *post_params, l, dec_seq, tm, latent=True)

    return (y_p.reshape(batch, seq, D_MODEL),
            y_s.reshape(dec_batch, dec_seq, D_MODEL),
            new_k.reshape(batch, depth, seq, N_HEADS, HEAD_DIM),
            new_v.reshape(batch, depth, seq, N_HEADS, HEAD_DIM),
            jnp.stack(sfs, axis=1), jnp.stack(sbs, axis=1))
```

```python
import functools

import numpy as np
import jax
import jax.numpy as jnp
from jax import lax
from jax.experimental import pallas as pl
from jax.experimental.pallas import tpu as pltpu

F32 = jnp.float32
BF16 = jnp.bfloat16

D_MODEL = 1024
HEAD_DIM = 64
N_HEADS = 8
HEAD_PAIRS = N_HEADS // 2
LANES = 128
WIDTH = N_HEADS * HEAD_DIM
N_GROUPS = 7
IN_WIDTH = N_GROUPS * WIDTH
D_FF = 2816
FF_CHUNK = 256
HEAD_ROWS = 32
CHUNK = 128
RET_UNROLL = 4
GRID_W = 64
NA_KH = 8
NA_KW = 16
NA_ROWS = 4
ROPE_BASE = 10000.0
EPS = 1e-6
NEG_INF = -1e9
HALO = 8
VMEM_LIMIT = 56 * 1024 * 1024


def _cparams(n_grid):
    return pltpu.CompilerParams(
        dimension_semantics=("arbitrary",) * n_grid, vmem_limit_bytes=VMEM_LIMIT)


def _rms(x, g):
    ms = jnp.mean(x * x, axis=-1, keepdims=True)
    return x * lax.rsqrt(ms + EPS) * g


def _silu(x):
    return x * jax.nn.sigmoid(x)


def _log_sigmoid(x):
    return jnp.minimum(x, 0.0) - jnp.log1p(jnp.exp(-jnp.abs(x)))


def _dot(a, b):
    return jnp.dot(a, b, preferred_element_type=F32)


def _dot_nt(a, b):
    return lax.dot_general(a, b, (((1,), (1,)), ((), ())), preferred_element_type=F32)


def _dot_tn(a, b):
    return lax.dot_general(a, b, (((0,), (0,)), ((), ())), preferred_element_type=F32)


def _mod_kernel(cond_ref, w_ref, b_ref, o_ref):
    s = _silu(cond_ref[...]).astype(BF16)
    o_ref[0] = _dot(s, w_ref[0].astype(BF16)) + b_ref[0]


def _modulation(cond, ada_w, ada_b):
    depth = ada_w.shape[0]
    nb = 6 * D_MODEL // D_MODEL
    return pl.pallas_call(
        _mod_kernel,
        grid=(depth, nb),
        in_specs=[
            pl.BlockSpec((8, D_MODEL), lambda l, j: (0, 0)),
            pl.BlockSpec((1, D_MODEL, D_MODEL), lambda l, j: (l, 0, j)),
            pl.BlockSpec((1, 1, D_MODEL), lambda l, j: (l, 0, j)),
        ],
        out_specs=pl.BlockSpec((1, 8, D_MODEL), lambda l, j: (l, 0, j)),
        out_shape=jax.ShapeDtypeStruct((depth, 8, 6 * D_MODEL), F32),
        compiler_params=_cparams(2),
        name="modulation",
    )(cond, ada_w, ada_b.reshape(depth, 1, 6 * D_MODEL))


def _inproj_kernel(*refs, rope, emit_kv, n_alias, layer):
    x_ref, mod_ref, g_ref, w_ref = refs[:4]
    refs = refs[4:]
    if rope:
        cos_ref, sin_up_ref, sin_dn_ref = refs[:3]
        refs = refs[3:]
    refs = refs[n_alias:]
    p_ref = refs[0]
    mod = mod_ref[0]
    h = (_rms(x_ref[...], g_ref[...]) * (1.0 + mod[1:2]) + mod[0:1]).astype(BF16)
    for g in range(N_GROUPS):
        cols = slice(g * WIDTH, (g + 1) * WIDTH)
        pg = _dot(h, w_ref[:, cols])
        if rope and g < 2:
            parts = []
            for j in range(WIDTH // LANES):
                xj = pg[:, j * LANES:(j + 1) * LANES]
                parts.append(xj * cos_ref[...]
                             + pltpu.roll(xj, 16, 1) * sin_up_ref[...]
                             + pltpu.roll(xj, LANES - 16, 1) * sin_dn_ref[...])
            pg = jnp.concatenate(parts, axis=1)
        if g in (0, 4):
            pg = pg * (HEAD_DIM ** -0.5)
        p_ref[:, cols] = pg.astype(BF16)
        if emit_kv and g >= 5:
            kv_ref = refs[g - 4]
            seq = kv_ref.shape[-2]
            for j in range(kv_ref.shape[0]):
                if n_alias:
                    kv_ref[j] = pg[j * seq:(j + 1) * seq]
                else:
                    for l in range(kv_ref.shape[1]):
                        kv_ref[j, l] = (pg[j * seq:(j + 1) * seq] if l == layer
                                        else jnp.zeros((seq, WIDTH), F32))


def _inproj(x, mods, g_pre, w, layer, seq_len, tm, rope_tabs=None, kv_out=None):
    n = x.shape[0]
    depth = w.shape[0]
    tiles_per_seq = max(seq_len // tm, 1)
    cond_row = (lambda i: 0) if rope_tabs is None else (lambda i: 1 + i // tiles_per_seq)
    in_specs = [
        pl.BlockSpec((tm, D_MODEL), lambda i: (i, 0)),
        pl.BlockSpec((None, 1, 6, D_MODEL), lambda i: (layer, cond_row(i), 0, 0)),
        pl.BlockSpec((None, 1, D_MODEL), lambda i: (layer, 0, 0)),
        pl.BlockSpec((None, D_MODEL, IN_WIDTH), lambda i: (layer, 0, 0),
                     pipeline_mode=pl.Buffered(1)),
    ]
    args = [x, mods, g_pre, w]
    if rope_tabs is not None:
        in_specs += [pl.BlockSpec((tm, LANES), lambda i: (i % tiles_per_seq, 0))] * 3
        args += list(rope_tabs)
    out_specs = [pl.BlockSpec((tm, IN_WIDTH), lambda i: (i, 0))]
    out_shape = [jax.ShapeDtypeStruct((n, IN_WIDTH), BF16)]
    aliases = {}
    if kv_out is not None:
        seqs_per_tile = tm // seq_len
        kv_shape = jax.ShapeDtypeStruct((n // seq_len, depth, seq_len, WIDTH), F32)
        assert (kv_out[0] is None) == (kv_out[1] is None)
        for j, prev in enumerate(kv_out):
            if prev is not None:
                aliases[len(args)] = 1 + j
                in_specs.append(pl.BlockSpec(memory_space=pl.ANY))
                args.append(prev)
                out_specs.append(pl.BlockSpec((seqs_per_tile, None, seq_len, WIDTH),
                                              lambda i: (i, layer, 0, 0)))
            else:
                out_specs.append(pl.BlockSpec((seqs_per_tile, depth, seq_len, WIDTH),
                                              lambda i: (i, 0, 0, 0)))
            out_shape.append(kv_shape)
    return pl.pallas_call(
        functools.partial(_inproj_kernel, rope=rope_tabs is not None,
                          emit_kv=kv_out is not None, n_alias=len(aliases), layer=layer),
        grid=(n // tm,),
        in_specs=in_specs,
        out_specs=out_specs,
        out_shape=out_shape,
        input_output_aliases=aliases,
        compiler_params=_cparams(1),
        name="inproj_latent" if rope_tabs is not None else "inproj_context",
    )(*args)


def _rope_tables(seq_len):
    t = np.arange(seq_len)
    lane = np.arange(LANES)
    d = lane % HEAD_DIM
    pos = np.where(d[None, :] < HEAD_DIM // 2, (t // GRID_W)[:, None], (t % GRID_W)[:, None])
    pos = pos.astype(np.float32)
    half = HEAD_DIM // 2
    inv = np.power(np.float32(ROPE_BASE), -np.arange(0, half, 2, dtype=np.float32) / half)
    ang = pos * inv[d % (half // 2)][None, :]
    cos, sin = np.cos(ang), np.sin(ang)
    upper = (d % half) >= half // 2
    sin_up = np.where(upper[None, :], sin, 0.0)
    sin_dn = np.where(upper[None, :], 0.0, -sin)
    return (jnp.asarray(cos, F32), jnp.asarray(sin_up, F32), jnp.asarray(sin_dn, F32))


def _retention_tables(dec_f, dec_b, dec_f2, dec_b2):
    lgf, lgb = _log_sigmoid(dec_f), _log_sigmoid(dec_b)
    pos = lax.broadcasted_iota(jnp.int32, (CHUNK, LANES), 0).astype(F32)
    tabs = dict(
        qdf=jnp.exp(lgf * (pos + 1.0)), kdf=jnp.exp(lgf * (CHUNK - 1.0 - pos)),
        cdf=jnp.exp(lgf * float(CHUNK)),
        qdb=jnp.exp(lgb * (CHUNK - pos)), kdb=jnp.exp(lgb * pos),
        cdb=jnp.exp(lgb * float(CHUNK)),
    )
    lgf2, lgb2 = _log_sigmoid(dec_f2), _log_sigmoid(dec_b2)
    i = lax.broadcasted_iota(jnp.int32, (CHUNK, 2 * CHUNK), 0)
    j = lax.broadcasted_iota(jnp.int32, (CHUNK, 2 * CHUNK), 1) & (CHUNK - 1)
    diff = (i - j).astype(F32)
    tabs["decay"] = (jnp.where(diff >= 0, jnp.exp(lgf2 * jnp.maximum(diff, 0.0)), 0.0)
                     + jnp.where(diff <= 0, jnp.exp(lgb2 * jnp.maximum(-diff, 0.0)), 0.0))
    lane = lax.broadcasted_iota(jnp.int32, (1, LANES), 1)
    tabs["head_a"] = lane < HEAD_DIM
    r = lax.broadcasted_iota(jnp.int32, (2 * LANES, LANES), 0) & (LANES - 1)
    c = lax.broadcasted_iota(jnp.int32, (2 * LANES, LANES), 1)
    tabs["same_head"] = (r < HEAD_DIM) == (c < HEAD_DIM)
    return tabs


def _split_heads(x, head_a, axis):
    zero = jnp.zeros_like(x)
    return jnp.concatenate([jnp.where(head_a, x, zero), jnp.where(head_a, zero, x)], axis=axis)


def _chunk_kv(k2, v2, tabs):
    kf = k2.astype(F32)
    kk = jnp.concatenate([kf * tabs["kdf"], kf * tabs["kdb"]], axis=1).astype(BF16)
    return jnp.where(tabs["same_head"], _dot_tn(kk, v2), 0.0)


def _chunk_out(q2, k2, v2, g2, state, tabs):
    head_a = tabs["head_a"]
    s = _dot_nt(q2, _split_heads(k2, head_a, 0))
    p = (s * tabs["decay"]).astype(BF16)
    qf = q2.astype(F32)
    lhs = jnp.concatenate(
        [p, (qf * tabs["qdf"]).astype(BF16), (qf * tabs["qdb"]).astype(BF16)], axis=1)
    rhs = jnp.concatenate([_split_heads(v2, head_a, 0), state], axis=0)
    o = _dot(lhs, rhs)
    inv = 1.0 / HEAD_DIM
    sum_a = jnp.sum(jnp.where(head_a, o, 0.0), axis=-1, keepdims=True)
    sum_b = jnp.sum(jnp.where(head_a, 0.0, o), axis=-1, keepdims=True)
    d = o - jnp.where(head_a, sum_a, sum_b) * inv
    d2 = d * d
    var_a = jnp.sum(jnp.where(head_a, d2, 0.0), axis=-1, keepdims=True)
    var_b = jnp.sum(jnp.where(head_a, 0.0, d2), axis=-1, keepdims=True)
    o = d * lax.rsqrt(jnp.where(head_a, var_a, var_b) * inv + EPS)
    return o * _silu(g2.astype(F32))


def _ctx_mixer_kernel(rq, rk, rv, rg, nq, nk, nv, df, db, df2, db2, mix_ref, st_ref):
    seq = rq.shape[0]
    nc = seq // CHUNK
    for hp in range(HEAD_PAIRS):
        cols = slice(hp * LANES, (hp + 1) * LANES)
        cols2 = slice(hp * 2 * LANES, (hp + 1) * 2 * LANES)
        tabs = _retention_tables(df[:, cols], db[:, cols], df2[:, cols2], db2[:, cols2])
        rows = [slice(c * CHUNK, (c + 1) * CHUNK) for c in range(nc)]
        kv = [_chunk_kv(rk[r, cols], rv[r, cols], tabs) for r in rows]
        sf = [jnp.zeros((LANES, LANES), F32)]
        for c in range(nc):
            sf.append(sf[-1] * tabs["cdf"] + kv[c][:LANES])
        sb = [jnp.zeros((LANES, LANES), F32)]
        for c in reversed(range(nc)):
            sb.append(sb[-1] * tabs["cdb"] + kv[c][LANES:])
        for c in range(nc):
            state = jnp.concatenate([sf[c], sb[nc - 1 - c]], axis=0).astype(BF16)
            o = _chunk_out(rq[rows[c], cols], rk[rows[c], cols], rv[rows[c], cols],
                           rg[rows[c], cols], state, tabs)
            mix_ref[rows[c], cols] = o.astype(BF16)
        st_ref[0, hp, 0] = sf[nc]
        st_ref[0, hp, 1] = sb[nc]

        head_a = tabs["head_a"]
        s = _dot_nt(nq[:, cols], _split_heads(nk[:, cols], head_a, 0))
        es, rinv = [], []
        for h in range(2):
            sh = s[:, h * seq:(h + 1) * seq]
            e = jnp.exp(sh - jnp.max(sh, axis=-1, keepdims=True))
            rinv.append(1.0 / jnp.sum(e, axis=-1, keepdims=True))
            es.append(e.astype(BF16))
        o = _dot(jnp.concatenate(es, axis=1), _split_heads(nv[:, cols], head_a, 0))
        o = o * jnp.where(head_a, rinv[0], rinv[1])
        mix_ref[:, WIDTH + hp * LANES:WIDTH + (hp + 1) * LANES] = o.astype(BF16)


def _ctx_mixer(p, dec, batch, seq):
    group = lambda g: pl.BlockSpec((seq, WIDTH), lambda b, g=g: (b, g))
    vec = lambda w: pl.BlockSpec((1, w), lambda b: (0, 0))
    return pl.pallas_call(
        _ctx_mixer_kernel,
        grid=(batch,),
        in_specs=[group(g) for g in range(N_GROUPS)] + [vec(WIDTH), vec(WIDTH),
                                                         vec(2 * WIDTH), vec(2 * WIDTH)],
        out_specs=[
            pl.BlockSpec((seq, 2 * WIDTH), lambda b: (b, 0)),
            pl.BlockSpec((1, HEAD_PAIRS, 2, LANES, LANES), lambda b: (b, 0, 0, 0, 0)),
        ],
        out_shape=[
            jax.ShapeDtypeStruct((batch * seq, 2 * WIDTH), BF16),
            jax.ShapeDtypeStruct((batch, HEAD_PAIRS, 2, LANES, LANES), F32),
        ],
        compiler_params=_cparams(1),
        name="context_mixer",
    )(*([p] * N_GROUPS), *dec)


def _lat_retention_kernel(q_ref, k_ref, v_ref, g_ref, df, db, df2, db2, s0_ref, o_ref,
                          kv_ref, st_ref):
    nc = q_ref.shape[0] // CHUNK
    tabs = _retention_tables(df[...], db[...], df2[...], db2[...])

    def rows(c):
        return pl.ds(pl.multiple_of(c * CHUNK, CHUNK), CHUNK)

    def kv_body(c, carry):
        kv_ref[c] = _chunk_kv(k_ref[rows(c), :], v_ref[rows(c), :], tabs)
        return carry
    lax.fori_loop(0, nc, kv_body, 0, unroll=RET_UNROLL)

    def fwd_body(c, s):
        st_ref[c, :LANES, :] = s.astype(BF16)
        return s * tabs["cdf"] + kv_ref[c, :LANES, :]
    lax.fori_loop(0, nc, fwd_body, s0_ref[0, 0, 0])

    def bwd_body(i, s):
        c = nc - 1 - i
        st_ref[c, LANES:, :] = s.astype(BF16)
        return s * tabs["cdb"] + kv_ref[c, LANES:, :]
    lax.fori_loop(0, nc, bwd_body, s0_ref[0, 0, 1])

    def out_body(c, carry):
        r = rows(c)
        o = _chunk_out(q_ref[r, :], k_ref[r, :], v_ref[r, :], g_ref[r, :], st_ref[c], tabs)
        o_ref[r, :] = o.astype(BF16)
        return carry
    lax.fori_loop(0, nc, out_body, 0, unroll=RET_UNROLL)


def _lat_retention(p, dec, s0, batch, seq):
    nc = seq // CHUNK
    group = lambda g: pl.BlockSpec((seq, LANES), lambda b, h, g=g: (b, g * HEAD_PAIRS + h))
    vec = lambda w: pl.BlockSpec((1, w), lambda b, h: (0, h))
    return pl.pallas_call(
        _lat_retention_kernel,
        grid=(batch, HEAD_PAIRS),
        in_specs=[group(g) for g in range(4)] + [vec(LANES), vec(LANES), vec(2 * LANES),
                                                 vec(2 * LANES)]
        + [pl.BlockSpec((1, 1, 2, LANES, LANES), lambda b, h: (b, h, 0, 0, 0))],
        out_specs=pl.BlockSpec((seq, LANES), lambda b, h: (b, h)),
        out_shape=jax.ShapeDtypeStruct((batch * seq, WIDTH), BF16),
        scratch_shapes=[pltpu.VMEM((nc, 2 * LANES, LANES), F32),
                        pltpu.VMEM((nc, 2 * LANES, LANES), BF16)],
        compiler_params=_cparams(2),
        name="latent_retention",
    )(*([p] * 4), *dec, s0)


def _block_diag_states(s_f, s_b):
    def bd(s):
        b = s.shape[0]
        s = s.reshape(b, HEAD_PAIRS, 2, HEAD_DIM, HEAD_DIM)
        z = jnp.zeros_like(s[:, :, 0])
        top = jnp.concatenate([s[:, :, 0], z], axis=-1)
        bot = jnp.concatenate([z, s[:, :, 1]], axis=-1)
        return jnp.concatenate([top, bot], axis=-2)
    return jnp.stack([bd(s_f), bd(s_b)], axis=2)


def _diag_states(st):
    b = st.shape[0]
    a = st[:, :, :HEAD_DIM, :HEAD_DIM]
    c = st[:, :, HEAD_DIM:, HEAD_DIM:]
    return jnp.stack([a, c], axis=2).reshape(b, N_HEADS, HEAD_DIM, HEAD_DIM)


def _na_kernel(q_ref, k_ref, v_ref, ck_ref, cv_ref, *refs):
    bias_refs, o_ref = refs[:NA_ROWS], refs[NA_ROWS]
    rows_total = k_ref.shape[0] // GRID_W
    lane = lax.broadcasted_iota(jnp.int32, (1, LANES), 1)
    head_a = lane < HEAD_DIM
    wins = []
    for i in range(NA_ROWS):
        r = pl.program_id(1) * NA_ROWS + i
        rs = jnp.clip(r - NA_KH // 2, 0, rows_total - NA_KH)
        wins.append(pl.ds(pl.multiple_of(rs * GRID_W, GRID_W), NA_KH * GRID_W))
    for hp in range(HEAD_PAIRS):
        cols = slice(hp * LANES, (hp + 1) * LANES)
        qq = jnp.concatenate(
            [_split_heads(q_ref[i * GRID_W:(i + 1) * GRID_W, cols], head_a, 0)
             for i in range(NA_ROWS)], axis=0)
        s_ctx = _dot_nt(qq, ck_ref[0, :, cols])
        m_ctx = jnp.max(s_ctx, axis=-1, keepdims=True)
        s_loc, m = [], []
        for i in range(NA_ROWS):
            blk = slice(i * 2 * GRID_W, (i + 1) * 2 * GRID_W)
            s = _dot_nt(qq[blk], k_ref[wins[i], cols]) + bias_refs[i][0, hp]
            s_loc.append(s)
            m.append(jnp.maximum(jnp.max(s, axis=-1, keepdims=True), m_ctx[blk]))
        e_ctx = jnp.exp(s_ctx - jnp.concatenate(m, axis=0))
        den_ctx = jnp.sum(e_ctx, axis=-1, keepdims=True)
        o_ctx = _dot(e_ctx.astype(BF16), cv_ref[0, :, cols])
        for i in range(NA_ROWS):
            blk = slice(i * 2 * GRID_W, (i + 1) * 2 * GRID_W)
            e = jnp.exp(s_loc[i] - m[i])
            den = jnp.sum(e, axis=-1, keepdims=True) + den_ctx[blk]
            o = (_dot(e.astype(BF16), v_ref[wins[i], cols]) + o_ctx[blk]) / den
            o_ref[i * GRID_W:(i + 1) * GRID_W, cols] = jnp.where(
                head_a, o[:GRID_W], o[GRID_W:]).astype(BF16)


def _bias_build_kernel(rpb_ref, o_ref, *, dr_first, n_dr):
    qc = lax.broadcasted_iota(jnp.int32, (GRID_W, LANES), 0)
    lane = lax.broadcasted_iota(jnp.int32, (GRID_W, LANES), 1)
    kc = lane & (GRID_W - 1)
    cs = jnp.clip(qc - NA_KW // 2, 0, GRID_W - NA_KW)
    inside = (kc >= cs) & (kc < cs + NA_KW)
    first = lane < GRID_W
    for h in range(N_HEADS):
        lo_half, hi_half = [], []
        for dr in range(n_dr):
            line = jnp.broadcast_to(rpb_ref[pl.ds(h * n_dr + dr, 1), :], (GRID_W, LANES))
            lo_half.append(pltpu.roll(line, LANES - (NA_KW - 1), 1, stride=1, stride_axis=0))
            hi_half.append(pltpu.roll(line, GRID_W - (NA_KW - 1), 1, stride=1, stride_axis=0))
        for cls, dr0 in enumerate(dr_first):
            for jp in range(NA_KH // 2):
                dr = dr0 + 2 * jp
                tile = jnp.where(inside, jnp.where(first, lo_half[dr], hi_half[dr + 1]), NEG_INF)
                o_ref[cls, h // 2, (h % 2) * GRID_W:(h % 2 + 1) * GRID_W,
                      jp * LANES:(jp + 1) * LANES] = tile


def _na_bias_tables(na_rpb, rows_total):
    depth, heads, n_dr, n_dc = na_rpb.shape
    kh = NA_KH
    lo, hi = kh // 2, rows_total - kh // 2 - 1
    reps = list(range(lo)) + [lo] + list(range(hi + 1, rows_total))
    dr_first = tuple(int(np.clip(r - kh // 2, 0, rows_total - kh)) - r + kh - 1 for r in reps)
    lines = jnp.pad(na_rpb.reshape(depth, heads * n_dr, n_dc), ((0, 0), (0, 0), (0, LANES - n_dc)))
    bias = pl.pallas_call(
        functools.partial(_bias_build_kernel, dr_first=dr_first, n_dr=n_dr),
        grid=(depth,),
        in_specs=[pl.BlockSpec((None, heads * n_dr, LANES), lambda l: (l, 0, 0))],
        out_specs=pl.BlockSpec((None, len(reps), HEAD_PAIRS, 2 * GRID_W, kh * GRID_W),
                               lambda l: (l, 0, 0, 0, 0)),
        out_shape=jax.ShapeDtypeStruct(
            (depth, len(reps), HEAD_PAIRS, 2 * GRID_W, kh * GRID_W), F32),
        compiler_params=_cparams(1),
        name="na_bias_build",
    )(lines)
    return bias, lo, hi


def _neighbourhood_attention(p, ctx_k, ctx_v, bias, layer, lo, hi, batch, seq):
    rows_total = seq // GRID_W
    past = ctx_k.shape[1]

    def cls(r):
        return jnp.where(r < lo, r, jnp.where(r > hi, r - hi + lo, lo))

    groups = rows_total // NA_ROWS
    bias_spec = lambda i: pl.BlockSpec(
        (None, 1, HEAD_PAIRS, 2 * GRID_W, NA_KH * GRID_W),
        lambda b, g: (layer, cls(g * NA_ROWS + i), 0, 0, 0))
    return pl.pallas_call(
        _na_kernel,
        grid=(batch, groups),
        in_specs=[
            pl.BlockSpec((NA_ROWS * GRID_W, WIDTH), lambda b, g: (b * groups + g, 4)),
            pl.BlockSpec((seq, WIDTH), lambda b, g: (b, 5)),
            pl.BlockSpec((seq, WIDTH), lambda b, g: (b, 6)),
            pl.BlockSpec((1, past, WIDTH), lambda b, g: (b, 0, 0)),
            pl.BlockSpec((1, past, WIDTH), lambda b, g: (b, 0, 0)),
        ] + [bias_spec(i) for i in range(NA_ROWS)],
        out_specs=pl.BlockSpec((NA_ROWS * GRID_W, WIDTH), lambda b, g: (b * groups + g, 0)),
        out_shape=jax.ShapeDtypeStruct((batch * seq, WIDTH), BF16),
        compiler_params=_cparams(2),
        name="neighbourhood_attention",
    )(p, p, p, ctx_k, ctx_v, *([bias] * NA_ROWS))


def _post_kernel(xm, xp, xn, am, ap, an, bm, bp, bn, mod_head, mod_tail, g_ref, wo, wu, cw, cb, wd,
                 o_ref, acc_ref, h_ref, hp_ref, x1_ref, y_ref, *, tiles_per_seq, n_tiles):
    step = pl.program_id(0)
    tm = xm.shape[0]
    nj = tm // HALO
    n_ct = D_MODEL // LANES
    g = g_ref[...]

    def ext(main, prev, nxt):
        lo = prev[...].astype(F32)[prev.shape[0] - HALO:]
        hi = nxt[...].astype(F32)[:HALO]
        return jnp.concatenate([lo, main[...].astype(F32), hi], axis=0)

    par = step % 2
    mod_h = mod_head[0]
    n_blocks = tm // HEAD_ROWS

    def head_matmul():
        a = ext(am, ap, an).astype(BF16)
        b = ext(bm, bp, bn).astype(BF16)
        half = wo.shape[0] // 2
        y_ref[...] = _dot(a, wo[:half, :]) + _dot(b, wo[half:, :])

    def head_rows(x, y):
        x1 = x + mod_h[2:3] * _rms(y, g[0:1])
        return x1, _rms(x1, g[1:2]) * (1.0 + mod_h[4:5]) + mod_h[3:4]

    def head_block(blk):
        if blk < n_blocks:
            rows = slice(blk * HEAD_ROWS, (blk + 1) * HEAD_ROWS)
            erows = slice(HALO + blk * HEAD_ROWS, HALO + (blk + 1) * HEAD_ROWS)
            x1, h = head_rows(xm[rows, :], y_ref[erows, :])
            x1_ref[par, rows, :] = x1
            for c in range(n_ct):
                h_ref[c, erows, :] = h[:, c * LANES:(c + 1) * LANES]
        else:
            x = jnp.concatenate([xp[...], xn[...]], axis=0)
            y = jnp.concatenate([y_ref[0:HALO, :], y_ref[HALO + tm:2 * HALO + tm, :]], axis=0)
            _, h = head_rows(x, y)
            t = jnp.minimum(step, n_tiles - 1) % tiles_per_seq
            ridx = lax.broadcasted_iota(jnp.int32, h.shape, 0)
            keep = ((ridx >= HALO) | (t != 0)) & ((ridx < HALO) | (t != tiles_per_seq - 1))
            h = jnp.where(keep, h, 0.0)
            for c in range(n_ct):
                h_ref[c, 0:HALO, :] = h[:HALO, c * LANES:(c + 1) * LANES]
                h_ref[c, HALO + tm:2 * HALO + tm, :] = h[HALO:, c * LANES:(c + 1) * LANES]

    def head_permute():
        def group(rows):
            return jnp.concatenate([h_ref[c, rows, :] for c in range(n_ct)], axis=1)
        def put(row0, first, second):
            hp_ref[row0:row0 + 2 * HALO, :] = jnp.concatenate(
                [group(first), group(second)], axis=0).astype(BF16)
        for j in range(0, nj, 2):
            put(j * HALO, pl.ds(HALO + j, HALO, stride=nj), pl.ds(HALO + j + 1, HALO, stride=nj))
        put(tm, pl.ds(0, HALO), pl.ds(HALO + tm, HALO))

    def tail(interleaved):
        mod = mod_tail[0]
        sub = lax.broadcasted_iota(jnp.int32, (HALO, FF_CHUNK), 0)

        def conv(u, cols):
            w = cw[:, cols]
            before = jnp.where(sub == 0, u[tm + HALO - 1:tm + HALO],
                               pltpu.roll(u[tm - HALO:tm], 1, 0))
            after = jnp.where(sub == HALO - 1, u[tm + HALO:tm + HALO + 1],
                              pltpu.roll(u[0:HALO], HALO - 1, 0))
            prev = jnp.concatenate([before, u[0:tm - HALO]], axis=0)
            nxt = jnp.concatenate([u[HALO:tm], after], axis=0)
            return prev * w[0:1] + u[0:tm] * w[1:2] + nxt * w[2:3] + cb[:, cols]

        n_chunks = D_FF // FF_CHUNK

        def up(ch):
            ca = slice(ch * FF_CHUNK, (ch + 1) * FF_CHUNK)
            cg = slice(D_FF + ch * FF_CHUNK, D_FF + (ch + 1) * FF_CHUNK)
            return _dot(hp_ref[...], wu[:, cg]), _dot(hp_ref[...], wu[:, ca])

        nxt_u = up(0)
        for ch in range(n_chunks):
            ug, ua = nxt_u
            if ch + 1 < n_chunks:
                nxt_u = up(ch + 1)
            for blk in interleaved[ch]:
                head_block(blk)
            ca = slice(ch * FF_CHUNK, (ch + 1) * FF_CHUNK)
            cg = slice(D_FF + ch * FF_CHUNK, D_FF + (ch + 1) * FF_CHUNK)
            part = _dot((_silu(conv(ug, cg)) * conv(ua, ca)).astype(BF16), wd[ca, :])
            if ch == 0:
                acc = part
            elif ch + 1 < n_chunks:
                acc = acc + part
            else:
                ffn = mod[5:6] * _rms(acc + part, g[2:3])
                for c in range(n_ct):
                    acc_ref[c] = ffn[:, c * LANES:(c + 1) * LANES]

        for s in range(HALO):
            for jb in range(nj // HALO):
                t0 = s * nj + HALO * jb
                rows = pl.ds(HALO * HALO * jb + s, HALO, stride=HALO)
                back = jnp.concatenate([acc_ref[c, rows, :] for c in range(n_ct)], axis=1)
                o_ref[t0:t0 + HALO, :] = x1_ref[1 - par, t0:t0 + HALO, :] + back

    @pl.when(step == 0)
    def _():
        head_matmul()
        for blk in range(n_blocks + 1):
            head_block(blk)
        head_permute()

    @pl.when(step > 0)
    def _():
        head_matmul()
        n_chunks = D_FF // FF_CHUNK
        tail([[blk for blk in range(n_blocks + 1) if blk * n_chunks // (n_blocks + 1) == ch]
              for ch in range(n_chunks)])
        head_permute()


def _post(x, mix_a, mix_b, mods, gains, wo, wu, cw, cb, wd, layer, seq_len, tm, latent):
    n = x.shape[0]
    tm = min(tm, seq_len)
    nt = n // tm
    (mix_a, col_a), (mix_b, col_b) = mix_a, mix_b
    tiles_per_seq = seq_len // tm
    cond_row = (lambda i: 1 + i // tiles_per_seq) if latent else (lambda i: 0)

    head_tile = lambda i: jnp.minimum(i, nt - 1)
    tail_tile = lambda i: jnp.maximum(i - 1, 0)

    def triple(width, halo_rows, col=0):
        per = tm // halo_rows
        last = n // halo_rows - 1
        return [
            pl.BlockSpec((tm, width), lambda i: (head_tile(i), col)),
            pl.BlockSpec((halo_rows, width),
                         lambda i: (jnp.maximum(head_tile(i) * per - 1, 0), col)),
            pl.BlockSpec((halo_rows, width),
                         lambda i: (jnp.minimum((head_tile(i) + 1) * per, last), col)),
        ]

    const = lambda *shape: pl.BlockSpec((None,) + shape, lambda i: (layer,) + (0,) * len(shape),
                                        pipeline_mode=pl.Buffered(1))
    in_specs = (triple(D_MODEL, HALO) + triple(WIDTH, 2 * HALO, col_a)
                + triple(WIDTH, 2 * HALO, col_b) + [
        pl.BlockSpec((None, 1, 6, D_MODEL), lambda i: (layer, cond_row(head_tile(i)), 0, 0)),
        pl.BlockSpec((None, 1, 6, D_MODEL), lambda i: (layer, cond_row(tail_tile(i)), 0, 0)),
        const(3, D_MODEL),
        const(2 * WIDTH, D_MODEL),
        const(D_MODEL, 2 * D_FF),
        const(3, 2 * D_FF),
        const(1, 2 * D_FF),
        const(D_FF, D_MODEL),
    ])
    return pl.pallas_call(
        functools.partial(_post_kernel, tiles_per_seq=tiles_per_seq, n_tiles=nt),
        grid=(nt + 1,),
        in_specs=in_specs,
        out_specs=pl.BlockSpec((tm, D_MODEL), lambda i: (tail_tile(i), 0)),
        out_shape=jax.ShapeDtypeStruct((n, D_MODEL), F32),
        scratch_shapes=[pltpu.VMEM((D_MODEL // LANES, tm, LANES), F32),
                        pltpu.VMEM((D_MODEL // LANES, tm + 2 * HALO, LANES), F32),
                        pltpu.VMEM((tm + 2 * HALO, D_MODEL), BF16),
                        pltpu.VMEM((2, tm, D_MODEL), F32),
                        pltpu.VMEM((tm + 2 * HALO, D_MODEL), F32)],
        compiler_params=_cparams(1),
        name="post_latent" if latent else "post_context",
    )(x, x, x, mix_a, mix_a, mix_a, mix_b, mix_b, mix_b, mods, mods, gains, wo, wu, cw, cb, wd)


def kernel(x_prompt, x_sample, c, cache_na_k, cache_na_v, state_ret_fwd, state_ret_bwd, c_ctx,
           ada_w, ada_b, g_pre_mix, g_post_mix, g_pre_ffn, g_post_ffn, w_in,
           ret_decay_fwd, ret_decay_bwd, na_rpb, w_out, w_up, conv_w, conv_b, w_down):
    depth = w_in.shape[0]
    batch, seq, _ = x_prompt.shape
    dec_batch, dec_seq, _ = x_sample.shape
    past = cache_na_k.shape[2]
    tm = 512

    cond = jnp.concatenate(
        [c_ctx[None, :], c, jnp.zeros((8 - 1 - dec_batch, D_MODEL), F32)], axis=0)
    mods = _modulation(cond, ada_w, ada_b).reshape(depth, 8, 6, D_MODEL)
    rope_tabs = _rope_tables(dec_seq)
    bias, lo, hi = _na_bias_tables(na_rpb, dec_seq // GRID_W)

    w_in_b, wo_b, wu_b, wd_b = (w.astype(BF16) for w in (w_in, w_out, w_up, w_down))
    g_pre = g_pre_mix.reshape(depth, 1, D_MODEL)
    gains = jnp.stack([g_post_mix, g_pre_ffn, g_post_ffn], axis=1)
    cb = conv_b.reshape(depth, 1, 2 * D_FF)
    post_params = (mods, gains, wo_b, wu_b, conv_w, cb, wd_b)

    y_p = x_prompt.reshape(batch * seq, D_MODEL)
    y_s = x_sample.reshape(dec_batch * dec_seq, D_MODEL)
    new_k = new_v = None
    sfs, sbs = [], []
    for l in range(depth):
        dec = (jnp.repeat(ret_decay_fwd[l], HEAD_DIM)[None, :],
               jnp.repeat(ret_decay_bwd[l], HEAD_DIM)[None, :],
               jnp.repeat(ret_decay_fwd[l], LANES)[None, :],
               jnp.repeat(ret_decay_bwd[l], LANES)[None, :])

        p_c, new_k, new_v = _inproj(y_p, mods, g_pre, w_in_b, l, seq, tm, kv_out=(new_k, new_v))
        mix_c, st_c = _ctx_mixer(p_c, dec, batch, seq)
        y_p = _post(y_p, (mix_c, 0), (mix_c, 1), *post_params, l, seq, tm, latent=False)
        sfs.append(_diag_states(st_c[:, :, 0]))
        sbs.append(_diag_states(st_c[:, :, 1]))

        (p_s,) = _inproj(y_s, mods, g_pre, w_in_b, l, dec_seq, tm, rope_tabs=rope_tabs)
        s0 = _block_diag_states(state_ret_fwd[:, l], state_ret_bwd[:, l])
        ret_s = _lat_retention(p_s, dec, s0, dec_batch, dec_seq)
        ck = cache_na_k[:, l].reshape(dec_batch, past, WIDTH).astype(BF16)
        cv = cache_na_v[:, l].reshape(dec_batch, past, WIDTH).astype(BF16)
        na_s = _neighbourhood_attention(p_s, ck, cv, bias, l, lo, hi, dec_batch, dec_seq)
        y_s = _post(y_s, (ret_s, 0), (na_s, 0), *post_params, l, dec_seq, tm, latent=True)

    return (y_p.reshape(batch, seq, D_MODEL),
            y_s.reshape(dec_batch, dec_seq, D_MODEL),
            new_k.reshape(batch, depth, seq, N_HEADS, HEAD_DIM),
            new_v.reshape(batch, depth, seq, N_HEADS, HEAD_DIM),
            jnp.stack(sfs, axis=1), jnp.stack(sbs, axis=1))
```

```python
import functools

import numpy as np
import jax
import jax.numpy as jnp
from jax import lax
from jax.experimental import pallas as pl
from jax.experimental.pallas import tpu as pltpu

F32 = jnp.float32
BF16 = jnp.bfloat16

D_MODEL = 1024
HEAD_DIM = 64
N_HEADS = 8
HEAD_PAIRS = N_HEADS // 2
LANES = 128
WIDTH = N_HEADS * HEAD_DIM
N_GROUPS = 7
IN_WIDTH = N_GROUPS * WIDTH
D_FF = 2816
FF_CHUNK = 256
HEAD_ROWS = 32
CHUNK = 128
RET_UNROLL = 4
GRID_W = 64
NA_KH = 8
NA_KW = 16
NA_ROWS = 4
ROPE_BASE = 10000.0
EPS = 1e-6
NEG_INF = -1e9
HALO = 8
VMEM_LIMIT = 56 * 1024 * 1024


def _cparams(n_grid):
    return pltpu.CompilerParams(
        dimension_semantics=("arbitrary",) * n_grid, vmem_limit_bytes=VMEM_LIMIT)


def _rms(x, g):
    ms = jnp.mean(x * x, axis=-1, keepdims=True)
    return x * lax.rsqrt(ms + EPS) * g


def _silu(x):
    return x * jax.nn.sigmoid(x)


def _log_sigmoid(x):
    return jnp.minimum(x, 0.0) - jnp.log1p(jnp.exp(-jnp.abs(x)))


def _dot(a, b):
    return jnp.dot(a, b, preferred_element_type=F32)


def _dot_nt(a, b):
    return lax.dot_general(a, b, (((1,), (1,)), ((), ())), preferred_element_type=F32)


def _dot_tn(a, b):
    return lax.dot_general(a, b, (((0,), (0,)), ((), ())), preferred_element_type=F32)


def _mod_kernel(cond_ref, w_ref, b_ref, o_ref):
    s = _silu(cond_ref[...]).astype(BF16)
    o_ref[0] = _dot(s, w_ref[0].astype(BF16)) + b_ref[0]


def _modulation(cond, ada_w, ada_b):
    depth = ada_w.shape[0]
    nb = 6 * D_MODEL // D_MODEL
    return pl.pallas_call(
        _mod_kernel,
        grid=(depth, nb),
        in_specs=[
            pl.BlockSpec((8, D_MODEL), lambda l, j: (0, 0)),
            pl.BlockSpec((1, D_MODEL, D_MODEL), lambda l, j: (l, 0, j)),
            pl.BlockSpec((1, 1, D_MODEL), lambda l, j: (l, 0, j)),
        ],
        out_specs=pl.BlockSpec((1, 8, D_MODEL), lambda l, j: (l, 0, j)),
        out_shape=jax.ShapeDtypeStruct((depth, 8, 6 * D_MODEL), F32),
        compiler_params=_cparams(2),
        name="modulation",
    )(cond, ada_w, ada_b.reshape(depth, 1, 6 * D_MODEL))


def _inproj_kernel(*refs, rope, emit_kv, n_alias, layer):
    x_ref, mod_ref, g_ref, w_ref = refs[:4]
    refs = refs[4:]
    if rope:
        cos_ref, sin_up_ref, sin_dn_ref = refs[:3]
        refs = refs[3:]
    refs = refs[n_alias:]
    p_ref = refs[0]
    mod = mod_ref[0]
    h = (_rms(x_ref[...], g_ref[...]) * (1.0 + mod[1:2]) + mod[0:1]).astype(BF16)
    for g in range(N_GROUPS):
        cols = slice(g * WIDTH, (g + 1) * WIDTH)
        pg = _dot(h, w_ref[:, cols])
        if rope and g < 2:
            parts = []
            for j in range(WIDTH // LANES):
                xj = pg[:, j * LANES:(j + 1) * LANES]
                parts.append(xj * cos_ref[...]
                             + pltpu.roll(xj, 16, 1) * sin_up_ref[...]
                             + pltpu.roll(xj, LANES - 16, 1) * sin_dn_ref[...])
            pg = jnp.concatenate(parts, axis=1)
        if g in (0, 4):
            pg = pg * (HEAD_DIM ** -0.5)
        p_ref[:, cols] = pg.astype(BF16)
        if emit_kv and g >= 5:
            kv_ref = refs[g - 4]
            seq = kv_ref.shape[-2]
            for j in range(kv_ref.shape[0]):
                if n_alias:
                    kv_ref[j] = pg[j * seq:(j + 1) * seq]
                else:
                    for l in range(kv_ref.shape[1]):
                        kv_ref[j, l] = (pg[j * seq:(j + 1) * seq] if l == layer
                                        else jnp.zeros((seq, WIDTH), F32))


def _inproj(x, mods, g_pre, w, layer, seq_len, tm, rope_tabs=None, kv_out=None):
    n = x.shape[0]
    depth = w.shape[0]
    tiles_per_seq = max(seq_len // tm, 1)
    cond_row = (lambda i: 0) if rope_tabs is None else (lambda i: 1 + i // tiles_per_seq)
    in_specs = [
        pl.BlockSpec((tm, D_MODEL), lambda i: (i, 0)),
        pl.BlockSpec((None, 1, 6, D_MODEL), lambda i: (layer, cond_row(i), 0, 0)),
        pl.BlockSpec((None, 1, D_MODEL), lambda i: (layer, 0, 0)),
        pl.BlockSpec((None, D_MODEL, IN_WIDTH), lambda i: (layer, 0, 0),
                     pipeline_mode=pl.Buffered(1)),
    ]
    args = [x, mods, g_pre, w]
    if rope_tabs is not None:
        in_specs += [pl.BlockSpec((tm, LANES), lambda i: (i % tiles_per_seq, 0))] * 3
        args += list(rope_tabs)
    out_specs = [pl.BlockSpec((tm, IN_WIDTH), lambda i: (i, 0))]
    out_shape = [jax.ShapeDtypeStruct((n, IN_WIDTH), BF16)]
    aliases = {}
    if kv_out is not None:
        seqs_per_tile = tm // seq_len
        kv_shape = jax.ShapeDtypeStruct((n // seq_len, depth, seq_len, WIDTH), F32)
        assert (kv_out[0] is None) == (kv_out[1] is None)
        for j, prev in enumerate(kv_out):
            if prev is not None:
                aliases[len(args)] = 1 + j
                in_specs.append(pl.BlockSpec(memory_space=pl.ANY))
                args.append(prev)
                out_specs.append(pl.BlockSpec((seqs_per_tile, None, seq_len, WIDTH),
                                              lambda i: (i, layer, 0, 0)))
            else:
                out_specs.append(pl.BlockSpec((seqs_per_tile, depth, seq_len, WIDTH),
                                              lambda i: (i, 0, 0, 0)))
            out_shape.append(kv_shape)
    return pl.pallas_call(
        functools.partial(_inproj_kernel, rope=rope_tabs is not None,
                          emit_kv=kv_out is not None, n_alias=len(aliases), layer=layer),
        grid=(n // tm,),
        in_specs=in_specs,
        out_specs=out_specs,
        out_shape=out_shape,
        input_output_aliases=aliases,
        compiler_params=_cparams(1),
        name="inproj_latent" if rope_tabs is not None else "inproj_context",
    )(*args)


def _rope_tables(seq_len):
    t = np.arange(seq_len)
    lane = np.arange(LANES)
    d = lane % HEAD_DIM
    pos = np.where(d[None, :] < HEAD_DIM // 2, (t // GRID_W)[:, None], (t % GRID_W)[:, None])
    pos = pos.astype(np.float32)
    half = HEAD_DIM // 2
    inv = np.power(np.float32(ROPE_BASE), -np.arange(0, half, 2, dtype=np.float32) / half)
    ang = pos * inv[d % (half // 2)][None, :]
    cos, sin = np.cos(ang), np.sin(ang)
    upper = (d % half) >= half // 2
    sin_up = np.where(upper[None, :], sin, 0.0)
    sin_dn = np.where(upper[None, :], 0.0, -sin)
    return (jnp.asarray(cos, F32), jnp.asarray(sin_up, F32), jnp.asarray(sin_dn, F32))


def _retention_tables(dec_f, dec_b, dec_f2, dec_b2):
    lgf, lgb = _log_sigmoid(dec_f), _log_sigmoid(dec_b)
    pos = lax.broadcasted_iota(jnp.int32, (CHUNK, LANES), 0).astype(F32)
    tabs = dict(
        qdf=jnp.exp(lgf * (pos + 1.0)), kdf=jnp.exp(lgf * (CHUNK - 1.0 - pos)),
        cdf=jnp.exp(lgf * float(CHUNK)),
        qdb=jnp.exp(lgb * (CHUNK - pos)), kdb=jnp.exp(lgb * pos),
        cdb=jnp.exp(lgb * float(CHUNK)),
    )
    lgf2, lgb2 = _log_sigmoid(dec_f2), _log_sigmoid(dec_b2)
    i = lax.broadcasted_iota(jnp.int32, (CHUNK, 2 * CHUNK), 0)
    j = lax.broadcasted_iota(jnp.int32, (CHUNK, 2 * CHUNK), 1) & (CHUNK - 1)
    diff = (i - j).astype(F32)
    tabs["decay"] = (jnp.where(diff >= 0, jnp.exp(lgf2 * jnp.maximum(diff, 0.0)), 0.0)
                     + jnp.where(diff <= 0, jnp.exp(lgb2 * jnp.maximum(-diff, 0.0)), 0.0))
    lane = lax.broadcasted_iota(jnp.int32, (1, LANES), 1)
    tabs["head_a"] = lane < HEAD_DIM
    r = lax.broadcasted_iota(jnp.int32, (2 * LANES, LANES), 0) & (LANES - 1)
    c = lax.broadcasted_iota(jnp.int32, (2 * LANES, LANES), 1)
    tabs["same_head"] = (r < HEAD_DIM) == (c < HEAD_DIM)
    return tabs


def _split_heads(x, head_a, axis):
    zero = jnp.zeros_like(x)
    return jnp.concatenate([jnp.where(head_a, x, zero), jnp.where(head_a, zero, x)], axis=axis)


def _chunk_kv(k2, v2, tabs):
    kf = k2.astype(F32)
    kk = jnp.concatenate([kf * tabs["kdf"], kf * tabs["kdb"]], axis=1).astype(BF16)
    return jnp.where(tabs["same_head"], _dot_tn(kk, v2), 0.0)


def _chunk_out(q2, k2, v2, g2, state, tabs):
    head_a = tabs["head_a"]
    s = _dot_nt(q2, _split_heads(k2, head_a, 0))
    p = (s * tabs["decay"]).astype(BF16)
    qf = q2.astype(F32)
    lhs = jnp.concatenate(
        [p, (qf * tabs["qdf"]).astype(BF16), (qf * tabs["qdb"]).astype(BF16)], axis=1)
    rhs = jnp.concatenate([_split_heads(v2, head_a, 0), state], axis=0)
    o = _dot(lhs, rhs)
    inv = 1.0 / HEAD_DIM
    sum_a = jnp.sum(jnp.where(head_a, o, 0.0), axis=-1, keepdims=True)
    sum_b = jnp.sum(jnp.where(head_a, 0.0, o), axis=-1, keepdims=True)
    d = o - jnp.where(head_a, sum_a, sum_b) * inv
    d2 = d * d
    var_a = jnp.sum(jnp.where(head_a, d2, 0.0), axis=-1, keepdims=True)
    var_b = jnp.sum(jnp.where(head_a, 0.0, d2), axis=-1, keepdims=True)
    o = d * lax.rsqrt(jnp.where(head_a, var_a, var_b) * inv + EPS)
    return o * _silu(g2.astype(F32))


def _ctx_mixer_kernel(rq, rk, rv, rg, nq, nk, nv, df, db, df2, db2, mix_ref, st_ref):
    seq = rq.shape[0]
    nc = seq // CHUNK
    for hp in range(HEAD_PAIRS):
        cols = slice(hp * LANES, (hp + 1) * LANES)
        cols2 = slice(hp * 2 * LANES, (hp + 1) * 2 * LANES)
        tabs = _retention_tables(df[:, cols], db[:, cols], df2[:, cols2], db2[:, cols2])
        rows = [slice(c * CHUNK, (c + 1) * CHUNK) for c in range(nc)]
        kv = [_chunk_kv(rk[r, cols], rv[r, cols], tabs) for r in rows]
        sf = [jnp.zeros((LANES, LANES), F32)]
        for c in range(nc):
            sf.append(sf[-1] * tabs["cdf"] + kv[c][:LANES])
        sb = [jnp.zeros((LANES, LANES), F32)]
        for c in reversed(range(nc)):
            sb.append(sb[-1] * tabs["cdb"] + kv[c][LANES:])
        for c in range(nc):
            state = jnp.concatenate([sf[c], sb[nc - 1 - c]], axis=0).astype(BF16)
            o = _chunk_out(rq[rows[c], cols], rk[rows[c], cols], rv[rows[c], cols],
                           rg[rows[c], cols], state, tabs)
            mix_ref[rows[c], cols] = o.astype(BF16)
        st_ref[0, hp, 0] = sf[nc]
        st_ref[0, hp, 1] = sb[nc]

        head_a = tabs["head_a"]
        s = _dot_nt(nq[:, cols], _split_heads(nk[:, cols], head_a, 0))
        es, rinv = [], []
        for h in range(2):
            sh = s[:, h * seq:(h + 1) * seq]
            e = jnp.exp(sh - jnp.max(sh, axis=-1, keepdims=True))
            rinv.append(1.0 / jnp.sum(e, axis=-1, keepdims=True))
            es.append(e.astype(BF16))
        o = _dot(jnp.concatenate(es, axis=1), _split_heads(nv[:, cols], head_a, 0))
        o = o * jnp.where(head_a, rinv[0], rinv[1])
        mix_ref[:, WIDTH + hp * LANES:WIDTH + (hp + 1) * LANES] = o.astype(BF16)


def _ctx_mixer(p, dec, batch, seq):
    group = lambda g: pl.BlockSpec((seq, WIDTH), lambda b, g=g: (b, g))
    vec = lambda w: pl.BlockSpec((1, w), lambda b: (0, 0))
    return pl.pallas_call(
        _ctx_mixer_kernel,
        grid=(batch,),
        in_specs=[group(g) for g in range(N_GROUPS)] + [vec(WIDTH), vec(WIDTH),
                                                         vec(2 * WIDTH), vec(2 * WIDTH)],
        out_specs=[
            pl.BlockSpec((seq, 2 * WIDTH), lambda b: (b, 0)),
            pl.BlockSpec((1, HEAD_PAIRS, 2, LANES, LANES), lambda b: (b, 0, 0, 0, 0)),
        ],
        out_shape=[
            jax.ShapeDtypeStruct((batch * seq, 2 * WIDTH), BF16),
            jax.ShapeDtypeStruct((batch, HEAD_PAIRS, 2, LANES, LANES), F32),
        ],
        compiler_params=_cparams(1),
        name="context_mixer",
    )(*([p] * N_GROUPS), *dec)


def _lat_retention_kernel(q_ref, k_ref, v_ref, g_ref, df, db, df2, db2, s0_ref, o_ref,
                          kv_ref, st_ref):
    nc = q_ref.shape[0] // CHUNK
    tabs = _retention_tables(df[...], db[...], df2[...], db2[...])

    def rows(c):
        return pl.ds(pl.multiple_of(c * CHUNK, CHUNK), CHUNK)

    def kv_body(c, carry):
        kv_ref[c] = _chunk_kv(k_ref[rows(c), :], v_ref[rows(c), :], tabs)
        return carry
    lax.fori_loop(0, nc, kv_body, 0, unroll=RET_UNROLL)

    def fwd_body(c, s):
        st_ref[c, :LANES, :] = s.astype(BF16)
        return s * tabs["cdf"] + kv_ref[c, :LANES, :]
    lax.fori_loop(0, nc, fwd_body, s0_ref[0, 0, 0])

    def bwd_body(i, s):
        c = nc - 1 - i
        st_ref[c, LANES:, :] = s.astype(BF16)
        return s * tabs["cdb"] + kv_ref[c, LANES:, :]
    lax.fori_loop(0, nc, bwd_body, s0_ref[0, 0, 1])

    def out_body(c, carry):
        r = rows(c)
        o = _chunk_out(q_ref[r, :], k_ref[r, :], v_ref[r, :], g_ref[r, :], st_ref[c], tabs)
        o_ref[r, :] = o.astype(BF16)
        return carry
    lax.fori_loop(0, nc, out_body, 0, unroll=RET_UNROLL)


def _lat_retention(p, dec, s0, batch, seq):
    nc = seq // CHUNK
    group = lambda g: pl.BlockSpec((seq, LANES), lambda b, h, g=g: (b, g * HEAD_PAIRS + h))
    vec = lambda w: pl.BlockSpec((1, w), lambda b, h: (0, h))
    return pl.pallas_call(
        _lat_retention_kernel,
        grid=(batch, HEAD_PAIRS),
        in_specs=[group(g) for g in range(4)] + [vec(LANES), vec(LANES), vec(2 * LANES),
                                                 vec(2 * LANES)]
        + [pl.BlockSpec((1, 1, 2, LANES, LANES), lambda b, h: (b, h, 0, 0, 0))],
        out_specs=pl.BlockSpec((seq, LANES), lambda b, h: (b, h)),
        out_shape=jax.ShapeDtypeStruct((batch * seq, WIDTH), BF16),
        scratch_shapes=[pltpu.VMEM((nc, 2 * LANES, LANES), F32),
                        pltpu.VMEM((nc, 2 * LANES, LANES), BF16)],
        compiler_params=_cparams(2),
        name="latent_retention",
    )(*([p] * 4), *dec, s0)


def _block_diag_states(s_f, s_b):
    def bd(s):
        b = s.shape[0]
        s = s.reshape(b, HEAD_PAIRS, 2, HEAD_DIM, HEAD_DIM)
        z = jnp.zeros_like(s[:, :, 0])
        top = jnp.concatenate([s[:, :, 0], z], axis=-1)
        bot = jnp.concatenate([z, s[:, :, 1]], axis=-1)
        return jnp.concatenate([top, bot], axis=-2)
    return jnp.stack([bd(s_f), bd(s_b)], axis=2)


def _diag_states(st):
    b = st.shape[0]
    a = st[:, :, :HEAD_DIM, :HEAD_DIM]
    c = st[:, :, HEAD_DIM:, HEAD_DIM:]
    return jnp.stack([a, c], axis=2).reshape(b, N_HEADS, HEAD_DIM, HEAD_DIM)


def _na_kernel(q_ref, k_ref, v_ref, ck_ref, cv_ref, *refs):
    bias_refs, o_ref = refs[:NA_ROWS], refs[NA_ROWS]
    rows_total = k_ref.shape[0] // GRID_W
    lane = lax.broadcasted_iota(jnp.int32, (1, LANES), 1)
    head_a = lane < HEAD_DIM
    wins = []
    for i in range(NA_ROWS):
        r = pl.program_id(1) * NA_ROWS + i
        rs = jnp.clip(r - NA_KH // 2, 0, rows_total - NA_KH)
        wins.append(pl.ds(pl.multiple_of(rs * GRID_W, GRID_W), NA_KH * GRID_W))
    for hp in range(HEAD_PAIRS):
        cols = slice(hp * LANES, (hp + 1) * LANES)
        qq = jnp.concatenate(
            [_split_heads(q_ref[i * GRID_W:(i + 1) * GRID_W, cols], head_a, 0)
             for i in range(NA_ROWS)], axis=0)
        s_ctx = _dot_nt(qq, ck_ref[0, :, cols])
        m_ctx = jnp.max(s_ctx, axis=-1, keepdims=True)
        s_loc, m = [], []
        for i in range(NA_ROWS):
            blk = slice(i * 2 * GRID_W, (i + 1) * 2 * GRID_W)
            s = _dot_nt(qq[blk], k_ref[wins[i], cols]) + bias_refs[i][0, hp]
            s_loc.append(s)
            m.append(jnp.maximum(jnp.max(s, axis=-1, keepdims=True), m_ctx[blk]))
        e_ctx = jnp.exp(s_ctx - jnp.concatenate(m, axis=0))
        den_ctx = jnp.sum(e_ctx, axis=-1, keepdims=True)
        o_ctx = _dot(e_ctx.astype(BF16), cv_ref[0, :, cols])
        for i in range(NA_ROWS):
            blk = slice(i * 2 * GRID_W, (i + 1) * 2 * GRID_W)
            e = jnp.exp(s_loc[i] - m[i])
            den = jnp.sum(e, axis=-1, keepdims=True) + den_ctx[blk]
            o = (_dot(e.astype(BF16), v_ref[wins[i], cols]) + o_ctx[blk]) / den
            o_ref[i * GRID_W:(i + 1) * GRID_W, cols] = jnp.where(
                head_a, o[:GRID_W], o[GRID_W:]).astype(BF16)


def _bias_build_kernel(rpb_ref, o_ref, *, dr_first, n_dr):
    qc = lax.broadcasted_iota(jnp.int32, (GRID_W, LANES), 0)
    lane = lax.broadcasted_iota(jnp.int32, (GRID_W, LANES), 1)
    kc = lane & (GRID_W - 1)
    cs = jnp.clip(qc - NA_KW // 2, 0, GRID_W - NA_KW)
    inside = (kc >= cs) & (kc < cs + NA_KW)
    first = lane < GRID_W
    for h in range(N_HEADS):
        lo_half, hi_half = [], []
        for dr in range(n_dr):
            line = jnp.broadcast_to(rpb_ref[pl.ds(h * n_dr + dr, 1), :], (GRID_W, LANES))
            lo_half.append(pltpu.roll(line, LANES - (NA_KW - 1), 1, stride=1, stride_axis=0))
            hi_half.append(pltpu.roll(line, GRID_W - (NA_KW - 1), 1, stride=1, stride_axis=0))
        for cls, dr0 in enumerate(dr_first):
            for jp in range(NA_KH // 2):
                dr = dr0 + 2 * jp
                tile = jnp.where(inside, jnp.where(first, lo_half[dr], hi_half[dr + 1]), NEG_INF)
                o_ref[cls, h // 2, (h % 2) * GRID_W:(h % 2 + 1) * GRID_W,
                      jp * LANES:(jp + 1) * LANES] = tile


def _na_bias_tables(na_rpb, rows_total):
    depth, heads, n_dr, n_dc = na_rpb.shape
    kh = NA_KH
    lo, hi = kh // 2, rows_total - kh // 2 - 1
    reps = list(range(lo)) + [lo] + list(range(hi + 1, rows_total))
    dr_first = tuple(int(np.clip(r - kh // 2, 0, rows_total - kh)) - r + kh - 1 for r in reps)
    lines = jnp.pad(na_rpb.reshape(depth, heads * n_dr, n_dc), ((0, 0), (0, 0), (0, LANES - n_dc)))
    bias = pl.pallas_call(
        functools.partial(_bias_build_kernel, dr_first=dr_first, n_dr=n_dr),
        grid=(depth,),
        in_specs=[pl.BlockSpec((None, heads * n_dr, LANES), lambda l: (l, 0, 0))],
        out_specs=pl.BlockSpec((None, len(reps), HEAD_PAIRS, 2 * GRID_W, kh * GRID_W),
                               lambda l: (l, 0, 0, 0, 0)),
        out_shape=jax.ShapeDtypeStruct(
            (depth, len(reps), HEAD_PAIRS, 2 * GRID_W, kh * GRID_W), F32),
        compiler_params=_cparams(1),
        name="na_bias_build",
    )(lines)
    return bias, lo, hi


def _neighbourhood_attention(p, ctx_k, ctx_v, bias, layer, lo, hi, batch, seq):
    rows_total = seq // GRID_W
    past = ctx_k.shape[1]

    def cls(r):
        return jnp.where(r < lo, r, jnp.where(r > hi, r - hi + lo, lo))

    groups = rows_total // NA_ROWS
    bias_spec = lambda i: pl.BlockSpec(
        (None, 1, HEAD_PAIRS, 2 * GRID_W, NA_KH * GRID_W),
        lambda b, g: (layer, cls(g * NA_ROWS + i), 0, 0, 0))
    return pl.pallas_call(
        _na_kernel,
        grid=(batch, groups),
        in_specs=[
            pl.BlockSpec((NA_ROWS * GRID_W, WIDTH), lambda b, g: (b * groups + g, 4)),
            pl.BlockSpec((seq, WIDTH), lambda b, g: (b, 5)),
            pl.BlockSpec((seq, WIDTH), lambda b, g: (b, 6)),
            pl.BlockSpec((1, past, WIDTH), lambda b, g: (b, 0, 0)),
            pl.BlockSpec((1, past, WIDTH), lambda b, g: (b, 0, 0)),
        ] + [bias_spec(i) for i in range(NA_ROWS)],
        out_specs=pl.BlockSpec((NA_ROWS * GRID_W, WIDTH), lambda b, g: (b * groups + g, 0)),
        out_shape=jax.ShapeDtypeStruct((batch * seq, WIDTH), BF16),
        compiler_params=_cparams(2),
        name="neighbourhood_attention",
    )(p, p, p, ctx_k, ctx_v, *([bias] * NA_ROWS))


def _post_kernel(xm, xp, xn, am, ap, an, bm, bp, bn, mod_head, mod_tail, g_ref, wo, wu, cw, cb, wd,
                 o_ref, acc_ref, h_ref, hp_ref, x1_ref, y_ref, act_ref, *, tiles_per_seq, n_tiles):
    step = pl.program_id(0)
    tm = xm.shape[0]
    nj = tm // HALO
    n_ct = D_MODEL // LANES
    g = g_ref[...]

    def ext(main, prev, nxt):
        lo = prev[...].astype(F32)[prev.shape[0] - HALO:]
        hi = nxt[...].astype(F32)[:HALO]
        return jnp.concatenate([lo, main[...].astype(F32), hi], axis=0)

    par = step % 2
    mod_h = mod_head[0]
    n_blocks = tm // HEAD_ROWS

    def head_matmul():
        mixed = jnp.concatenate([ext(am, ap, an), ext(bm, bp, bn)], axis=1).astype(BF16)
        y_ref[...] = _dot(mixed, wo[...])

    def head_rows(x, y):
        x1 = x + mod_h[2:3] * _rms(y, g[0:1])
        return x1, _rms(x1, g[1:2]) * (1.0 + mod_h[4:5]) + mod_h[3:4]

    def head_block(blk):
        if blk < n_blocks:
            rows = slice(blk * HEAD_ROWS, (blk + 1) * HEAD_ROWS)
            erows = slice(HALO + blk * HEAD_ROWS, HALO + (blk + 1) * HEAD_ROWS)
            x1, h = head_rows(xm[rows, :], y_ref[erows, :])
            x1_ref[par, rows, :] = x1
            for c in range(n_ct):
                h_ref[c, erows, :] = h[:, c * LANES:(c + 1) * LANES]
        else:
            x = jnp.concatenate([xp[...], xn[...]], axis=0)
            y = jnp.concatenate([y_ref[0:HALO, :], y_ref[HALO + tm:2 * HALO + tm, :]], axis=0)
            _, h = head_rows(x, y)
            t = jnp.minimum(step, n_tiles - 1) % tiles_per_seq
            ridx = lax.broadcasted_iota(jnp.int32, h.shape, 0)
            keep = ((ridx >= HALO) | (t != 0)) & ((ridx < HALO) | (t != tiles_per_seq - 1))
            h = jnp.where(keep, h, 0.0)
            for c in range(n_ct):
                h_ref[c, 0:HALO, :] = h[:HALO, c * LANES:(c + 1) * LANES]
                h_ref[c, HALO + tm:2 * HALO + tm, :] = h[HALO:, c * LANES:(c + 1) * LANES]

    def head_permute():
        def group(rows):
            return jnp.concatenate([h_ref[c, rows, :] for c in range(n_ct)], axis=1)
        def put(row0, first, second):
            hp_ref[row0:row0 + 2 * HALO, :] = jnp.concatenate(
                [group(first), group(second)], axis=0).astype(BF16)
        for j in range(0, nj, 2):
            put(j * HALO, pl.ds(HALO + j, HALO, stride=nj), pl.ds(HALO + j + 1, HALO, stride=nj))
        put(tm, pl.ds(0, HALO), pl.ds(HALO + tm, HALO))

    def tail(interleaved):
        mod = mod_tail[0]
        sub = lax.broadcasted_iota(jnp.int32, (HALO, FF_CHUNK), 0)

        def conv(u, cols):
            w = cw[:, cols]
            before = jnp.where(sub == 0, u[tm + HALO - 1:tm + HALO],
                               pltpu.roll(u[tm - HALO:tm], 1, 0))
            after = jnp.where(sub == HALO - 1, u[tm + HALO:tm + HALO + 1],
                              pltpu.roll(u[0:HALO], HALO - 1, 0))
            prev = jnp.concatenate([before, u[0:tm - HALO]], axis=0)
            nxt = jnp.concatenate([u[HALO:tm], after], axis=0)
            return prev * w[0:1] + u[0:tm] * w[1:2] + nxt * w[2:3] + cb[:, cols]

        n_chunks = D_FF // FF_CHUNK

        def up(ch):
            ca = slice(ch * FF_CHUNK, (ch + 1) * FF_CHUNK)
            cg = slice(D_FF + ch * FF_CHUNK, D_FF + (ch + 1) * FF_CHUNK)
            return _dot(hp_ref[...], wu[:, cg]), _dot(hp_ref[...], wu[:, ca])

        nxt_u = up(0)
        for ch in range(n_chunks):
            ug, ua = nxt_u
            if ch + 1 < n_chunks:
                nxt_u = up(ch + 1)
            for blk in interleaved[ch]:
                head_block(blk)
            ca = slice(ch * FF_CHUNK, (ch + 1) * FF_CHUNK)
            cg = slice(D_FF + ch * FF_CHUNK, D_FF + (ch + 1) * FF_CHUNK)
            act_ref[:, ca] = (_silu(conv(ug, cg)) * conv(ua, ca)).astype(BF16)
        ffn = mod[5:6] * _rms(_dot(act_ref[...], wd[...]), g[2:3])
        for c in range(n_ct):
            acc_ref[c] = ffn[:, c * LANES:(c + 1) * LANES]

        for s in range(HALO):
            for jb in range(nj // HALO):
                t0 = s * nj + HALO * jb
                rows = pl.ds(HALO * HALO * jb + s, HALO, stride=HALO)
                back = jnp.concatenate([acc_ref[c, rows, :] for c in range(n_ct)], axis=1)
                o_ref[t0:t0 + HALO, :] = x1_ref[1 - par, t0:t0 + HALO, :] + back

    @pl.when(step == 0)
    def _():
        head_matmul()
        for blk in range(n_blocks + 1):
            head_block(blk)
        head_permute()

    @pl.when(step > 0)
    def _():
        head_matmul()
        n_chunks = D_FF // FF_CHUNK
        tail([[blk for blk in range(n_blocks + 1) if blk * n_chunks // (n_blocks + 1) == ch]
              for ch in range(n_chunks)])
        head_permute()


def _post(x, mix_a, mix_b, mods, gains, wo, wu, cw, cb, wd, layer, seq_len, tm, latent):
    n = x.shape[0]
    tm = min(tm, seq_len)
    nt = n // tm
    (mix_a, col_a), (mix_b, col_b) = mix_a, mix_b
    tiles_per_seq = seq_len // tm
    cond_row = (lambda i: 1 + i // tiles_per_seq) if latent else (lambda i: 0)

    head_tile = lambda i: jnp.minimum(i, nt - 1)
    tail_tile = lambda i: jnp.maximum(i - 1, 0)

    def triple(width, halo_rows, col=0):
        per = tm // halo_rows
        last = n // halo_rows - 1
        return [
            pl.BlockSpec((tm, width), lambda i: (head_tile(i), col)),
            pl.BlockSpec((halo_rows, width),
                         lambda i: (jnp.maximum(head_tile(i) * per - 1, 0), col)),
            pl.BlockSpec((halo_rows, width),
                         lambda i: (jnp.minimum((head_tile(i) + 1) * per, last), col)),
        ]

    const = lambda *shape: pl.BlockSpec((None,) + shape, lambda i: (layer,) + (0,) * len(shape),
                                        pipeline_mode=pl.Buffered(1))
    in_specs = (triple(D_MODEL, HALO) + triple(WIDTH, 2 * HALO, col_a)
                + triple(WIDTH, 2 * HALO, col_b) + [
        pl.BlockSpec((None, 1, 6, D_MODEL), lambda i: (layer, cond_row(head_tile(i)), 0, 0)),
        pl.BlockSpec((None, 1, 6, D_MODEL), lambda i: (layer, cond_row(tail_tile(i)), 0, 0)),
        const(3, D_MODEL),
        const(2 * WIDTH, D_MODEL),
        const(D_MODEL, 2 * D_FF),
        const(3, 2 * D_FF),
        const(1, 2 * D_FF),
        const(D_FF, D_MODEL),
    ])
    return pl.pallas_call(
        functools.partial(_post_kernel, tiles_per_seq=tiles_per_seq, n_tiles=nt),
        grid=(nt + 1,),
        in_specs=in_specs,
        out_specs=pl.BlockSpec((tm, D_MODEL), lambda i: (tail_tile(i), 0)),
        out_shape=jax.ShapeDtypeStruct((n, D_MODEL), F32),
        scratch_shapes=[pltpu.VMEM((D_MODEL // LANES, tm, LANES), F32),
                        pltpu.VMEM((D_MODEL // LANES, tm + 2 * HALO, LANES), F32),
                        pltpu.VMEM((tm + 2 * HALO, D_MODEL), BF16),
                        pltpu.VMEM((2, tm, D_MODEL), F32),
                        pltpu.VMEM((tm + 2 * HALO, D_MODEL), F32),
                        pltpu.VMEM((tm, D_FF), BF16)],
        compiler_params=_cparams(1),
        name="post_latent" if latent else "post_context",
    )(x, x, x, mix_a, mix_a, mix_a, mix_b, mix_b, mix_b, mods, mods, gains, wo, wu, cw, cb, wd)


def kernel(x_prompt, x_sample, c, cache_na_k, cache_na_v, state_ret_fwd, state_ret_bwd, c_ctx,
           ada_w, ada_b, g_pre_mix, g_post_mix, g_pre_ffn, g_post_ffn, w_in,
           ret_decay_fwd, ret_decay_bwd, na_rpb, w_out, w_up, conv_w, conv_b, w_down):
    depth = w_in.shape[0]
    batch, seq, _ = x_prompt.shape
    dec_batch, dec_seq, _ = x_sample.shape
    past = cache_na_k.shape[2]
    tm = 512

    cond = jnp.concatenate(
        [c_ctx[None, :], c, jnp.zeros((8 - 1 - dec_batch, D_MODEL), F32)], axis=0)
    mods = _modulation(cond, ada_w, ada_b).reshape(depth, 8, 6, D_MODEL)
    rope_tabs = _rope_tables(dec_seq)
    bias, lo, hi = _na_bias_tables(na_rpb, dec_seq // GRID_W)

    w_in_b, wo_b, wu_b, wd_b = (w.astype(BF16) for w in (w_in, w_out, w_up, w_down))
    g_pre = g_pre_mix.reshape(depth, 1, D_MODEL)
    gains = jnp.stack([g_post_mix, g_pre_ffn, g_post_ffn], axis=1)
    cb = conv_b.reshape(depth, 1, 2 * D_FF)
    post_params = (mods, gains, wo_b, wu_b, conv_w, cb, wd_b)

    y_p = x_prompt.reshape(batch * seq, D_MODEL)
    y_s = x_sample.reshape(dec_batch * dec_seq, D_MODEL)
    new_k = new_v = None
    sfs, sbs = [], []
    for l in range(depth):
        dec = (jnp.repeat(ret_decay_fwd[l], HEAD_DIM)[None, :],
               jnp.repeat(ret_decay_bwd[l], HEAD_DIM)[None, :],
               jnp.repeat(ret_decay_fwd[l], LANES)[None, :],
               jnp.repeat(ret_decay_bwd[l], LANES)[None, :])

        p_c, new_k, new_v = _inproj(y_p, mods, g_pre, w_in_b, l, seq, tm, kv_out=(new_k, new_v))
        mix_c, st_c = _ctx_mixer(p_c, dec, batch, seq)
        y_p = _post(y_p, (mix_c, 0), (mix_c, 1), *post_params, l, seq, tm, latent=False)
        sfs.append(_diag_states(st_c[:, :, 0]))
        sbs.append(_diag_states(st_c[:, :, 1]))

        (p_s,) = _inproj(y_s, mods, g_pre, w_in_b, l, dec_seq, tm, rope_tabs=rope_tabs)
        s0 = _block_diag_states(state_ret_fwd[:, l], state_ret_bwd[:, l])
        ret_s = _lat_retention(p_s, dec, s0, dec_batch, dec_seq)
        ck = cache_na_k[:, l].reshape(dec_batch, past, WIDTH).astype(BF16)
        cv = cache_na_v[:, l].reshape(dec_batch, past, WIDTH).astype(BF16)
        na_s = _neighbourhood_attention(p_s, ck, cv, bias, l, lo, hi, dec_batch, dec_seq)
        y_s = _post(y_s, (ret_s, 0), (na_s, 0), *post_params, l, dec_seq, tm, latent=True)

    return (y_p.reshape(batch, seq, D_MODEL),
            y_s.reshape(dec_batch, dec_seq, D_MODEL),
            new_k.reshape(batch, depth, seq, N_HEADS, HEAD_DIM),
            new_v.reshape(batch, depth, seq, N_HEADS, HEAD_DIM),
            jnp.stack(sfs, axis=1), jnp.stack(sbs, axis=1))
```

```python
import functools

import numpy as np
import jax
import jax.numpy as jnp
from jax import lax
from jax.experimental import pallas as pl
from jax.experimental.pallas import tpu as pltpu

F32 = jnp.float32
BF16 = jnp.bfloat16

D_MODEL = 1024
HEAD_DIM = 64
N_HEADS = 8
HEAD_PAIRS = N_HEADS // 2
LANES = 128
WIDTH = N_HEADS * HEAD_DIM
N_GROUPS = 7
IN_WIDTH = N_GROUPS * WIDTH
D_FF = 2816
FF_CHUNK = 256
HEAD_ROWS = 32
CHUNK = 128
RET_UNROLL = 8
GRID_W = 64
NA_KH = 8
NA_KW = 16
NA_ROWS = 8
ROPE_BASE = 10000.0
EPS = 1e-6
NEG_INF = -1e9
HALO = 8
VMEM_LIMIT = 56 * 1024 * 1024


def _cparams(n_grid):
    return pltpu.CompilerParams(
        dimension_semantics=("arbitrary",) * n_grid, vmem_limit_bytes=VMEM_LIMIT)


def _rms(x, g):
    ms = jnp.mean(x * x, axis=-1, keepdims=True)
    return x * lax.rsqrt(ms + EPS) * g


def _silu(x):
    return x * jax.nn.sigmoid(x)


def _log_sigmoid(x):
    return jnp.minimum(x, 0.0) - jnp.log1p(jnp.exp(-jnp.abs(x)))


def _dot(a, b):
    return jnp.dot(a, b, preferred_element_type=F32)


def _dot_nt(a, b):
    return lax.dot_general(a, b, (((1,), (1,)), ((), ())), preferred_element_type=F32)


def _dot_tn(a, b):
    return lax.dot_general(a, b, (((0,), (0,)), ((), ())), preferred_element_type=F32)


def _mod_kernel(cond_ref, w_ref, b_ref, o_ref):
    s = _silu(cond_ref[...]).astype(BF16)
    o_ref[0] = _dot(s, w_ref[0].astype(BF16)) + b_ref[0]


def _modulation(cond, ada_w, ada_b):
    depth = ada_w.shape[0]
    nb = 6 * D_MODEL // D_MODEL
    return pl.pallas_call(
        _mod_kernel,
        grid=(depth, nb),
        in_specs=[
            pl.BlockSpec((8, D_MODEL), lambda l, j: (0, 0)),
            pl.BlockSpec((1, D_MODEL, D_MODEL), lambda l, j: (l, 0, j)),
            pl.BlockSpec((1, 1, D_MODEL), lambda l, j: (l, 0, j)),
        ],
        out_specs=pl.BlockSpec((1, 8, D_MODEL), lambda l, j: (l, 0, j)),
        out_shape=jax.ShapeDtypeStruct((depth, 8, 6 * D_MODEL), F32),
        compiler_params=_cparams(2),
        name="modulation",
    )(cond, ada_w, ada_b.reshape(depth, 1, 6 * D_MODEL))


def _inproj_kernel(*refs, rope, emit_kv, n_alias, layer):
    x_ref, mod_ref, g_ref, w_ref = refs[:4]
    refs = refs[4:]
    if rope:
        cos_ref, sin_up_ref, sin_dn_ref = refs[:3]
        refs = refs[3:]
    refs = refs[n_alias:]
    p_ref = refs[0]
    mod = mod_ref[0]
    h = (_rms(x_ref[...], g_ref[...]) * (1.0 + mod[1:2]) + mod[0:1]).astype(BF16)
    for g in range(N_GROUPS):
        cols = slice(g * WIDTH, (g + 1) * WIDTH)
        pg = _dot(h, w_ref[:, cols])
        if rope and g < 2:
            parts = []
            for j in range(WIDTH // LANES):
                xj = pg[:, j * LANES:(j + 1) * LANES]
                parts.append(xj * cos_ref[...]
                             + pltpu.roll(xj, 16, 1) * sin_up_ref[...]
                             + pltpu.roll(xj, LANES - 16, 1) * sin_dn_ref[...])
            pg = jnp.concatenate(parts, axis=1)
        if g in (0, 4):
            pg = pg * (HEAD_DIM ** -0.5)
        p_ref[:, cols] = pg.astype(BF16)
        if emit_kv and g >= 5:
            kv_ref = refs[g - 4]
            seq = kv_ref.shape[-2]
            for j in range(kv_ref.shape[0]):
                if n_alias:
                    kv_ref[j] = pg[j * seq:(j + 1) * seq]
                else:
                    for l in range(kv_ref.shape[1]):
                        kv_ref[j, l] = (pg[j * seq:(j + 1) * seq] if l == layer
                                        else jnp.zeros((seq, WIDTH), F32))


def _inproj(x, mods, g_pre, w, layer, seq_len, tm, rope_tabs=None, kv_out=None):
    n = x.shape[0]
    depth = w.shape[0]
    tiles_per_seq = max(seq_len // tm, 1)
    cond_row = (lambda i: 0) if rope_tabs is None else (lambda i: 1 + i // tiles_per_seq)
    in_specs = [
        pl.BlockSpec((tm, D_MODEL), lambda i: (i, 0)),
        pl.BlockSpec((None, 1, 6, D_MODEL), lambda i: (layer, cond_row(i), 0, 0)),
        pl.BlockSpec((None, 1, D_MODEL), lambda i: (layer, 0, 0)),
        pl.BlockSpec((None, D_MODEL, IN_WIDTH), lambda i: (layer, 0, 0),
                     pipeline_mode=pl.Buffered(1)),
    ]
    args = [x, mods, g_pre, w]
    if rope_tabs is not None:
        in_specs += [pl.BlockSpec((tm, LANES), lambda i: (i % tiles_per_seq, 0))] * 3
        args += list(rope_tabs)
    out_specs = [pl.BlockSpec((tm, IN_WIDTH), lambda i: (i, 0))]
    out_shape = [jax.ShapeDtypeStruct((n, IN_WIDTH), BF16)]
    aliases = {}
    if kv_out is not None:
        seqs_per_tile = tm // seq_len
        kv_shape = jax.ShapeDtypeStruct((n // seq_len, depth, seq_len, WIDTH), F32)
        assert (kv_out[0] is None) == (kv_out[1] is None)
        for j, prev in enumerate(kv_out):
            if prev is not None:
                aliases[len(args)] = 1 + j
                in_specs.append(pl.BlockSpec(memory_space=pl.ANY))
                args.append(prev)
                out_specs.append(pl.BlockSpec((seqs_per_tile, None, seq_len, WIDTH),
                                              lambda i: (i, layer, 0, 0)))
            else:
                out_specs.append(pl.BlockSpec((seqs_per_tile, depth, seq_len, WIDTH),
                                              lambda i: (i, 0, 0, 0)))
            out_shape.append(kv_shape)
    return pl.pallas_call(
        functools.partial(_inproj_kernel, rope=rope_tabs is not None,
                          emit_kv=kv_out is not None, n_alias=len(aliases), layer=layer),
        grid=(n // tm,),
        in_specs=in_specs,
        out_specs=out_specs,
        out_shape=out_shape,
        input_output_aliases=aliases,
        compiler_params=_cparams(1),
        name="inproj_latent" if rope_tabs is not None else "inproj_context",
    )(*args)


def _rope_tables(seq_len):
    t = np.arange(seq_len)
    lane = np.arange(LANES)
    d = lane % HEAD_DIM
    pos = np.where(d[None, :] < HEAD_DIM // 2, (t // GRID_W)[:, None], (t % GRID_W)[:, None])
    pos = pos.astype(np.float32)
    half = HEAD_DIM // 2
    inv = np.power(np.float32(ROPE_BASE), -np.arange(0, half, 2, dtype=np.float32) / half)
    ang = pos * inv[d % (half // 2)][None, :]
    cos, sin = np.cos(ang), np.sin(ang)
    upper = (d % half) >= half // 2
    sin_up = np.where(upper[None, :], sin, 0.0)
    sin_dn = np.where(upper[None, :], 0.0, -sin)
    return (jnp.asarray(cos, F32), jnp.asarray(sin_up, F32), jnp.asarray(sin_dn, F32))


def _retention_tables(dec_f, dec_b, dec_f2, dec_b2):
    lgf, lgb = _log_sigmoid(dec_f), _log_sigmoid(dec_b)
    pos = lax.broadcasted_iota(jnp.int32, (CHUNK, LANES), 0).astype(F32)
    tabs = dict(
        qdf=jnp.exp(lgf * (pos + 1.0)), kdf=jnp.exp(lgf * (CHUNK - 1.0 - pos)),
        cdf=jnp.exp(lgf * float(CHUNK)),
        qdb=jnp.exp(lgb * (CHUNK - pos)), kdb=jnp.exp(lgb * pos),
        cdb=jnp.exp(lgb * float(CHUNK)),
    )
    lgf2, lgb2 = _log_sigmoid(dec_f2), _log_sigmoid(dec_b2)
    i = lax.broadcasted_iota(jnp.int32, (CHUNK, 2 * CHUNK), 0)
    j = lax.broadcasted_iota(jnp.int32, (CHUNK, 2 * CHUNK), 1) & (CHUNK - 1)
    diff = (i - j).astype(F32)
    tabs["decay"] = (jnp.where(diff >= 0, jnp.exp(lgf2 * jnp.maximum(diff, 0.0)), 0.0)
                     + jnp.where(diff <= 0, jnp.exp(lgb2 * jnp.maximum(-diff, 0.0)), 0.0))
    lane = lax.broadcasted_iota(jnp.int32, (1, LANES), 1)
    tabs["head_a"] = lane < HEAD_DIM
    r = lax.broadcasted_iota(jnp.int32, (2 * LANES, LANES), 0) & (LANES - 1)
    c = lax.broadcasted_iota(jnp.int32, (2 * LANES, LANES), 1)
    tabs["same_head"] = (r < HEAD_DIM) == (c < HEAD_DIM)
    return tabs


def _split_heads(x, head_a, axis):
    zero = jnp.zeros_like(x)
    return jnp.concatenate([jnp.where(head_a, x, zero), jnp.where(head_a, zero, x)], axis=axis)


def _chunk_kv(k2, v2, tabs):
    kf = k2.astype(F32)
    kk = jnp.concatenate([kf * tabs["kdf"], kf * tabs["kdb"]], axis=1).astype(BF16)
    return jnp.where(tabs["same_head"], _dot_tn(kk, v2), 0.0)


def _chunk_out(q2, k2, v2, g2, state, tabs):
    head_a = tabs["head_a"]
    s = _dot_nt(q2, _split_heads(k2, head_a, 0))
    p = (s * tabs["decay"]).astype(BF16)
    qf = q2.astype(F32)
    lhs = jnp.concatenate(
        [p, (qf * tabs["qdf"]).astype(BF16), (qf * tabs["qdb"]).astype(BF16)], axis=1)
    rhs = jnp.concatenate([_split_heads(v2, head_a, 0), state], axis=0)
    o = _dot(lhs, rhs)
    inv = 1.0 / HEAD_DIM
    sum_a = jnp.sum(jnp.where(head_a, o, 0.0), axis=-1, keepdims=True)
    sum_b = jnp.sum(jnp.where(head_a, 0.0, o), axis=-1, keepdims=True)
    d = o - jnp.where(head_a, sum_a, sum_b) * inv
    d2 = d * d
    var_a = jnp.sum(jnp.where(head_a, d2, 0.0), axis=-1, keepdims=True)
    var_b = jnp.sum(jnp.where(head_a, 0.0, d2), axis=-1, keepdims=True)
    o = d * lax.rsqrt(jnp.where(head_a, var_a, var_b) * inv + EPS)
    return o * _silu(g2.astype(F32))


def _ctx_mixer_kernel(rq, rk, rv, rg, nq, nk, nv, df, db, df2, db2, mix_ref, st_ref):
    seq = rq.shape[0]
    nc = seq // CHUNK
    for hp in range(HEAD_PAIRS):
        cols = slice(hp * LANES, (hp + 1) * LANES)
        cols2 = slice(hp * 2 * LANES, (hp + 1) * 2 * LANES)
        tabs = _retention_tables(df[:, cols], db[:, cols], df2[:, cols2], db2[:, cols2])
        rows = [slice(c * CHUNK, (c + 1) * CHUNK) for c in range(nc)]
        kv = [_chunk_kv(rk[r, cols], rv[r, cols], tabs) for r in rows]
        sf = [jnp.zeros((LANES, LANES), F32)]
        for c in range(nc):
            sf.append(sf[-1] * tabs["cdf"] + kv[c][:LANES])
        sb = [jnp.zeros((LANES, LANES), F32)]
        for c in reversed(range(nc)):
            sb.append(sb[-1] * tabs["cdb"] + kv[c][LANES:])
        for c in range(nc):
            state = jnp.concatenate([sf[c], sb[nc - 1 - c]], axis=0).astype(BF16)
            o = _chunk_out(rq[rows[c], cols], rk[rows[c], cols], rv[rows[c], cols],
                           rg[rows[c], cols], state, tabs)
            mix_ref[rows[c], cols] = o.astype(BF16)
        st_ref[0, hp, 0] = sf[nc]
        st_ref[0, hp, 1] = sb[nc]

        head_a = tabs["head_a"]
        s = _dot_nt(nq[:, cols], _split_heads(nk[:, cols], head_a, 0))
        es, rinv = [], []
        for h in range(2):
            sh = s[:, h * seq:(h + 1) * seq]
            e = jnp.exp(sh - jnp.max(sh, axis=-1, keepdims=True))
            rinv.append(1.0 / jnp.sum(e, axis=-1, keepdims=True))
            es.append(e.astype(BF16))
        o = _dot(jnp.concatenate(es, axis=1), _split_heads(nv[:, cols], head_a, 0))
        o = o * jnp.where(head_a, rinv[0], rinv[1])
        mix_ref[:, WIDTH + hp * LANES:WIDTH + (hp + 1) * LANES] = o.astype(BF16)


def _ctx_mixer(p, dec, batch, seq):
    group = lambda g: pl.BlockSpec((seq, WIDTH), lambda b, g=g: (b, g))
    vec = lambda w: pl.BlockSpec((1, w), lambda b: (0, 0))
    return pl.pallas_call(
        _ctx_mixer_kernel,
        grid=(batch,),
        in_specs=[group(g) for g in range(N_GROUPS)] + [vec(WIDTH), vec(WIDTH),
                                                         vec(2 * WIDTH), vec(2 * WIDTH)],
        out_specs=[
            pl.BlockSpec((seq, 2 * WIDTH), lambda b: (b, 0)),
            pl.BlockSpec((1, HEAD_PAIRS, 2, LANES, LANES), lambda b: (b, 0, 0, 0, 0)),
        ],
        out_shape=[
            jax.ShapeDtypeStruct((batch * seq, 2 * WIDTH), BF16),
            jax.ShapeDtypeStruct((batch, HEAD_PAIRS, 2, LANES, LANES), F32),
        ],
        compiler_params=_cparams(1),
        name="context_mixer",
    )(*([p] * N_GROUPS), *dec)


def _lat_retention_kernel(q_ref, k_ref, v_ref, g_ref, df, db, df2, db2, s0_ref, o_ref,
                          kv_ref, st_ref):
    nc = q_ref.shape[0] // CHUNK
    tabs = _retention_tables(df[...], db[...], df2[...], db2[...])

    def rows(c):
        return pl.ds(pl.multiple_of(c * CHUNK, CHUNK), CHUNK)

    def kv_body(c, carry):
        kv_ref[c] = _chunk_kv(k_ref[rows(c), :], v_ref[rows(c), :], tabs)
        return carry
    lax.fori_loop(0, nc, kv_body, 0, unroll=RET_UNROLL)

    def fwd_body(c, s):
        st_ref[c, :LANES, :] = s.astype(BF16)
        return s * tabs["cdf"] + kv_ref[c, :LANES, :]
    lax.fori_loop(0, nc, fwd_body, s0_ref[0, 0, 0])

    def bwd_body(i, s):
        c = nc - 1 - i
        st_ref[c, LANES:, :] = s.astype(BF16)
        return s * tabs["cdb"] + kv_ref[c, LANES:, :]
    lax.fori_loop(0, nc, bwd_body, s0_ref[0, 0, 1])

    def out_body(c, carry):
        r = rows(c)
        o = _chunk_out(q_ref[r, :], k_ref[r, :], v_ref[r, :], g_ref[r, :], st_ref[c], tabs)
        o_ref[r, :] = o.astype(BF16)
        return carry
    lax.fori_loop(0, nc, out_body, 0, unroll=RET_UNROLL)


def _lat_retention(p, dec, s0, batch, seq):
    nc = seq // CHUNK
    group = lambda g: pl.BlockSpec((seq, LANES), lambda b, h, g=g: (b, g * HEAD_PAIRS + h))
    vec = lambda w: pl.BlockSpec((1, w), lambda b, h: (0, h))
    return pl.pallas_call(
        _lat_retention_kernel,
        grid=(batch, HEAD_PAIRS),
        in_specs=[group(g) for g in range(4)] + [vec(LANES), vec(LANES), vec(2 * LANES),
                                                 vec(2 * LANES)]
        + [pl.BlockSpec((1, 1, 2, LANES, LANES), lambda b, h: (b, h, 0, 0, 0))],
        out_specs=pl.BlockSpec((seq, LANES), lambda b, h: (b, h)),
        out_shape=jax.ShapeDtypeStruct((batch * seq, WIDTH), BF16),
        scratch_shapes=[pltpu.VMEM((nc, 2 * LANES, LANES), F32),
                        pltpu.VMEM((nc, 2 * LANES, LANES), BF16)],
        compiler_params=_cparams(2),
        name="latent_retention",
    )(*([p] * 4), *dec, s0)


def _block_diag_states(s_f, s_b):
    def bd(s):
        b = s.shape[0]
        s = s.reshape(b, HEAD_PAIRS, 2, HEAD_DIM, HEAD_DIM)
        z = jnp.zeros_like(s[:, :, 0])
        top = jnp.concatenate([s[:, :, 0], z], axis=-1)
        bot = jnp.concatenate([z, s[:, :, 1]], axis=-1)
        return jnp.concatenate([top, bot], axis=-2)
    return jnp.stack([bd(s_f), bd(s_b)], axis=2)


def _diag_states(st):
    b = st.shape[0]
    a = st[:, :, :HEAD_DIM, :HEAD_DIM]
    c = st[:, :, HEAD_DIM:, HEAD_DIM:]
    return jnp.stack([a, c], axis=2).reshape(b, N_HEADS, HEAD_DIM, HEAD_DIM)


def _na_kernel(q_ref, k_ref, v_ref, ck_ref, cv_ref, *refs):
    bias_refs, o_ref = refs[:NA_ROWS], refs[NA_ROWS]
    rows_total = k_ref.shape[0] // GRID_W
    lane = lax.broadcasted_iota(jnp.int32, (1, LANES), 1)
    head_a = lane < HEAD_DIM
    wins = []
    for i in range(NA_ROWS):
        r = pl.program_id(1) * NA_ROWS + i
        rs = jnp.clip(r - NA_KH // 2, 0, rows_total - NA_KH)
        wins.append(pl.ds(pl.multiple_of(rs * GRID_W, GRID_W), NA_KH * GRID_W))
    for hp in range(HEAD_PAIRS):
        cols = slice(hp * LANES, (hp + 1) * LANES)
        qq = jnp.concatenate(
            [_split_heads(q_ref[i * GRID_W:(i + 1) * GRID_W, cols], head_a, 0)
             for i in range(NA_ROWS)], axis=0)
        s_ctx = _dot_nt(qq, ck_ref[0, :, cols])
        m_ctx = jnp.max(s_ctx, axis=-1, keepdims=True)
        s_loc, m = [], []
        for i in range(NA_ROWS):
            blk = slice(i * 2 * GRID_W, (i + 1) * 2 * GRID_W)
            s = _dot_nt(qq[blk], k_ref[wins[i], cols]) + bias_refs[i][0, hp]
            s_loc.append(s)
            m.append(jnp.maximum(jnp.max(s, axis=-1, keepdims=True), m_ctx[blk]))
        e_ctx = jnp.exp(s_ctx - jnp.concatenate(m, axis=0))
        den_ctx = jnp.sum(e_ctx, axis=-1, keepdims=True)
        o_ctx = _dot(e_ctx.astype(BF16), cv_ref[0, :, cols])
        for i in range(NA_ROWS):
            blk = slice(i * 2 * GRID_W, (i + 1) * 2 * GRID_W)
            e = jnp.exp(s_loc[i] - m[i])
            den = jnp.sum(e, axis=-1, keepdims=True) + den_ctx[blk]
            o = (_dot(e.astype(BF16), v_ref[wins[i], cols]) + o_ctx[blk]) / den
            o_ref[i * GRID_W:(i + 1) * GRID_W, cols] = jnp.where(
                head_a, o[:GRID_W], o[GRID_W:]).astype(BF16)


def _bias_build_kernel(rpb_ref, o_ref, *, dr_first, n_dr):
    qc = lax.broadcasted_iota(jnp.int32, (GRID_W, LANES), 0)
    lane = lax.broadcasted_iota(jnp.int32, (GRID_W, LANES), 1)
    kc = lane & (GRID_W - 1)
    cs = jnp.clip(qc - NA_KW // 2, 0, GRID_W - NA_KW)
    inside = (kc >= cs) & (kc < cs + NA_KW)
    first = lane < GRID_W
    for h in range(N_HEADS):
        lo_half, hi_half = [], []
        for dr in range(n_dr):
            line = jnp.broadcast_to(rpb_ref[pl.ds(h * n_dr + dr, 1), :], (GRID_W, LANES))
            lo_half.append(pltpu.roll(line, LANES - (NA_KW - 1), 1, stride=1, stride_axis=0))
            hi_half.append(pltpu.roll(line, GRID_W - (NA_KW - 1), 1, stride=1, stride_axis=0))
        for cls, dr0 in enumerate(dr_first):
            for jp in range(NA_KH // 2):
                dr = dr0 + 2 * jp
                tile = jnp.where(inside, jnp.where(first, lo_half[dr], hi_half[dr + 1]), NEG_INF)
                o_ref[cls, h // 2, (h % 2) * GRID_W:(h % 2 + 1) * GRID_W,
                      jp * LANES:(jp + 1) * LANES] = tile


def _na_bias_tables(na_rpb, rows_total):
    depth, heads, n_dr, n_dc = na_rpb.shape
    kh = NA_KH
    lo, hi = kh // 2, rows_total - kh // 2 - 1
    reps = list(range(lo)) + [lo] + list(range(hi + 1, rows_total))
    dr_first = tuple(int(np.clip(r - kh // 2, 0, rows_total - kh)) - r + kh - 1 for r in reps)
    lines = jnp.pad(na_rpb.reshape(depth, heads * n_dr, n_dc), ((0, 0), (0, 0), (0, LANES - n_dc)))
    bias = pl.pallas_call(
        functools.partial(_bias_build_kernel, dr_first=dr_first, n_dr=n_dr),
        grid=(depth,),
        in_specs=[pl.BlockSpec((None, heads * n_dr, LANES), lambda l: (l, 0, 0))],
        out_specs=pl.BlockSpec((None, len(reps), HEAD_PAIRS, 2 * GRID_W, kh * GRID_W),
                               lambda l: (l, 0, 0, 0, 0)),
        out_shape=jax.ShapeDtypeStruct(
            (depth, len(reps), HEAD_PAIRS, 2 * GRID_W, kh * GRID_W), F32),
        compiler_params=_cparams(1),
        name="na_bias_build",
    )(lines)
    return bias, lo, hi


def _neighbourhood_attention(p, ctx_k, ctx_v, bias, layer, lo, hi, batch, seq):
    rows_total = seq // GRID_W
    past = ctx_k.shape[1]

    def cls(r):
        return jnp.where(r < lo, r, jnp.where(r > hi, r - hi + lo, lo))

    groups = rows_total // NA_ROWS
    bias_spec = lambda i: pl.BlockSpec(
        (None, 1, HEAD_PAIRS, 2 * GRID_W, NA_KH * GRID_W),
        lambda b, g: (layer, cls(g * NA_ROWS + i), 0, 0, 0))
    return pl.pallas_call(
        _na_kernel,
        grid=(batch, groups),
        in_specs=[
            pl.BlockSpec((NA_ROWS * GRID_W, WIDTH), lambda b, g: (b * groups + g, 4)),
            pl.BlockSpec((seq, WIDTH), lambda b, g: (b, 5)),
            pl.BlockSpec((seq, WIDTH), lambda b, g: (b, 6)),
            pl.BlockSpec((1, past, WIDTH), lambda b, g: (b, 0, 0)),
            pl.BlockSpec((1, past, WIDTH), lambda b, g: (b, 0, 0)),
        ] + [bias_spec(i) for i in range(NA_ROWS)],
        out_specs=pl.BlockSpec((NA_ROWS * GRID_W, WIDTH), lambda b, g: (b * groups + g, 0)),
        out_shape=jax.ShapeDtypeStruct((batch * seq, WIDTH), BF16),
        compiler_params=_cparams(2),
        name="neighbourhood_attention",
    )(p, p, p, ctx_k, ctx_v, *([bias] * NA_ROWS))


def _post_kernel(xm, xp, xn, am, ap, an, bm, bp, bn, mod_head, mod_tail, g_ref, wo, wu, cw, cb, wd,
                 o_ref, acc_ref, h_ref, hp_ref, x1_ref, y_ref, act_ref, *, seq_len, n_tiles):
    step = pl.program_id(0)
    tm = xm.shape[0]
    nj = tm // HALO
    tiles_per_seq = max(seq_len // tm, 1)
    assert (tm % seq_len == 0 and seq_len % nj == 0) or seq_len % tm == 0
    n_ct = D_MODEL // LANES
    g = g_ref[...]

    def ext(main, prev, nxt):
        lo = prev[...].astype(F32)[prev.shape[0] - HALO:]
        hi = nxt[...].astype(F32)[:HALO]
        return jnp.concatenate([lo, main[...].astype(F32), hi], axis=0)

    par = step % 2
    mod_h = mod_head[0]
    n_blocks = tm // HEAD_ROWS

    def head_matmul():
        mixed = jnp.concatenate([ext(am, ap, an), ext(bm, bp, bn)], axis=1).astype(BF16)
        y_ref[...] = _dot(mixed, wo[...])

    def head_rows(x, y):
        x1 = x + mod_h[2:3] * _rms(y, g[0:1])
        return x1, _rms(x1, g[1:2]) * (1.0 + mod_h[4:5]) + mod_h[3:4]

    def head_block(blk):
        if blk < n_blocks:
            rows = slice(blk * HEAD_ROWS, (blk + 1) * HEAD_ROWS)
            erows = slice(HALO + blk * HEAD_ROWS, HALO + (blk + 1) * HEAD_ROWS)
            x1, h = head_rows(xm[rows, :], y_ref[erows, :])
            x1_ref[par, rows, :] = x1
            for c in range(n_ct):
                h_ref[c, erows, :] = h[:, c * LANES:(c + 1) * LANES]
        else:
            x = jnp.concatenate([xp[...], xn[...]], axis=0)
            y = jnp.concatenate([y_ref[0:HALO, :], y_ref[HALO + tm:2 * HALO + tm, :]], axis=0)
            _, h = head_rows(x, y)
            t = jnp.minimum(step, n_tiles - 1) % tiles_per_seq
            ridx = lax.broadcasted_iota(jnp.int32, h.shape, 0)
            keep = ((ridx >= HALO) | (t != 0)) & ((ridx < HALO) | (t != tiles_per_seq - 1))
            h = jnp.where(keep, h, 0.0)
            for c in range(n_ct):
                h_ref[c, 0:HALO, :] = h[:HALO, c * LANES:(c + 1) * LANES]
                h_ref[c, HALO + tm:2 * HALO + tm, :] = h[HALO:, c * LANES:(c + 1) * LANES]

    def head_permute():
        def group(rows):
            return jnp.concatenate([h_ref[c, rows, :] for c in range(n_ct)], axis=1)
        def put(row0, first, second):
            hp_ref[row0:row0 + 2 * HALO, :] = jnp.concatenate(
                [group(first), group(second)], axis=0).astype(BF16)
        for j in range(0, nj, 2):
            put(j * HALO, pl.ds(HALO + j, HALO, stride=nj), pl.ds(HALO + j + 1, HALO, stride=nj))
        put(tm, pl.ds(0, HALO), pl.ds(HALO + tm, HALO))

    def tail(interleaved):
        mod = mod_tail[0]
        sub = lax.broadcasted_iota(jnp.int32, (HALO, FF_CHUNK), 0)

        def conv(u, cols):
            w = cw[:, cols]
            before = jnp.where(sub == 0, u[tm + HALO - 1:tm + HALO],
                               pltpu.roll(u[tm - HALO:tm], 1, 0))
            after = jnp.where(sub == HALO - 1, u[tm + HALO:tm + HALO + 1],
                              pltpu.roll(u[0:HALO], HALO - 1, 0))
            for s in range(1, HALO):
                if (s * nj) % seq_len == 0:
                    before = jnp.where(sub == s, 0.0, before)
                    after = jnp.where(sub == s - 1, 0.0, after)
            prev = jnp.concatenate([before, u[0:tm - HALO]], axis=0)
            nxt = jnp.concatenate([u[HALO:tm], after], axis=0)
            return prev * w[0:1] + u[0:tm] * w[1:2] + nxt * w[2:3] + cb[:, cols]

        n_chunks = D_FF // FF_CHUNK

        def up(ch):
            ca = slice(ch * FF_CHUNK, (ch + 1) * FF_CHUNK)
            cg = slice(D_FF + ch * FF_CHUNK, D_FF + (ch + 1) * FF_CHUNK)
            return _dot(hp_ref[...], wu[:, cg]), _dot(hp_ref[...], wu[:, ca])

        nxt_u = up(0)
        for ch in range(n_chunks):
            ug, ua = nxt_u
            if ch + 1 < n_chunks:
                nxt_u = up(ch + 1)
            for blk in interleaved[ch]:
                head_block(blk)
            ca = slice(ch * FF_CHUNK, (ch + 1) * FF_CHUNK)
            cg = slice(D_FF + ch * FF_CHUNK, D_FF + (ch + 1) * FF_CHUNK)
            act_ref[:, ca] = (_silu(conv(ug, cg)) * conv(ua, ca)).astype(BF16)
        ffn = mod[5:6] * _rms(_dot(act_ref[...], wd[...]), g[2:3])
        for c in range(n_ct):
            acc_ref[c] = ffn[:, c * LANES:(c + 1) * LANES]

        for s in range(HALO):
            for jb in range(nj // HALO):
                t0 = s * nj + HALO * jb
                rows = pl.ds(HALO * HALO * jb + s, HALO, stride=HALO)
                back = jnp.concatenate([acc_ref[c, rows, :] for c in range(n_ct)], axis=1)
                o_ref[t0:t0 + HALO, :] = x1_ref[1 - par, t0:t0 + HALO, :] + back

    @pl.when(step == 0)
    def _():
        head_matmul()
        for blk in range(n_blocks + 1):
            head_block(blk)
        head_permute()

    @pl.when(step > 0)
    def _():
        head_matmul()
        n_chunks = D_FF // FF_CHUNK
        tail([[blk for blk in range(n_blocks + 1) if blk * n_chunks // (n_blocks + 1) == ch]
              for ch in range(n_chunks)])
        head_permute()


def _post(x, mix_a, mix_b, mods, gains, wo, wu, cw, cb, wd, layer, seq_len, tm, latent):
    n = x.shape[0]
    nt = n // tm
    (mix_a, col_a), (mix_b, col_b) = mix_a, mix_b
    tiles_per_seq = max(seq_len // tm, 1)
    cond_row = (lambda i: 1 + i // tiles_per_seq) if latent else (lambda i: 0)

    head_tile = lambda i: jnp.minimum(i, nt - 1)
    tail_tile = lambda i: jnp.maximum(i - 1, 0)

    def triple(width, halo_rows, col=0):
        per = tm // halo_rows
        last = n // halo_rows - 1
        return [
            pl.BlockSpec((tm, width), lambda i: (head_tile(i), col)),
            pl.BlockSpec((halo_rows, width),
                         lambda i: (jnp.maximum(head_tile(i) * per - 1, 0), col)),
            pl.BlockSpec((halo_rows, width),
                         lambda i: (jnp.minimum((head_tile(i) + 1) * per, last), col)),
        ]

    const = lambda *shape: pl.BlockSpec((None,) + shape, lambda i: (layer,) + (0,) * len(shape),
                                        pipeline_mode=pl.Buffered(1))
    in_specs = (triple(D_MODEL, HALO) + triple(WIDTH, 2 * HALO, col_a)
                + triple(WIDTH, 2 * HALO, col_b) + [
        pl.BlockSpec((None, 1, 6, D_MODEL), lambda i: (layer, cond_row(head_tile(i)), 0, 0)),
        pl.BlockSpec((None, 1, 6, D_MODEL), lambda i: (layer, cond_row(tail_tile(i)), 0, 0)),
        const(3, D_MODEL),
        const(2 * WIDTH, D_MODEL),
        const(D_MODEL, 2 * D_FF),
        const(3, 2 * D_FF),
        const(1, 2 * D_FF),
        const(D_FF, D_MODEL),
    ])
    return pl.pallas_call(
        functools.partial(_post_kernel, seq_len=seq_len, n_tiles=nt),
        grid=(nt + 1,),
        in_specs=in_specs,
        out_specs=pl.BlockSpec((tm, D_MODEL), lambda i: (tail_tile(i), 0)),
        out_shape=jax.ShapeDtypeStruct((n, D_MODEL), F32),
        scratch_shapes=[pltpu.VMEM((D_MODEL // LANES, tm, LANES), F32),
                        pltpu.VMEM((D_MODEL // LANES, tm + 2 * HALO, LANES), F32),
                        pltpu.VMEM((tm + 2 * HALO, D_MODEL), BF16),
                        pltpu.VMEM((2, tm, D_MODEL), F32),
                        pltpu.VMEM((tm + 2 * HALO, D_MODEL), F32),
                        pltpu.VMEM((tm, D_FF), BF16)],
        compiler_params=_cparams(1),
        name="post_latent" if latent else "post_context",
    )(x, x, x, mix_a, mix_a, mix_a, mix_b, mix_b, mix_b, mods, mods, gains, wo, wu, cw, cb, wd)


def kernel(x_prompt, x_sample, c, cache_na_k, cache_na_v, state_ret_fwd, state_ret_bwd, c_ctx,
           ada_w, ada_b, g_pre_mix, g_post_mix, g_pre_ffn, g_post_ffn, w_in,
           ret_decay_fwd, ret_decay_bwd, na_rpb, w_out, w_up, conv_w, conv_b, w_down):
    depth = w_in.shape[0]
    batch, seq, _ = x_prompt.shape
    dec_batch, dec_seq, _ = x_sample.shape
    past = cache_na_k.shape[2]
    tm = 512

    cond = jnp.concatenate(
        [c_ctx[None, :], c, jnp.zeros((8 - 1 - dec_batch, D_MODEL), F32)], axis=0)
    mods = _modulation(cond, ada_w, ada_b).reshape(depth, 8, 6, D_MODEL)
    rope_tabs = _rope_tables(dec_seq)
    bias, lo, hi = _na_bias_tables(na_rpb, dec_seq // GRID_W)

    w_in_b, wo_b, wu_b, wd_b = (w.astype(BF16) for w in (w_in, w_out, w_up, w_down))
    g_pre = g_pre_mix.reshape(depth, 1, D_MODEL)
    gains = jnp.stack([g_post_mix, g_pre_ffn, g_post_ffn], axis=1)
    cb = conv_b.reshape(depth, 1, 2 * D_FF)
    post_params = (mods, gains, wo_b, wu_b, conv_w, cb, wd_b)

    y_p = x_prompt.reshape(batch * seq, D_MODEL)
    y_s = x_sample.reshape(dec_batch * dec_seq, D_MODEL)
    new_k = new_v = None
    sfs, sbs = [], []
    for l in range(depth):
        dec = (jnp.repeat(ret_decay_fwd[l], HEAD_DIM)[None, :],
               jnp.repeat(ret_decay_bwd[l], HEAD_DIM)[None, :],
               jnp.repeat(ret_decay_fwd[l], LANES)[None, :],
               jnp.repeat(ret_decay_bwd[l], LANES)[None, :])

        p_c, new_k, new_v = _inproj(y_p, mods, g_pre, w_in_b, l, seq, tm, kv_out=(new_k, new_v))
        mix_c, st_c = _ctx_mixer(p_c, dec, batch, seq)
        y_p = _post(y_p, (mix_c, 0), (mix_c, 1), *post_params, l, seq, tm, latent=False)
        sfs.append(_diag_states(st_c[:, :, 0]))
        sbs.append(_diag_states(st_c[:, :, 1]))

        (p_s,) = _inproj(y_s, mods, g_pre, w_in_b, l, dec_seq, tm, rope_tabs=rope_tabs)
        s0 = _block_diag_states(state_ret_fwd[:, l], state_ret_bwd[:, l])
        ret_s = _lat_retention(p_s, dec, s0, dec_batch, dec_seq)
        ck = cache_na_k[:, l].reshape(dec_batch, past, WIDTH).astype(BF16)
        cv = cache_na_v[:, l].reshape(dec_batch, past, WIDTH).astype(BF16)
        na_s = _neighbourhood_attention(p_s, ck, cv, bias, l, lo, hi, dec_batch, dec_seq)
        y_s = _post(y_s, (ret_s, 0), (na_s, 0), *post_params, l, dec_seq, tm, latent=True)

    return (y_p.reshape(batch, seq, D_MODEL),
            y_s.reshape(dec_batch, dec_seq, D_MODEL),
            new_k.reshape(batch, depth, seq, N_HEADS, HEAD_DIM),
            new_v.reshape(batch, depth, seq, N_HEADS, HEAD_DIM),
            jnp.stack(sfs, axis=1), jnp.stack(sbs, axis=1))
```

```python
import functools

import numpy as np
import jax
import jax.numpy as jnp
from jax import lax
from jax.experimental import pallas as pl
from jax.experimental.pallas import tpu as pltpu

F32 = jnp.float32
BF16 = jnp.bfloat16

D_MODEL = 1024
HEAD_DIM = 64
N_HEADS = 8
HEAD_PAIRS = N_HEADS // 2
LANES = 128
WIDTH = N_HEADS * HEAD_DIM
N_GROUPS = 7
IN_WIDTH = N_GROUPS * WIDTH
D_FF = 2816
FF_CHUNK = 768
HEAD_ROWS = 32
CHUNK = 128
RET_UNROLL = 8
GRID_W = 64
NA_KH = 8
NA_KW = 16
NA_ROWS = 8
ROPE_BASE = 10000.0
EPS = 1e-6
NEG_INF = -1e9
HALO = 8
VMEM_LIMIT = 56 * 1024 * 1024


def _cparams(n_grid):
    return pltpu.CompilerParams(
        dimension_semantics=("arbitrary",) * n_grid, vmem_limit_bytes=VMEM_LIMIT)


def _rms(x, g):
    ms = jnp.mean(x * x, axis=-1, keepdims=True)
    return x * lax.rsqrt(ms + EPS) * g


def _silu(x):
    return x * jax.nn.sigmoid(x)


def _log_sigmoid(x):
    return jnp.minimum(x, 0.0) - jnp.log1p(jnp.exp(-jnp.abs(x)))


def _dot(a, b):
    return jnp.dot(a, b, preferred_element_type=F32)


def _dot_nt(a, b):
    return lax.dot_general(a, b, (((1,), (1,)), ((), ())), preferred_element_type=F32)


def _dot_tn(a, b):
    return lax.dot_general(a, b, (((0,), (0,)), ((), ())), preferred_element_type=F32)


def _mod_kernel(cond_ref, w_ref, b_ref, o_ref):
    s = _silu(cond_ref[...]).astype(BF16)
    o_ref[0] = _dot(s, w_ref[0].astype(BF16)) + b_ref[0]


def _modulation(cond, ada_w, ada_b):
    depth = ada_w.shape[0]
    nb = 6 * D_MODEL // D_MODEL
    return pl.pallas_call(
        _mod_kernel,
        grid=(depth, nb),
        in_specs=[
            pl.BlockSpec((8, D_MODEL), lambda l, j: (0, 0)),
            pl.BlockSpec((1, D_MODEL, D_MODEL), lambda l, j: (l, 0, j)),
            pl.BlockSpec((1, 1, D_MODEL), lambda l, j: (l, 0, j)),
        ],
        out_specs=pl.BlockSpec((1, 8, D_MODEL), lambda l, j: (l, 0, j)),
        out_shape=jax.ShapeDtypeStruct((depth, 8, 6 * D_MODEL), F32),
        compiler_params=_cparams(2),
        name="modulation",
    )(cond, ada_w, ada_b.reshape(depth, 1, 6 * D_MODEL))


def _inproj_kernel(*refs, rope, emit_kv, n_alias, layer):
    x_ref, mod_ref, g_ref, w_ref = refs[:4]
    refs = refs[4:]
    if rope:
        cos_ref, sin_up_ref, sin_dn_ref = refs[:3]
        refs = refs[3:]
    refs = refs[n_alias:]
    p_ref = refs[0]
    mod = mod_ref[0]
    h = (_rms(x_ref[...], g_ref[...]) * (1.0 + mod[1:2]) + mod[0:1]).astype(BF16)
    for g in range(N_GROUPS):
        cols = slice(g * WIDTH, (g + 1) * WIDTH)
        pg = _dot(h, w_ref[:, cols])
        if rope and g < 2:
            parts = []
            for j in range(WIDTH // LANES):
                xj = pg[:, j * LANES:(j + 1) * LANES]
                parts.append(xj * cos_ref[...]
                             + pltpu.roll(xj, 16, 1) * sin_up_ref[...]
                             + pltpu.roll(xj, LANES - 16, 1) * sin_dn_ref[...])
            pg = jnp.concatenate(parts, axis=1)
        if g in (0, 4):
            pg = pg * (HEAD_DIM ** -0.5)
        p_ref[:, cols] = pg.astype(BF16)
        if emit_kv and g >= 5:
            kv_ref = refs[g - 4]
            seq = kv_ref.shape[-2]
            for j in range(kv_ref.shape[0]):
                if n_alias:
                    kv_ref[j] = pg[j * seq:(j + 1) * seq]
                else:
                    for l in range(kv_ref.shape[1]):
                        kv_ref[j, l] = (pg[j * seq:(j + 1) * seq] if l == layer
                                        else jnp.zeros((seq, WIDTH), F32))


def _inproj(x, mods, g_pre, w, layer, seq_len, tm, rope_tabs=None, kv_out=None):
    n = x.shape[0]
    depth = w.shape[0]
    tiles_per_seq = max(seq_len // tm, 1)
    cond_row = (lambda i: 0) if rope_tabs is None else (lambda i: 1 + i // tiles_per_seq)
    in_specs = [
        pl.BlockSpec((tm, D_MODEL), lambda i: (i, 0)),
        pl.BlockSpec((None, 1, 6, D_MODEL), lambda i: (layer, cond_row(i), 0, 0)),
        pl.BlockSpec((None, 1, D_MODEL), lambda i: (layer, 0, 0)),
        pl.BlockSpec((None, D_MODEL, IN_WIDTH), lambda i: (layer, 0, 0),
                     pipeline_mode=pl.Buffered(1)),
    ]
    args = [x, mods, g_pre, w]
    if rope_tabs is not None:
        in_specs += [pl.BlockSpec((tm, LANES), lambda i: (i % tiles_per_seq, 0))] * 3
        args += list(rope_tabs)
    out_specs = [pl.BlockSpec((tm, IN_WIDTH), lambda i: (i, 0))]
    out_shape = [jax.ShapeDtypeStruct((n, IN_WIDTH), BF16)]
    aliases = {}
    if kv_out is not None:
        seqs_per_tile = tm // seq_len
        kv_shape = jax.ShapeDtypeStruct((n // seq_len, depth, seq_len, WIDTH), F32)
        assert (kv_out[0] is None) == (kv_out[1] is None)
        for j, prev in enumerate(kv_out):
            if prev is not None:
                aliases[len(args)] = 1 + j
                in_specs.append(pl.BlockSpec(memory_space=pl.ANY))
                args.append(prev)
                out_specs.append(pl.BlockSpec((seqs_per_tile, None, seq_len, WIDTH),
                                              lambda i: (i, layer, 0, 0)))
            else:
                out_specs.append(pl.BlockSpec((seqs_per_tile, depth, seq_len, WIDTH),
                                              lambda i: (i, 0, 0, 0)))
            out_shape.append(kv_shape)
    return pl.pallas_call(
        functools.partial(_inproj_kernel, rope=rope_tabs is not None,
                          emit_kv=kv_out is not None, n_alias=len(aliases), layer=layer),
        grid=(n // tm,),
        in_specs=in_specs,
        out_specs=out_specs,
        out_shape=out_shape,
        input_output_aliases=aliases,
        compiler_params=_cparams(1),
        name="inproj_latent" if rope_tabs is not None else "inproj_context",
    )(*args)


def _rope_tables(seq_len):
    t = np.arange(seq_len)
    lane = np.arange(LANES)
    d = lane % HEAD_DIM
    pos = np.where(d[None, :] < HEAD_DIM // 2, (t // GRID_W)[:, None], (t % GRID_W)[:, None])
    pos = pos.astype(np.float32)
    half = HEAD_DIM // 2
    inv = np.power(np.float32(ROPE_BASE), -np.arange(0, half, 2, dtype=np.float32) / half)
    ang = pos * inv[d % (half // 2)][None, :]
    cos, sin = np.cos(ang), np.sin(ang)
    upper = (d % half) >= half // 2
    sin_up = np.where(upper[None, :], sin, 0.0)
    sin_dn = np.where(upper[None, :], 0.0, -sin)
    return (jnp.asarray(cos, F32), jnp.asarray(sin_up, F32), jnp.asarray(sin_dn, F32))


def _retention_tables(dec_f, dec_b, dec_f2, dec_b2):
    lgf, lgb = _log_sigmoid(dec_f), _log_sigmoid(dec_b)
    pos = lax.broadcasted_iota(jnp.int32, (CHUNK, LANES), 0).astype(F32)
    tabs = dict(
        qdf=jnp.exp(lgf * (pos + 1.0)), kdf=jnp.exp(lgf * (CHUNK - 1.0 - pos)),
        cdf=jnp.exp(lgf * float(CHUNK)),
        qdb=jnp.exp(lgb * (CHUNK - pos)), kdb=jnp.exp(lgb * pos),
        cdb=jnp.exp(lgb * float(CHUNK)),
    )
    lgf2, lgb2 = _log_sigmoid(dec_f2), _log_sigmoid(dec_b2)
    i = lax.broadcasted_iota(jnp.int32, (CHUNK, 2 * CHUNK), 0)
    j = lax.broadcasted_iota(jnp.int32, (CHUNK, 2 * CHUNK), 1) & (CHUNK - 1)
    diff = (i - j).astype(F32)
    tabs["decay"] = (jnp.where(diff >= 0, jnp.exp(lgf2 * jnp.maximum(diff, 0.0)), 0.0)
                     + jnp.where(diff <= 0, jnp.exp(lgb2 * jnp.maximum(-diff, 0.0)), 0.0))
    lane = lax.broadcasted_iota(jnp.int32, (1, LANES), 1)
    tabs["head_a"] = lane < HEAD_DIM
    r = lax.broadcasted_iota(jnp.int32, (2 * LANES, LANES), 0) & (LANES - 1)
    c = lax.broadcasted_iota(jnp.int32, (2 * LANES, LANES), 1)
    tabs["same_head"] = (r < HEAD_DIM) == (c < HEAD_DIM)
    return tabs


def _split_heads(x, head_a, axis):
    zero = jnp.zeros_like(x)
    return jnp.concatenate([jnp.where(head_a, x, zero), jnp.where(head_a, zero, x)], axis=axis)


def _chunk_kv(k2, v2, tabs):
    kf = k2.astype(F32)
    kk = jnp.concatenate([kf * tabs["kdf"], kf * tabs["kdb"]], axis=1).astype(BF16)
    return jnp.where(tabs["same_head"], _dot_tn(kk, v2), 0.0)


def _chunk_out(q2, k2, v2, g2, state, tabs):
    head_a = tabs["head_a"]
    s = _dot_nt(q2, _split_heads(k2, head_a, 0))
    p = (s * tabs["decay"]).astype(BF16)
    qf = q2.astype(F32)
    lhs = jnp.concatenate(
        [p, (qf * tabs["qdf"]).astype(BF16), (qf * tabs["qdb"]).astype(BF16)], axis=1)
    rhs = jnp.concatenate([_split_heads(v2, head_a, 0), state], axis=0)
    o = _dot(lhs, rhs)
    inv = 1.0 / HEAD_DIM
    sum_a = jnp.sum(jnp.where(head_a, o, 0.0), axis=-1, keepdims=True)
    sum_b = jnp.sum(jnp.where(head_a, 0.0, o), axis=-1, keepdims=True)
    d = o - jnp.where(head_a, sum_a, sum_b) * inv
    d2 = d * d
    var_a = jnp.sum(jnp.where(head_a, d2, 0.0), axis=-1, keepdims=True)
    var_b = jnp.sum(jnp.where(head_a, 0.0, d2), axis=-1, keepdims=True)
    o = d * lax.rsqrt(jnp.where(head_a, var_a, var_b) * inv + EPS)
    return o * _silu(g2.astype(F32))


def _ctx_mixer_kernel(rq, rk, rv, rg, nq, nk, nv, df, db, df2, db2, mix_ref, st_ref):
    seq = rq.shape[0]
    nc = seq // CHUNK
    for hp in range(HEAD_PAIRS):
        cols = slice(hp * LANES, (hp + 1) * LANES)
        cols2 = slice(hp * 2 * LANES, (hp + 1) * 2 * LANES)
        tabs = _retention_tables(df[:, cols], db[:, cols], df2[:, cols2], db2[:, cols2])
        rows = [slice(c * CHUNK, (c + 1) * CHUNK) for c in range(nc)]
        kv = [_chunk_kv(rk[r, cols], rv[r, cols], tabs) for r in rows]
        sf = [jnp.zeros((LANES, LANES), F32)]
        for c in range(nc):
            sf.append(sf[-1] * tabs["cdf"] + kv[c][:LANES])
        sb = [jnp.zeros((LANES, LANES), F32)]
        for c in reversed(range(nc)):
            sb.append(sb[-1] * tabs["cdb"] + kv[c][LANES:])
        for c in range(nc):
            state = jnp.concatenate([sf[c], sb[nc - 1 - c]], axis=0).astype(BF16)
            o = _chunk_out(rq[rows[c], cols], rk[rows[c], cols], rv[rows[c], cols],
                           rg[rows[c], cols], state, tabs)
            mix_ref[rows[c], cols] = o.astype(BF16)
        st_ref[0, hp, 0] = sf[nc]
        st_ref[0, hp, 1] = sb[nc]

        head_a = tabs["head_a"]
        s = _dot_nt(nq[:, cols], _split_heads(nk[:, cols], head_a, 0))
        es, rinv = [], []
        for h in range(2):
            sh = s[:, h * seq:(h + 1) * seq]
            e = jnp.exp(sh - jnp.max(sh, axis=-1, keepdims=True))
            rinv.append(1.0 / jnp.sum(e, axis=-1, keepdims=True))
            es.append(e.astype(BF16))
        o = _dot(jnp.concatenate(es, axis=1), _split_heads(nv[:, cols], head_a, 0))
        o = o * jnp.where(head_a, rinv[0], rinv[1])
        mix_ref[:, WIDTH + hp * LANES:WIDTH + (hp + 1) * LANES] = o.astype(BF16)


def _ctx_mixer(p, dec, batch, seq):
    group = lambda g: pl.BlockSpec((seq, WIDTH), lambda b, g=g: (b, g))
    vec = lambda w: pl.BlockSpec((1, w), lambda b: (0, 0))
    return pl.pallas_call(
        _ctx_mixer_kernel,
        grid=(batch,),
        in_specs=[group(g) for g in range(N_GROUPS)] + [vec(WIDTH), vec(WIDTH),
                                                         vec(2 * WIDTH), vec(2 * WIDTH)],
        out_specs=[
            pl.BlockSpec((seq, 2 * WIDTH), lambda b: (b, 0)),
            pl.BlockSpec((1, HEAD_PAIRS, 2, LANES, LANES), lambda b: (b, 0, 0, 0, 0)),
        ],
        out_shape=[
            jax.ShapeDtypeStruct((batch * seq, 2 * WIDTH), BF16),
            jax.ShapeDtypeStruct((batch, HEAD_PAIRS, 2, LANES, LANES), F32),
        ],
        compiler_params=_cparams(1),
        name="context_mixer",
    )(*([p] * N_GROUPS), *dec)


def _lat_retention_kernel(q_ref, k_ref, v_ref, g_ref, df, db, df2, db2, s0_ref, o_ref,
                          kv_ref, st_ref):
    nc = q_ref.shape[0] // CHUNK
    tabs = _retention_tables(df[...], db[...], df2[...], db2[...])

    def rows(c):
        return pl.ds(pl.multiple_of(c * CHUNK, CHUNK), CHUNK)

    def kv_body(c, carry):
        kv_ref[c] = _chunk_kv(k_ref[rows(c), :], v_ref[rows(c), :], tabs)
        return carry
    lax.fori_loop(0, nc, kv_body, 0, unroll=RET_UNROLL)

    def fwd_body(c, s):
        st_ref[c, :LANES, :] = s.astype(BF16)
        return s * tabs["cdf"] + kv_ref[c, :LANES, :]
    lax.fori_loop(0, nc, fwd_body, s0_ref[0, 0, 0])

    def bwd_body(i, s):
        c = nc - 1 - i
        st_ref[c, LANES:, :] = s.astype(BF16)
        return s * tabs["cdb"] + kv_ref[c, LANES:, :]
    lax.fori_loop(0, nc, bwd_body, s0_ref[0, 0, 1])

    def out_body(c, carry):
        r = rows(c)
        o = _chunk_out(q_ref[r, :], k_ref[r, :], v_ref[r, :], g_ref[r, :], st_ref[c], tabs)
        o_ref[r, :] = o.astype(BF16)
        return carry
    lax.fori_loop(0, nc, out_body, 0, unroll=RET_UNROLL)


def _lat_retention(p, dec, s0, batch, seq):
    nc = seq // CHUNK
    group = lambda g: pl.BlockSpec((seq, LANES), lambda b, h, g=g: (b, g * HEAD_PAIRS + h))
    vec = lambda w: pl.BlockSpec((1, w), lambda b, h: (0, h))
    return pl.pallas_call(
        _lat_retention_kernel,
        grid=(batch, HEAD_PAIRS),
        in_specs=[group(g) for g in range(4)] + [vec(LANES), vec(LANES), vec(2 * LANES),
                                                 vec(2 * LANES)]
        + [pl.BlockSpec((1, 1, 2, LANES, LANES), lambda b, h: (b, h, 0, 0, 0))],
        out_specs=pl.BlockSpec((seq, LANES), lambda b, h: (b, h)),
        out_shape=jax.ShapeDtypeStruct((batch * seq, WIDTH), BF16),
        scratch_shapes=[pltpu.VMEM((nc, 2 * LANES, LANES), F32),
                        pltpu.VMEM((nc, 2 * LANES, LANES), BF16)],
        compiler_params=_cparams(2),
        name="latent_retention",
    )(*([p] * 4), *dec, s0)


def _block_diag_states(s_f, s_b):
    def bd(s):
        b = s.shape[0]
        s = s.reshape(b, HEAD_PAIRS, 2, HEAD_DIM, HEAD_DIM)
        z = jnp.zeros_like(s[:, :, 0])
        top = jnp.concatenate([s[:, :, 0], z], axis=-1)
        bot = jnp.concatenate([z, s[:, :, 1]], axis=-1)
        return jnp.concatenate([top, bot], axis=-2)
    return jnp.stack([bd(s_f), bd(s_b)], axis=2)


def _diag_states(st):
    b = st.shape[0]
    a = st[:, :, :HEAD_DIM, :HEAD_DIM]
    c = st[:, :, HEAD_DIM:, HEAD_DIM:]
    return jnp.stack([a, c], axis=2).reshape(b, N_HEADS, HEAD_DIM, HEAD_DIM)


def _na_kernel(q_ref, k_ref, v_ref, ck_ref, cv_ref, *refs):
    bias_refs, o_ref = refs[:NA_ROWS], refs[NA_ROWS]
    rows_total = k_ref.shape[0] // GRID_W
    lane = lax.broadcasted_iota(jnp.int32, (1, LANES), 1)
    head_a = lane < HEAD_DIM
    wins = []
    for i in range(NA_ROWS):
        r = pl.program_id(1) * NA_ROWS + i
        rs = jnp.clip(r - NA_KH // 2, 0, rows_total - NA_KH)
        wins.append(pl.ds(pl.multiple_of(rs * GRID_W, GRID_W), NA_KH * GRID_W))
    for hp in range(HEAD_PAIRS):
        cols = slice(hp * LANES, (hp + 1) * LANES)
        qq = jnp.concatenate(
            [_split_heads(q_ref[i * GRID_W:(i + 1) * GRID_W, cols], head_a, 0)
             for i in range(NA_ROWS)], axis=0)
        s_ctx = _dot_nt(qq, ck_ref[0, :, cols])
        m_ctx = jnp.max(s_ctx, axis=-1, keepdims=True)
        s_loc, m = [], []
        for i in range(NA_ROWS):
            blk = slice(i * 2 * GRID_W, (i + 1) * 2 * GRID_W)
            s = _dot_nt(qq[blk], k_ref[wins[i], cols]) + bias_refs[i][0, hp]
            s_loc.append(s)
            m.append(jnp.maximum(jnp.max(s, axis=-1, keepdims=True), m_ctx[blk]))
        e_ctx = jnp.exp(s_ctx - jnp.concatenate(m, axis=0))
        den_ctx = jnp.sum(e_ctx, axis=-1, keepdims=True)
        o_ctx = _dot(e_ctx.astype(BF16), cv_ref[0, :, cols])
        for i in range(NA_ROWS):
            blk = slice(i * 2 * GRID_W, (i + 1) * 2 * GRID_W)
            e = jnp.exp(s_loc[i] - m[i])
            den = jnp.sum(e, axis=-1, keepdims=True) + den_ctx[blk]
            o = (_dot(e.astype(BF16), v_ref[wins[i], cols]) + o_ctx[blk]) / den
            o_ref[i * GRID_W:(i + 1) * GRID_W, cols] = jnp.where(
                head_a, o[:GRID_W], o[GRID_W:]).astype(BF16)


def _bias_build_kernel(rpb_ref, o_ref, *, dr_first, n_dr):
    qc = lax.broadcasted_iota(jnp.int32, (GRID_W, LANES), 0)
    lane = lax.broadcasted_iota(jnp.int32, (GRID_W, LANES), 1)
    kc = lane & (GRID_W - 1)
    cs = jnp.clip(qc - NA_KW // 2, 0, GRID_W - NA_KW)
    inside = (kc >= cs) & (kc < cs + NA_KW)
    first = lane < GRID_W
    for h in range(N_HEADS):
        lo_half, hi_half = [], []
        for dr in range(n_dr):
            line = jnp.broadcast_to(rpb_ref[pl.ds(h * n_dr + dr, 1), :], (GRID_W, LANES))
            lo_half.append(pltpu.roll(line, LANES - (NA_KW - 1), 1, stride=1, stride_axis=0))
            hi_half.append(pltpu.roll(line, GRID_W - (NA_KW - 1), 1, stride=1, stride_axis=0))
        for cls, dr0 in enumerate(dr_first):
            for jp in range(NA_KH // 2):
                dr = dr0 + 2 * jp
                tile = jnp.where(inside, jnp.where(first, lo_half[dr], hi_half[dr + 1]), NEG_INF)
                o_ref[cls, h // 2, (h % 2) * GRID_W:(h % 2 + 1) * GRID_W,
                      jp * LANES:(jp + 1) * LANES] = tile


def _na_bias_tables(na_rpb, rows_total):
    depth, heads, n_dr, n_dc = na_rpb.shape
    kh = NA_KH
    lo, hi = kh // 2, rows_total - kh // 2 - 1
    reps = list(range(lo)) + [lo] + list(range(hi + 1, rows_total))
    dr_first = tuple(int(np.clip(r - kh // 2, 0, rows_total - kh)) - r + kh - 1 for r in reps)
    lines = jnp.pad(na_rpb.reshape(depth, heads * n_dr, n_dc), ((0, 0), (0, 0), (0, LANES - n_dc)))
    bias = pl.pallas_call(
        functools.partial(_bias_build_kernel, dr_first=dr_first, n_dr=n_dr),
        grid=(depth,),
        in_specs=[pl.BlockSpec((None, heads * n_dr, LANES), lambda l: (l, 0, 0))],
        out_specs=pl.BlockSpec((None, len(reps), HEAD_PAIRS, 2 * GRID_W, kh * GRID_W),
                               lambda l: (l, 0, 0, 0, 0)),
        out_shape=jax.ShapeDtypeStruct(
            (depth, len(reps), HEAD_PAIRS, 2 * GRID_W, kh * GRID_W), F32),
        compiler_params=_cparams(1),
        name="na_bias_build",
    )(lines)
    return bias, lo, hi


def _neighbourhood_attention(p, ctx_k, ctx_v, bias, layer, lo, hi, batch, seq):
    rows_total = seq // GRID_W
    past = ctx_k.shape[1]

    def cls(r):
        return jnp.where(r < lo, r, jnp.where(r > hi, r - hi + lo, lo))

    groups = rows_total // NA_ROWS
    bias_spec = lambda i: pl.BlockSpec(
        (None, 1, HEAD_PAIRS, 2 * GRID_W, NA_KH * GRID_W),
        lambda b, g: (layer, cls(g * NA_ROWS + i), 0, 0, 0))
    return pl.pallas_call(
        _na_kernel,
        grid=(batch, groups),
        in_specs=[
            pl.BlockSpec((NA_ROWS * GRID_W, WIDTH), lambda b, g: (b * groups + g, 4)),
            pl.BlockSpec((seq, WIDTH), lambda b, g: (b, 5)),
            pl.BlockSpec((seq, WIDTH), lambda b, g: (b, 6)),
            pl.BlockSpec((1, past, WIDTH), lambda b, g: (b, 0, 0)),
            pl.BlockSpec((1, past, WIDTH), lambda b, g: (b, 0, 0)),
        ] + [bias_spec(i) for i in range(NA_ROWS)],
        out_specs=pl.BlockSpec((NA_ROWS * GRID_W, WIDTH), lambda b, g: (b * groups + g, 0)),
        out_shape=jax.ShapeDtypeStruct((batch * seq, WIDTH), BF16),
        compiler_params=_cparams(2),
        name="neighbourhood_attention",
    )(p, p, p, ctx_k, ctx_v, *([bias] * NA_ROWS))


def _post_kernel(xm, xp, xn, am, ap, an, bm, bp, bn, mod_head, mod_tail, g_ref, wo, wu, cw, cb, wd,
                 o_ref, acc_ref, h_ref, hp_ref, x1_ref, y_ref, act_ref, *, seq_len, n_tiles):
    step = pl.program_id(0)
    tm = xm.shape[0]
    nj = tm // HALO
    tiles_per_seq = max(seq_len // tm, 1)
    assert (tm % seq_len == 0 and seq_len % nj == 0) or seq_len % tm == 0
    h_rows = h_ref.shape[1]
    seg_pitch = nj + HALO
    assert h_rows == 2 * HALO + HALO * seg_pitch and nj % HEAD_ROWS == 0
    n_ct = D_MODEL // LANES
    g = g_ref[...]

    def ext(main, prev, nxt):
        lo = prev[...].astype(F32)[prev.shape[0] - HALO:]
        hi = nxt[...].astype(F32)[:HALO]
        return jnp.concatenate([lo, main[...].astype(F32), hi], axis=0)

    chunks = [(c0, min(FF_CHUNK, D_FF - c0)) for c0 in range(0, D_FF, FF_CHUNK)]
    par = step % 2
    mod_h = mod_head[0]
    n_blocks = tm // HEAD_ROWS

    def head_matmul():
        mixed = jnp.concatenate([ext(am, ap, an), ext(bm, bp, bn)], axis=1).astype(BF16)
        y_ref[...] = _dot(mixed, wo[...])

    def head_rows(x, y):
        x1 = x + mod_h[2:3] * _rms(y, g[0:1])
        return x1, _rms(x1, g[1:2]) * (1.0 + mod_h[4:5]) + mod_h[3:4]

    def head_block(blk):
        if blk < n_blocks:
            rows = slice(blk * HEAD_ROWS, (blk + 1) * HEAD_ROWS)
            erows = slice(HALO + blk * HEAD_ROWS, HALO + (blk + 1) * HEAD_ROWS)
            x1, h = head_rows(xm[rows, :], y_ref[erows, :])
            x1_ref[par, rows, :] = x1
            t0 = blk * HEAD_ROWS
            hrow = HALO + (t0 // nj) * seg_pitch + t0 % nj
            for c in range(n_ct):
                h_ref[c, hrow:hrow + HEAD_ROWS, :] = h[:, c * LANES:(c + 1) * LANES]
        else:
            x = jnp.concatenate([xp[...], xn[...]], axis=0)
            y = jnp.concatenate([y_ref[0:HALO, :], y_ref[HALO + tm:2 * HALO + tm, :]], axis=0)
            _, h = head_rows(x, y)
            t = jnp.minimum(step, n_tiles - 1) % tiles_per_seq
            ridx = lax.broadcasted_iota(jnp.int32, h.shape, 0)
            keep = ((ridx >= HALO) | (t != 0)) & ((ridx < HALO) | (t != tiles_per_seq - 1))
            h = jnp.where(keep, h, 0.0)
            for c in range(n_ct):
                h_ref[c, 0:HALO, :] = h[:HALO, c * LANES:(c + 1) * LANES]
                h_ref[c, h_rows - HALO:h_rows, :] = h[HALO:, c * LANES:(c + 1) * LANES]

    def head_permute():
        def group(rows):
            return jnp.concatenate([h_ref[c, rows, :] for c in range(n_ct)], axis=1)
        def put(row0, first, second):
            hp_ref[row0:row0 + 2 * HALO, :] = jnp.concatenate(
                [group(first), group(second)], axis=0).astype(BF16)
        for j in range(0, nj, 2):
            put(j * HALO, pl.ds(HALO + j, HALO, stride=seg_pitch),
                pl.ds(HALO + j + 1, HALO, stride=seg_pitch))
        put(tm, pl.ds(0, HALO), pl.ds(h_rows - HALO, HALO))

    def tail(interleaved):
        mod = mod_tail[0]

        def conv(u, cols):
            w = cw[:, cols]
            sub = lax.broadcasted_iota(jnp.int32, (HALO, u.shape[1]), 0)
            before = jnp.where(sub == 0, u[tm + HALO - 1:tm + HALO],
                               pltpu.roll(u[tm - HALO:tm], 1, 0))
            after = jnp.where(sub == HALO - 1, u[tm + HALO:tm + HALO + 1],
                              pltpu.roll(u[0:HALO], HALO - 1, 0))
            for s in range(1, HALO):
                if (s * nj) % seq_len == 0:
                    before = jnp.where(sub == s, 0.0, before)
                    after = jnp.where(sub == s - 1, 0.0, after)
            prev = jnp.concatenate([before, u[0:tm - HALO]], axis=0)
            nxt = jnp.concatenate([u[HALO:tm], after], axis=0)
            return prev * w[0:1] + u[0:tm] * w[1:2] + nxt * w[2:3] + cb[:, cols]

        def cols(ch):
            ca = slice(chunks[ch][0], chunks[ch][0] + chunks[ch][1])
            return ca, slice(D_FF + ca.start, D_FF + ca.stop)

        def up(ch):
            ca, cg = cols(ch)
            return _dot(hp_ref[...], wu[:, cg]), _dot(hp_ref[...], wu[:, ca])

        nxt_u = up(0)
        for ch in range(len(chunks)):
            ug, ua = nxt_u
            if ch + 1 < len(chunks):
                nxt_u = up(ch + 1)
            for blk in interleaved[ch]:
                head_block(blk)
            ca, cg = cols(ch)
            act_ref[:, ca] = (_silu(conv(ug, cg)) * conv(ua, ca)).astype(BF16)
        ffn = mod[5:6] * _rms(_dot(act_ref[...], wd[...]), g[2:3])
        for c in range(n_ct):
            acc_ref[c] = ffn[:, c * LANES:(c + 1) * LANES]

        for s in range(HALO):
            for jb in range(nj // HALO):
                t0 = s * nj + HALO * jb
                rows = pl.ds(HALO * HALO * jb + s, HALO, stride=HALO)
                back = jnp.concatenate([acc_ref[c, rows, :] for c in range(n_ct)], axis=1)
                o_ref[t0:t0 + HALO, :] = x1_ref[1 - par, t0:t0 + HALO, :] + back

    @pl.when(step == 0)
    def _():
        head_matmul()
        for blk in range(n_blocks + 1):
            head_block(blk)
        head_permute()

    @pl.when(step > 0)
    def _():
        head_matmul()
        tail([[blk for blk in range(n_blocks + 1) if blk * len(chunks) // (n_blocks + 1) == ch]
              for ch in range(len(chunks))])
        head_permute()


def _post(x, mix_a, mix_b, mods, gains, wo, wu, cw, cb, wd, layer, seq_len, tm, latent):
    n = x.shape[0]
    nt = n // tm
    (mix_a, col_a), (mix_b, col_b) = mix_a, mix_b
    tiles_per_seq = max(seq_len // tm, 1)
    cond_row = (lambda i: 1 + i // tiles_per_seq) if latent else (lambda i: 0)

    head_tile = lambda i: jnp.minimum(i, nt - 1)
    tail_tile = lambda i: jnp.maximum(i - 1, 0)

    def triple(width, halo_rows, col=0):
        per = tm // halo_rows
        last = n // halo_rows - 1
        return [
            pl.BlockSpec((tm, width), lambda i: (head_tile(i), col)),
            pl.BlockSpec((halo_rows, width),
                         lambda i: (jnp.maximum(head_tile(i) * per - 1, 0), col)),
            pl.BlockSpec((halo_rows, width),
                         lambda i: (jnp.minimum((head_tile(i) + 1) * per, last), col)),
        ]

    const = lambda *shape: pl.BlockSpec((None,) + shape, lambda i: (layer,) + (0,) * len(shape),
                                        pipeline_mode=pl.Buffered(1))
    in_specs = (triple(D_MODEL, HALO) + triple(WIDTH, 2 * HALO, col_a)
                + triple(WIDTH, 2 * HALO, col_b) + [
        pl.BlockSpec((None, 1, 6, D_MODEL), lambda i: (layer, cond_row(head_tile(i)), 0, 0)),
        pl.BlockSpec((None, 1, 6, D_MODEL), lambda i: (layer, cond_row(tail_tile(i)), 0, 0)),
        const(3, D_MODEL),
        const(2 * WIDTH, D_MODEL),
        const(D_MODEL, 2 * D_FF),
        const(3, 2 * D_FF),
        const(1, 2 * D_FF),
        const(D_FF, D_MODEL),
    ])
    return pl.pallas_call(
        functools.partial(_post_kernel, seq_len=seq_len, n_tiles=nt),
        grid=(nt + 1,),
        in_specs=in_specs,
        out_specs=pl.BlockSpec((tm, D_MODEL), lambda i: (tail_tile(i), 0)),
        out_shape=jax.ShapeDtypeStruct((n, D_MODEL), F32),
        scratch_shapes=[pltpu.VMEM((D_MODEL // LANES, tm, LANES), F32),
                        pltpu.VMEM((D_MODEL // LANES, 2 * HALO + HALO * (tm // HALO + HALO), LANES),
                                   F32),
                        pltpu.VMEM((tm + 2 * HALO, D_MODEL), BF16),
                        pltpu.VMEM((2, tm, D_MODEL), F32),
                        pltpu.VMEM((tm + 2 * HALO, D_MODEL), F32),
                        pltpu.VMEM((tm, D_FF), BF16)],
        compiler_params=_cparams(1),
        name="post_latent" if latent else "post_context",
    )(x, x, x, mix_a, mix_a, mix_a, mix_b, mix_b, mix_b, mods, mods, gains, wo, wu, cw, cb, wd)


def kernel(x_prompt, x_sample, c, cache_na_k, cache_na_v, state_ret_fwd, state_ret_bwd, c_ctx,
           ada_w, ada_b, g_pre_mix, g_post_mix, g_pre_ffn, g_post_ffn, w_in,
           ret_decay_fwd, ret_decay_bwd, na_rpb, w_out, w_up, conv_w, conv_b, w_down):
    depth = w_in.shape[0]
    batch, seq, _ = x_prompt.shape
    dec_batch, dec_seq, _ = x_sample.shape
    past = cache_na_k.shape[2]
    tm = 512

    cond = jnp.concatenate(
        [c_ctx[None, :], c, jnp.zeros((8 - 1 - dec_batch, D_MODEL), F32)], axis=0)
    mods = _modulation(cond, ada_w, ada_b).reshape(depth, 8, 6, D_MODEL)
    rope_tabs = _rope_tables(dec_seq)
    bias, lo, hi = _na_bias_tables(na_rpb, dec_seq // GRID_W)

    w_in_b, wo_b, wu_b, wd_b = (w.astype(BF16) for w in (w_in, w_out, w_up, w_down))
    g_pre = g_pre_mix.reshape(depth, 1, D_MODEL)
    gains = jnp.stack([g_post_mix, g_pre_ffn, g_post_ffn], axis=1)
    cb = conv_b.reshape(depth, 1, 2 * D_FF)
    post_params = (mods, gains, wo_b, wu_b, conv_w, cb, wd_b)

    y_p = x_prompt.reshape(batch * seq, D_MODEL)
    y_s = x_sample.reshape(dec_batch * dec_seq, D_MODEL)
    new_k = new_v = None
    sfs, sbs = [], []
    for l in range(depth):
        dec = (jnp.repeat(ret_decay_fwd[l], HEAD_DIM)[None, :],
               jnp.repeat(ret_decay_bwd[l], HEAD_DIM)[None, :],
               jnp.repeat(ret_decay_fwd[l], LANES)[None, :],
               jnp.repeat(ret_decay_bwd[l], LANES)[None, :])

        p_c, new_k, new_v = _inproj(y_p, mods, g_pre, w_in_b, l, seq, tm, kv_out=(new_k, new_v))
        mix_c, st_c = _ctx_mixer(p_c, dec, batch, seq)
        y_p = _post(y_p, (mix_c, 0), (mix_c, 1), *post_params, l, seq, tm, latent=False)
        sfs.append(_diag_states(st_c[:, :, 0]))
        sbs.append(_diag_states(st_c[:, :, 1]))

        (p_s,) = _inproj(y_s, mods, g_pre, w_in_b, l, dec_seq, tm, rope_tabs=rope_tabs)
        s0 = _block_diag_states(state_ret_fwd[:, l], state_ret_bwd[:, l])
        ret_s = _lat_retention(p_s, dec, s0, dec_batch, dec_seq)
        ck = cache_na_k[:, l].reshape(dec_batch, past, WIDTH).astype(BF16)
        cv = cache_na_v[:, l].reshape(dec_batch, past, WIDTH).astype(BF16)
        na_s = _neighbourhood_attention(p_s, ck, cv, bias, l, lo, hi, dec_batch, dec_seq)
        y_s = _post(y_s, (ret_s, 0), (na_s, 0), *post_params, l, dec_seq, tm, latent=True)

    return (y_p.reshape(batch, seq, D_MODEL),
            y_s.reshape(dec_batch, dec_seq, D_MODEL),
            new_k.reshape(batch, depth, seq, N_HEADS, HEAD_DIM),
            new_v.reshape(batch, depth, seq, N_HEADS, HEAD_DIM),
            jnp.stack(sfs, axis=1), jnp.stack(sbs, axis=1))
```

```python
import functools

import numpy as np
import jax
import jax.numpy as jnp
from jax import lax
from jax.experimental import pallas as pl
from jax.experimental.pallas import tpu as pltpu

F32 = jnp.float32
BF16 = jnp.bfloat16

D_MODEL = 1024
HEAD_DIM = 64
N_HEADS = 8
HEAD_PAIRS = N_HEADS // 2
LANES = 128
WIDTH = N_HEADS * HEAD_DIM
N_GROUPS = 7
IN_WIDTH = N_GROUPS * WIDTH
D_FF = 2816
FF_CHUNK = 768
HEAD_ROWS = 32
CHUNK = 128
CTX_SEQS_PER_STEP = 2
RET_UNROLL = 8
GRID_W = 64
NA_KH = 8
NA_KW = 16
NA_STAGE_ROWS = 8
NA_ROWS = 8
ROPE_BASE = 10000.0
EPS = 1e-6
NEG_INF = -1e9
HALO = 8
VMEM_LIMIT = 56 * 1024 * 1024


def _cparams(n_grid):
    return pltpu.CompilerParams(
        dimension_semantics=("arbitrary",) * n_grid, vmem_limit_bytes=VMEM_LIMIT)


def _rms(x, g):
    ms = jnp.mean(x * x, axis=-1, keepdims=True)
    return x * lax.rsqrt(ms + EPS) * g


def _silu(x):
    return x * jax.nn.sigmoid(x)


def _log_sigmoid(x):
    return jnp.minimum(x, 0.0) - jnp.log1p(jnp.exp(-jnp.abs(x)))


def _dot(a, b):
    return jnp.dot(a, b, preferred_element_type=F32)


def _dot_nt(a, b):
    return lax.dot_general(a, b, (((1,), (1,)), ((), ())), preferred_element_type=F32)


def _dot_tn(a, b):
    return lax.dot_general(a, b, (((0,), (0,)), ((), ())), preferred_element_type=F32)


def _mod_kernel(cond_ref, w_ref, b_ref, o_ref):
    s = _silu(cond_ref[...]).astype(BF16)
    o_ref[0] = _dot(s, w_ref[0].astype(BF16)) + b_ref[0]


def _modulation(cond, ada_w, ada_b):
    depth = ada_w.shape[0]
    nb = 6 * D_MODEL // D_MODEL
    return pl.pallas_call(
        _mod_kernel,
        grid=(depth, nb),
        in_specs=[
            pl.BlockSpec((8, D_MODEL), lambda l, j: (0, 0)),
            pl.BlockSpec((1, D_MODEL, D_MODEL), lambda l, j: (l, 0, j)),
            pl.BlockSpec((1, 1, D_MODEL), lambda l, j: (l, 0, j)),
        ],
        out_specs=pl.BlockSpec((1, 8, D_MODEL), lambda l, j: (l, 0, j)),
        out_shape=jax.ShapeDtypeStruct((depth, 8, 6 * D_MODEL), F32),
        compiler_params=_cparams(2),
        name="modulation",
    )(cond, ada_w, ada_b.reshape(depth, 1, 6 * D_MODEL))


def _inproj_kernel(*refs, rope, emit_kv, n_alias, layer):
    x_ref, mod_ref, g_ref, w_ref = refs[:4]
    refs = refs[4:]
    if rope:
        cos_ref, sin_up_ref, sin_dn_ref = refs[:3]
        refs = refs[3:]
    refs = refs[n_alias:]
    p_ref = refs[0]
    mod = mod_ref[0]
    h = (_rms(x_ref[...], g_ref[...]) * (1.0 + mod[1:2]) + mod[0:1]).astype(BF16)
    for g in range(N_GROUPS):
        cols = slice(g * WIDTH, (g + 1) * WIDTH)
        pg = _dot(h, w_ref[:, cols])
        if rope and g < 2:
            parts = []
            for j in range(WIDTH // LANES):
                xj = pg[:, j * LANES:(j + 1) * LANES]
                parts.append(xj * cos_ref[...]
                             + pltpu.roll(xj, 16, 1) * sin_up_ref[...]
                             + pltpu.roll(xj, LANES - 16, 1) * sin_dn_ref[...])
            pg = jnp.concatenate(parts, axis=1)
        if g in (0, 4):
            pg = pg * (HEAD_DIM ** -0.5)
        p_ref[:, cols] = pg.astype(BF16)
        if emit_kv and g >= 5:
            kv_ref = refs[g - 4]
            seq = kv_ref.shape[-2]
            for j in range(kv_ref.shape[0]):
                if n_alias:
                    kv_ref[j] = pg[j * seq:(j + 1) * seq]
                else:
                    for l in range(kv_ref.shape[1]):
                        kv_ref[j, l] = (pg[j * seq:(j + 1) * seq] if l == layer
                                        else jnp.zeros((seq, WIDTH), F32))


def _inproj(x, mods, g_pre, w, layer, seq_len, tm, rope_tabs=None, kv_out=None):
    n = x.shape[0]
    depth = w.shape[0]
    tiles_per_seq = max(seq_len // tm, 1)
    cond_row = (lambda i: 0) if rope_tabs is None else (lambda i: 1 + i // tiles_per_seq)
    in_specs = [
        pl.BlockSpec((tm, D_MODEL), lambda i: (i, 0)),
        pl.BlockSpec((None, 1, 6, D_MODEL), lambda i: (layer, cond_row(i), 0, 0)),
        pl.BlockSpec((None, 1, D_MODEL), lambda i: (layer, 0, 0)),
        pl.BlockSpec((None, D_MODEL, IN_WIDTH), lambda i: (layer, 0, 0),
                     pipeline_mode=pl.Buffered(1)),
    ]
    args = [x, mods, g_pre, w]
    if rope_tabs is not None:
        in_specs += [pl.BlockSpec((tm, LANES), lambda i: (i % tiles_per_seq, 0))] * 3
        args += list(rope_tabs)
    out_specs = [pl.BlockSpec((tm, IN_WIDTH), lambda i: (i, 0))]
    out_shape = [jax.ShapeDtypeStruct((n, IN_WIDTH), BF16)]
    aliases = {}
    if kv_out is not None:
        seqs_per_tile = tm // seq_len
        kv_shape = jax.ShapeDtypeStruct((n // seq_len, depth, seq_len, WIDTH), F32)
        assert (kv_out[0] is None) == (kv_out[1] is None)
        for j, prev in enumerate(kv_out):
            if prev is not None:
                aliases[len(args)] = 1 + j
                in_specs.append(pl.BlockSpec(memory_space=pl.ANY))
                args.append(prev)
                out_specs.append(pl.BlockSpec((seqs_per_tile, None, seq_len, WIDTH),
                                              lambda i: (i, layer, 0, 0)))
            else:
                out_specs.append(pl.BlockSpec((seqs_per_tile, depth, seq_len, WIDTH),
                                              lambda i: (i, 0, 0, 0)))
            out_shape.append(kv_shape)
    return pl.pallas_call(
        functools.partial(_inproj_kernel, rope=rope_tabs is not None,
                          emit_kv=kv_out is not None, n_alias=len(aliases), layer=layer),
        grid=(n // tm,),
        in_specs=in_specs,
        out_specs=out_specs,
        out_shape=out_shape,
        input_output_aliases=aliases,
        compiler_params=_cparams(1),
        name="inproj_latent" if rope_tabs is not None else "inproj_context",
    )(*args)


def _rope_tables(seq_len):
    t = np.arange(seq_len)
    lane = np.arange(LANES)
    d = lane % HEAD_DIM
    pos = np.where(d[None, :] < HEAD_DIM // 2, (t // GRID_W)[:, None], (t % GRID_W)[:, None])
    pos = pos.astype(np.float32)
    half = HEAD_DIM // 2
    inv = np.power(np.float32(ROPE_BASE), -np.arange(0, half, 2, dtype=np.float32) / half)
    ang = pos * inv[d % (half // 2)][None, :]
    cos, sin = np.cos(ang), np.sin(ang)
    upper = (d % half) >= half // 2
    sin_up = np.where(upper[None, :], sin, 0.0)
    sin_dn = np.where(upper[None, :], 0.0, -sin)
    return (jnp.asarray(cos, F32), jnp.asarray(sin_up, F32), jnp.asarray(sin_dn, F32))


def _retention_tables(dec_f, dec_b, dec_f2, dec_b2):
    lgf, lgb = _log_sigmoid(dec_f), _log_sigmoid(dec_b)
    pos = lax.broadcasted_iota(jnp.int32, (CHUNK, LANES), 0).astype(F32)
    tabs = dict(
        qdf=jnp.exp(lgf * (pos + 1.0)), kdf=jnp.exp(lgf * (CHUNK - 1.0 - pos)),
        cdf=jnp.exp(lgf * float(CHUNK)),
        qdb=jnp.exp(lgb * (CHUNK - pos)), kdb=jnp.exp(lgb * pos),
        cdb=jnp.exp(lgb * float(CHUNK)),
    )
    lgf2, lgb2 = _log_sigmoid(dec_f2), _log_sigmoid(dec_b2)
    i = lax.broadcasted_iota(jnp.int32, (CHUNK, 2 * CHUNK), 0)
    j = lax.broadcasted_iota(jnp.int32, (CHUNK, 2 * CHUNK), 1) & (CHUNK - 1)
    diff = (i - j).astype(F32)
    tabs["decay"] = (jnp.where(diff >= 0, jnp.exp(lgf2 * jnp.maximum(diff, 0.0)), 0.0)
                     + jnp.where(diff <= 0, jnp.exp(lgb2 * jnp.maximum(-diff, 0.0)), 0.0))
    lane = lax.broadcasted_iota(jnp.int32, (1, LANES), 1)
    tabs["head_a"] = lane < HEAD_DIM
    r = lax.broadcasted_iota(jnp.int32, (2 * LANES, LANES), 0) & (LANES - 1)
    c = lax.broadcasted_iota(jnp.int32, (2 * LANES, LANES), 1)
    tabs["same_head"] = (r < HEAD_DIM) == (c < HEAD_DIM)
    return tabs


def _split_heads(x, head_a, axis):
    zero = jnp.zeros_like(x)
    return jnp.concatenate([jnp.where(head_a, x, zero), jnp.where(head_a, zero, x)], axis=axis)


def _chunk_kv(k2, v2, tabs):
    kf = k2.astype(F32)
    kk = jnp.concatenate([kf * tabs["kdf"], kf * tabs["kdb"]], axis=1).astype(BF16)
    return jnp.where(tabs["same_head"], _dot_tn(kk, v2), 0.0)


def _chunk_out(q2, k2, v2, g2, state, tabs):
    head_a = tabs["head_a"]
    s = _dot_nt(q2, _split_heads(k2, head_a, 0))
    p = (s * tabs["decay"]).astype(BF16)
    qf = q2.astype(F32)
    lhs = jnp.concatenate(
        [p, (qf * tabs["qdf"]).astype(BF16), (qf * tabs["qdb"]).astype(BF16)], axis=1)
    rhs = jnp.concatenate([_split_heads(v2, head_a, 0), state], axis=0)
    o = _dot(lhs, rhs)
    inv = 1.0 / HEAD_DIM
    sum_a = jnp.sum(jnp.where(head_a, o, 0.0), axis=-1, keepdims=True)
    sum_b = jnp.sum(jnp.where(head_a, 0.0, o), axis=-1, keepdims=True)
    d = o - jnp.where(head_a, sum_a, sum_b) * inv
    d2 = d * d
    var_a = jnp.sum(jnp.where(head_a, d2, 0.0), axis=-1, keepdims=True)
    var_b = jnp.sum(jnp.where(head_a, 0.0, d2), axis=-1, keepdims=True)
    o = d * lax.rsqrt(jnp.where(head_a, var_a, var_b) * inv + EPS)
    return o * _silu(g2.astype(F32))


def _ctx_mixer_kernel(rq, rk, rv, rg, nq, nk, nv, df, db, df2, db2, mix_ref, st_ref):
    n_seq = st_ref.shape[0]
    seq = rq.shape[0] // n_seq
    nc = seq // CHUNK
    for hp in range(HEAD_PAIRS):
        cols = slice(hp * LANES, (hp + 1) * LANES)
        cols2 = slice(hp * 2 * LANES, (hp + 1) * 2 * LANES)
        tabs = _retention_tables(df[:, cols], db[:, cols], df2[:, cols2], db2[:, cols2])
        head_a = tabs["head_a"]
        for b in range(n_seq):
            rows = [slice(b * seq + c * CHUNK, b * seq + (c + 1) * CHUNK) for c in range(nc)]
            kv = [_chunk_kv(rk[r, cols], rv[r, cols], tabs) for r in rows]
            sf = [jnp.zeros((LANES, LANES), F32)]
            for c in range(nc):
                sf.append(sf[-1] * tabs["cdf"] + kv[c][:LANES])
            sb = [jnp.zeros((LANES, LANES), F32)]
            for c in reversed(range(nc)):
                sb.append(sb[-1] * tabs["cdb"] + kv[c][LANES:])
            for c in range(nc):
                state = jnp.concatenate([sf[c], sb[nc - 1 - c]], axis=0).astype(BF16)
                o = _chunk_out(rq[rows[c], cols], rk[rows[c], cols], rv[rows[c], cols],
                               rg[rows[c], cols], state, tabs)
                mix_ref[rows[c], cols] = o.astype(BF16)
            st_ref[b, hp, 0] = sf[nc]
            st_ref[b, hp, 1] = sb[nc]

            tok = slice(b * seq, (b + 1) * seq)
            s = _dot_nt(nq[tok, cols], _split_heads(nk[tok, cols], head_a, 0))
            es, rinv = [], []
            for h in range(2):
                sh = s[:, h * seq:(h + 1) * seq]
                e = jnp.exp(sh - jnp.max(sh, axis=-1, keepdims=True))
                rinv.append(1.0 / jnp.sum(e, axis=-1, keepdims=True))
                es.append(e.astype(BF16))
            o = _dot(jnp.concatenate(es, axis=1), _split_heads(nv[tok, cols], head_a, 0))
            o = o * jnp.where(head_a, rinv[0], rinv[1])
            mix_ref[tok, WIDTH + hp * LANES:WIDTH + (hp + 1) * LANES] = o.astype(BF16)


def _ctx_mixer(p, dec, batch, seq):
    n_seq = CTX_SEQS_PER_STEP
    group = lambda g: pl.BlockSpec((n_seq * seq, WIDTH), lambda b, g=g: (b, g))
    vec = lambda w: pl.BlockSpec((1, w), lambda b: (0, 0))
    return pl.pallas_call(
        _ctx_mixer_kernel,
        grid=(batch // n_seq,),
        in_specs=[group(g) for g in range(N_GROUPS)] + [vec(WIDTH), vec(WIDTH),
                                                         vec(2 * WIDTH), vec(2 * WIDTH)],
        out_specs=[
            pl.BlockSpec((n_seq * seq, 2 * WIDTH), lambda b: (b, 0)),
            pl.BlockSpec((n_seq, HEAD_PAIRS, 2, LANES, LANES), lambda b: (b, 0, 0, 0, 0)),
        ],
        out_shape=[
            jax.ShapeDtypeStruct((batch * seq, 2 * WIDTH), BF16),
            jax.ShapeDtypeStruct((batch, HEAD_PAIRS, 2, LANES, LANES), F32),
        ],
        compiler_params=_cparams(1),
        name="context_mixer",
    )(*([p] * N_GROUPS), *dec)


def _lat_retention_kernel(q_ref, k_ref, v_ref, g_ref, df, db, df2, db2, s0_ref, o_ref,
                          kv_ref, st_ref):
    nc = q_ref.shape[0] // CHUNK
    tabs = _retention_tables(df[...], db[...], df2[...], db2[...])

    def rows(c):
        return pl.ds(pl.multiple_of(c * CHUNK, CHUNK), CHUNK)

    def kv_body(c, carry):
        kv_ref[c] = _chunk_kv(k_ref[rows(c), :], v_ref[rows(c), :], tabs)
        return carry
    lax.fori_loop(0, nc, kv_body, 0, unroll=RET_UNROLL)

    def fwd_body(c, s):
        st_ref[c, :LANES, :] = s.astype(BF16)
        return s * tabs["cdf"] + kv_ref[c, :LANES, :]
    lax.fori_loop(0, nc, fwd_body, s0_ref[0, 0, 0])

    def bwd_body(i, s):
        c = nc - 1 - i
        st_ref[c, LANES:, :] = s.astype(BF16)
        return s * tabs["cdb"] + kv_ref[c, LANES:, :]
    lax.fori_loop(0, nc, bwd_body, s0_ref[0, 0, 1])

    def out_body(c, carry):
        r = rows(c)
        o = _chunk_out(q_ref[r, :], k_ref[r, :], v_ref[r, :], g_ref[r, :], st_ref[c], tabs)
        o_ref[r, :] = o.astype(BF16)
        return carry
    lax.fori_loop(0, nc, out_body, 0, unroll=RET_UNROLL)


def _lat_retention(p, dec, s0, batch, seq):
    nc = seq // CHUNK
    group = lambda g: pl.BlockSpec((seq, LANES), lambda b, h, g=g: (b, g * HEAD_PAIRS + h))
    vec = lambda w: pl.BlockSpec((1, w), lambda b, h: (0, h))
    return pl.pallas_call(
        _lat_retention_kernel,
        grid=(batch, HEAD_PAIRS),
        in_specs=[group(g) for g in range(4)] + [vec(LANES), vec(LANES), vec(2 * LANES),
                                                 vec(2 * LANES)]
        + [pl.BlockSpec((1, 1, 2, LANES, LANES), lambda b, h: (b, h, 0, 0, 0))],
        out_specs=pl.BlockSpec((seq, LANES), lambda b, h: (b, h)),
        out_shape=jax.ShapeDtypeStruct((batch * seq, WIDTH), BF16),
        scratch_shapes=[pltpu.VMEM((nc, 2 * LANES, LANES), F32),
                        pltpu.VMEM((nc, 2 * LANES, LANES), BF16)],
        compiler_params=_cparams(2),
        name="latent_retention",
    )(*([p] * 4), *dec, s0)


def _block_diag_states(s_f, s_b):
    def bd(s):
        b = s.shape[0]
        s = s.reshape(b, HEAD_PAIRS, 2, HEAD_DIM, HEAD_DIM)
        z = jnp.zeros_like(s[:, :, 0])
        top = jnp.concatenate([s[:, :, 0], z], axis=-1)
        bot = jnp.concatenate([z, s[:, :, 1]], axis=-1)
        return jnp.concatenate([top, bot], axis=-2)
    return jnp.stack([bd(s_f), bd(s_b)], axis=2)


def _diag_states(st):
    b = st.shape[0]
    a = st[:, :, :HEAD_DIM, :HEAD_DIM]
    c = st[:, :, HEAD_DIM:, HEAD_DIM:]
    return jnp.stack([a, c], axis=2).reshape(b, N_HEADS, HEAD_DIM, HEAD_DIM)


def _na_kernel(q_ref, k_ref, v_ref, ck_ref, cv_ref, bias_ref, o_ref, *, lo, hi):
    rows_total = k_ref.shape[0] // GRID_W
    lane = lax.broadcasted_iota(jnp.int32, (1, LANES), 1)
    head_a = lane < HEAD_DIM
    wins, cls = [], []
    for i in range(NA_ROWS):
        r = pl.program_id(1) * NA_ROWS + i
        rs = jnp.clip(r - NA_KH // 2, 0, rows_total - NA_KH)
        wins.append(pl.ds(pl.multiple_of(rs * GRID_W, GRID_W), NA_KH * GRID_W))
        cls.append(jnp.where(r < lo, r, jnp.where(r > hi, r - hi + lo, lo)))
    stages = [(hp, r0) for hp in range(HEAD_PAIRS) for r0 in range(0, NA_ROWS, NA_STAGE_ROWS)]

    def scores(stage):
        hp, r0 = stage
        cols = slice(hp * LANES, (hp + 1) * LANES)
        qq = jnp.concatenate(
            [_split_heads(q_ref[i * GRID_W:(i + 1) * GRID_W, cols], head_a, 0)
             for i in range(r0, r0 + NA_STAGE_ROWS)], axis=0)
        s_ctx = _dot_nt(qq, ck_ref[0, :, cols])
        s_loc = [_dot_nt(qq[j * 2 * GRID_W:(j + 1) * 2 * GRID_W], k_ref[wins[r0 + j], cols])
                 for j in range(NA_STAGE_ROWS)]
        return s_ctx, s_loc

    def softmax(stage, s_ctx, s_loc):
        hp, r0 = stage
        m_ctx = jnp.max(s_ctx, axis=-1, keepdims=True)
        m = []
        for j in range(NA_STAGE_ROWS):
            blk = slice(j * 2 * GRID_W, (j + 1) * 2 * GRID_W)
            s_loc[j] = s_loc[j] + bias_ref[cls[r0 + j], hp]
            m.append(jnp.maximum(jnp.max(s_loc[j], axis=-1, keepdims=True), m_ctx[blk]))
        e_ctx = jnp.exp(s_ctx - jnp.concatenate(m, axis=0))
        den_ctx = jnp.sum(e_ctx, axis=-1, keepdims=True)
        e_loc, den = [], []
        for j in range(NA_STAGE_ROWS):
            blk = slice(j * 2 * GRID_W, (j + 1) * 2 * GRID_W)
            e = jnp.exp(s_loc[j] - m[j])
            den.append(jnp.sum(e, axis=-1, keepdims=True) + den_ctx[blk])
            e_loc.append(e.astype(BF16))
        return e_ctx.astype(BF16), e_loc, den

    def weighted_values(stage, e_ctx, e_loc, den):
        hp, r0 = stage
        cols = slice(hp * LANES, (hp + 1) * LANES)
        o_ctx = _dot(e_ctx, cv_ref[0, :, cols])
        for j in range(NA_STAGE_ROWS):
            i = r0 + j
            blk = slice(j * 2 * GRID_W, (j + 1) * 2 * GRID_W)
            o = (_dot(e_loc[j], v_ref[wins[i], cols]) + o_ctx[blk]) / den[j]
            o_ref[i * GRID_W:(i + 1) * GRID_W, cols] = jnp.where(
                head_a, o[:GRID_W], o[GRID_W:]).astype(BF16)

    nxt = scores(stages[0])
    pending = None
    for n, stage in enumerate(stages):
        s_ctx, s_loc = nxt
        if n + 1 < len(stages):
            nxt = scores(stages[n + 1])
        probs = softmax(stage, s_ctx, s_loc)
        if pending is not None:
            weighted_values(*pending)
        pending = (stage,) + probs
    weighted_values(*pending)


def _bias_build_kernel(rpb_ref, o_ref, *, dr_first, n_dr):
    qc = lax.broadcasted_iota(jnp.int32, (GRID_W, LANES), 0)
    lane = lax.broadcasted_iota(jnp.int32, (GRID_W, LANES), 1)
    kc = lane & (GRID_W - 1)
    cs = jnp.clip(qc - NA_KW // 2, 0, GRID_W - NA_KW)
    inside = (kc >= cs) & (kc < cs + NA_KW)
    first = lane < GRID_W
    for h in range(N_HEADS):
        lo_half, hi_half = [], []
        for dr in range(n_dr):
            line = jnp.broadcast_to(rpb_ref[pl.ds(h * n_dr + dr, 1), :], (GRID_W, LANES))
            lo_half.append(pltpu.roll(line, LANES - (NA_KW - 1), 1, stride=1, stride_axis=0))
            hi_half.append(pltpu.roll(line, GRID_W - (NA_KW - 1), 1, stride=1, stride_axis=0))
        for cls, dr0 in enumerate(dr_first):
            for jp in range(NA_KH // 2):
                dr = dr0 + 2 * jp
                tile = jnp.where(inside, jnp.where(first, lo_half[dr], hi_half[dr + 1]), NEG_INF)
                o_ref[cls, h // 2, (h % 2) * GRID_W:(h % 2 + 1) * GRID_W,
                      jp * LANES:(jp + 1) * LANES] = tile


def _na_bias_tables(na_rpb, rows_total):
    depth, heads, n_dr, n_dc = na_rpb.shape
    kh = NA_KH
    lo, hi = kh // 2, rows_total - kh // 2 - 1
    reps = list(range(lo)) + [lo] + list(range(hi + 1, rows_total))
    dr_first = tuple(int(np.clip(r - kh // 2, 0, rows_total - kh)) - r + kh - 1 for r in reps)
    lines = jnp.pad(na_rpb.reshape(depth, heads * n_dr, n_dc), ((0, 0), (0, 0), (0, LANES - n_dc)))
    bias = pl.pallas_call(
        functools.partial(_bias_build_kernel, dr_first=dr_first, n_dr=n_dr),
        grid=(depth,),
        in_specs=[pl.BlockSpec((None, heads * n_dr, LANES), lambda l: (l, 0, 0))],
        out_specs=pl.BlockSpec((None, len(reps), HEAD_PAIRS, 2 * GRID_W, kh * GRID_W),
                               lambda l: (l, 0, 0, 0, 0)),
        out_shape=jax.ShapeDtypeStruct(
            (depth, len(reps), HEAD_PAIRS, 2 * GRID_W, kh * GRID_W), F32),
        compiler_params=_cparams(1),
        name="na_bias_build",
    )(lines)
    return bias, lo, hi


def _neighbourhood_attention(p, ctx_k, ctx_v, bias, layer, lo, hi, batch, seq):
    rows_total = seq // GRID_W
    past = ctx_k.shape[1]

    groups = rows_total // NA_ROWS
    return pl.pallas_call(
        functools.partial(_na_kernel, lo=lo, hi=hi),
        grid=(batch, groups),
        in_specs=[
            pl.BlockSpec((NA_ROWS * GRID_W, WIDTH), lambda b, g: (b * groups + g, 4)),
            pl.BlockSpec((seq, WIDTH), lambda b, g: (b, 5), pipeline_mode=pl.Buffered(1)),
            pl.BlockSpec((seq, WIDTH), lambda b, g: (b, 6), pipeline_mode=pl.Buffered(1)),
            pl.BlockSpec((1, past, WIDTH), lambda b, g: (b, 0, 0), pipeline_mode=pl.Buffered(1)),
            pl.BlockSpec((1, past, WIDTH), lambda b, g: (b, 0, 0), pipeline_mode=pl.Buffered(1)),
            pl.BlockSpec((None,) + bias.shape[1:], lambda b, g: (layer, 0, 0, 0, 0),
                         pipeline_mode=pl.Buffered(1)),
        ],
        out_specs=pl.BlockSpec((NA_ROWS * GRID_W, WIDTH), lambda b, g: (b * groups + g, 0)),
        out_shape=jax.ShapeDtypeStruct((batch * seq, WIDTH), BF16),
        compiler_params=_cparams(2),
        name="neighbourhood_attention",
    )(p, p, p, ctx_k, ctx_v, bias)


def _post_kernel(xm, xp, xn, am, ap, an, bm, bp, bn, mod_head, mod_tail, g_ref, wo, wu, cw, cb, wd,
                 o_ref, acc_ref, h_ref, hp_ref, x1_ref, y_ref, act_ref, *, seq_len, n_tiles):
    step = pl.program_id(0)
    tm = xm.shape[0]
    nj = tm // HALO
    tiles_per_seq = max(seq_len // tm, 1)
    assert (tm % seq_len == 0 and seq_len % nj == 0) or seq_len % tm == 0
    h_rows = h_ref.shape[1]
    seg_pitch = nj + HALO
    assert h_rows == 2 * HALO + HALO * seg_pitch and nj % HEAD_ROWS == 0
    n_ct = D_MODEL // LANES
    g = g_ref[...]

    def ext(main, prev, nxt):
        lo = prev[...].astype(F32)[prev.shape[0] - HALO:]
        hi = nxt[...].astype(F32)[:HALO]
        return jnp.concatenate([lo, main[...].astype(F32), hi], axis=0)

    chunks = [(c0, min(FF_CHUNK, D_FF - c0)) for c0 in range(0, D_FF, FF_CHUNK)]
    par = step % 2
    mod_h = mod_head[0]
    n_blocks = tm // HEAD_ROWS

    def head_matmul():
        mixed = jnp.concatenate([ext(am, ap, an), ext(bm, bp, bn)], axis=1).astype(BF16)
        y_ref[...] = _dot(mixed, wo[...])

    def head_rows(x, y):
        x1 = x + mod_h[2:3] * _rms(y, g[0:1])
        return x1, _rms(x1, g[1:2]) * (1.0 + mod_h[4:5]) + mod_h[3:4]

    def head_block(blk):
        if blk < n_blocks:
            rows = slice(blk * HEAD_ROWS, (blk + 1) * HEAD_ROWS)
            erows = slice(HALO + blk * HEAD_ROWS, HALO + (blk + 1) * HEAD_ROWS)
            x1, h = head_rows(xm[rows, :], y_ref[erows, :])
            x1_ref[par, rows, :] = x1
            t0 = blk * HEAD_ROWS
            hrow = HALO + (t0 // nj) * seg_pitch + t0 % nj
            for c in range(n_ct):
                h_ref[c, hrow:hrow + HEAD_ROWS, :] = h[:, c * LANES:(c + 1) * LANES]
        else:
            x = jnp.concatenate([xp[...], xn[...]], axis=0)
            y = jnp.concatenate([y_ref[0:HALO, :], y_ref[HALO + tm:2 * HALO + tm, :]], axis=0)
            _, h = head_rows(x, y)
            t = jnp.minimum(step, n_tiles - 1) % tiles_per_seq
            ridx = lax.broadcasted_iota(jnp.int32, h.shape, 0)
            keep = ((ridx >= HALO) | (t != 0)) & ((ridx < HALO) | (t != tiles_per_seq - 1))
            h = jnp.where(keep, h, 0.0)
            for c in range(n_ct):
                h_ref[c, 0:HALO, :] = h[:HALO, c * LANES:(c + 1) * LANES]
                h_ref[c, h_rows - HALO:h_rows, :] = h[HALO:, c * LANES:(c + 1) * LANES]

    def head_permute():
        def group(rows):
            return jnp.concatenate([h_ref[c, rows, :] for c in range(n_ct)], axis=1)
        def put(row0, first, second):
            hp_ref[row0:row0 + 2 * HALO, :] = jnp.concatenate(
                [group(first), group(second)], axis=0).astype(BF16)
        for j in range(0, nj, 2):
            put(j * HALO, pl.ds(HALO + j, HALO, stride=seg_pitch),
                pl.ds(HALO + j + 1, HALO, stride=seg_pitch))
        put(tm, pl.ds(0, HALO), pl.ds(h_rows - HALO, HALO))

    def tail(interleaved):
        mod = mod_tail[0]

        def conv(u, cols):
            w = cw[:, cols]
            sub = lax.broadcasted_iota(jnp.int32, (HALO, u.shape[1]), 0)
            before = jnp.where(sub == 0, u[tm + HALO - 1:tm + HALO],
                               pltpu.roll(u[tm - HALO:tm], 1, 0))
            after = jnp.where(sub == HALO - 1, u[tm + HALO:tm + HALO + 1],
                              pltpu.roll(u[0:HALO], HALO - 1, 0))
            for s in range(1, HALO):
                if (s * nj) % seq_len == 0:
                    before = jnp.where(sub == s, 0.0, before)
                    after = jnp.where(sub == s - 1, 0.0, after)
            prev = jnp.concatenate([before, u[0:tm - HALO]], axis=0)
            nxt = jnp.concatenate([u[HALO:tm], after], axis=0)
            return prev * w[0:1] + u[0:tm] * w[1:2] + nxt * w[2:3] + cb[:, cols]

        def cols(ch):
            ca = slice(chunks[ch][0], chunks[ch][0] + chunks[ch][1])
            return ca, slice(D_FF + ca.start, D_FF + ca.stop)

        def up(ch):
            ca, cg = cols(ch)
            return _dot(hp_ref[...], wu[:, cg]), _dot(hp_ref[...], wu[:, ca])

        nxt_u = up(0)
        for ch in range(len(chunks)):
            ug, ua = nxt_u
            if ch + 1 < len(chunks):
                nxt_u = up(ch + 1)
            for blk in interleaved[ch]:
                head_block(blk)
            ca, cg = cols(ch)
            act_ref[:, ca] = (_silu(conv(ug, cg)) * conv(ua, ca)).astype(BF16)
        ffn = mod[5:6] * _rms(_dot(act_ref[...], wd[...]), g[2:3])
        for c in range(n_ct):
            acc_ref[c] = ffn[:, c * LANES:(c + 1) * LANES]

        for s in range(HALO):
            for jb in range(nj // HALO):
                t0 = s * nj + HALO * jb
                rows = pl.ds(HALO * HALO * jb + s, HALO, stride=HALO)
                back = jnp.concatenate([acc_ref[c, rows, :] for c in range(n_ct)], axis=1)
                o_ref[t0:t0 + HALO, :] = x1_ref[1 - par, t0:t0 + HALO, :] + back

    @pl.when(step == 0)
    def _():
        head_matmul()
        for blk in range(n_blocks + 1):
            head_block(blk)
        head_permute()

    @pl.when(step > 0)
    def _():
        head_matmul()
        tail([[blk for blk in range(n_blocks + 1) if blk * len(chunks) // (n_blocks + 1) == ch]
              for ch in range(len(chunks))])
        head_permute()


def _post(x, mix_a, mix_b, mods, gains, wo, wu, cw, cb, wd, layer, seq_len, tm, latent):
    n = x.shape[0]
    nt = n // tm
    (mix_a, col_a), (mix_b, col_b) = mix_a, mix_b
    tiles_per_seq = max(seq_len // tm, 1)
    cond_row = (lambda i: 1 + i // tiles_per_seq) if latent else (lambda i: 0)

    head_tile = lambda i: jnp.minimum(i, nt - 1)
    tail_tile = lambda i: jnp.maximum(i - 1, 0)

    def triple(width, halo_rows, col=0):
        per = tm // halo_rows
        last = n // halo_rows - 1
        return [
            pl.BlockSpec((tm, width), lambda i: (head_tile(i), col)),
            pl.BlockSpec((halo_rows, width),
                         lambda i: (jnp.maximum(head_tile(i) * per - 1, 0), col)),
            pl.BlockSpec((halo_rows, width),
                         lambda i: (jnp.minimum((head_tile(i) + 1) * per, last), col)),
        ]

    const = lambda *shape: pl.BlockSpec((None,) + shape, lambda i: (layer,) + (0,) * len(shape),
                                        pipeline_mode=pl.Buffered(1))
    in_specs = (triple(D_MODEL, HALO) + triple(WIDTH, 2 * HALO, col_a)
                + triple(WIDTH, 2 * HALO, col_b) + [
        pl.BlockSpec((None, 1, 6, D_MODEL), lambda i: (layer, cond_row(head_tile(i)), 0, 0)),
        pl.BlockSpec((None, 1, 6, D_MODEL), lambda i: (layer, cond_row(tail_tile(i)), 0, 0)),
        const(3, D_MODEL),
        const(2 * WIDTH, D_MODEL),
        const(D_MODEL, 2 * D_FF),
        const(3, 2 * D_FF),
        const(1, 2 * D_FF),
        const(D_FF, D_MODEL),
    ])
    return pl.pallas_call(
        functools.partial(_post_kernel, seq_len=seq_len, n_tiles=nt),
        grid=(nt + 1,),
        in_specs=in_specs,
        out_specs=pl.BlockSpec((tm, D_MODEL), lambda i: (tail_tile(i), 0)),
        out_shape=jax.ShapeDtypeStruct((n, D_MODEL), F32),
        scratch_shapes=[pltpu.VMEM((D_MODEL // LANES, tm, LANES), F32),
                        pltpu.VMEM((D_MODEL // LANES, 2 * HALO + HALO * (tm // HALO + HALO), LANES),
                                   F32),
                        pltpu.VMEM((tm + 2 * HALO, D_MODEL), BF16),
                        pltpu.VMEM((2, tm, D_MODEL), F32),
                        pltpu.VMEM((tm + 2 * HALO, D_MODEL), F32),
                        pltpu.VMEM((tm, D_FF), BF16)],
        compiler_params=_cparams(1),
        name="post_latent" if latent else "post_context",
    )(x, x, x, mix_a, mix_a, mix_a, mix_b, mix_b, mix_b, mods, mods, gains, wo, wu, cw, cb, wd)


def kernel(x_prompt, x_sample, c, cache_na_k, cache_na_v, state_ret_fwd, state_ret_bwd, c_ctx,
           ada_w, ada_b, g_pre_mix, g_post_mix, g_pre_ffn, g_post_ffn, w_in,
           ret_decay_fwd, ret_decay_bwd, na_rpb, w_out, w_up, conv_w, conv_b, w_down):
    depth = w_in.shape[0]
    batch, seq, _ = x_prompt.shape
    dec_batch, dec_seq, _ = x_sample.shape
    past = cache_na_k.shape[2]
    tm = 512

    cond = jnp.concatenate(
        [c_ctx[None, :], c, jnp.zeros((8 - 1 - dec_batch, D_MODEL), F32)], axis=0)
    mods = _modulation(cond, ada_w, ada_b).reshape(depth, 8, 6, D_MODEL)
    rope_tabs = _rope_tables(dec_seq)
    bias, lo, hi = _na_bias_tables(na_rpb, dec_seq // GRID_W)

    w_in_b, wo_b, wu_b, wd_b = (w.astype(BF16) for w in (w_in, w_out, w_up, w_down))
    g_pre = g_pre_mix.reshape(depth, 1, D_MODEL)
    gains = jnp.stack([g_post_mix, g_pre_ffn, g_post_ffn], axis=1)
    cb = conv_b.reshape(depth, 1, 2 * D_FF)
    post_params = (mods, gains, wo_b, wu_b, conv_w, cb, wd_b)

    y_p = x_prompt.reshape(batch * seq, D_MODEL)
    y_s = x_sample.reshape(dec_batch * dec_seq, D_MODEL)
    new_k = new_v = None
    sfs, sbs = [], []
    for l in range(depth):
        dec = (jnp.repeat(ret_decay_fwd[l], HEAD_DIM)[None, :],
               jnp.repeat(ret_decay_bwd[l], HEAD_DIM)[None, :],
               jnp.repeat(ret_decay_fwd[l], LANES)[None, :],
               jnp.repeat(ret_decay_bwd[l], LANES)[None, :])

        p_c, new_k, new_v = _inproj(y_p, mods, g_pre, w_in_b, l, seq, tm, kv_out=(new_k, new_v))
        mix_c, st_c = _ctx_mixer(p_c, dec, batch, seq)
        y_p = _post(y_p, (mix_c, 0), (mix_c, 1), *post_params, l, seq, tm, latent=False)
        sfs.append(_diag_states(st_c[:, :, 0]))
        sbs.append(_diag_states(st_c[:, :, 1]))

        (p_s,) = _inproj(y_s, mods, g_pre, w_in_b, l, dec_seq, tm, rope_tabs=rope_tabs)
        s0 = _block_diag_states(state_ret_fwd[:, l], state_ret_bwd[:, l])
        ret_s = _lat_retention(p_s, dec, s0, dec_batch, dec_seq)
        ck = cache_na_k[:, l].reshape(dec_batch, past, WIDTH).astype(BF16)
        cv = cache_na_v[:, l].reshape(dec_batch, past, WIDTH).astype(BF16)
        na_s = _neighbourhood_attention(p_s, ck, cv, bias, l, lo, hi, dec_batch, dec_seq)
        y_s = _post(y_s, (ret_s, 0), (na_s, 0), *post_params, l, dec_seq, tm, latent=True)

    return (y_p.reshape(batch, seq, D_MODEL),
            y_s.reshape(dec_batch, dec_seq, D_MODEL),
            new_k.reshape(batch, depth, seq, N_HEADS, HEAD_DIM),
            new_v.reshape(batch, depth, seq, N_HEADS, HEAD_DIM),
            jnp.stack(sfs, axis=1), jnp.stack(sbs, axis=1))
```

```python
import functools

import numpy as np
import jax
import jax.numpy as jnp
from jax import lax
from jax.experimental import pallas as pl
from jax.experimental.pallas import tpu as pltpu

F32 = jnp.float32
BF16 = jnp.bfloat16

D_MODEL = 1024
HEAD_DIM = 64
N_HEADS = 8
HEAD_PAIRS = N_HEADS // 2
LANES = 128
WIDTH = N_HEADS * HEAD_DIM
N_GROUPS = 7
IN_WIDTH = N_GROUPS * WIDTH
D_FF = 2816
FF_CHUNK = 768
DOWN_SLABS = 4
HEAD_ROWS = 32
CHUNK = 128
CTX_SEQS_PER_STEP = 2
RET_UNROLL = 8
GRID_W = 64
NA_KH = 8
NA_KW = 16
NA_STAGE_ROWS = 8
NA_ROWS = 8
ROPE_BASE = 10000.0
EPS = 1e-6
NEG_INF = -1e9
HALO = 8
VMEM_LIMIT = 56 * 1024 * 1024


def _cparams(n_grid):
    return pltpu.CompilerParams(
        dimension_semantics=("arbitrary",) * n_grid, vmem_limit_bytes=VMEM_LIMIT)


def _rms(x, g):
    ms = jnp.mean(x * x, axis=-1, keepdims=True)
    return x * lax.rsqrt(ms + EPS) * g


def _silu(x):
    return x * jax.nn.sigmoid(x)


def _log_sigmoid(x):
    return jnp.minimum(x, 0.0) - jnp.log1p(jnp.exp(-jnp.abs(x)))


def _dot(a, b):
    return jnp.dot(a, b, preferred_element_type=F32)


def _dot_nt(a, b):
    return lax.dot_general(a, b, (((1,), (1,)), ((), ())), preferred_element_type=F32)


def _dot_tn(a, b):
    return lax.dot_general(a, b, (((0,), (0,)), ((), ())), preferred_element_type=F32)


def _cast_kernel(w_ref, o_ref):
    o_ref[...] = w_ref[...].astype(o_ref.dtype)


def _to_bf16(w, rows):
    depth, k, n = w.shape
    return pl.pallas_call(
        _cast_kernel,
        grid=(depth, k // rows),
        in_specs=[pl.BlockSpec((None, rows, n), lambda l, i: (l, i, 0))],
        out_specs=pl.BlockSpec((None, rows, n), lambda l, i: (l, i, 0)),
        out_shape=jax.ShapeDtypeStruct(w.shape, BF16),
        compiler_params=_cparams(2),
        name="weights_to_bf16",
    )(w)


def _mod_kernel(cond_ref, w_ref, b_ref, o_ref):
    s = _silu(cond_ref[...]).astype(BF16)
    o_ref[0] = _dot(s, w_ref[0].astype(BF16)) + b_ref[0]


def _modulation(cond, ada_w, ada_b):
    depth = ada_w.shape[0]
    nb = 6 * D_MODEL // D_MODEL
    return pl.pallas_call(
        _mod_kernel,
        grid=(depth, nb),
        in_specs=[
            pl.BlockSpec((8, D_MODEL), lambda l, j: (0, 0)),
            pl.BlockSpec((1, D_MODEL, D_MODEL), lambda l, j: (l, 0, j)),
            pl.BlockSpec((1, 1, D_MODEL), lambda l, j: (l, 0, j)),
        ],
        out_specs=pl.BlockSpec((1, 8, D_MODEL), lambda l, j: (l, 0, j)),
        out_shape=jax.ShapeDtypeStruct((depth, 8, 6 * D_MODEL), F32),
        compiler_params=_cparams(2),
        name="modulation",
    )(cond, ada_w, ada_b.reshape(depth, 1, 6 * D_MODEL))


def _inproj_kernel(*refs, rope, emit_kv, n_alias, layer):
    x_ref, mod_ref, g_ref, w_ref = refs[:4]
    refs = refs[4:]
    if rope:
        cos_ref, sin_up_ref, sin_dn_ref = refs[:3]
        refs = refs[3:]
    refs = refs[n_alias:]
    p_ref = refs[0]
    mod = mod_ref[0]
    h = (_rms(x_ref[...], g_ref[...]) * (1.0 + mod[1:2]) + mod[0:1]).astype(BF16)
    for g in range(N_GROUPS):
        cols = slice(g * WIDTH, (g + 1) * WIDTH)
        pg = _dot(h, w_ref[:, cols])
        if rope and g < 2:
            parts = []
            for j in range(WIDTH // LANES):
                xj = pg[:, j * LANES:(j + 1) * LANES]
                parts.append(xj * cos_ref[...]
                             + pltpu.roll(xj, 16, 1) * sin_up_ref[...]
                             + pltpu.roll(xj, LANES - 16, 1) * sin_dn_ref[...])
            pg = jnp.concatenate(parts, axis=1)
        if g in (0, 4):
            pg = pg * (HEAD_DIM ** -0.5)
        p_ref[:, cols] = pg.astype(BF16)
        if emit_kv and g >= 5:
            kv_ref = refs[g - 4]
            seq = kv_ref.shape[-2]
            for j in range(kv_ref.shape[0]):
                if n_alias:
                    kv_ref[j] = pg[j * seq:(j + 1) * seq]
                else:
                    for l in range(kv_ref.shape[1]):
                        kv_ref[j, l] = (pg[j * seq:(j + 1) * seq] if l == layer
                                        else jnp.zeros((seq, WIDTH), F32))


def _inproj(x, mods, g_pre, w, layer, seq_len, tm, rope_tabs=None, kv_out=None):
    n = x.shape[0]
    depth = w.shape[0]
    tiles_per_seq = max(seq_len // tm, 1)
    cond_row = (lambda i: 0) if rope_tabs is None else (lambda i: 1 + i // tiles_per_seq)
    in_specs = [
        pl.BlockSpec((tm, D_MODEL), lambda i: (i, 0)),
        pl.BlockSpec((None, 1, 6, D_MODEL), lambda i: (layer, cond_row(i), 0, 0)),
        pl.BlockSpec((None, 1, D_MODEL), lambda i: (layer, 0, 0)),
        pl.BlockSpec((None, D_MODEL, IN_WIDTH), lambda i: (layer, 0, 0),
                     pipeline_mode=pl.Buffered(1)),
    ]
    args = [x, mods, g_pre, w]
    if rope_tabs is not None:
        in_specs += [pl.BlockSpec((tm, LANES), lambda i: (i % tiles_per_seq, 0))] * 3
        args += list(rope_tabs)
    out_specs = [pl.BlockSpec((tm, IN_WIDTH), lambda i: (i, 0))]
    out_shape = [jax.ShapeDtypeStruct((n, IN_WIDTH), BF16)]
    aliases = {}
    if kv_out is not None:
        seqs_per_tile = tm // seq_len
        kv_shape = jax.ShapeDtypeStruct((n // seq_len, depth, seq_len, WIDTH), F32)
        assert (kv_out[0] is None) == (kv_out[1] is None)
        for j, prev in enumerate(kv_out):
            if prev is not None:
                aliases[len(args)] = 1 + j
                in_specs.append(pl.BlockSpec(memory_space=pl.ANY))
                args.append(prev)
                out_specs.append(pl.BlockSpec((seqs_per_tile, None, seq_len, WIDTH),
                                              lambda i: (i, layer, 0, 0)))
            else:
                out_specs.append(pl.BlockSpec((seqs_per_tile, depth, seq_len, WIDTH),
                                              lambda i: (i, 0, 0, 0)))
            out_shape.append(kv_shape)
    return pl.pallas_call(
        functools.partial(_inproj_kernel, rope=rope_tabs is not None,
                          emit_kv=kv_out is not None, n_alias=len(aliases), layer=layer),
        grid=(n // tm,),
        in_specs=in_specs,
        out_specs=out_specs,
        out_shape=out_shape,
        input_output_aliases=aliases,
        compiler_params=_cparams(1),
        name="inproj_latent" if rope_tabs is not None else "inproj_context",
    )(*args)


def _rope_tables(seq_len):
    t = np.arange(seq_len)
    lane = np.arange(LANES)
    d = lane % HEAD_DIM
    pos = np.where(d[None, :] < HEAD_DIM // 2, (t // GRID_W)[:, None], (t % GRID_W)[:, None])
    pos = pos.astype(np.float32)
    half = HEAD_DIM // 2
    inv = np.power(np.float32(ROPE_BASE), -np.arange(0, half, 2, dtype=np.float32) / half)
    ang = pos * inv[d % (half // 2)][None, :]
    cos, sin = np.cos(ang), np.sin(ang)
    upper = (d % half) >= half // 2
    sin_up = np.where(upper[None, :], sin, 0.0)
    sin_dn = np.where(upper[None, :], 0.0, -sin)
    return (jnp.asarray(cos, F32), jnp.asarray(sin_up, F32), jnp.asarray(sin_dn, F32))


def _retention_tables(dec_f, dec_b, dec_f2, dec_b2):
    lgf, lgb = _log_sigmoid(dec_f), _log_sigmoid(dec_b)
    pos = lax.broadcasted_iota(jnp.int32, (CHUNK, LANES), 0).astype(F32)
    tabs = dict(
        qdf=jnp.exp(lgf * (pos + 1.0)), kdf=jnp.exp(lgf * (CHUNK - 1.0 - pos)),
        cdf=jnp.exp(lgf * float(CHUNK)),
        qdb=jnp.exp(lgb * (CHUNK - pos)), kdb=jnp.exp(lgb * pos),
        cdb=jnp.exp(lgb * float(CHUNK)),
    )
    lgf2, lgb2 = _log_sigmoid(dec_f2), _log_sigmoid(dec_b2)
    i = lax.broadcasted_iota(jnp.int32, (CHUNK, 2 * CHUNK), 0)
    j = lax.broadcasted_iota(jnp.int32, (CHUNK, 2 * CHUNK), 1) & (CHUNK - 1)
    diff = (i - j).astype(F32)
    tabs["decay"] = (jnp.where(diff >= 0, jnp.exp(lgf2 * jnp.maximum(diff, 0.0)), 0.0)
                     + jnp.where(diff <= 0, jnp.exp(lgb2 * jnp.maximum(-diff, 0.0)), 0.0))
    lane = lax.broadcasted_iota(jnp.int32, (1, LANES), 1)
    tabs["head_a"] = lane < HEAD_DIM
    r = lax.broadcasted_iota(jnp.int32, (2 * LANES, LANES), 0) & (LANES - 1)
    c = lax.broadcasted_iota(jnp.int32, (2 * LANES, LANES), 1)
    tabs["same_head"] = (r < HEAD_DIM) == (c < HEAD_DIM)
    return tabs


def _split_heads(x, head_a, axis):
    zero = jnp.zeros_like(x)
    return jnp.concatenate([jnp.where(head_a, x, zero), jnp.where(head_a, zero, x)], axis=axis)


def _chunk_kv(k2, v2, tabs):
    kf = k2.astype(F32)
    kk = jnp.concatenate([kf * tabs["kdf"], kf * tabs["kdb"]], axis=1).astype(BF16)
    return jnp.where(tabs["same_head"], _dot_tn(kk, v2), 0.0)


def _chunk_out(q2, k2, v2, g2, state, tabs):
    head_a = tabs["head_a"]
    s = _dot_nt(q2, _split_heads(k2, head_a, 0))
    p = (s * tabs["decay"]).astype(BF16)
    qf = q2.astype(F32)
    lhs = jnp.concatenate(
        [p, (qf * tabs["qdf"]).astype(BF16), (qf * tabs["qdb"]).astype(BF16)], axis=1)
    rhs = jnp.concatenate([_split_heads(v2, head_a, 0), state], axis=0)
    o = _dot(lhs, rhs)
    inv = 1.0 / HEAD_DIM
    sum_a = jnp.sum(jnp.where(head_a, o, 0.0), axis=-1, keepdims=True)
    sum_b = jnp.sum(jnp.where(head_a, 0.0, o), axis=-1, keepdims=True)
    d = o - jnp.where(head_a, sum_a, sum_b) * inv
    d2 = d * d
    var_a = jnp.sum(jnp.where(head_a, d2, 0.0), axis=-1, keepdims=True)
    var_b = jnp.sum(jnp.where(head_a, 0.0, d2), axis=-1, keepdims=True)
    o = d * lax.rsqrt(jnp.where(head_a, var_a, var_b) * inv + EPS)
    return o * _silu(g2.astype(F32))


def _ctx_mixer_kernel(rq, rk, rv, rg, nq, nk, nv, df, db, df2, db2, mix_ref, st_ref):
    n_seq = st_ref.shape[0]
    seq = rq.shape[0] // n_seq
    nc = seq // CHUNK
    for hp in range(HEAD_PAIRS):
        cols = slice(hp * LANES, (hp + 1) * LANES)
        cols2 = slice(hp * 2 * LANES, (hp + 1) * 2 * LANES)
        tabs = _retention_tables(df[:, cols], db[:, cols], df2[:, cols2], db2[:, cols2])
        head_a = tabs["head_a"]
        for b in range(n_seq):
            rows = [slice(b * seq + c * CHUNK, b * seq + (c + 1) * CHUNK) for c in range(nc)]
            kv = [_chunk_kv(rk[r, cols], rv[r, cols], tabs) for r in rows]
            sf = [jnp.zeros((LANES, LANES), F32)]
            for c in range(nc):
                sf.append(sf[-1] * tabs["cdf"] + kv[c][:LANES])
            sb = [jnp.zeros((LANES, LANES), F32)]
            for c in reversed(range(nc)):
                sb.append(sb[-1] * tabs["cdb"] + kv[c][LANES:])
            for c in range(nc):
                state = jnp.concatenate([sf[c], sb[nc - 1 - c]], axis=0).astype(BF16)
                o = _chunk_out(rq[rows[c], cols], rk[rows[c], cols], rv[rows[c], cols],
                               rg[rows[c], cols], state, tabs)
                mix_ref[rows[c], cols] = o.astype(BF16)
            st_ref[b, hp, 0] = sf[nc]
            st_ref[b, hp, 1] = sb[nc]

            tok = slice(b * seq, (b + 1) * seq)
            s = _dot_nt(nq[tok, cols], _split_heads(nk[tok, cols], head_a, 0))
            es, rinv = [], []
            for h in range(2):
                sh = s[:, h * seq:(h + 1) * seq]
                e = jnp.exp(sh - jnp.max(sh, axis=-1, keepdims=True))
                rinv.append(1.0 / jnp.sum(e, axis=-1, keepdims=True))
                es.append(e.astype(BF16))
            o = _dot(jnp.concatenate(es, axis=1), _split_heads(nv[tok, cols], head_a, 0))
            o = o * jnp.where(head_a, rinv[0], rinv[1])
            mix_ref[tok, WIDTH + hp * LANES:WIDTH + (hp + 1) * LANES] = o.astype(BF16)


def _ctx_mixer(p, dec, batch, seq):
    n_seq = CTX_SEQS_PER_STEP
    group = lambda g: pl.BlockSpec((n_seq * seq, WIDTH), lambda b, g=g: (b, g))
    vec = lambda w: pl.BlockSpec((1, w), lambda b: (0, 0))
    return pl.pallas_call(
        _ctx_mixer_kernel,
        grid=(batch // n_seq,),
        in_specs=[group(g) for g in range(N_GROUPS)] + [vec(WIDTH), vec(WIDTH),
                                                         vec(2 * WIDTH), vec(2 * WIDTH)],
        out_specs=[
            pl.BlockSpec((n_seq * seq, 2 * WIDTH), lambda b: (b, 0)),
            pl.BlockSpec((n_seq, HEAD_PAIRS, 2, LANES, LANES), lambda b: (b, 0, 0, 0, 0)),
        ],
        out_shape=[
            jax.ShapeDtypeStruct((batch * seq, 2 * WIDTH), BF16),
            jax.ShapeDtypeStruct((batch, HEAD_PAIRS, 2, LANES, LANES), F32),
        ],
        compiler_params=_cparams(1),
        name="context_mixer",
    )(*([p] * N_GROUPS), *dec)


def _lat_retention_kernel(q_ref, k_ref, v_ref, g_ref, df, db, df2, db2, s0_ref, o_ref,
                          kv_ref, st_ref):
    nc = q_ref.shape[0] // CHUNK
    tabs = _retention_tables(df[...], db[...], df2[...], db2[...])

    def rows(c):
        return pl.ds(pl.multiple_of(c * CHUNK, CHUNK), CHUNK)

    def kv_body(c, carry):
        kv_ref[c] = _chunk_kv(k_ref[rows(c), :], v_ref[rows(c), :], tabs)
        return carry
    lax.fori_loop(0, nc, kv_body, 0, unroll=RET_UNROLL)

    def fwd_body(c, s):
        st_ref[c, :LANES, :] = s.astype(BF16)
        return s * tabs["cdf"] + kv_ref[c, :LANES, :]
    lax.fori_loop(0, nc, fwd_body, s0_ref[0, 0, 0])

    def bwd_body(i, s):
        c = nc - 1 - i
        st_ref[c, LANES:, :] = s.astype(BF16)
        return s * tabs["cdb"] + kv_ref[c, LANES:, :]
    lax.fori_loop(0, nc, bwd_body, s0_ref[0, 0, 1])

    def out_body(c, carry):
        r = rows(c)
        o = _chunk_out(q_ref[r, :], k_ref[r, :], v_ref[r, :], g_ref[r, :], st_ref[c], tabs)
        o_ref[r, :] = o.astype(BF16)
        return carry
    lax.fori_loop(0, nc, out_body, 0, unroll=RET_UNROLL)


def _lat_retention(p, dec, s0, batch, seq):
    nc = seq // CHUNK
    group = lambda g: pl.BlockSpec((seq, LANES), lambda b, h, g=g: (b, g * HEAD_PAIRS + h))
    vec = lambda w: pl.BlockSpec((1, w), lambda b, h: (0, h))
    return pl.pallas_call(
        _lat_retention_kernel,
        grid=(batch, HEAD_PAIRS),
        in_specs=[group(g) for g in range(4)] + [vec(LANES), vec(LANES), vec(2 * LANES),
                                                 vec(2 * LANES)]
        + [pl.BlockSpec((1, 1, 2, LANES, LANES), lambda b, h: (b, h, 0, 0, 0))],
        out_specs=pl.BlockSpec((seq, LANES), lambda b, h: (b, h)),
        out_shape=jax.ShapeDtypeStruct((batch * seq, WIDTH), BF16),
        scratch_shapes=[pltpu.VMEM((nc, 2 * LANES, LANES), F32),
                        pltpu.VMEM((nc, 2 * LANES, LANES), BF16)],
        compiler_params=_cparams(2),
        name="latent_retention",
    )(*([p] * 4), *dec, s0)


def _block_diag_states(s_f, s_b):
    def bd(s):
        b = s.shape[0]
        s = s.reshape(b, HEAD_PAIRS, 2, HEAD_DIM, HEAD_DIM)
        z = jnp.zeros_like(s[:, :, 0])
        top = jnp.concatenate([s[:, :, 0], z], axis=-1)
        bot = jnp.concatenate([z, s[:, :, 1]], axis=-1)
        return jnp.concatenate([top, bot], axis=-2)
    return jnp.stack([bd(s_f), bd(s_b)], axis=2)


def _diag_states(st):
    b = st.shape[0]
    a = st[:, :, :HEAD_DIM, :HEAD_DIM]
    c = st[:, :, HEAD_DIM:, HEAD_DIM:]
    return jnp.stack([a, c], axis=2).reshape(b, N_HEADS, HEAD_DIM, HEAD_DIM)


def _na_kernel(q_ref, k_ref, v_ref, ck_ref, cv_ref, bias_ref, o_ref, *, lo, hi):
    rows_total = k_ref.shape[0] // GRID_W
    lane = lax.broadcasted_iota(jnp.int32, (1, LANES), 1)
    head_a = lane < HEAD_DIM
    wins, cls = [], []
    for i in range(NA_ROWS):
        r = pl.program_id(1) * NA_ROWS + i
        rs = jnp.clip(r - NA_KH // 2, 0, rows_total - NA_KH)
        wins.append(pl.ds(pl.multiple_of(rs * GRID_W, GRID_W), NA_KH * GRID_W))
        cls.append(jnp.where(r < lo, r, jnp.where(r > hi, r - hi + lo, lo)))
    stages = [(hp, r0) for hp in range(HEAD_PAIRS) for r0 in range(0, NA_ROWS, NA_STAGE_ROWS)]

    def scores(stage):
        hp, r0 = stage
        cols = slice(hp * LANES, (hp + 1) * LANES)
        qq = jnp.concatenate(
            [_split_heads(q_ref[i * GRID_W:(i + 1) * GRID_W, cols], head_a, 0)
             for i in range(r0, r0 + NA_STAGE_ROWS)], axis=0)
        s_ctx = _dot_nt(qq, ck_ref[0, :, cols])
        s_loc = [_dot_nt(qq[j * 2 * GRID_W:(j + 1) * 2 * GRID_W], k_ref[wins[r0 + j], cols])
                 for j in range(NA_STAGE_ROWS)]
        return s_ctx, s_loc

    def softmax(stage, s_ctx, s_loc):
        hp, r0 = stage
        m_ctx = jnp.max(s_ctx, axis=-1, keepdims=True)
        m = []
        for j in range(NA_STAGE_ROWS):
            blk = slice(j * 2 * GRID_W, (j + 1) * 2 * GRID_W)
            s_loc[j] = s_loc[j] + bias_ref[cls[r0 + j], hp]
            m.append(jnp.maximum(jnp.max(s_loc[j], axis=-1, keepdims=True), m_ctx[blk]))
        e_ctx = jnp.exp(s_ctx - jnp.concatenate(m, axis=0))
        den_ctx = jnp.sum(e_ctx, axis=-1, keepdims=True)
        e_loc, den = [], []
        for j in range(NA_STAGE_ROWS):
            blk = slice(j * 2 * GRID_W, (j + 1) * 2 * GRID_W)
            e = jnp.exp(s_loc[j] - m[j])
            den.append(jnp.sum(e, axis=-1, keepdims=True) + den_ctx[blk])
            e_loc.append(e.astype(BF16))
        return e_ctx.astype(BF16), e_loc, den

    def weighted_values(stage, e_ctx, e_loc, den):
        hp, r0 = stage
        cols = slice(hp * LANES, (hp + 1) * LANES)
        o_ctx = _dot(e_ctx, cv_ref[0, :, cols])
        for j in range(NA_STAGE_ROWS):
            i = r0 + j
            blk = slice(j * 2 * GRID_W, (j + 1) * 2 * GRID_W)
            o = (_dot(e_loc[j], v_ref[wins[i], cols]) + o_ctx[blk]) / den[j]
            o_ref[i * GRID_W:(i + 1) * GRID_W, cols] = jnp.where(
                head_a, o[:GRID_W], o[GRID_W:]).astype(BF16)

    nxt = scores(stages[0])
    pending = None
    for n, stage in enumerate(stages):
        s_ctx, s_loc = nxt
        if n + 1 < len(stages):
            nxt = scores(stages[n + 1])
        probs = softmax(stage, s_ctx, s_loc)
        if pending is not None:
            weighted_values(*pending)
        pending = (stage,) + probs
    weighted_values(*pending)


def _bias_build_kernel(rpb_ref, o_ref, *, dr_first, n_dr):
    qc = lax.broadcasted_iota(jnp.int32, (GRID_W, LANES), 0)
    lane = lax.broadcasted_iota(jnp.int32, (GRID_W, LANES), 1)
    kc = lane & (GRID_W - 1)
    cs = jnp.clip(qc - NA_KW // 2, 0, GRID_W - NA_KW)
    inside = (kc >= cs) & (kc < cs + NA_KW)
    first = lane < GRID_W
    for h in range(N_HEADS):
        lo_half, hi_half = [], []
        for dr in range(n_dr):
            line = jnp.broadcast_to(rpb_ref[pl.ds(h * n_dr + dr, 1), :], (GRID_W, LANES))
            lo_half.append(pltpu.roll(line, LANES - (NA_KW - 1), 1, stride=1, stride_axis=0))
            hi_half.append(pltpu.roll(line, GRID_W - (NA_KW - 1), 1, stride=1, stride_axis=0))
        for cls, dr0 in enumerate(dr_first):
            for jp in range(NA_KH // 2):
                dr = dr0 + 2 * jp
                tile = jnp.where(inside, jnp.where(first, lo_half[dr], hi_half[dr + 1]), NEG_INF)
                o_ref[cls, h // 2, (h % 2) * GRID_W:(h % 2 + 1) * GRID_W,
                      jp * LANES:(jp + 1) * LANES] = tile


def _na_bias_tables(na_rpb, rows_total):
    depth, heads, n_dr, n_dc = na_rpb.shape
    kh = NA_KH
    lo, hi = kh // 2, rows_total - kh // 2 - 1
    reps = list(range(lo)) + [lo] + list(range(hi + 1, rows_total))
    dr_first = tuple(int(np.clip(r - kh // 2, 0, rows_total - kh)) - r + kh - 1 for r in reps)
    lines = jnp.pad(na_rpb.reshape(depth, heads * n_dr, n_dc), ((0, 0), (0, 0), (0, LANES - n_dc)))
    bias = pl.pallas_call(
        functools.partial(_bias_build_kernel, dr_first=dr_first, n_dr=n_dr),
        grid=(depth,),
        in_specs=[pl.BlockSpec((None, heads * n_dr, LANES), lambda l: (l, 0, 0))],
        out_specs=pl.BlockSpec((None, len(reps), HEAD_PAIRS, 2 * GRID_W, kh * GRID_W),
                               lambda l: (l, 0, 0, 0, 0)),
        out_shape=jax.ShapeDtypeStruct(
            (depth, len(reps), HEAD_PAIRS, 2 * GRID_W, kh * GRID_W), F32),
        compiler_params=_cparams(1),
        name="na_bias_build",
    )(lines)
    return bias, lo, hi


def _neighbourhood_attention(p, ctx_k, ctx_v, bias, layer, lo, hi, batch, seq):
    rows_total = seq // GRID_W
    past = ctx_k.shape[1]

    groups = rows_total // NA_ROWS
    return pl.pallas_call(
        functools.partial(_na_kernel, lo=lo, hi=hi),
        grid=(batch, groups),
        in_specs=[
            pl.BlockSpec((NA_ROWS * GRID_W, WIDTH), lambda b, g: (b * groups + g, 4)),
            pl.BlockSpec((seq, WIDTH), lambda b, g: (b, 5), pipeline_mode=pl.Buffered(1)),
            pl.BlockSpec((seq, WIDTH), lambda b, g: (b, 6), pipeline_mode=pl.Buffered(1)),
            pl.BlockSpec((1, past, WIDTH), lambda b, g: (b, 0, 0), pipeline_mode=pl.Buffered(1)),
            pl.BlockSpec((1, past, WIDTH), lambda b, g: (b, 0, 0), pipeline_mode=pl.Buffered(1)),
            pl.BlockSpec((None,) + bias.shape[1:], lambda b, g: (layer, 0, 0, 0, 0),
                         pipeline_mode=pl.Buffered(1)),
        ],
        out_specs=pl.BlockSpec((NA_ROWS * GRID_W, WIDTH), lambda b, g: (b * groups + g, 0)),
        out_shape=jax.ShapeDtypeStruct((batch * seq, WIDTH), BF16),
        compiler_params=_cparams(2),
        name="neighbourhood_attention",
    )(p, p, p, ctx_k, ctx_v, bias)


def _post_kernel(xm, xp, xn, am, ap, an, bm, bp, bn, mod_head, mod_tail, g_ref, wo, wu, cw, cb, wd,
                 o_ref, acc_ref, h_ref, hp_ref, x1_ref, y_ref, act_ref, *, seq_len, n_tiles):
    step = pl.program_id(0)
    tm = xm.shape[0]
    nj = tm // HALO
    tiles_per_seq = max(seq_len // tm, 1)
    assert (tm % seq_len == 0 and seq_len % nj == 0) or seq_len % tm == 0
    h_rows = h_ref.shape[1]
    seg_pitch = nj + HALO
    assert h_rows == 2 * HALO + HALO * seg_pitch and nj % HEAD_ROWS == 0
    n_ct = D_MODEL // LANES
    g = g_ref[...]

    def ext(main, prev, nxt):
        lo = prev[...].astype(F32)[prev.shape[0] - HALO:]
        hi = nxt[...].astype(F32)[:HALO]
        return jnp.concatenate([lo, main[...].astype(F32), hi], axis=0)

    chunks = [(c0, min(FF_CHUNK, D_FF - c0)) for c0 in range(0, D_FF, FF_CHUNK)]
    par = step % 2
    mod_h = mod_head[0]
    n_blocks = tm // HEAD_ROWS

    def head_matmul():
        mixed = jnp.concatenate([ext(am, ap, an), ext(bm, bp, bn)], axis=1).astype(BF16)
        y_ref[...] = _dot(mixed, wo[...])

    def head_rows(x, y):
        x1 = x + mod_h[2:3] * _rms(y, g[0:1])
        return x1, _rms(x1, g[1:2]) * (1.0 + mod_h[4:5]) + mod_h[3:4]

    def head_block(blk):
        if blk < n_blocks:
            rows = slice(blk * HEAD_ROWS, (blk + 1) * HEAD_ROWS)
            erows = slice(HALO + blk * HEAD_ROWS, HALO + (blk + 1) * HEAD_ROWS)
            x1, h = head_rows(xm[rows, :], y_ref[erows, :])
            x1_ref[par, rows, :] = x1
            t0 = blk * HEAD_ROWS
            hrow = HALO + (t0 // nj) * seg_pitch + t0 % nj
            for c in range(n_ct):
                h_ref[c, hrow:hrow + HEAD_ROWS, :] = h[:, c * LANES:(c + 1) * LANES]
        else:
            x = jnp.concatenate([xp[...], xn[...]], axis=0)
            y = jnp.concatenate([y_ref[0:HALO, :], y_ref[HALO + tm:2 * HALO + tm, :]], axis=0)
            _, h = head_rows(x, y)
            t = jnp.minimum(step, n_tiles - 1) % tiles_per_seq
            ridx = lax.broadcasted_iota(jnp.int32, h.shape, 0)
            keep = ((ridx >= HALO) | (t != 0)) & ((ridx < HALO) | (t != tiles_per_seq - 1))
            h = jnp.where(keep, h, 0.0)
            for c in range(n_ct):
                h_ref[c, 0:HALO, :] = h[:HALO, c * LANES:(c + 1) * LANES]
                h_ref[c, h_rows - HALO:h_rows, :] = h[HALO:, c * LANES:(c + 1) * LANES]

    def head_permute():
        def group(rows):
            return jnp.concatenate([h_ref[c, rows, :] for c in range(n_ct)], axis=1)
        def put(row0, first, second):
            hp_ref[row0:row0 + 2 * HALO, :] = jnp.concatenate(
                [group(first), group(second)], axis=0).astype(BF16)
        for j in range(0, nj, 2):
            put(j * HALO, pl.ds(HALO + j, HALO, stride=seg_pitch),
                pl.ds(HALO + j + 1, HALO, stride=seg_pitch))
        put(tm, pl.ds(0, HALO), pl.ds(h_rows - HALO, HALO))

    def tail(interleaved):
        mod = mod_tail[0]

        def conv(u, cols):
            w = cw[:, cols]
            sub = lax.broadcasted_iota(jnp.int32, (HALO, u.shape[1]), 0)
            before = jnp.where(sub == 0, u[tm + HALO - 1:tm + HALO],
                               pltpu.roll(u[tm - HALO:tm], 1, 0))
            after = jnp.where(sub == HALO - 1, u[tm + HALO:tm + HALO + 1],
                              pltpu.roll(u[0:HALO], HALO - 1, 0))
            for s in range(1, HALO):
                if (s * nj) % seq_len == 0:
                    before = jnp.where(sub == s, 0.0, before)
                    after = jnp.where(sub == s - 1, 0.0, after)
            prev = jnp.concatenate([before, u[0:tm - HALO]], axis=0)
            nxt = jnp.concatenate([u[HALO:tm], after], axis=0)
            return prev * w[0:1] + u[0:tm] * w[1:2] + nxt * w[2:3] + cb[:, cols]

        def cols(ch):
            ca = slice(chunks[ch][0], chunks[ch][0] + chunks[ch][1])
            return ca, slice(D_FF + ca.start, D_FF + ca.stop)

        def up(ch):
            ca, cg = cols(ch)
            return _dot(hp_ref[...], wu[:, cg]), _dot(hp_ref[...], wu[:, ca])

        nxt_u = up(0)
        for ch in range(len(chunks)):
            ug, ua = nxt_u
            if ch + 1 < len(chunks):
                nxt_u = up(ch + 1)
            for blk in interleaved[ch]:
                head_block(blk)
            ca, cg = cols(ch)
            act_ref[:, ca] = (_silu(conv(ug, cg)) * conv(ua, ca)).astype(BF16)
        slab = tm // DOWN_SLABS
        for q in range(DOWN_SLABS):
            prow = slice(q * slab, (q + 1) * slab)
            ffn = mod[5:6] * _rms(_dot(act_ref[prow, :], wd[...]), g[2:3])
            for c in range(n_ct):
                acc_ref[c, prow, :] = ffn[:, c * LANES:(c + 1) * LANES]
            for s in range(HALO):
                for jb in range(q * slab // (HALO * HALO), (q + 1) * slab // (HALO * HALO)):
                    t0 = s * nj + HALO * jb
                    rows = pl.ds(HALO * HALO * jb + s, HALO, stride=HALO)
                    back = jnp.concatenate([acc_ref[c, rows, :] for c in range(n_ct)], axis=1)
                    o_ref[t0:t0 + HALO, :] = x1_ref[1 - par, t0:t0 + HALO, :] + back

    @pl.when(step == 0)
    def _():
        head_matmul()
        for blk in range(n_blocks + 1):
            head_block(blk)
        head_permute()

    @pl.when(step > 0)
    def _():
        head_matmul()
        tail([[blk for blk in range(n_blocks + 1) if blk * len(chunks) // (n_blocks + 1) == ch]
              for ch in range(len(chunks))])
        head_permute()


def _post(x, mix_a, mix_b, mods, gains, wo, wu, cw, cb, wd, layer, seq_len, tm, latent):
    n = x.shape[0]
    nt = n // tm
    (mix_a, col_a), (mix_b, col_b) = mix_a, mix_b
    tiles_per_seq = max(seq_len // tm, 1)
    cond_row = (lambda i: 1 + i // tiles_per_seq) if latent else (lambda i: 0)

    head_tile = lambda i: jnp.minimum(i, nt - 1)
    tail_tile = lambda i: jnp.maximum(i - 1, 0)

    def triple(width, halo_rows, col=0):
        per = tm // halo_rows
        last = n // halo_rows - 1
        return [
            pl.BlockSpec((tm, width), lambda i: (head_tile(i), col)),
            pl.BlockSpec((halo_rows, width),
                         lambda i: (jnp.maximum(head_tile(i) * per - 1, 0), col)),
            pl.BlockSpec((halo_rows, width),
                         lambda i: (jnp.minimum((head_tile(i) + 1) * per, last), col)),
        ]

    const = lambda *shape: pl.BlockSpec((None,) + shape, lambda i: (layer,) + (0,) * len(shape),
                                        pipeline_mode=pl.Buffered(1))
    in_specs = (triple(D_MODEL, HALO) + triple(WIDTH, 2 * HALO, col_a)
                + triple(WIDTH, 2 * HALO, col_b) + [
        pl.BlockSpec((None, 1, 6, D_MODEL), lambda i: (layer, cond_row(head_tile(i)), 0, 0)),
        pl.BlockSpec((None, 1, 6, D_MODEL), lambda i: (layer, cond_row(tail_tile(i)), 0, 0)),
        const(3, D_MODEL),
        const(2 * WIDTH, D_MODEL),
        const(D_MODEL, 2 * D_FF),
        const(3, 2 * D_FF),
        const(1, 2 * D_FF),
        const(D_FF, D_MODEL),
    ])
    return pl.pallas_call(
        functools.partial(_post_kernel, seq_len=seq_len, n_tiles=nt),
        grid=(nt + 1,),
        in_specs=in_specs,
        out_specs=pl.BlockSpec((tm, D_MODEL), lambda i: (tail_tile(i), 0)),
        out_shape=jax.ShapeDtypeStruct((n, D_MODEL), F32),
        scratch_shapes=[pltpu.VMEM((D_MODEL // LANES, tm, LANES), F32),
                        pltpu.VMEM((D_MODEL // LANES, 2 * HALO + HALO * (tm // HALO + HALO), LANES),
                                   F32),
                        pltpu.VMEM((tm + 2 * HALO, D_MODEL), BF16),
                        pltpu.VMEM((2, tm, D_MODEL), F32),
                        pltpu.VMEM((tm + 2 * HALO, D_MODEL), F32),
                        pltpu.VMEM((tm, D_FF), BF16)],
        compiler_params=_cparams(1),
        name="post_latent" if latent else "post_context",
    )(x, x, x, mix_a, mix_a, mix_a, mix_b, mix_b, mix_b, mods, mods, gains, wo, wu, cw, cb, wd)


def kernel(x_prompt, x_sample, c, cache_na_k, cache_na_v, state_ret_fwd, state_ret_bwd, c_ctx,
           ada_w, ada_b, g_pre_mix, g_post_mix, g_pre_ffn, g_post_ffn, w_in,
           ret_decay_fwd, ret_decay_bwd, na_rpb, w_out, w_up, conv_w, conv_b, w_down):
    depth = w_in.shape[0]
    batch, seq, _ = x_prompt.shape
    dec_batch, dec_seq, _ = x_sample.shape
    past = cache_na_k.shape[2]
    tm = 512

    cond = jnp.concatenate(
        [c_ctx[None, :], c, jnp.zeros((8 - 1 - dec_batch, D_MODEL), F32)], axis=0)
    mods = _modulation(cond, ada_w, ada_b).reshape(depth, 8, 6, D_MODEL)
    rope_tabs = _rope_tables(dec_seq)
    bias, lo, hi = _na_bias_tables(na_rpb, dec_seq // GRID_W)

    w_in_b, wo_b, wu_b, wd_b = (_to_bf16(w, 256) for w in (w_in, w_out, w_up, w_down))
    g_pre = g_pre_mix.reshape(depth, 1, D_MODEL)
    gains = jnp.stack([g_post_mix, g_pre_ffn, g_post_ffn], axis=1)
    cb = conv_b.reshape(depth, 1, 2 * D_FF)
    post_params = (mods, gains, wo_b, wu_b, conv_w, cb, wd_b)

    y_p = x_prompt.reshape(batch * seq, D_MODEL)
    y_s = x_sample.reshape(dec_batch * dec_seq, D_MODEL)
    new_k = new_v = None
    sfs, sbs = [], []
    for l in range(depth):
        dec = (jnp.repeat(ret_decay_fwd[l], HEAD_DIM)[None, :],
               jnp.repeat(ret_decay_bwd[l], HEAD_DIM)[None, :],
               jnp.repeat(ret_decay_fwd[l], LANES)[None, :],
               jnp.repeat(ret_decay_bwd[l], LANES)[None, :])

        p_c, new_k, new_v = _inproj(y_p, mods, g_pre, w_in_b, l, seq, tm, kv_out=(new_k, new_v))
        mix_c, st_c = _ctx_mixer(p_c, dec, batch, seq)
        y_p = _post(y_p, (mix_c, 0), (mix_c, 1), *post_params, l, seq, tm, latent=False)
        sfs.append(_diag_states(st_c[:, :, 0]))
        sbs.append(_diag_states(st_c[:, :, 1]))

        (p_s,) = _inproj(y_s, mods, g_pre, w_in_b, l, dec_seq, tm, rope_tabs=rope_tabs)
        s0 = _block_diag_states(state_ret_fwd[:, l], state_ret_bwd[:, l])
        ret_s = _lat_retention(p_s, dec, s0, dec_batch, dec_seq)
        ck = cache_na_k[:, l].reshape(dec_batch, past, WIDTH).astype(BF16)
        cv = cache_na_v[:, l].reshape(dec_batch, past, WIDTH).astype(BF16)
        na_s = _neighbourhood_attention(p_s, ck, cv, bias, l, lo, hi, dec_batch, dec_seq)
        y_s = _post(y_s, (ret_s, 0), (na_s, 0), *post_params, l, dec_seq, tm, latent=True)

    return (y_p.reshape(batch, seq, D_MODEL),
            y_s.reshape(dec_batch, dec_seq, D_MODEL),
            new_k.reshape(batch, depth, seq, N_HEADS, HEAD_DIM),
            new_v.reshape(batch, depth, seq, N_HEADS, HEAD_DIM),
            jnp.stack(sfs, axis=1), jnp.stack(sbs, axis=1))
```

```python
import functools

import numpy as np
import jax
import jax.numpy as jnp
from jax import lax
from jax.experimental import pallas as pl
from jax.experimental.pallas import tpu as pltpu

F32 = jnp.float32
BF16 = jnp.bfloat16

D_MODEL = 1024
HEAD_DIM = 64
N_HEADS = 8
HEAD_PAIRS = N_HEADS // 2
LANES = 128
WIDTH = N_HEADS * HEAD_DIM
N_GROUPS = 7
IN_WIDTH = N_GROUPS * WIDTH
D_FF = 2816
FF_CHUNK = 768
HEAD_ROWS = 32
CHUNK = 128
RET_UNROLL = 8
GRID_W = 64
NA_KH = 8
NA_KW = 16
NA_STAGE_ROWS = 8
NA_ROWS = 8
ROPE_BASE = 10000.0
EPS = 1e-6
NEG_INF = -1e9
HALO = 8
VMEM_LIMIT = 56 * 1024 * 1024


def _cparams(n_grid):
    return pltpu.CompilerParams(
        dimension_semantics=("arbitrary",) * n_grid, vmem_limit_bytes=VMEM_LIMIT)


def _rms(x, g):
    ms = jnp.mean(x * x, axis=-1, keepdims=True)
    return x * lax.rsqrt(ms + EPS) * g


def _silu(x):
    return x * jax.nn.sigmoid(x)


def _log_sigmoid(x):
    return jnp.minimum(x, 0.0) - jnp.log1p(jnp.exp(-jnp.abs(x)))


def _dot(a, b):
    return jnp.dot(a, b, preferred_element_type=F32)


def _dot_nt(a, b):
    return lax.dot_general(a, b, (((1,), (1,)), ((), ())), preferred_element_type=F32)


def _dot_tn(a, b):
    return lax.dot_general(a, b, (((0,), (0,)), ((), ())), preferred_element_type=F32)


def _mod_kernel(cond_ref, w_ref, b_ref, o_ref):
    s = _silu(cond_ref[...]).astype(BF16)
    o_ref[0] = _dot(s, w_ref[0].astype(BF16)) + b_ref[0]


def _modulation(cond, ada_w, ada_b):
    depth = ada_w.shape[0]
    nb = 6 * D_MODEL // D_MODEL
    return pl.pallas_call(
        _mod_kernel,
        grid=(depth, nb),
        in_specs=[
            pl.BlockSpec((8, D_MODEL), lambda l, j: (0, 0)),
            pl.BlockSpec((1, D_MODEL, D_MODEL), lambda l, j: (l, 0, j)),
            pl.BlockSpec((1, 1, D_MODEL), lambda l, j: (l, 0, j)),
        ],
        out_specs=pl.BlockSpec((1, 8, D_MODEL), lambda l, j: (l, 0, j)),
        out_shape=jax.ShapeDtypeStruct((depth, 8, 6 * D_MODEL), F32),
        compiler_params=_cparams(2),
        name="modulation",
    )(cond, ada_w, ada_b.reshape(depth, 1, 6 * D_MODEL))


def _inproj_kernel(*refs, n_alias, layer):
    outs = refs[4 + n_alias:]
    for thunk in _inproj_groups(*refs[:4], None, None, None, outs[0], outs[1:], n_alias, layer):
        thunk()


def _inproj_groups(x_ref, mod_ref, g_ref, w_ref, cos_ref, sin_up_ref, sin_dn_ref, p_ref, kv_refs,
                   n_alias, layer, width=WIDTH):
    rope = cos_ref is not None
    emit_kv = len(kv_refs) > 0
    assert WIDTH % width == 0 and (width == WIDTH or not emit_kv)
    mod = mod_ref[0]
    h = (_rms(x_ref[...], g_ref[...]) * (1.0 + mod[1:2]) + mod[0:1]).astype(BF16)

    def piece(n):
        cols = slice(n * width, (n + 1) * width)
        g = n * width // WIDTH
        pg = _dot(h, w_ref[:, cols])
        if rope and g < 2:
            parts = []
            for j in range(width // LANES):
                xj = pg[:, j * LANES:(j + 1) * LANES]
                parts.append(xj * cos_ref[...]
                             + pltpu.roll(xj, 16, 1) * sin_up_ref[...]
                             + pltpu.roll(xj, LANES - 16, 1) * sin_dn_ref[...])
            pg = jnp.concatenate(parts, axis=1)
        if g in (0, 4):
            pg = pg * (HEAD_DIM ** -0.5)
        p_ref[:, cols] = pg.astype(BF16)
        if emit_kv and g >= 5:
            kv_ref = kv_refs[g - 5]
            seq = kv_ref.shape[-2]
            for j in range(kv_ref.shape[0]):
                if n_alias:
                    kv_ref[j] = pg[j * seq:(j + 1) * seq]
                else:
                    for l in range(kv_ref.shape[1]):
                        kv_ref[j, l] = (pg[j * seq:(j + 1) * seq] if l == layer
                                        else jnp.zeros((seq, WIDTH), F32))

    return [functools.partial(piece, n) for n in range(IN_WIDTH // width)]


def _inproj_context(x, mods, g_pre, w, layer, seq_len, tm, kv_out):
    n = x.shape[0]
    depth = w.shape[0]
    in_specs = [
        pl.BlockSpec((tm, D_MODEL), lambda i: (i, 0)),
        pl.BlockSpec((None, 1, 6, D_MODEL), lambda i: (layer, 0, 0, 0)),
        pl.BlockSpec((None, 1, D_MODEL), lambda i: (layer, 0, 0)),
        pl.BlockSpec((None, D_MODEL, IN_WIDTH), lambda i: (layer, 0, 0),
                     pipeline_mode=pl.Buffered(1)),
    ]
    args = [x, mods, g_pre, w]
    out_specs = [pl.BlockSpec((tm, IN_WIDTH), lambda i: (i, 0))]
    out_shape = [jax.ShapeDtypeStruct((n, IN_WIDTH), BF16)]
    aliases = {}
    seqs_per_tile = tm // seq_len
    kv_shape = jax.ShapeDtypeStruct((n // seq_len, depth, seq_len, WIDTH), F32)
    assert (kv_out[0] is None) == (kv_out[1] is None)
    for j, prev in enumerate(kv_out):
        if prev is not None:
            aliases[len(args)] = 1 + j
            in_specs.append(pl.BlockSpec(memory_space=pl.ANY))
            args.append(prev)
            out_specs.append(pl.BlockSpec((seqs_per_tile, None, seq_len, WIDTH),
                                          lambda i: (i, layer, 0, 0)))
        else:
            out_specs.append(pl.BlockSpec((seqs_per_tile, depth, seq_len, WIDTH),
                                          lambda i: (i, 0, 0, 0)))
        out_shape.append(kv_shape)
    return pl.pallas_call(
        functools.partial(_inproj_kernel, n_alias=len(aliases), layer=layer),
        grid=(n // tm,),
        in_specs=in_specs,
        out_specs=out_specs,
        out_shape=out_shape,
        input_output_aliases=aliases,
        compiler_params=_cparams(1),
        name="inproj_context",
    )(*args)


def _rope_tables(seq_len):
    t = np.arange(seq_len)
    lane = np.arange(LANES)
    d = lane % HEAD_DIM
    pos = np.where(d[None, :] < HEAD_DIM // 2, (t // GRID_W)[:, None], (t % GRID_W)[:, None])
    pos = pos.astype(np.float32)
    half = HEAD_DIM // 2
    inv = np.power(np.float32(ROPE_BASE), -np.arange(0, half, 2, dtype=np.float32) / half)
    ang = pos * inv[d % (half // 2)][None, :]
    cos, sin = np.cos(ang), np.sin(ang)
    upper = (d % half) >= half // 2
    sin_up = np.where(upper[None, :], sin, 0.0)
    sin_dn = np.where(upper[None, :], 0.0, -sin)
    return (jnp.asarray(cos, F32), jnp.asarray(sin_up, F32), jnp.asarray(sin_dn, F32))


def _retention_tables(dec_f, dec_b, dec_f2, dec_b2):
    lgf, lgb = _log_sigmoid(dec_f), _log_sigmoid(dec_b)
    pos = lax.broadcasted_iota(jnp.int32, (CHUNK, LANES), 0).astype(F32)
    tabs = dict(
        qdf=jnp.exp(lgf * (pos + 1.0)), kdf=jnp.exp(lgf * (CHUNK - 1.0 - pos)),
        cdf=jnp.exp(lgf * float(CHUNK)),
        qdb=jnp.exp(lgb * (CHUNK - pos)), kdb=jnp.exp(lgb * pos),
        cdb=jnp.exp(lgb * float(CHUNK)),
    )
    lgf2, lgb2 = _log_sigmoid(dec_f2), _log_sigmoid(dec_b2)
    i = lax.broadcasted_iota(jnp.int32, (CHUNK, 2 * CHUNK), 0)
    j = lax.broadcasted_iota(jnp.int32, (CHUNK, 2 * CHUNK), 1) & (CHUNK - 1)
    diff = (i - j).astype(F32)
    tabs["decay"] = (jnp.where(diff >= 0, jnp.exp(lgf2 * jnp.maximum(diff, 0.0)), 0.0)
                     + jnp.where(diff <= 0, jnp.exp(lgb2 * jnp.maximum(-diff, 0.0)), 0.0))
    lane = lax.broadcasted_iota(jnp.int32, (1, LANES), 1)
    tabs["head_a"] = lane < HEAD_DIM
    r = lax.broadcasted_iota(jnp.int32, (2 * LANES, LANES), 0) & (LANES - 1)
    c = lax.broadcasted_iota(jnp.int32, (2 * LANES, LANES), 1)
    tabs["same_head"] = (r < HEAD_DIM) == (c < HEAD_DIM)
    return tabs


def _split_heads(x, head_a, axis):
    zero = jnp.zeros_like(x)
    return jnp.concatenate([jnp.where(head_a, x, zero), jnp.where(head_a, zero, x)], axis=axis)


def _chunk_kv(k2, v2, tabs):
    kf = k2.astype(F32)
    kk = jnp.concatenate([kf * tabs["kdf"], kf * tabs["kdb"]], axis=1).astype(BF16)
    return jnp.where(tabs["same_head"], _dot_tn(kk, v2), 0.0)


def _chunk_mix(q2, k2, v2, state, tabs):
    head_a = tabs["head_a"]
    s = _dot_nt(q2, _split_heads(k2, head_a, 0))
    p = (s * tabs["decay"]).astype(BF16)
    qf = q2.astype(F32)
    lhs = jnp.concatenate(
        [p, (qf * tabs["qdf"]).astype(BF16), (qf * tabs["qdb"]).astype(BF16)], axis=1)
    rhs = jnp.concatenate([_split_heads(v2, head_a, 0), state], axis=0)
    return _dot(lhs, rhs)


def _chunk_norm_gate(o, g2, tabs):
    head_a = tabs["head_a"]
    inv = 1.0 / HEAD_DIM
    sum_a = jnp.sum(jnp.where(head_a, o, 0.0), axis=-1, keepdims=True)
    sum_b = jnp.sum(jnp.where(head_a, 0.0, o), axis=-1, keepdims=True)
    d = o - jnp.where(head_a, sum_a, sum_b) * inv
    d2 = d * d
    var_a = jnp.sum(jnp.where(head_a, d2, 0.0), axis=-1, keepdims=True)
    var_b = jnp.sum(jnp.where(head_a, 0.0, d2), axis=-1, keepdims=True)
    o = d * lax.rsqrt(jnp.where(head_a, var_a, var_b) * inv + EPS)
    return o * _silu(g2.astype(F32))


def _ctx_mixer_units(rq, rk, rv, rg, nq, nk, nv, df, db, df2, db2, mix_ref, st_ref):
    n_seq = st_ref.shape[0]
    seq = rq.shape[0] // n_seq
    nc = seq // CHUNK

    thunks = []
    for hp in range(HEAD_PAIRS):
        cols = slice(hp * LANES, (hp + 1) * LANES)
        cols2 = slice(hp * 2 * LANES, (hp + 1) * 2 * LANES)
        shared = {}

        def scan(hp=hp, cols=cols, cols2=cols2, shared=shared):
            tabs = _retention_tables(df[:, cols], db[:, cols], df2[:, cols2], db2[:, cols2])
            shared["tabs"] = tabs
            for b in range(n_seq):
                rows = [slice(b * seq + c * CHUNK, b * seq + (c + 1) * CHUNK) for c in range(nc)]
                kv = [_chunk_kv(rk[r, cols], rv[r, cols], tabs) for r in rows]
                sf = [jnp.zeros((LANES, LANES), F32)]
                for c in range(nc):
                    sf.append(sf[-1] * tabs["cdf"] + kv[c][:LANES])
                sb = [jnp.zeros((LANES, LANES), F32)]
                for c in reversed(range(nc)):
                    sb.append(sb[-1] * tabs["cdb"] + kv[c][LANES:])
                st_ref[b, hp, 0] = sf[nc]
                st_ref[b, hp, 1] = sb[nc]
                shared[b] = [jnp.concatenate([sf[c], sb[nc - 1 - c]], axis=0).astype(BF16)
                             for c in range(nc)]

        def chunk_out(b, c, cols=cols, shared=shared):
            rows = slice(b * seq + c * CHUNK, b * seq + (c + 1) * CHUNK)
            tabs = shared["tabs"]
            o = _chunk_mix(rq[rows, cols], rk[rows, cols], rv[rows, cols], shared[b][c], tabs)
            mix_ref[rows, cols] = _chunk_norm_gate(o, rg[rows, cols], tabs).astype(BF16)

        def attention(b, hp=hp, cols=cols, shared=shared):
            head_a = shared["tabs"]["head_a"]
            tok = slice(b * seq, (b + 1) * seq)
            s = _dot_nt(nq[tok, cols], _split_heads(nk[tok, cols], head_a, 0))
            es, rinv = [], []
            for h in range(2):
                sh = s[:, h * seq:(h + 1) * seq]
                e = jnp.exp(sh - jnp.max(sh, axis=-1, keepdims=True))
                rinv.append(1.0 / jnp.sum(e, axis=-1, keepdims=True))
                es.append(e.astype(BF16))
            o = _dot(jnp.concatenate(es, axis=1), _split_heads(nv[tok, cols], head_a, 0))
            o = o * jnp.where(head_a, rinv[0], rinv[1])
            mix_ref[tok, WIDTH + hp * LANES:WIDTH + (hp + 1) * LANES] = o.astype(BF16)

        thunks.append(scan)
        for b in range(n_seq):
            thunks += [functools.partial(chunk_out, b, c) for c in range(nc)]
            thunks.append(functools.partial(attention, b))
    return thunks


def _inproj_mix_kernel(*refs):
    n_in = 7
    groups = _inproj_groups(*refs[:n_in], refs[-3], (), n_alias=0, layer=0, width=2 * LANES)
    units = _ctx_mixer_units(*refs[n_in:-3], refs[-2], refs[-1])
    done = 0
    for u, unit in enumerate(units):
        upto = (u + 1) * len(groups) // len(units)
        for thunk in groups[done:upto]:
            thunk()
        done = upto
        unit()


def _inproj_latent_and_ctx_mixer(x, mods, g_pre, w, layer, seq_len, tm, rope_tabs,
                                 p_ctx, dec, batch, seq):
    n = x.shape[0]
    n_tiles = n // tm
    n_seq = batch // n_tiles
    assert n_seq * n_tiles == batch and seq_len % tm == 0
    tiles_per_seq = seq_len // tm
    group = lambda g: pl.BlockSpec((n_seq * seq, WIDTH), lambda i, g=g: (i, g))
    vec = lambda width: pl.BlockSpec((1, width), lambda i: (0, 0))
    in_specs = [
        pl.BlockSpec((tm, D_MODEL), lambda i: (i, 0)),
        pl.BlockSpec((None, 1, 6, D_MODEL), lambda i: (layer, 1 + i // tiles_per_seq, 0, 0)),
        pl.BlockSpec((None, 1, D_MODEL), lambda i: (layer, 0, 0)),
        pl.BlockSpec((None, D_MODEL, IN_WIDTH), lambda i: (layer, 0, 0),
                     pipeline_mode=pl.Buffered(1)),
    ] + [pl.BlockSpec((tm, LANES), lambda i: (i % tiles_per_seq, 0))] * 3 + [
        group(g) for g in range(N_GROUPS)] + [vec(WIDTH), vec(WIDTH), vec(2 * WIDTH),
                                              vec(2 * WIDTH)]
    return pl.pallas_call(
        _inproj_mix_kernel,
        grid=(n_tiles,),
        in_specs=in_specs,
        out_specs=[
            pl.BlockSpec((tm, IN_WIDTH), lambda i: (i, 0)),
            pl.BlockSpec((n_seq * seq, 2 * WIDTH), lambda i: (i, 0)),
            pl.BlockSpec((n_seq, HEAD_PAIRS, 2, LANES, LANES), lambda i: (i, 0, 0, 0, 0)),
        ],
        out_shape=[
            jax.ShapeDtypeStruct((n, IN_WIDTH), BF16),
            jax.ShapeDtypeStruct((batch * seq, 2 * WIDTH), BF16),
            jax.ShapeDtypeStruct((batch, HEAD_PAIRS, 2, LANES, LANES), F32),
        ],
        compiler_params=_cparams(1),
        name="inproj_latent_ctx_mixer",
    )(x, mods, g_pre, w, *rope_tabs, *([p_ctx] * N_GROUPS), *dec)


def _lat_retention_kernel(q_ref, k_ref, v_ref, g_ref, df, db, df2, db2, s0_ref, o_ref,
                          kv_ref, st_ref):
    nc = q_ref.shape[0] // CHUNK
    tabs = _retention_tables(df[...], db[...], df2[...], db2[...])

    def rows(c):
        return pl.ds(pl.multiple_of(c * CHUNK, CHUNK), CHUNK)

    def kv_body(c, carry):
        kv_ref[c] = _chunk_kv(k_ref[rows(c), :], v_ref[rows(c), :], tabs)
        return carry
    lax.fori_loop(0, nc, kv_body, 0, unroll=RET_UNROLL)

    def fwd_body(c, s):
        st_ref[c, :LANES, :] = s.astype(BF16)
        return s * tabs["cdf"] + kv_ref[c, :LANES, :]
    lax.fori_loop(0, nc, fwd_body, s0_ref[0, 0, 0])

    def bwd_body(i, s):
        c = nc - 1 - i
        st_ref[c, LANES:, :] = s.astype(BF16)
        return s * tabs["cdb"] + kv_ref[c, LANES:, :]
    lax.fori_loop(0, nc, bwd_body, s0_ref[0, 0, 1])

    def out_body(c, carry):
        r = rows(c)
        o = _chunk_mix(q_ref[r, :], k_ref[r, :], v_ref[r, :], st_ref[c], tabs)
        o_ref[r, :] = _chunk_norm_gate(o, g_ref[r, :], tabs).astype(BF16)
        return carry
    lax.fori_loop(0, nc, out_body, 0, unroll=RET_UNROLL)


def _lat_retention(p, dec, s0, batch, seq):
    nc = seq // CHUNK
    group = lambda g: pl.BlockSpec((seq, LANES), lambda b, h, g=g: (b, g * HEAD_PAIRS + h))
    vec = lambda w: pl.BlockSpec((1, w), lambda b, h: (0, h))
    return pl.pallas_call(
        _lat_retention_kernel,
        grid=(batch, HEAD_PAIRS),
        in_specs=[group(g) for g in range(4)] + [vec(LANES), vec(LANES), vec(2 * LANES),
                                                 vec(2 * LANES)]
        + [pl.BlockSpec((1, 1, 2, LANES, LANES), lambda b, h: (b, h, 0, 0, 0))],
        out_specs=pl.BlockSpec((seq, LANES), lambda b, h: (b, h)),
        out_shape=jax.ShapeDtypeStruct((batch * seq, WIDTH), BF16),
        scratch_shapes=[pltpu.VMEM((nc, 2 * LANES, LANES), F32),
                        pltpu.VMEM((nc, 2 * LANES, LANES), BF16)],
        compiler_params=_cparams(2),
        name="latent_retention",
    )(*([p] * 4), *dec, s0)


def _block_diag_states(s_f, s_b):
    def bd(s):
        b = s.shape[0]
        s = s.reshape(b, HEAD_PAIRS, 2, HEAD_DIM, HEAD_DIM)
        z = jnp.zeros_like(s[:, :, 0])
        top = jnp.concatenate([s[:, :, 0], z], axis=-1)
        bot = jnp.concatenate([z, s[:, :, 1]], axis=-1)
        return jnp.concatenate([top, bot], axis=-2)
    return jnp.stack([bd(s_f), bd(s_b)], axis=2)


def _diag_states(st):
    b = st.shape[0]
    a = st[:, :, :HEAD_DIM, :HEAD_DIM]
    c = st[:, :, HEAD_DIM:, HEAD_DIM:]
    return jnp.stack([a, c], axis=2).reshape(b, N_HEADS, HEAD_DIM, HEAD_DIM)


def _na_kernel(q_ref, k_ref, v_ref, ck_ref, cv_ref, bias_ref, o_ref, *, lo, hi):
    rows_total = k_ref.shape[0] // GRID_W
    lane = lax.broadcasted_iota(jnp.int32, (1, LANES), 1)
    head_a = lane < HEAD_DIM
    wins, cls = [], []
    for i in range(NA_ROWS):
        r = pl.program_id(1) * NA_ROWS + i
        rs = jnp.clip(r - NA_KH // 2, 0, rows_total - NA_KH)
        wins.append(pl.ds(pl.multiple_of(rs * GRID_W, GRID_W), NA_KH * GRID_W))
        cls.append(jnp.where(r < lo, r, jnp.where(r > hi, r - hi + lo, lo)))
    stages = [(hp, r0) for hp in range(HEAD_PAIRS) for r0 in range(0, NA_ROWS, NA_STAGE_ROWS)]

    def scores(stage):
        hp, r0 = stage
        cols = slice(hp * LANES, (hp + 1) * LANES)
        qq = jnp.concatenate(
            [_split_heads(q_ref[i * GRID_W:(i + 1) * GRID_W, cols], head_a, 0)
             for i in range(r0, r0 + NA_STAGE_ROWS)], axis=0)
        s_ctx = _dot_nt(qq, ck_ref[0, :, cols])
        s_loc = [_dot_nt(qq[j * 2 * GRID_W:(j + 1) * 2 * GRID_W], k_ref[wins[r0 + j], cols])
                 for j in range(NA_STAGE_ROWS)]
        return s_ctx, s_loc

    def softmax(stage, s_ctx, s_loc):
        hp, r0 = stage
        m_ctx = jnp.max(s_ctx, axis=-1, keepdims=True)
        m = []
        for j in range(NA_STAGE_ROWS):
            blk = slice(j * 2 * GRID_W, (j + 1) * 2 * GRID_W)
            s_loc[j] = s_loc[j] + bias_ref[cls[r0 + j], hp]
            m.append(jnp.maximum(jnp.max(s_loc[j], axis=-1, keepdims=True), m_ctx[blk]))
        e_ctx = jnp.exp(s_ctx - jnp.concatenate(m, axis=0))
        den_ctx = jnp.sum(e_ctx, axis=-1, keepdims=True)
        e_loc, den = [], []
        for j in range(NA_STAGE_ROWS):
            blk = slice(j * 2 * GRID_W, (j + 1) * 2 * GRID_W)
            e = jnp.exp(s_loc[j] - m[j])
            den.append(jnp.sum(e, axis=-1, keepdims=True) + den_ctx[blk])
            e_loc.append(e.astype(BF16))
        return e_ctx.astype(BF16), e_loc, den

    def weighted_values(stage, e_ctx, e_loc, den):
        hp, r0 = stage
        cols = slice(hp * LANES, (hp + 1) * LANES)
        o_ctx = _dot(e_ctx, cv_ref[0, :, cols])
        for j in range(NA_STAGE_ROWS):
            i = r0 + j
            blk = slice(j * 2 * GRID_W, (j + 1) * 2 * GRID_W)
            o = (_dot(e_loc[j], v_ref[wins[i], cols]) + o_ctx[blk]) / den[j]
            o_ref[i * GRID_W:(i + 1) * GRID_W, cols] = jnp.where(
                head_a, o[:GRID_W], o[GRID_W:]).astype(BF16)

    nxt = scores(stages[0])
    pending = None
    for n, stage in enumerate(stages):
        s_ctx, s_loc = nxt
        if n + 1 < len(stages):
            nxt = scores(stages[n + 1])
        probs = softmax(stage, s_ctx, s_loc)
        if pending is not None:
            weighted_values(*pending)
        pending = (stage,) + probs
    weighted_values(*pending)


def _bias_build_kernel(rpb_ref, o_ref, *, dr_first, n_dr):
    qc = lax.broadcasted_iota(jnp.int32, (GRID_W, LANES), 0)
    lane = lax.broadcasted_iota(jnp.int32, (GRID_W, LANES), 1)
    kc = lane & (GRID_W - 1)
    cs = jnp.clip(qc - NA_KW // 2, 0, GRID_W - NA_KW)
    inside = (kc >= cs) & (kc < cs + NA_KW)
    first = lane < GRID_W
    for h in range(N_HEADS):
        lo_half, hi_half = [], []
        for dr in range(n_dr):
            line = jnp.broadcast_to(rpb_ref[pl.ds(h * n_dr + dr, 1), :], (GRID_W, LANES))
            lo_half.append(pltpu.roll(line, LANES - (NA_KW - 1), 1, stride=1, stride_axis=0))
            hi_half.append(pltpu.roll(line, GRID_W - (NA_KW - 1), 1, stride=1, stride_axis=0))
        for cls, dr0 in enumerate(dr_first):
            for jp in range(NA_KH // 2):
                dr = dr0 + 2 * jp
                tile = jnp.where(inside, jnp.where(first, lo_half[dr], hi_half[dr + 1]), NEG_INF)
                o_ref[cls, h // 2, (h % 2) * GRID_W:(h % 2 + 1) * GRID_W,
                      jp * LANES:(jp + 1) * LANES] = tile


def _na_bias_tables(na_rpb, rows_total):
    depth, heads, n_dr, n_dc = na_rpb.shape
    kh = NA_KH
    lo, hi = kh // 2, rows_total - kh // 2 - 1
    reps = list(range(lo)) + [lo] + list(range(hi + 1, rows_total))
    dr_first = tuple(int(np.clip(r - kh // 2, 0, rows_total - kh)) - r + kh - 1 for r in reps)
    lines = jnp.pad(na_rpb.reshape(depth, heads * n_dr, n_dc), ((0, 0), (0, 0), (0, LANES - n_dc)))
    bias = pl.pallas_call(
        functools.partial(_bias_build_kernel, dr_first=dr_first, n_dr=n_dr),
        grid=(depth,),
        in_specs=[pl.BlockSpec((None, heads * n_dr, LANES), lambda l: (l, 0, 0))],
        out_specs=pl.BlockSpec((None, len(reps), HEAD_PAIRS, 2 * GRID_W, kh * GRID_W),
                               lambda l: (l, 0, 0, 0, 0)),
        out_shape=jax.ShapeDtypeStruct(
            (depth, len(reps), HEAD_PAIRS, 2 * GRID_W, kh * GRID_W), F32),
        compiler_params=_cparams(1),
        name="na_bias_build",
    )(lines)
    return bias, lo, hi


def _neighbourhood_attention(p, ctx_k, ctx_v, bias, layer, lo, hi, batch, seq):
    rows_total = seq // GRID_W
    past = ctx_k.shape[1]

    groups = rows_total // NA_ROWS
    return pl.pallas_call(
        functools.partial(_na_kernel, lo=lo, hi=hi),
        grid=(batch, groups),
        in_specs=[
            pl.BlockSpec((NA_ROWS * GRID_W, WIDTH), lambda b, g: (b * groups + g, 4)),
            pl.BlockSpec((seq, WIDTH), lambda b, g: (b, 5), pipeline_mode=pl.Buffered(1)),
            pl.BlockSpec((seq, WIDTH), lambda b, g: (b, 6), pipeline_mode=pl.Buffered(1)),
            pl.BlockSpec((1, past, WIDTH), lambda b, g: (b, 0, 0), pipeline_mode=pl.Buffered(1)),
            pl.BlockSpec((1, past, WIDTH), lambda b, g: (b, 0, 0), pipeline_mode=pl.Buffered(1)),
            pl.BlockSpec((None,) + bias.shape[1:], lambda b, g: (layer, 0, 0, 0, 0),
                         pipeline_mode=pl.Buffered(1)),
        ],
        out_specs=pl.BlockSpec((NA_ROWS * GRID_W, WIDTH), lambda b, g: (b * groups + g, 0)),
        out_shape=jax.ShapeDtypeStruct((batch * seq, WIDTH), BF16),
        compiler_params=_cparams(2),
        name="neighbourhood_attention",
    )(p, p, p, ctx_k, ctx_v, bias)


def _post_kernel(xm, xp, xn, am, ap, an, bm, bp, bn, mod_head, mod_tail, g_ref, wo, wu, cw, cb, wd,
                 o_ref, acc_ref, h_ref, hp_ref, x1_ref, y_ref, act_ref, *, seq_len, n_tiles):
    step = pl.program_id(0)
    tm = xm.shape[0]
    nj = tm // HALO
    tiles_per_seq = max(seq_len // tm, 1)
    assert (tm % seq_len == 0 and seq_len % nj == 0) or seq_len % tm == 0
    h_rows = h_ref.shape[1]
    seg_pitch = nj + HALO
    assert h_rows == 2 * HALO + HALO * seg_pitch and nj % HEAD_ROWS == 0
    n_ct = D_MODEL // LANES
    g = g_ref[...]

    def ext(main, prev, nxt):
        lo = prev[...].astype(F32)[prev.shape[0] - HALO:]
        hi = nxt[...].astype(F32)[:HALO]
        return jnp.concatenate([lo, main[...].astype(F32), hi], axis=0)

    chunks = [(c0, min(FF_CHUNK, D_FF - c0)) for c0 in range(0, D_FF, FF_CHUNK)]
    par = step % 2
    mod_h = mod_head[0]
    n_blocks = tm // HEAD_ROWS

    def head_matmul():
        mixed = jnp.concatenate([ext(am, ap, an), ext(bm, bp, bn)], axis=1).astype(BF16)
        y_ref[...] = _dot(mixed, wo[...])

    def head_rows(x, y):
        x1 = x + mod_h[2:3] * _rms(y, g[0:1])
        return x1, _rms(x1, g[1:2]) * (1.0 + mod_h[4:5]) + mod_h[3:4]

    def head_block(blk):
        if blk < n_blocks:
            rows = slice(blk * HEAD_ROWS, (blk + 1) * HEAD_ROWS)
            erows = slice(HALO + blk * HEAD_ROWS, HALO + (blk + 1) * HEAD_ROWS)
            x1, h = head_rows(xm[rows, :], y_ref[erows, :])
            x1_ref[par, rows, :] = x1
            t0 = blk * HEAD_ROWS
            hrow = HALO + (t0 // nj) * seg_pitch + t0 % nj
            for c in range(n_ct):
                h_ref[c, hrow:hrow + HEAD_ROWS, :] = h[:, c * LANES:(c + 1) * LANES]
        else:
            x = jnp.concatenate([xp[...], xn[...]], axis=0)
            y = jnp.concatenate([y_ref[0:HALO, :], y_ref[HALO + tm:2 * HALO + tm, :]], axis=0)
            _, h = head_rows(x, y)
            t = jnp.minimum(step, n_tiles - 1) % tiles_per_seq
            ridx = lax.broadcasted_iota(jnp.int32, h.shape, 0)
            keep = ((ridx >= HALO) | (t != 0)) & ((ridx < HALO) | (t != tiles_per_seq - 1))
            h = jnp.where(keep, h, 0.0)
            for c in range(n_ct):
                h_ref[c, 0:HALO, :] = h[:HALO, c * LANES:(c + 1) * LANES]
                h_ref[c, h_rows - HALO:h_rows, :] = h[HALO:, c * LANES:(c + 1) * LANES]

    def head_permute():
        def group(rows):
            return jnp.concatenate([h_ref[c, rows, :] for c in range(n_ct)], axis=1)
        def put(row0, first, second):
            hp_ref[row0:row0 + 2 * HALO, :] = jnp.concatenate(
                [group(first), group(second)], axis=0).astype(BF16)
        for j in range(0, nj, 2):
            put(j * HALO, pl.ds(HALO + j, HALO, stride=seg_pitch),
                pl.ds(HALO + j + 1, HALO, stride=seg_pitch))
        put(tm, pl.ds(0, HALO), pl.ds(h_rows - HALO, HALO))

    def tail(interleaved):
        mod = mod_tail[0]

        def conv(u, cols):
            w = cw[:, cols]
            sub = lax.broadcasted_iota(jnp.int32, (HALO, u.shape[1]), 0)
            before = jnp.where(sub == 0, u[tm + HALO - 1:tm + HALO],
                               pltpu.roll(u[tm - HALO:tm], 1, 0))
            after = jnp.where(sub == HALO - 1, u[tm + HALO:tm + HALO + 1],
                              pltpu.roll(u[0:HALO], HALO - 1, 0))
            for s in range(1, HALO):
                if (s * nj) % seq_len == 0:
                    before = jnp.where(sub == s, 0.0, before)
                    after = jnp.where(sub == s - 1, 0.0, after)
            prev = jnp.concatenate([before, u[0:tm - HALO]], axis=0)
            nxt = jnp.concatenate([u[HALO:tm], after], axis=0)
            return prev * w[0:1] + u[0:tm] * w[1:2] + nxt * w[2:3] + cb[:, cols]

        def cols(ch):
            ca = slice(chunks[ch][0], chunks[ch][0] + chunks[ch][1])
            return ca, slice(D_FF + ca.start, D_FF + ca.stop)

        def up(ch):
            ca, cg = cols(ch)
            return _dot(hp_ref[...], wu[:, cg]), _dot(hp_ref[...], wu[:, ca])

        nxt_u = up(0)
        for ch in range(len(chunks)):
            ug, ua = nxt_u
            if ch + 1 < len(chunks):
                nxt_u = up(ch + 1)
            for blk in interleaved[ch]:
                head_block(blk)
            ca, cg = cols(ch)
            act_ref[:, ca] = (_silu(conv(ug, cg)) * conv(ua, ca)).astype(BF16)
        ffn = mod[5:6] * _rms(_dot(act_ref[...], wd[...]), g[2:3])
        for c in range(n_ct):
            acc_ref[c] = ffn[:, c * LANES:(c + 1) * LANES]

        for s in range(HALO):
            for jb in range(nj // HALO):
                t0 = s * nj + HALO * jb
                rows = pl.ds(HALO * HALO * jb + s, HALO, stride=HALO)
                back = jnp.concatenate([acc_ref[c, rows, :] for c in range(n_ct)], axis=1)
                o_ref[t0:t0 + HALO, :] = x1_ref[1 - par, t0:t0 + HALO, :] + back

    @pl.when(step == 0)
    def _():
        head_matmul()
        for blk in range(n_blocks + 1):
            head_block(blk)
        head_permute()

    @pl.when(step > 0)
    def _():
        head_matmul()
        tail([[blk for blk in range(n_blocks + 1) if blk * len(chunks) // (n_blocks + 1) == ch]
              for ch in range(len(chunks))])
        head_permute()


def _post(x, mix_a, mix_b, mods, gains, wo, wu, cw, cb, wd, layer, seq_len, tm, latent):
    n = x.shape[0]
    nt = n // tm
    (mix_a, col_a), (mix_b, col_b) = mix_a, mix_b
    tiles_per_seq = max(seq_len // tm, 1)
    cond_row = (lambda i: 1 + i // tiles_per_seq) if latent else (lambda i: 0)

    head_tile = lambda i: jnp.minimum(i, nt - 1)
    tail_tile = lambda i: jnp.maximum(i - 1, 0)

    def triple(width, halo_rows, col=0):
        per = tm // halo_rows
        last = n // halo_rows - 1
        return [
            pl.BlockSpec((tm, width), lambda i: (head_tile(i), col)),
            pl.BlockSpec((halo_rows, width),
                         lambda i: (jnp.maximum(head_tile(i) * per - 1, 0), col)),
            pl.BlockSpec((halo_rows, width),
                         lambda i: (jnp.minimum((head_tile(i) + 1) * per, last), col)),
        ]

    const = lambda *shape: pl.BlockSpec((None,) + shape, lambda i: (layer,) + (0,) * len(shape),
                                        pipeline_mode=pl.Buffered(1))
    in_specs = (triple(D_MODEL, HALO) + triple(WIDTH, 2 * HALO, col_a)
                + triple(WIDTH, 2 * HALO, col_b) + [
        pl.BlockSpec((None, 1, 6, D_MODEL), lambda i: (layer, cond_row(head_tile(i)), 0, 0)),
        pl.BlockSpec((None, 1, 6, D_MODEL), lambda i: (layer, cond_row(tail_tile(i)), 0, 0)),
        const(3, D_MODEL),
        const(2 * WIDTH, D_MODEL),
        const(D_MODEL, 2 * D_FF),
        const(3, 2 * D_FF),
        const(1, 2 * D_FF),
        const(D_FF, D_MODEL),
    ])
    return pl.pallas_call(
        functools.partial(_post_kernel, seq_len=seq_len, n_tiles=nt),
        grid=(nt + 1,),
        in_specs=in_specs,
        out_specs=pl.BlockSpec((tm, D_MODEL), lambda i: (tail_tile(i), 0)),
        out_shape=jax.ShapeDtypeStruct((n, D_MODEL), F32),
        scratch_shapes=[pltpu.VMEM((D_MODEL // LANES, tm, LANES), F32),
                        pltpu.VMEM((D_MODEL // LANES, 2 * HALO + HALO * (tm // HALO + HALO), LANES),
                                   F32),
                        pltpu.VMEM((tm + 2 * HALO, D_MODEL), BF16),
                        pltpu.VMEM((2, tm, D_MODEL), F32),
                        pltpu.VMEM((tm + 2 * HALO, D_MODEL), F32),
                        pltpu.VMEM((tm, D_FF), BF16)],
        compiler_params=_cparams(1),
        name="post_latent" if latent else "post_context",
    )(x, x, x, mix_a, mix_a, mix_a, mix_b, mix_b, mix_b, mods, mods, gains, wo, wu, cw, cb, wd)


def kernel(x_prompt, x_sample, c, cache_na_k, cache_na_v, state_ret_fwd, state_ret_bwd, c_ctx,
           ada_w, ada_b, g_pre_mix, g_post_mix, g_pre_ffn, g_post_ffn, w_in,
           ret_decay_fwd, ret_decay_bwd, na_rpb, w_out, w_up, conv_w, conv_b, w_down):
    depth = w_in.shape[0]
    batch, seq, _ = x_prompt.shape
    dec_batch, dec_seq, _ = x_sample.shape
    past = cache_na_k.shape[2]
    tm = 512

    cond = jnp.concatenate(
        [c_ctx[None, :], c, jnp.zeros((8 - 1 - dec_batch, D_MODEL), F32)], axis=0)
    mods = _modulation(cond, ada_w, ada_b).reshape(depth, 8, 6, D_MODEL)
    rope_tabs = _rope_tables(dec_seq)
    bias, lo, hi = _na_bias_tables(na_rpb, dec_seq // GRID_W)

    w_in_b, wo_b, wu_b, wd_b = (w.astype(BF16) for w in (w_in, w_out, w_up, w_down))
    g_pre = g_pre_mix.reshape(depth, 1, D_MODEL)
    gains = jnp.stack([g_post_mix, g_pre_ffn, g_post_ffn], axis=1)
    cb = conv_b.reshape(depth, 1, 2 * D_FF)
    post_params = (mods, gains, wo_b, wu_b, conv_w, cb, wd_b)

    y_p = x_prompt.reshape(batch * seq, D_MODEL)
    y_s = x_sample.reshape(dec_batch * dec_seq, D_MODEL)
    new_k = new_v = None
    sfs, sbs = [], []
    for l in range(depth):
        dec = (jnp.repeat(ret_decay_fwd[l], HEAD_DIM)[None, :],
               jnp.repeat(ret_decay_bwd[l], HEAD_DIM)[None, :],
               jnp.repeat(ret_decay_fwd[l], LANES)[None, :],
               jnp.repeat(ret_decay_bwd[l], LANES)[None, :])

        p_c, new_k, new_v = _inproj_context(y_p, mods, g_pre, w_in_b, l, seq, tm, (new_k, new_v))
        p_s, mix_c, st_c = _inproj_latent_and_ctx_mixer(
            y_s, mods, g_pre, w_in_b, l, dec_seq, tm, rope_tabs, p_c, dec, batch, seq)
        y_p = _post(y_p, (mix_c, 0), (mix_c, 1), *post_params, l, seq, tm, latent=False)
        sfs.append(_diag_states(st_c[:, :, 0]))
        sbs.append(_diag_states(st_c[:, :, 1]))

        s0 = _block_diag_states(state_ret_fwd[:, l], state_ret_bwd[:, l])
        ret_s = _lat_retention(p_s, dec, s0, dec_batch, dec_seq)
        ck = cache_na_k[:, l].reshape(dec_batch, past, WIDTH).astype(BF16)
        cv = cache_na_v[:, l].reshape(dec_batch, past, WIDTH).astype(BF16)
        na_s = _neighbourhood_attention(p_s, ck, cv, bias, l, lo, hi, dec_batch, dec_seq)
        y_s = _post(y_s, (ret_s, 0), (na_s, 0), *post_params, l, dec_seq, tm, latent=True)

    return (y_p.reshape(batch, seq, D_MODEL),
            y_s.reshape(dec_batch, dec_seq, D_MODEL),
            new_k.reshape(batch, depth, seq, N_HEADS, HEAD_DIM),
            new_v.reshape(batch, depth, seq, N_HEADS, HEAD_DIM),
            jnp.stack(sfs, axis=1), jnp.stack(sbs, axis=1))
```

```python
import functools

import numpy as np
import jax
import jax.numpy as jnp
from jax import lax
from jax.experimental import pallas as pl
from jax.experimental.pallas import tpu as pltpu

F32 = jnp.float32
BF16 = jnp.bfloat16

D_MODEL = 1024
HEAD_DIM = 64
N_HEADS = 8
HEAD_PAIRS = N_HEADS // 2
LANES = 128
WIDTH = N_HEADS * HEAD_DIM
N_GROUPS = 7
IN_WIDTH = N_GROUPS * WIDTH
D_FF = 2816
FF_CHUNK = 768
HEAD_ROWS = 32
CHUNK = 128
RET_UNROLL = 8
GRID_W = 64
NA_KH = 8
NA_KW = 16
NA_STAGE_ROWS = 8
NA_ROWS = 8
ROPE_BASE = 10000.0
EPS = 1e-6
NEG_INF = -1e9
HALO = 8
VMEM_LIMIT = 56 * 1024 * 1024


def _cparams(n_grid):
    return pltpu.CompilerParams(
        dimension_semantics=("arbitrary",) * n_grid, vmem_limit_bytes=VMEM_LIMIT)


def _rms(x, g):
    ms = jnp.mean(x * x, axis=-1, keepdims=True)
    return x * lax.rsqrt(ms + EPS) * g


def _silu(x):
    return x * jax.nn.sigmoid(x)


def _log_sigmoid(x):
    return jnp.minimum(x, 0.0) - jnp.log1p(jnp.exp(-jnp.abs(x)))


def _dot(a, b):
    return jnp.dot(a, b, preferred_element_type=F32)


def _dot_nt(a, b):
    return lax.dot_general(a, b, (((1,), (1,)), ((), ())), preferred_element_type=F32)


def _dot_tn(a, b):
    return lax.dot_general(a, b, (((0,), (0,)), ((), ())), preferred_element_type=F32)


def _inproj_kernel(*refs, n_alias, layer):
    outs = refs[4 + n_alias:]
    for start, finish in _inproj_groups(*refs[:4], None, None, None, outs[0], outs[1:], n_alias,
                                        layer):
        finish(start())


def _inproj_groups(x_ref, mod_ref, g_ref, w_ref, cos_ref, sin_up_ref, sin_dn_ref, p_ref, kv_refs,
                   n_alias, layer, width=WIDTH):
    rope = cos_ref is not None
    emit_kv = len(kv_refs) > 0
    assert WIDTH % width == 0 and (width == WIDTH or not emit_kv)
    mod = mod_ref[0]
    h = (_rms(x_ref[...], g_ref[...]) * (1.0 + mod[1:2]) + mod[0:1]).astype(BF16)

    def start(n):
        return _dot(h, w_ref[:, n * width:(n + 1) * width])

    def finish(n, pg):
        cols = slice(n * width, (n + 1) * width)
        g = n * width // WIDTH
        if rope and g < 2:
            parts = []
            for j in range(width // LANES):
                xj = pg[:, j * LANES:(j + 1) * LANES]
                parts.append(xj * cos_ref[...]
                             + pltpu.roll(xj, 16, 1) * sin_up_ref[...]
                             + pltpu.roll(xj, LANES - 16, 1) * sin_dn_ref[...])
            pg = jnp.concatenate(parts, axis=1)
        if g in (0, 4):
            pg = pg * (HEAD_DIM ** -0.5)
        p_ref[:, cols] = pg.astype(BF16)
        if emit_kv and g >= 5:
            kv_ref = kv_refs[g - 5]
            seq = kv_ref.shape[-2]
            for j in range(kv_ref.shape[0]):
                if n_alias:
                    kv_ref[j] = pg[j * seq:(j + 1) * seq]
                else:
                    for l in range(kv_ref.shape[1]):
                        kv_ref[j, l] = (pg[j * seq:(j + 1) * seq] if l == layer
                                        else jnp.zeros((seq, WIDTH), F32))

    return [(functools.partial(start, n), functools.partial(finish, n))
            for n in range(IN_WIDTH // width)]


def _inproj_context(x, mods, g_pre, w, layer, seq_len, tm, kv_out):
    n = x.shape[0]
    depth = w.shape[0]
    in_specs = [
        pl.BlockSpec((tm, D_MODEL), lambda i: (i, 0)),
        pl.BlockSpec((None, 1, 6, D_MODEL), lambda i: (layer, 0, 0, 0)),
        pl.BlockSpec((None, 1, D_MODEL), lambda i: (layer, 0, 0)),
        pl.BlockSpec((None, D_MODEL, IN_WIDTH), lambda i: (layer, 0, 0),
                     pipeline_mode=pl.Buffered(1)),
    ]
    args = [x, mods, g_pre, w]
    out_specs = [pl.BlockSpec((tm, IN_WIDTH), lambda i: (i, 0))]
    out_shape = [jax.ShapeDtypeStruct((n, IN_WIDTH), BF16)]
    aliases = {}
    seqs_per_tile = tm // seq_len
    kv_shape = jax.ShapeDtypeStruct((n // seq_len, depth, seq_len, WIDTH), F32)
    assert (kv_out[0] is None) == (kv_out[1] is None)
    for j, prev in enumerate(kv_out):
        if prev is not None:
            aliases[len(args)] = 1 + j
            in_specs.append(pl.BlockSpec(memory_space=pl.ANY))
            args.append(prev)
            out_specs.append(pl.BlockSpec((seqs_per_tile, None, seq_len, WIDTH),
                                          lambda i: (i, layer, 0, 0)))
        else:
            out_specs.append(pl.BlockSpec((seqs_per_tile, depth, seq_len, WIDTH),
                                          lambda i: (i, 0, 0, 0)))
        out_shape.append(kv_shape)
    return pl.pallas_call(
        functools.partial(_inproj_kernel, n_alias=len(aliases), layer=layer),
        grid=(n // tm,),
        in_specs=in_specs,
        out_specs=out_specs,
        out_shape=out_shape,
        input_output_aliases=aliases,
        compiler_params=_cparams(1),
        name="inproj_context",
    )(*args)


def _rope_tables(seq_len):
    t = np.arange(seq_len)
    lane = np.arange(LANES)
    d = lane % HEAD_DIM
    pos = np.where(d[None, :] < HEAD_DIM // 2, (t // GRID_W)[:, None], (t % GRID_W)[:, None])
    pos = pos.astype(np.float32)
    half = HEAD_DIM // 2
    inv = np.power(np.float32(ROPE_BASE), -np.arange(0, half, 2, dtype=np.float32) / half)
    ang = pos * inv[d % (half // 2)][None, :]
    cos, sin = np.cos(ang), np.sin(ang)
    upper = (d % half) >= half // 2
    sin_up = np.where(upper[None, :], sin, 0.0)
    sin_dn = np.where(upper[None, :], 0.0, -sin)
    return (jnp.asarray(cos, F32), jnp.asarray(sin_up, F32), jnp.asarray(sin_dn, F32))


def _retention_tables(dec_f, dec_b, dec_f2, dec_b2):
    lgf, lgb = _log_sigmoid(dec_f), _log_sigmoid(dec_b)
    pos = lax.broadcasted_iota(jnp.int32, (CHUNK, LANES), 0).astype(F32)
    tabs = dict(
        qdf=jnp.exp(lgf * (pos + 1.0)), kdf=jnp.exp(lgf * (CHUNK - 1.0 - pos)),
        cdf=jnp.exp(lgf * float(CHUNK)),
        qdb=jnp.exp(lgb * (CHUNK - pos)), kdb=jnp.exp(lgb * pos),
        cdb=jnp.exp(lgb * float(CHUNK)),
    )
    lgf2, lgb2 = _log_sigmoid(dec_f2), _log_sigmoid(dec_b2)
    i = lax.broadcasted_iota(jnp.int32, (CHUNK, 2 * CHUNK), 0)
    j = lax.broadcasted_iota(jnp.int32, (CHUNK, 2 * CHUNK), 1) & (CHUNK - 1)
    diff = (i - j).astype(F32)
    tabs["decay"] = (jnp.where(diff >= 0, jnp.exp(lgf2 * jnp.maximum(diff, 0.0)), 0.0)
                     + jnp.where(diff <= 0, jnp.exp(lgb2 * jnp.maximum(-diff, 0.0)), 0.0))
    lane = lax.broadcasted_iota(jnp.int32, (1, LANES), 1)
    tabs["head_a"] = lane < HEAD_DIM
    r = lax.broadcasted_iota(jnp.int32, (2 * LANES, LANES), 0) & (LANES - 1)
    c = lax.broadcasted_iota(jnp.int32, (2 * LANES, LANES), 1)
    tabs["same_head"] = (r < HEAD_DIM) == (c < HEAD_DIM)
    return tabs


def _split_heads(x, head_a, axis):
    zero = jnp.zeros_like(x)
    return jnp.concatenate([jnp.where(head_a, x, zero), jnp.where(head_a, zero, x)], axis=axis)


def _chunk_kv(k2, v2, tabs):
    kf = k2.astype(F32)
    kk = jnp.concatenate([kf * tabs["kdf"], kf * tabs["kdb"]], axis=1).astype(BF16)
    return jnp.where(tabs["same_head"], _dot_tn(kk, v2), 0.0)


def _chunk_mix(q2, k2, v2, state, tabs):
    head_a = tabs["head_a"]
    s = _dot_nt(q2, _split_heads(k2, head_a, 0))
    p = (s * tabs["decay"]).astype(BF16)
    qf = q2.astype(F32)
    lhs = jnp.concatenate(
        [p, (qf * tabs["qdf"]).astype(BF16), (qf * tabs["qdb"]).astype(BF16)], axis=1)
    rhs = jnp.concatenate([_split_heads(v2, head_a, 0), state], axis=0)
    return _dot(lhs, rhs)


def _chunk_norm_gate(o, g2, tabs):
    head_a = tabs["head_a"]
    inv = 1.0 / HEAD_DIM
    sum_a = jnp.sum(jnp.where(head_a, o, 0.0), axis=-1, keepdims=True)
    sum_b = jnp.sum(jnp.where(head_a, 0.0, o), axis=-1, keepdims=True)
    d = o - jnp.where(head_a, sum_a, sum_b) * inv
    d2 = d * d
    var_a = jnp.sum(jnp.where(head_a, d2, 0.0), axis=-1, keepdims=True)
    var_b = jnp.sum(jnp.where(head_a, 0.0, d2), axis=-1, keepdims=True)
    o = d * lax.rsqrt(jnp.where(head_a, var_a, var_b) * inv + EPS)
    return o * _silu(g2.astype(F32))


def _ctx_mixer_units(rq, rk, rv, rg, nq, nk, nv, df, db, df2, db2, mix_ref, st_ref):
    n_seq = st_ref.shape[0]
    seq = rq.shape[0] // n_seq
    nc = seq // CHUNK

    thunks = []
    for hp in range(HEAD_PAIRS):
        cols = slice(hp * LANES, (hp + 1) * LANES)
        cols2 = slice(hp * 2 * LANES, (hp + 1) * 2 * LANES)
        shared = {}

        def scan(hp=hp, cols=cols, cols2=cols2, shared=shared):
            tabs = _retention_tables(df[:, cols], db[:, cols], df2[:, cols2], db2[:, cols2])
            shared["tabs"] = tabs
            for b in range(n_seq):
                rows = [slice(b * seq + c * CHUNK, b * seq + (c + 1) * CHUNK) for c in range(nc)]
                kv = [_chunk_kv(rk[r, cols], rv[r, cols], tabs) for r in rows]
                sf = [jnp.zeros((LANES, LANES), F32)]
                for c in range(nc):
                    sf.append(sf[-1] * tabs["cdf"] + kv[c][:LANES])
                sb = [jnp.zeros((LANES, LANES), F32)]
                for c in reversed(range(nc)):
                    sb.append(sb[-1] * tabs["cdb"] + kv[c][LANES:])
                st_ref[b, hp, 0] = sf[nc]
                st_ref[b, hp, 1] = sb[nc]
                shared[b] = [jnp.concatenate([sf[c], sb[nc - 1 - c]], axis=0).astype(BF16)
                             for c in range(nc)]

        def chunk_out(b, c, cols=cols, shared=shared):
            rows = slice(b * seq + c * CHUNK, b * seq + (c + 1) * CHUNK)
            tabs = shared["tabs"]
            o = _chunk_mix(rq[rows, cols], rk[rows, cols], rv[rows, cols], shared[b][c], tabs)
            mix_ref[rows, cols] = _chunk_norm_gate(o, rg[rows, cols], tabs).astype(BF16)

        def attention(b, hp=hp, cols=cols, shared=shared):
            head_a = shared["tabs"]["head_a"]
            tok = slice(b * seq, (b + 1) * seq)
            s = _dot_nt(nq[tok, cols], _split_heads(nk[tok, cols], head_a, 0))
            es, rinv = [], []
            for h in range(2):
                sh = s[:, h * seq:(h + 1) * seq]
                e = jnp.exp(sh - jnp.max(sh, axis=-1, keepdims=True))
                rinv.append(1.0 / jnp.sum(e, axis=-1, keepdims=True))
                es.append(e.astype(BF16))
            o = _dot(jnp.concatenate(es, axis=1), _split_heads(nv[tok, cols], head_a, 0))
            o = o * jnp.where(head_a, rinv[0], rinv[1])
            mix_ref[tok, WIDTH + hp * LANES:WIDTH + (hp + 1) * LANES] = o.astype(BF16)

        thunks.append(scan)
        for b in range(n_seq):
            thunks += [functools.partial(chunk_out, b, c) for c in range(nc)]
            thunks.append(functools.partial(attention, b))
    return thunks


def _inproj_mix_kernel(*refs):
    n_in = 7
    groups = _inproj_groups(*refs[:n_in], refs[-3], (), n_alias=0, layer=0, width=2 * LANES)
    units = _ctx_mixer_units(*refs[n_in:-3], refs[-2], refs[-1])
    done = 0
    for u, unit in enumerate(units):
        upto = (u + 1) * len(groups) // len(units)
        started = [(finish, start()) for start, finish in groups[done:upto]]
        done = upto
        unit()
        for finish, result in started:
            finish(result)


def _inproj_latent_and_ctx_mixer(x, mods, g_pre, w, layer, seq_len, tm, rope_tabs,
                                 p_ctx, dec, batch, seq):
    n = x.shape[0]
    n_tiles = n // tm
    n_seq = batch // n_tiles
    assert n_seq * n_tiles == batch and seq_len % tm == 0
    tiles_per_seq = seq_len // tm
    group = lambda g: pl.BlockSpec((n_seq * seq, WIDTH), lambda i, g=g: (i, g))
    vec = lambda width: pl.BlockSpec((1, width), lambda i: (0, 0))
    in_specs = [
        pl.BlockSpec((tm, D_MODEL), lambda i: (i, 0)),
        pl.BlockSpec((None, 1, 6, D_MODEL), lambda i: (layer, 1 + i // tiles_per_seq, 0, 0)),
        pl.BlockSpec((None, 1, D_MODEL), lambda i: (layer, 0, 0)),
        pl.BlockSpec((None, D_MODEL, IN_WIDTH), lambda i: (layer, 0, 0),
                     pipeline_mode=pl.Buffered(1)),
    ] + [pl.BlockSpec((tm, LANES), lambda i: (i % tiles_per_seq, 0))] * 3 + [
        group(g) for g in range(N_GROUPS)] + [vec(WIDTH), vec(WIDTH), vec(2 * WIDTH),
                                              vec(2 * WIDTH)]
    return pl.pallas_call(
        _inproj_mix_kernel,
        grid=(n_tiles,),
        in_specs=in_specs,
        out_specs=[
            pl.BlockSpec((tm, IN_WIDTH), lambda i: (i, 0)),
            pl.BlockSpec((n_seq * seq, 2 * WIDTH), lambda i: (i, 0)),
            pl.BlockSpec((n_seq, HEAD_PAIRS, 2, LANES, LANES), lambda i: (i, 0, 0, 0, 0)),
        ],
        out_shape=[
            jax.ShapeDtypeStruct((n, IN_WIDTH), BF16),
            jax.ShapeDtypeStruct((batch * seq, 2 * WIDTH), BF16),
            jax.ShapeDtypeStruct((batch, HEAD_PAIRS, 2, LANES, LANES), F32),
        ],
        compiler_params=_cparams(1),
        name="inproj_latent_ctx_mixer",
    )(x, mods, g_pre, w, *rope_tabs, *([p_ctx] * N_GROUPS), *dec)


def _lat_retention_kernel(q_ref, k_ref, v_ref, g_ref, df, db, df2, db2, s0_ref, o_ref,
                          kv_ref, st_ref):
    nc = q_ref.shape[0] // CHUNK
    tabs = _retention_tables(df[...], db[...], df2[...], db2[...])

    def rows(c):
        return pl.ds(pl.multiple_of(c * CHUNK, CHUNK), CHUNK)

    def kv_body(c, carry):
        kv_ref[c] = _chunk_kv(k_ref[rows(c), :], v_ref[rows(c), :], tabs)
        return carry
    lax.fori_loop(0, nc, kv_body, 0, unroll=RET_UNROLL)

    def fwd_body(c, s):
        st_ref[c, :LANES, :] = s.astype(BF16)
        return s * tabs["cdf"] + kv_ref[c, :LANES, :]
    lax.fori_loop(0, nc, fwd_body, s0_ref[0, 0, 0])

    def bwd_body(i, s):
        c = nc - 1 - i
        st_ref[c, LANES:, :] = s.astype(BF16)
        return s * tabs["cdb"] + kv_ref[c, LANES:, :]
    lax.fori_loop(0, nc, bwd_body, s0_ref[0, 0, 1])

    def out_body(c, carry):
        r = rows(c)
        o = _chunk_mix(q_ref[r, :], k_ref[r, :], v_ref[r, :], st_ref[c], tabs)
        o_ref[r, :] = _chunk_norm_gate(o, g_ref[r, :], tabs).astype(BF16)
        return carry
    lax.fori_loop(0, nc, out_body, 0, unroll=RET_UNROLL)


def _lat_retention(p, dec, s0, batch, seq):
    nc = seq // CHUNK
    group = lambda g: pl.BlockSpec((seq, LANES), lambda b, h, g=g: (b, g * HEAD_PAIRS + h))
    vec = lambda w: pl.BlockSpec((1, w), lambda b, h: (0, h))
    return pl.pallas_call(
        _lat_retention_kernel,
        grid=(batch, HEAD_PAIRS),
        in_specs=[group(g) for g in range(4)] + [vec(LANES), vec(LANES), vec(2 * LANES),
                                                 vec(2 * LANES)]
        + [pl.BlockSpec((1, 1, 2, LANES, LANES), lambda b, h: (b, h, 0, 0, 0))],
        out_specs=pl.BlockSpec((seq, LANES), lambda b, h: (b, h)),
        out_shape=jax.ShapeDtypeStruct((batch * seq, WIDTH), BF16),
        scratch_shapes=[pltpu.VMEM((nc, 2 * LANES, LANES), F32),
                        pltpu.VMEM((nc, 2 * LANES, LANES), BF16)],
        compiler_params=_cparams(2),
        name="latent_retention",
    )(*([p] * 4), *dec, s0)


def _block_diag_states(s_f, s_b):
    def bd(s):
        b = s.shape[0]
        s = s.reshape(b, HEAD_PAIRS, 2, HEAD_DIM, HEAD_DIM)
        z = jnp.zeros_like(s[:, :, 0])
        top = jnp.concatenate([s[:, :, 0], z], axis=-1)
        bot = jnp.concatenate([z, s[:, :, 1]], axis=-1)
        return jnp.concatenate([top, bot], axis=-2)
    return jnp.stack([bd(s_f), bd(s_b)], axis=2)


def _diag_states(st):
    b = st.shape[0]
    a = st[:, :, :HEAD_DIM, :HEAD_DIM]
    c = st[:, :, HEAD_DIM:, HEAD_DIM:]
    return jnp.stack([a, c], axis=2).reshape(b, N_HEADS, HEAD_DIM, HEAD_DIM)


def _na_kernel(q_ref, k_ref, v_ref, ck_ref, cv_ref, bias_ref, o_ref, *, lo, hi):
    rows_total = k_ref.shape[0] // GRID_W
    lane = lax.broadcasted_iota(jnp.int32, (1, LANES), 1)
    head_a = lane < HEAD_DIM
    wins, cls = [], []
    for i in range(NA_ROWS):
        r = pl.program_id(1) * NA_ROWS + i
        rs = jnp.clip(r - NA_KH // 2, 0, rows_total - NA_KH)
        wins.append(pl.ds(pl.multiple_of(rs * GRID_W, GRID_W), NA_KH * GRID_W))
        cls.append(jnp.where(r < lo, r, jnp.where(r > hi, r - hi + lo, lo)))
    stages = [(hp, r0) for hp in range(HEAD_PAIRS) for r0 in range(0, NA_ROWS, NA_STAGE_ROWS)]

    def scores(stage):
        hp, r0 = stage
        cols = slice(hp * LANES, (hp + 1) * LANES)
        qq = jnp.concatenate(
            [_split_heads(q_ref[i * GRID_W:(i + 1) * GRID_W, cols], head_a, 0)
             for i in range(r0, r0 + NA_STAGE_ROWS)], axis=0)
        s_ctx = _dot_nt(qq, ck_ref[0, :, cols])
        s_loc = [_dot_nt(qq[j * 2 * GRID_W:(j + 1) * 2 * GRID_W], k_ref[wins[r0 + j], cols])
                 for j in range(NA_STAGE_ROWS)]
        return s_ctx, s_loc

    def softmax(stage, s_ctx, s_loc):
        hp, r0 = stage
        m_ctx = jnp.max(s_ctx, axis=-1, keepdims=True)
        m = []
        for j in range(NA_STAGE_ROWS):
            blk = slice(j * 2 * GRID_W, (j + 1) * 2 * GRID_W)
            s_loc[j] = s_loc[j] + bias_ref[cls[r0 + j], hp]
            m.append(jnp.maximum(jnp.max(s_loc[j], axis=-1, keepdims=True), m_ctx[blk]))
        e_ctx = jnp.exp(s_ctx - jnp.concatenate(m, axis=0))
        den_ctx = jnp.sum(e_ctx, axis=-1, keepdims=True)
        e_loc, den = [], []
        for j in range(NA_STAGE_ROWS):
            blk = slice(j * 2 * GRID_W, (j + 1) * 2 * GRID_W)
            e = jnp.exp(s_loc[j] - m[j])
            den.append(jnp.sum(e, axis=-1, keepdims=True) + den_ctx[blk])
            e_loc.append(e.astype(BF16))
        return e_ctx.astype(BF16), e_loc, den

    def weighted_values(stage, e_ctx, e_loc, den):
        hp, r0 = stage
        cols = slice(hp * LANES, (hp + 1) * LANES)
        o_ctx = _dot(e_ctx, cv_ref[0, :, cols])
        for j in range(NA_STAGE_ROWS):
            i = r0 + j
            blk = slice(j * 2 * GRID_W, (j + 1) * 2 * GRID_W)
            o = (_dot(e_loc[j], v_ref[wins[i], cols]) + o_ctx[blk]) / den[j]
            o_ref[i * GRID_W:(i + 1) * GRID_W, cols] = jnp.where(
                head_a, o[:GRID_W], o[GRID_W:]).astype(BF16)

    nxt = scores(stages[0])
    pending = None
    for n, stage in enumerate(stages):
        s_ctx, s_loc = nxt
        if n + 1 < len(stages):
            nxt = scores(stages[n + 1])
        probs = softmax(stage, s_ctx, s_loc)
        if pending is not None:
            weighted_values(*pending)
        pending = (stage,) + probs
    weighted_values(*pending)


def _mod_bias_kernel(cond_ref, w_ref, b_ref, rpb_ref, mod_ref, bias_ref, *, dr_first, n_dr):
    s = _silu(cond_ref[...]).astype(BF16)
    mod_ref[...] = _dot(s, w_ref[...].astype(BF16)) + b_ref[...]

    qc = lax.broadcasted_iota(jnp.int32, (GRID_W, LANES), 0)
    lane = lax.broadcasted_iota(jnp.int32, (GRID_W, LANES), 1)
    kc = lane & (GRID_W - 1)
    cs = jnp.clip(qc - NA_KW // 2, 0, GRID_W - NA_KW)
    inside = (kc >= cs) & (kc < cs + NA_KW)
    first = lane < GRID_W
    lo_half, hi_half = [], []
    for dr in range(n_dr):
        line = jnp.broadcast_to(rpb_ref[pl.ds(dr, 1), :], (GRID_W, LANES))
        lo_half.append(pltpu.roll(line, LANES - (NA_KW - 1), 1, stride=1, stride_axis=0))
        hi_half.append(pltpu.roll(line, GRID_W - (NA_KW - 1), 1, stride=1, stride_axis=0))
    for cls, dr0 in enumerate(dr_first):
        for jp in range(NA_KH // 2):
            dr = dr0 + 2 * jp
            tile = jnp.where(inside, jnp.where(first, lo_half[dr], hi_half[dr + 1]), NEG_INF)
            bias_ref[cls, :, jp * LANES:(jp + 1) * LANES] = tile


def _modulation_and_na_bias(cond, ada_w, ada_b, na_rpb, rows_total):
    depth, heads, n_dr, n_dc = na_rpb.shape
    kh = NA_KH
    lo, hi = kh // 2, rows_total - kh // 2 - 1
    reps = list(range(lo)) + [lo] + list(range(hi + 1, rows_total))
    dr_first = tuple(int(np.clip(r - kh // 2, 0, rows_total - kh)) - r + kh - 1 for r in reps)
    n_dr_pad = -(-n_dr // HALO) * HALO
    lines = jnp.pad(na_rpb, ((0, 0), (0, 0), (0, n_dr_pad - n_dr), (0, LANES - n_dc)))
    mod_cols = 6 * D_MODEL // heads
    assert mod_cols % LANES == 0
    mod, bias = pl.pallas_call(
        functools.partial(_mod_bias_kernel, dr_first=dr_first, n_dr=n_dr),
        grid=(depth, heads),
        in_specs=[
            pl.BlockSpec((8, D_MODEL), lambda l, h: (0, 0)),
            pl.BlockSpec((None, D_MODEL, mod_cols), lambda l, h: (l, 0, h)),
            pl.BlockSpec((None, 1, mod_cols), lambda l, h: (l, 0, h)),
            pl.BlockSpec((None, None, n_dr_pad, LANES), lambda l, h: (l, h, 0, 0)),
        ],
        out_specs=[
            pl.BlockSpec((None, 8, mod_cols), lambda l, h: (l, 0, h)),
            pl.BlockSpec((None, len(reps), None, GRID_W, kh * GRID_W),
                         lambda l, h: (l, 0, h // 2, h % 2, 0)),
        ],
        out_shape=[
            jax.ShapeDtypeStruct((depth, 8, 6 * D_MODEL), F32),
            jax.ShapeDtypeStruct((depth, len(reps), HEAD_PAIRS, 2 * GRID_W, kh * GRID_W), F32),
        ],
        compiler_params=_cparams(2),
        name="modulation_and_na_bias",
    )(cond, ada_w, ada_b.reshape(depth, 1, 6 * D_MODEL), lines)
    return mod, bias, lo, hi


def _neighbourhood_attention(p, ctx_k, ctx_v, bias, layer, lo, hi, batch, seq):
    rows_total = seq // GRID_W
    past = ctx_k.shape[1]

    groups = rows_total // NA_ROWS
    return pl.pallas_call(
        functools.partial(_na_kernel, lo=lo, hi=hi),
        grid=(batch, groups),
        in_specs=[
            pl.BlockSpec((NA_ROWS * GRID_W, WIDTH), lambda b, g: (b * groups + g, 4)),
            pl.BlockSpec((seq, WIDTH), lambda b, g: (b, 5), pipeline_mode=pl.Buffered(1)),
            pl.BlockSpec((seq, WIDTH), lambda b, g: (b, 6), pipeline_mode=pl.Buffered(1)),
            pl.BlockSpec((1, past, WIDTH), lambda b, g: (b, 0, 0), pipeline_mode=pl.Buffered(1)),
            pl.BlockSpec((1, past, WIDTH), lambda b, g: (b, 0, 0), pipeline_mode=pl.Buffered(1)),
            pl.BlockSpec((None,) + bias.shape[1:], lambda b, g: (layer, 0, 0, 0, 0),
                         pipeline_mode=pl.Buffered(1)),
        ],
        out_specs=pl.BlockSpec((NA_ROWS * GRID_W, WIDTH), lambda b, g: (b * groups + g, 0)),
        out_shape=jax.ShapeDtypeStruct((batch * seq, WIDTH), BF16),
        compiler_params=_cparams(2),
        name="neighbourhood_attention",
    )(p, p, p, ctx_k, ctx_v, bias)


def _post_kernel(xm, xp, xn, am, ap, an, bm, bp, bn, mod_head, mod_tail, g_ref, wo, wu, cw, cb, wd,
                 o_ref, acc_ref, h_ref, hp_ref, x1_ref, y_ref, act_ref, *, seq_len, n_tiles):
    step = pl.program_id(0)
    tm = xm.shape[0]
    nj = tm // HALO
    tiles_per_seq = max(seq_len // tm, 1)
    assert (tm % seq_len == 0 and seq_len % nj == 0) or seq_len % tm == 0
    h_rows = h_ref.shape[1]
    seg_pitch = nj + HALO
    assert h_rows == 2 * HALO + HALO * seg_pitch and nj % HEAD_ROWS == 0
    n_ct = D_MODEL // LANES
    g = g_ref[...]

    def ext(main, prev, nxt):
        lo = prev[...].astype(F32)[prev.shape[0] - HALO:]
        hi = nxt[...].astype(F32)[:HALO]
        return jnp.concatenate([lo, main[...].astype(F32), hi], axis=0)

    chunks = [(c0, min(FF_CHUNK, D_FF - c0)) for c0 in range(0, D_FF, FF_CHUNK)]
    par = step % 2
    mod_h = mod_head[0]
    n_blocks = tm // HEAD_ROWS

    def head_matmul():
        mixed = jnp.concatenate([ext(am, ap, an), ext(bm, bp, bn)], axis=1).astype(BF16)
        y_ref[...] = _dot(mixed, wo[...])

    def head_rows(x, y):
        x1 = x + mod_h[2:3] * _rms(y, g[0:1])
        return x1, _rms(x1, g[1:2]) * (1.0 + mod_h[4:5]) + mod_h[3:4]

    def head_block(blk):
        if blk < n_blocks:
            rows = slice(blk * HEAD_ROWS, (blk + 1) * HEAD_ROWS)
            erows = slice(HALO + blk * HEAD_ROWS, HALO + (blk + 1) * HEAD_ROWS)
            x1, h = head_rows(xm[rows, :], y_ref[erows, :])
            x1_ref[par, rows, :] = x1
            t0 = blk * HEAD_ROWS
            hrow = HALO + (t0 // nj) * seg_pitch + t0 % nj
            for c in range(n_ct):
                h_ref[c, hrow:hrow + HEAD_ROWS, :] = h[:, c * LANES:(c + 1) * LANES]
        else:
            x = jnp.concatenate([xp[...], xn[...]], axis=0)
            y = jnp.concatenate([y_ref[0:HALO, :], y_ref[HALO + tm:2 * HALO + tm, :]], axis=0)
            _, h = head_rows(x, y)
            t = jnp.minimum(step, n_tiles - 1) % tiles_per_seq
            ridx = lax.broadcasted_iota(jnp.int32, h.shape, 0)
            keep = ((ridx >= HALO) | (t != 0)) & ((ridx < HALO) | (t != tiles_per_seq - 1))
            h = jnp.where(keep, h, 0.0)
            for c in range(n_ct):
                h_ref[c, 0:HALO, :] = h[:HALO, c * LANES:(c + 1) * LANES]
                h_ref[c, h_rows - HALO:h_rows, :] = h[HALO:, c * LANES:(c + 1) * LANES]

    def head_permute():
        def group(rows):
            return jnp.concatenate([h_ref[c, rows, :] for c in range(n_ct)], axis=1)
        def put(row0, first, second):
            hp_ref[row0:row0 + 2 * HALO, :] = jnp.concatenate(
                [group(first), group(second)], axis=0).astype(BF16)
        for j in range(0, nj, 2):
            put(j * HALO, pl.ds(HALO + j, HALO, stride=seg_pitch),
                pl.ds(HALO + j + 1, HALO, stride=seg_pitch))
        put(tm, pl.ds(0, HALO), pl.ds(h_rows - HALO, HALO))

    def tail(interleaved):
        mod = mod_tail[0]

        def conv(u, cols):
            w = cw[:, cols]
            sub = lax.broadcasted_iota(jnp.int32, (HALO, u.shape[1]), 0)
            before = jnp.where(sub == 0, u[tm + HALO - 1:tm + HALO],
                               pltpu.roll(u[tm - HALO:tm], 1, 0))
            after = jnp.where(sub == HALO - 1, u[tm + HALO:tm + HALO + 1],
                              pltpu.roll(u[0:HALO], HALO - 1, 0))
            for s in range(1, HALO):
                if (s * nj) % seq_len == 0:
                    before = jnp.where(sub == s, 0.0, before)
                    after = jnp.where(sub == s - 1, 0.0, after)
            prev = jnp.concatenate([before, u[0:tm - HALO]], axis=0)
            nxt = jnp.concatenate([u[HALO:tm], after], axis=0)
            return prev * w[0:1] + u[0:tm] * w[1:2] + nxt * w[2:3] + cb[:, cols]

        def cols(ch):
            ca = slice(chunks[ch][0], chunks[ch][0] + chunks[ch][1])
            return ca, slice(D_FF + ca.start, D_FF + ca.stop)

        def up(ch):
            ca, cg = cols(ch)
            return _dot(hp_ref[...], wu[:, cg]), _dot(hp_ref[...], wu[:, ca])

        nxt_u = up(0)
        for ch in range(len(chunks)):
            ug, ua = nxt_u
            if ch + 1 < len(chunks):
                nxt_u = up(ch + 1)
            for blk in interleaved[ch]:
                head_block(blk)
            ca, cg = cols(ch)
            act_ref[:, ca] = (_silu(conv(ug, cg)) * conv(ua, ca)).astype(BF16)
        ffn = mod[5:6] * _rms(_dot(act_ref[...], wd[...]), g[2:3])
        for c in range(n_ct):
            acc_ref[c] = ffn[:, c * LANES:(c + 1) * LANES]

        for s in range(HALO):
            for jb in range(nj // HALO):
                t0 = s * nj + HALO * jb
                rows = pl.ds(HALO * HALO * jb + s, HALO, stride=HALO)
                back = jnp.concatenate([acc_ref[c, rows, :] for c in range(n_ct)], axis=1)
                o_ref[t0:t0 + HALO, :] = x1_ref[1 - par, t0:t0 + HALO, :] + back

    @pl.when(step == 0)
    def _():
        head_matmul()
        for blk in range(n_blocks + 1):
            head_block(blk)
        head_permute()

    @pl.when(step > 0)
    def _():
        head_matmul()
        tail([[blk for blk in range(n_blocks + 1) if blk * len(chunks) // (n_blocks + 1) == ch]
              for ch in range(len(chunks))])
        head_permute()


def _post(x, mix_a, mix_b, mods, gains, wo, wu, cw, cb, wd, layer, seq_len, tm, latent):
    n = x.shape[0]
    nt = n // tm
    (mix_a, col_a), (mix_b, col_b) = mix_a, mix_b
    tiles_per_seq = max(seq_len // tm, 1)
    cond_row = (lambda i: 1 + i // tiles_per_seq) if latent else (lambda i: 0)

    head_tile = lambda i: jnp.minimum(i, nt - 1)
    tail_tile = lambda i: jnp.maximum(i - 1, 0)

    def triple(width, halo_rows, col=0):
        per = tm // halo_rows
        last = n // halo_rows - 1
        return [
            pl.BlockSpec((tm, width), lambda i: (head_tile(i), col)),
            pl.BlockSpec((halo_rows, width),
                         lambda i: (jnp.maximum(head_tile(i) * per - 1, 0), col)),
            pl.BlockSpec((halo_rows, width),
                         lambda i: (jnp.minimum((head_tile(i) + 1) * per, last), col)),
        ]

    const = lambda *shape: pl.BlockSpec((None,) + shape, lambda i: (layer,) + (0,) * len(shape),
                                        pipeline_mode=pl.Buffered(1))
    in_specs = (triple(D_MODEL, HALO) + triple(WIDTH, 2 * HALO, col_a)
                + triple(WIDTH, 2 * HALO, col_b) + [
        pl.BlockSpec((None, 1, 6, D_MODEL), lambda i: (layer, cond_row(head_tile(i)), 0, 0)),
        pl.BlockSpec((None, 1, 6, D_MODEL), lambda i: (layer, cond_row(tail_tile(i)), 0, 0)),
        const(3, D_MODEL),
        const(2 * WIDTH, D_MODEL),
        const(D_MODEL, 2 * D_FF),
        const(3, 2 * D_FF),
        const(1, 2 * D_FF),
        const(D_FF, D_MODEL),
    ])
    return pl.pallas_call(
        functools.partial(_post_kernel, seq_len=seq_len, n_tiles=nt),
        grid=(nt + 1,),
        in_specs=in_specs,
        out_specs=pl.BlockSpec((tm, D_MODEL), lambda i: (tail_tile(i), 0)),
        out_shape=jax.ShapeDtypeStruct((n, D_MODEL), F32),
        scratch_shapes=[pltpu.VMEM((D_MODEL // LANES, tm, LANES), F32),
                        pltpu.VMEM((D_MODEL // LANES, 2 * HALO + HALO * (tm // HALO + HALO), LANES),
                                   F32),
                        pltpu.VMEM((tm + 2 * HALO, D_MODEL), BF16),
                        pltpu.VMEM((2, tm, D_MODEL), F32),
                        pltpu.VMEM((tm + 2 * HALO, D_MODEL), F32),
                        pltpu.VMEM((tm, D_FF), BF16)],
        compiler_params=_cparams(1),
        name="post_latent" if latent else "post_context",
    )(x, x, x, mix_a, mix_a, mix_a, mix_b, mix_b, mix_b, mods, mods, gains, wo, wu, cw, cb, wd)


def kernel(x_prompt, x_sample, c, cache_na_k, cache_na_v, state_ret_fwd, state_ret_bwd, c_ctx,
           ada_w, ada_b, g_pre_mix, g_post_mix, g_pre_ffn, g_post_ffn, w_in,
           ret_decay_fwd, ret_decay_bwd, na_rpb, w_out, w_up, conv_w, conv_b, w_down):
    depth = w_in.shape[0]
    batch, seq, _ = x_prompt.shape
    dec_batch, dec_seq, _ = x_sample.shape
    past = cache_na_k.shape[2]
    tm = 512

    cond = jnp.concatenate(
        [c_ctx[None, :], c, jnp.zeros((8 - 1 - dec_batch, D_MODEL), F32)], axis=0)
    mods, bias, lo, hi = _modulation_and_na_bias(cond, ada_w, ada_b, na_rpb, dec_seq // GRID_W)
    mods = mods.reshape(depth, 8, 6, D_MODEL)
    rope_tabs = _rope_tables(dec_seq)

    w_in_b, wo_b, wu_b, wd_b = (w.astype(BF16) for w in (w_in, w_out, w_up, w_down))
    g_pre = g_pre_mix.reshape(depth, 1, D_MODEL)
    gains = jnp.stack([g_post_mix, g_pre_ffn, g_post_ffn], axis=1)
    cb = conv_b.reshape(depth, 1, 2 * D_FF)
    post_params = (mods, gains, wo_b, wu_b, conv_w, cb, wd_b)

    y_p = x_prompt.reshape(batch * seq, D_MODEL)
    y_s = x_sample.reshape(dec_batch * dec_seq, D_MODEL)
    new_k = new_v = None
    sfs, sbs = [], []
    for l in range(depth):
        dec = (jnp.repeat(ret_decay_fwd[l], HEAD_DIM)[None, :],
               jnp.repeat(ret_decay_bwd[l], HEAD_DIM)[None, :],
               jnp.repeat(ret_decay_fwd[l], LANES)[None, :],
               jnp.repeat(ret_decay_bwd[l], LANES)[None, :])

        p_c, new_k, new_v = _inproj_context(y_p, mods, g_pre, w_in_b, l, seq, tm, (new_k, new_v))
        p_s, mix_c, st_c = _inproj_latent_and_ctx_mixer(
            y_s, mods, g_pre, w_in_b, l, dec_seq, tm, rope_tabs, p_c, dec, batch, seq)
        y_p = _post(y_p, (mix_c, 0), (mix_c, 1), *post_params, l, seq, tm, latent=False)
        sfs.append(_diag_states(st_c[:, :, 0]))
        sbs.append(_diag_states(st_c[:, :, 1]))

        s0 = _block_diag_states(state_ret_fwd[:, l], state_ret_bwd[:, l])
        ret_s = _lat_retention(p_s, dec, s0, dec_batch, dec_seq)
        ck = cache_na_k[:, l].reshape(dec_batch, past, WIDTH).astype(BF16)
        cv = cache_na_v[:, l].reshape(dec_batch, past, WIDTH).astype(BF16)
        na_s = _neighbourhood_attention(p_s, ck, cv, bias, l, lo, hi, dec_batch, dec_seq)
        y_s = _post(y_s, (ret_s, 0), (na_s, 0), *post_params, l, dec_seq, tm, latent=True)

    return (y_p.reshape(batch, seq, D_MODEL),
            y_s.reshape(dec_batch, dec_seq, D_MODEL),
            new_k.reshape(batch, depth, seq, N_HEADS, HEAD_DIM),
            new_v.reshape(batch, depth, seq, N_HEADS, HEAD_DIM),
            jnp.stack(sfs, axis=1), jnp.stack(sbs, axis=1))
```

```python
import functools

import numpy as np
import jax
import jax.numpy as jnp
from jax import lax
from jax.experimental import pallas as pl
from jax.experimental.pallas import tpu as pltpu

F32 = jnp.float32
BF16 = jnp.bfloat16

D_MODEL = 1024
HEAD_DIM = 64
N_HEADS = 8
HEAD_PAIRS = N_HEADS // 2
LANES = 128
WIDTH = N_HEADS * HEAD_DIM
N_GROUPS = 7
IN_WIDTH = N_GROUPS * WIDTH
D_FF = 2816
FF_CHUNK = 768
HEAD_ROWS = 32
CHUNK = 128
RET_UNROLL = 8
GRID_W = 64
NA_KH = 8
NA_KW = 16
NA_STAGE_ROWS = 8
NA_ROWS = 8
ROPE_BASE = 10000.0
EPS = 1e-6
NEG_INF = -1e9
HALO = 8
VMEM_LIMIT = 56 * 1024 * 1024


def _cparams(n_grid):
    return pltpu.CompilerParams(
        dimension_semantics=("arbitrary",) * n_grid, vmem_limit_bytes=VMEM_LIMIT)


def _rms(x, g):
    ms = jnp.mean(x * x, axis=-1, keepdims=True)
    return x * lax.rsqrt(ms + EPS) * g


def _silu(x):
    return x * jax.nn.sigmoid(x)


def _log_sigmoid(x):
    return jnp.minimum(x, 0.0) - jnp.log1p(jnp.exp(-jnp.abs(x)))


def _dot(a, b):
    return jnp.dot(a, b, preferred_element_type=F32)


def _dot_nt(a, b):
    return lax.dot_general(a, b, (((1,), (1,)), ((), ())), preferred_element_type=F32)


def _dot_tn(a, b):
    return lax.dot_general(a, b, (((0,), (0,)), ((), ())), preferred_element_type=F32)


def _cast_streams(weights, n_steps):
    arrays = [w.reshape(-1, w.shape[-1]) for w in weights]
    in_specs, out_specs, out_shapes = [], [], []
    for a in arrays:
        rows = a.shape[0] // n_steps
        assert rows * n_steps == a.shape[0] and rows % (2 * HALO) == 0
        in_specs.append(pl.BlockSpec((rows, a.shape[1]), lambda i: (i, 0)))
        out_specs.append(pl.BlockSpec((rows, a.shape[1]), lambda i: (i, 0)))
        out_shapes.append(jax.ShapeDtypeStruct(a.shape, BF16))
    return arrays, in_specs, out_specs, out_shapes


def _cast_blocks(in_refs, out_refs):
    for src, dst in zip(in_refs, out_refs):
        dst[...] = src[...].astype(dst.dtype)


def _inproj_kernel(*refs, n_alias, n_cast, layer):
    n_in = 4 + n_alias + n_cast
    outs = refs[n_in:]
    _cast_blocks(refs[4 + n_alias:n_in], outs[3:])
    for start, finish in _inproj_groups(*refs[:4], None, None, None, outs[0], outs[1:3], n_alias,
                                        layer):
        finish(start())


def _inproj_groups(x_ref, mod_ref, g_ref, w_ref, cos_ref, sin_up_ref, sin_dn_ref, p_ref, kv_refs,
                   n_alias, layer, width=WIDTH):
    rope = cos_ref is not None
    emit_kv = len(kv_refs) > 0
    assert WIDTH % width == 0 and (width == WIDTH or not emit_kv)
    mod = mod_ref[0]
    h = (_rms(x_ref[...], g_ref[...]) * (1.0 + mod[1:2]) + mod[0:1]).astype(BF16)

    def start(n):
        return _dot(h, w_ref[:, n * width:(n + 1) * width])

    def finish(n, pg):
        cols = slice(n * width, (n + 1) * width)
        g = n * width // WIDTH
        if rope and g < 2:
            parts = []
            for j in range(width // LANES):
                xj = pg[:, j * LANES:(j + 1) * LANES]
                parts.append(xj * cos_ref[...]
                             + pltpu.roll(xj, 16, 1) * sin_up_ref[...]
                             + pltpu.roll(xj, LANES - 16, 1) * sin_dn_ref[...])
            pg = jnp.concatenate(parts, axis=1)
        if g in (0, 4):
            pg = pg * (HEAD_DIM ** -0.5)
        p_ref[:, cols] = pg.astype(BF16)
        if emit_kv and g >= 5:
            kv_ref = kv_refs[g - 5]
            seq = kv_ref.shape[-2]
            for j in range(kv_ref.shape[0]):
                if n_alias:
                    kv_ref[j] = pg[j * seq:(j + 1) * seq]
                else:
                    for l in range(kv_ref.shape[1]):
                        kv_ref[j, l] = (pg[j * seq:(j + 1) * seq] if l == layer
                                        else jnp.zeros((seq, WIDTH), F32))

    return [(functools.partial(start, n), functools.partial(finish, n))
            for n in range(IN_WIDTH // width)]


def _inproj_context(x, mods, g_pre, w, layer, seq_len, tm, kv_out, cast=()):
    n = x.shape[0]
    depth = w.shape[0]
    cast_arrays, cast_in, cast_out, cast_shapes = _cast_streams(cast, n // tm)
    in_specs = [
        pl.BlockSpec((tm, D_MODEL), lambda i: (i, 0)),
        pl.BlockSpec((None, 1, 6, D_MODEL), lambda i: (layer, 0, 0, 0)),
        pl.BlockSpec((None, 1, D_MODEL), lambda i: (layer, 0, 0)),
        pl.BlockSpec((None, D_MODEL, IN_WIDTH), lambda i: (layer, 0, 0),
                     pipeline_mode=pl.Buffered(1)),
    ]
    args = [x, mods, g_pre, w]
    out_specs = [pl.BlockSpec((tm, IN_WIDTH), lambda i: (i, 0))]
    out_shape = [jax.ShapeDtypeStruct((n, IN_WIDTH), BF16)]
    aliases = {}
    seqs_per_tile = tm // seq_len
    kv_shape = jax.ShapeDtypeStruct((n // seq_len, depth, seq_len, WIDTH), F32)
    assert (kv_out[0] is None) == (kv_out[1] is None)
    for j, prev in enumerate(kv_out):
        if prev is not None:
            aliases[len(args)] = 1 + j
            in_specs.append(pl.BlockSpec(memory_space=pl.ANY))
            args.append(prev)
            out_specs.append(pl.BlockSpec((seqs_per_tile, None, seq_len, WIDTH),
                                          lambda i: (i, layer, 0, 0)))
        else:
            out_specs.append(pl.BlockSpec((seqs_per_tile, depth, seq_len, WIDTH),
                                          lambda i: (i, 0, 0, 0)))
        out_shape.append(kv_shape)
    results = pl.pallas_call(
        functools.partial(_inproj_kernel, n_alias=len(aliases), n_cast=len(cast), layer=layer),
        grid=(n // tm,),
        in_specs=in_specs + cast_in,
        out_specs=out_specs + cast_out,
        out_shape=out_shape + cast_shapes,
        input_output_aliases=aliases,
        compiler_params=_cparams(1),
        name="inproj_context",
    )(*args, *cast_arrays)
    return list(results[:3]) + [r.reshape(w.shape) for r, w in zip(results[3:], cast)]


def _rope_tables(seq_len):
    t = np.arange(seq_len)
    lane = np.arange(LANES)
    d = lane % HEAD_DIM
    pos = np.where(d[None, :] < HEAD_DIM // 2, (t // GRID_W)[:, None], (t % GRID_W)[:, None])
    pos = pos.astype(np.float32)
    half = HEAD_DIM // 2
    inv = np.power(np.float32(ROPE_BASE), -np.arange(0, half, 2, dtype=np.float32) / half)
    ang = pos * inv[d % (half // 2)][None, :]
    cos, sin = np.cos(ang), np.sin(ang)
    upper = (d % half) >= half // 2
    sin_up = np.where(upper[None, :], sin, 0.0)
    sin_dn = np.where(upper[None, :], 0.0, -sin)
    return (jnp.asarray(cos, F32), jnp.asarray(sin_up, F32), jnp.asarray(sin_dn, F32))


def _retention_tables(dec_f, dec_b, dec_f2, dec_b2):
    lgf, lgb = _log_sigmoid(dec_f), _log_sigmoid(dec_b)
    pos = lax.broadcasted_iota(jnp.int32, (CHUNK, LANES), 0).astype(F32)
    tabs = dict(
        qdf=jnp.exp(lgf * (pos + 1.0)), kdf=jnp.exp(lgf * (CHUNK - 1.0 - pos)),
        cdf=jnp.exp(lgf * float(CHUNK)),
        qdb=jnp.exp(lgb * (CHUNK - pos)), kdb=jnp.exp(lgb * pos),
        cdb=jnp.exp(lgb * float(CHUNK)),
    )
    lgf2, lgb2 = _log_sigmoid(dec_f2), _log_sigmoid(dec_b2)
    i = lax.broadcasted_iota(jnp.int32, (CHUNK, 2 * CHUNK), 0)
    j = lax.broadcasted_iota(jnp.int32, (CHUNK, 2 * CHUNK), 1) & (CHUNK - 1)
    diff = (i - j).astype(F32)
    tabs["decay"] = (jnp.where(diff >= 0, jnp.exp(lgf2 * jnp.maximum(diff, 0.0)), 0.0)
                     + jnp.where(diff <= 0, jnp.exp(lgb2 * jnp.maximum(-diff, 0.0)), 0.0))
    lane = lax.broadcasted_iota(jnp.int32, (1, LANES), 1)
    tabs["head_a"] = lane < HEAD_DIM
    r = lax.broadcasted_iota(jnp.int32, (2 * LANES, LANES), 0) & (LANES - 1)
    c = lax.broadcasted_iota(jnp.int32, (2 * LANES, LANES), 1)
    tabs["same_head"] = (r < HEAD_DIM) == (c < HEAD_DIM)
    return tabs


def _split_heads(x, head_a, axis):
    zero = jnp.zeros_like(x)
    return jnp.concatenate([jnp.where(head_a, x, zero), jnp.where(head_a, zero, x)], axis=axis)


def _chunk_kv(k2, v2, tabs):
    kf = k2.astype(F32)
    kk = jnp.concatenate([kf * tabs["kdf"], kf * tabs["kdb"]], axis=1).astype(BF16)
    return jnp.where(tabs["same_head"], _dot_tn(kk, v2), 0.0)


def _chunk_mix(q2, k2, v2, state, tabs):
    head_a = tabs["head_a"]
    s = _dot_nt(q2, _split_heads(k2, head_a, 0))
    p = (s * tabs["decay"]).astype(BF16)
    qf = q2.astype(F32)
    lhs = jnp.concatenate(
        [p, (qf * tabs["qdf"]).astype(BF16), (qf * tabs["qdb"]).astype(BF16)], axis=1)
    rhs = jnp.concatenate([_split_heads(v2, head_a, 0), state], axis=0)
    return _dot(lhs, rhs)


def _chunk_norm_gate(o, g2, tabs):
    head_a = tabs["head_a"]
    inv = 1.0 / HEAD_DIM
    sum_a = jnp.sum(jnp.where(head_a, o, 0.0), axis=-1, keepdims=True)
    sum_b = jnp.sum(jnp.where(head_a, 0.0, o), axis=-1, keepdims=True)
    d = o - jnp.where(head_a, sum_a, sum_b) * inv
    d2 = d * d
    var_a = jnp.sum(jnp.where(head_a, d2, 0.0), axis=-1, keepdims=True)
    var_b = jnp.sum(jnp.where(head_a, 0.0, d2), axis=-1, keepdims=True)
    o = d * lax.rsqrt(jnp.where(head_a, var_a, var_b) * inv + EPS)
    return o * _silu(g2.astype(F32))


def _ctx_mixer_units(rq, rk, rv, rg, nq, nk, nv, df, db, df2, db2, mix_ref, st_ref):
    n_seq = st_ref.shape[0]
    seq = rq.shape[0] // n_seq
    nc = seq // CHUNK

    thunks = []
    for hp in range(HEAD_PAIRS):
        cols = slice(hp * LANES, (hp + 1) * LANES)
        cols2 = slice(hp * 2 * LANES, (hp + 1) * 2 * LANES)
        shared = {}

        def scan(hp=hp, cols=cols, cols2=cols2, shared=shared):
            tabs = _retention_tables(df[:, cols], db[:, cols], df2[:, cols2], db2[:, cols2])
            shared["tabs"] = tabs
            for b in range(n_seq):
                rows = [slice(b * seq + c * CHUNK, b * seq + (c + 1) * CHUNK) for c in range(nc)]
                kv = [_chunk_kv(rk[r, cols], rv[r, cols], tabs) for r in rows]
                sf = [jnp.zeros((LANES, LANES), F32)]
                for c in range(nc):
                    sf.append(sf[-1] * tabs["cdf"] + kv[c][:LANES])
                sb = [jnp.zeros((LANES, LANES), F32)]
                for c in reversed(range(nc)):
                    sb.append(sb[-1] * tabs["cdb"] + kv[c][LANES:])
                st_ref[b, hp, 0] = sf[nc]
                st_ref[b, hp, 1] = sb[nc]
                shared[b] = [jnp.concatenate([sf[c], sb[nc - 1 - c]], axis=0).astype(BF16)
                             for c in range(nc)]

        def chunk_out(b, c, cols=cols, shared=shared):
            rows = slice(b * seq + c * CHUNK, b * seq + (c + 1) * CHUNK)
            tabs = shared["tabs"]
            o = _chunk_mix(rq[rows, cols], rk[rows, cols], rv[rows, cols], shared[b][c], tabs)
            mix_ref[rows, cols] = _chunk_norm_gate(o, rg[rows, cols], tabs).astype(BF16)

        def attention(b, hp=hp, cols=cols, shared=shared):
            head_a = shared["tabs"]["head_a"]
            tok = slice(b * seq, (b + 1) * seq)
            s = _dot_nt(nq[tok, cols], _split_heads(nk[tok, cols], head_a, 0))
            es, rinv = [], []
            for h in range(2):
                sh = s[:, h * seq:(h + 1) * seq]
                e = jnp.exp(sh - jnp.max(sh, axis=-1, keepdims=True))
                rinv.append(1.0 / jnp.sum(e, axis=-1, keepdims=True))
                es.append(e.astype(BF16))
            o = _dot(jnp.concatenate(es, axis=1), _split_heads(nv[tok, cols], head_a, 0))
            o = o * jnp.where(head_a, rinv[0], rinv[1])
            mix_ref[tok, WIDTH + hp * LANES:WIDTH + (hp + 1) * LANES] = o.astype(BF16)

        thunks.append(scan)
        for b in range(n_seq):
            thunks += [functools.partial(chunk_out, b, c) for c in range(nc)]
            thunks.append(functools.partial(attention, b))
    return thunks


def _inproj_mix_kernel(*refs, n_cast):
    n_proj, n_mix = 7, 11
    outs = refs[n_proj + n_mix + n_cast:]
    _cast_blocks(refs[n_proj + n_mix:n_proj + n_mix + n_cast], outs[3:])
    groups = _inproj_groups(*refs[:n_proj], outs[0], (), n_alias=0, layer=0, width=2 * LANES)
    units = _ctx_mixer_units(*refs[n_proj:n_proj + n_mix], outs[1], outs[2])
    done = 0
    for u, unit in enumerate(units):
        upto = (u + 1) * len(groups) // len(units)
        started = [(finish, start()) for start, finish in groups[done:upto]]
        done = upto
        unit()
        for finish, result in started:
            finish(result)


def _inproj_latent_and_ctx_mixer(x, mods, g_pre, w, layer, seq_len, tm, rope_tabs,
                                 p_ctx, dec, batch, seq, cast=()):
    n = x.shape[0]
    n_tiles = n // tm
    cast_arrays, cast_in, cast_out, cast_shapes = _cast_streams(cast, n_tiles)
    n_seq = batch // n_tiles
    assert n_seq * n_tiles == batch and seq_len % tm == 0
    tiles_per_seq = seq_len // tm
    group = lambda g: pl.BlockSpec((n_seq * seq, WIDTH), lambda i, g=g: (i, g))
    vec = lambda width: pl.BlockSpec((1, width), lambda i: (0, 0))
    in_specs = [
        pl.BlockSpec((tm, D_MODEL), lambda i: (i, 0)),
        pl.BlockSpec((None, 1, 6, D_MODEL), lambda i: (layer, 1 + i // tiles_per_seq, 0, 0)),
        pl.BlockSpec((None, 1, D_MODEL), lambda i: (layer, 0, 0)),
        pl.BlockSpec((None, D_MODEL, IN_WIDTH), lambda i: (layer, 0, 0),
                     pipeline_mode=pl.Buffered(1)),
    ] + [pl.BlockSpec((tm, LANES), lambda i: (i % tiles_per_seq, 0))] * 3 + [
        group(g) for g in range(N_GROUPS)] + [vec(WIDTH), vec(WIDTH), vec(2 * WIDTH),
                                              vec(2 * WIDTH)]
    results = pl.pallas_call(
        functools.partial(_inproj_mix_kernel, n_cast=len(cast)),
        grid=(n_tiles,),
        in_specs=in_specs + cast_in,
        out_specs=[
            pl.BlockSpec((tm, IN_WIDTH), lambda i: (i, 0)),
            pl.BlockSpec((n_seq * seq, 2 * WIDTH), lambda i: (i, 0)),
            pl.BlockSpec((n_seq, HEAD_PAIRS, 2, LANES, LANES), lambda i: (i, 0, 0, 0, 0)),
        ] + cast_out,
        out_shape=[
            jax.ShapeDtypeStruct((n, IN_WIDTH), BF16),
            jax.ShapeDtypeStruct((batch * seq, 2 * WIDTH), BF16),
            jax.ShapeDtypeStruct((batch, HEAD_PAIRS, 2, LANES, LANES), F32),
        ] + cast_shapes,
        compiler_params=_cparams(1),
        name="inproj_latent_ctx_mixer",
    )(x, mods, g_pre, w, *rope_tabs, *([p_ctx] * N_GROUPS), *dec, *cast_arrays)
    return list(results[:3]) + [r.reshape(w.shape) for r, w in zip(results[3:], cast)]


def _lat_retention_kernel(q_ref, k_ref, v_ref, g_ref, df, db, df2, db2, s0_ref, o_ref,
                          kv_ref, st_ref):
    nc = q_ref.shape[0] // CHUNK
    tabs = _retention_tables(df[...], db[...], df2[...], db2[...])

    def rows(c):
        return pl.ds(pl.multiple_of(c * CHUNK, CHUNK), CHUNK)

    def kv_body(c, carry):
        kv_ref[c] = _chunk_kv(k_ref[rows(c), :], v_ref[rows(c), :], tabs)
        return carry
    lax.fori_loop(0, nc, kv_body, 0, unroll=RET_UNROLL)

    def fwd_body(c, s):
        st_ref[c, :LANES, :] = s.astype(BF16)
        return s * tabs["cdf"] + kv_ref[c, :LANES, :]
    lax.fori_loop(0, nc, fwd_body, s0_ref[0, 0, 0])

    def bwd_body(i, s):
        c = nc - 1 - i
        st_ref[c, LANES:, :] = s.astype(BF16)
        return s * tabs["cdb"] + kv_ref[c, LANES:, :]
    lax.fori_loop(0, nc, bwd_body, s0_ref[0, 0, 1])

    def out_body(c, carry):
        r = rows(c)
        o = _chunk_mix(q_ref[r, :], k_ref[r, :], v_ref[r, :], st_ref[c], tabs)
        o_ref[r, :] = _chunk_norm_gate(o, g_ref[r, :], tabs).astype(BF16)
        return carry
    lax.fori_loop(0, nc, out_body, 0, unroll=RET_UNROLL)


def _lat_retention(p, dec, s0, batch, seq):
    nc = seq // CHUNK
    group = lambda g: pl.BlockSpec((seq, LANES), lambda b, h, g=g: (b, g * HEAD_PAIRS + h))
    vec = lambda w: pl.BlockSpec((1, w), lambda b, h: (0, h))
    return pl.pallas_call(
        _lat_retention_kernel,
        grid=(batch, HEAD_PAIRS),
        in_specs=[group(g) for g in range(4)] + [vec(LANES), vec(LANES), vec(2 * LANES),
                                                 vec(2 * LANES)]
        + [pl.BlockSpec((1, 1, 2, LANES, LANES), lambda b, h: (b, h, 0, 0, 0))],
        out_specs=pl.BlockSpec((seq, LANES), lambda b, h: (b, h)),
        out_shape=jax.ShapeDtypeStruct((batch * seq, WIDTH), BF16),
        scratch_shapes=[pltpu.VMEM((nc, 2 * LANES, LANES), F32),
                        pltpu.VMEM((nc, 2 * LANES, LANES), BF16)],
        compiler_params=_cparams(2),
        name="latent_retention",
    )(*([p] * 4), *dec, s0)


def _block_diag_states(s_f, s_b):
    def bd(s):
        b = s.shape[0]
        s = s.reshape(b, HEAD_PAIRS, 2, HEAD_DIM, HEAD_DIM)
        z = jnp.zeros_like(s[:, :, 0])
        top = jnp.concatenate([s[:, :, 0], z], axis=-1)
        bot = jnp.concatenate([z, s[:, :, 1]], axis=-1)
        return jnp.concatenate([top, bot], axis=-2)
    return jnp.stack([bd(s_f), bd(s_b)], axis=2)


def _diag_states(st):
    b = st.shape[0]
    a = st[:, :, :HEAD_DIM, :HEAD_DIM]
    c = st[:, :, HEAD_DIM:, HEAD_DIM:]
    return jnp.stack([a, c], axis=2).reshape(b, N_HEADS, HEAD_DIM, HEAD_DIM)


def _na_kernel(q_ref, k_ref, v_ref, ck_ref, cv_ref, bias_ref, o_ref, *, lo, hi):
    rows_total = k_ref.shape[0] // GRID_W
    lane = lax.broadcasted_iota(jnp.int32, (1, LANES), 1)
    head_a = lane < HEAD_DIM
    wins, cls = [], []
    for i in range(NA_ROWS):
        r = pl.program_id(1) * NA_ROWS + i
        rs = jnp.clip(r - NA_KH // 2, 0, rows_total - NA_KH)
        wins.append(pl.ds(pl.multiple_of(rs * GRID_W, GRID_W), NA_KH * GRID_W))
        cls.append(jnp.where(r < lo, r, jnp.where(r > hi, r - hi + lo, lo)))
    stages = [(hp, r0) for hp in range(HEAD_PAIRS) for r0 in range(0, NA_ROWS, NA_STAGE_ROWS)]

    def scores(stage):
        hp, r0 = stage
        cols = slice(hp * LANES, (hp + 1) * LANES)
        qq = jnp.concatenate(
            [_split_heads(q_ref[i * GRID_W:(i + 1) * GRID_W, cols], head_a, 0)
             for i in range(r0, r0 + NA_STAGE_ROWS)], axis=0)
        s_ctx = _dot_nt(qq, ck_ref[0, :, cols])
        s_loc = [_dot_nt(qq[j * 2 * GRID_W:(j + 1) * 2 * GRID_W], k_ref[wins[r0 + j], cols])
                 for j in range(NA_STAGE_ROWS)]
        return s_ctx, s_loc

    def softmax(stage, s_ctx, s_loc):
        hp, r0 = stage
        m_ctx = jnp.max(s_ctx, axis=-1, keepdims=True)
        m = []
        for j in range(NA_STAGE_ROWS):
            blk = slice(j * 2 * GRID_W, (j + 1) * 2 * GRID_W)
            s_loc[j] = s_loc[j] + bias_ref[cls[r0 + j], hp]
            m.append(jnp.maximum(jnp.max(s_loc[j], axis=-1, keepdims=True), m_ctx[blk]))
        e_ctx = jnp.exp(s_ctx - jnp.concatenate(m, axis=0))
        den_ctx = jnp.sum(e_ctx, axis=-1, keepdims=True)
        e_loc, den = [], []
        for j in range(NA_STAGE_ROWS):
            blk = slice(j * 2 * GRID_W, (j + 1) * 2 * GRID_W)
            e = jnp.exp(s_loc[j] - m[j])
            den.append(jnp.sum(e, axis=-1, keepdims=True) + den_ctx[blk])
            e_loc.append(e.astype(BF16))
        return e_ctx.astype(BF16), e_loc, den

    def weighted_values(stage, e_ctx, e_loc, den):
        hp, r0 = stage
        cols = slice(hp * LANES, (hp + 1) * LANES)
        o_ctx = _dot(e_ctx, cv_ref[0, :, cols])
        for j in range(NA_STAGE_ROWS):
            i = r0 + j
            blk = slice(j * 2 * GRID_W, (j + 1) * 2 * GRID_W)
            o = (_dot(e_loc[j], v_ref[wins[i], cols]) + o_ctx[blk]) / den[j]
            o_ref[i * GRID_W:(i + 1) * GRID_W, cols] = jnp.where(
                head_a, o[:GRID_W], o[GRID_W:]).astype(BF16)

    nxt = scores(stages[0])
    pending = None
    for n, stage in enumerate(stages):
        s_ctx, s_loc = nxt
        if n + 1 < len(stages):
            nxt = scores(stages[n + 1])
        probs = softmax(stage, s_ctx, s_loc)
        if pending is not None:
            weighted_values(*pending)
        pending = (stage,) + probs
    weighted_values(*pending)


def _mod_bias_kernel(cond_ref, w_ref, b_ref, rpb_ref, mod_ref, bias_ref, *, dr_first, n_dr):
    s = _silu(cond_ref[...]).astype(BF16)
    mod_ref[...] = _dot(s, w_ref[...].astype(BF16)) + b_ref[...]

    qc = lax.broadcasted_iota(jnp.int32, (GRID_W, LANES), 0)
    lane = lax.broadcasted_iota(jnp.int32, (GRID_W, LANES), 1)
    kc = lane & (GRID_W - 1)
    cs = jnp.clip(qc - NA_KW // 2, 0, GRID_W - NA_KW)
    inside = (kc >= cs) & (kc < cs + NA_KW)
    first = lane < GRID_W
    lo_half, hi_half = [], []
    for dr in range(n_dr):
        line = jnp.broadcast_to(rpb_ref[pl.ds(dr, 1), :], (GRID_W, LANES))
        lo_half.append(pltpu.roll(line, LANES - (NA_KW - 1), 1, stride=1, stride_axis=0))
        hi_half.append(pltpu.roll(line, GRID_W - (NA_KW - 1), 1, stride=1, stride_axis=0))
    for cls, dr0 in enumerate(dr_first):
        for jp in range(NA_KH // 2):
            dr = dr0 + 2 * jp
            tile = jnp.where(inside, jnp.where(first, lo_half[dr], hi_half[dr + 1]), NEG_INF)
            bias_ref[cls, :, jp * LANES:(jp + 1) * LANES] = tile


def _modulation_and_na_bias(cond, ada_w, ada_b, na_rpb, rows_total):
    depth, heads, n_dr, n_dc = na_rpb.shape
    kh = NA_KH
    lo, hi = kh // 2, rows_total - kh // 2 - 1
    reps = list(range(lo)) + [lo] + list(range(hi + 1, rows_total))
    dr_first = tuple(int(np.clip(r - kh // 2, 0, rows_total - kh)) - r + kh - 1 for r in reps)
    n_dr_pad = -(-n_dr // HALO) * HALO
    lines = jnp.pad(na_rpb, ((0, 0), (0, 0), (0, n_dr_pad - n_dr), (0, LANES - n_dc)))
    mod_cols = 6 * D_MODEL // heads
    assert mod_cols % LANES == 0
    mod, bias = pl.pallas_call(
        functools.partial(_mod_bias_kernel, dr_first=dr_first, n_dr=n_dr),
        grid=(depth, heads),
        in_specs=[
            pl.BlockSpec((8, D_MODEL), lambda l, h: (0, 0)),
            pl.BlockSpec((None, D_MODEL, mod_cols), lambda l, h: (l, 0, h)),
            pl.BlockSpec((None, 1, mod_cols), lambda l, h: (l, 0, h)),
            pl.BlockSpec((None, None, n_dr_pad, LANES), lambda l, h: (l, h, 0, 0)),
        ],
        out_specs=[
            pl.BlockSpec((None, 8, mod_cols), lambda l, h: (l, 0, h)),
            pl.BlockSpec((None, len(reps), None, GRID_W, kh * GRID_W),
                         lambda l, h: (l, 0, h // 2, h % 2, 0)),
        ],
        out_shape=[
            jax.ShapeDtypeStruct((depth, 8, 6 * D_MODEL), F32),
            jax.ShapeDtypeStruct((depth, len(reps), HEAD_PAIRS, 2 * GRID_W, kh * GRID_W), F32),
        ],
        compiler_params=_cparams(2),
        name="modulation_and_na_bias",
    )(cond, ada_w, ada_b.reshape(depth, 1, 6 * D_MODEL), lines)
    return mod, bias, lo, hi


def _neighbourhood_attention(p, ctx_k, ctx_v, bias, layer, lo, hi, batch, seq):
    rows_total = seq // GRID_W
    past = ctx_k.shape[1]

    groups = rows_total // NA_ROWS
    return pl.pallas_call(
        functools.partial(_na_kernel, lo=lo, hi=hi),
        grid=(batch, groups),
        in_specs=[
            pl.BlockSpec((NA_ROWS * GRID_W, WIDTH), lambda b, g: (b * groups + g, 4)),
            pl.BlockSpec((seq, WIDTH), lambda b, g: (b, 5), pipeline_mode=pl.Buffered(1)),
            pl.BlockSpec((seq, WIDTH), lambda b, g: (b, 6), pipeline_mode=pl.Buffered(1)),
            pl.BlockSpec((1, past, WIDTH), lambda b, g: (b, 0, 0), pipeline_mode=pl.Buffered(1)),
            pl.BlockSpec((1, past, WIDTH), lambda b, g: (b, 0, 0), pipeline_mode=pl.Buffered(1)),
            pl.BlockSpec((None,) + bias.shape[1:], lambda b, g: (layer, 0, 0, 0, 0),
                         pipeline_mode=pl.Buffered(1)),
        ],
        out_specs=pl.BlockSpec((NA_ROWS * GRID_W, WIDTH), lambda b, g: (b * groups + g, 0)),
        out_shape=jax.ShapeDtypeStruct((batch * seq, WIDTH), BF16),
        compiler_params=_cparams(2),
        name="neighbourhood_attention",
    )(p, p, p, ctx_k, ctx_v, bias)


def _post_kernel(xm, xp, xn, am, ap, an, bm, bp, bn, mod_head, mod_tail, g_ref, wo, wu, cw, cb, wd,
                 o_ref, acc_ref, h_ref, hp_ref, x1_ref, y_ref, act_ref, *, seq_len, n_tiles):
    step = pl.program_id(0)
    tm = xm.shape[0]
    nj = tm // HALO
    tiles_per_seq = max(seq_len // tm, 1)
    assert (tm % seq_len == 0 and seq_len % nj == 0) or seq_len % tm == 0
    h_rows = h_ref.shape[1]
    seg_pitch = nj + HALO
    assert h_rows == 2 * HALO + HALO * seg_pitch and nj % HEAD_ROWS == 0
    n_ct = D_MODEL // LANES
    g = g_ref[...]

    def ext(main, prev, nxt):
        lo = prev[...].astype(F32)[prev.shape[0] - HALO:]
        hi = nxt[...].astype(F32)[:HALO]
        return jnp.concatenate([lo, main[...].astype(F32), hi], axis=0)

    chunks = [(c0, min(FF_CHUNK, D_FF - c0)) for c0 in range(0, D_FF, FF_CHUNK)]
    par = step % 2
    mod_h = mod_head[0]
    n_blocks = tm // HEAD_ROWS

    def head_matmul():
        mixed = jnp.concatenate([ext(am, ap, an), ext(bm, bp, bn)], axis=1).astype(BF16)
        y_ref[...] = _dot(mixed, wo[...])

    def head_rows(x, y):
        x1 = x + mod_h[2:3] * _rms(y, g[0:1])
        return x1, _rms(x1, g[1:2]) * (1.0 + mod_h[4:5]) + mod_h[3:4]

    def head_block(blk):
        if blk < n_blocks:
            rows = slice(blk * HEAD_ROWS, (blk + 1) * HEAD_ROWS)
            erows = slice(HALO + blk * HEAD_ROWS, HALO + (blk + 1) * HEAD_ROWS)
            x1, h = head_rows(xm[rows, :], y_ref[erows, :])
            x1_ref[par, rows, :] = x1
            t0 = blk * HEAD_ROWS
            hrow = HALO + (t0 // nj) * seg_pitch + t0 % nj
            for c in range(n_ct):
                h_ref[c, hrow:hrow + HEAD_ROWS, :] = h[:, c * LANES:(c + 1) * LANES]
        else:
            x = jnp.concatenate([xp[...], xn[...]], axis=0)
            y = jnp.concatenate([y_ref[0:HALO, :], y_ref[HALO + tm:2 * HALO + tm, :]], axis=0)
            _, h = head_rows(x, y)
            t = jnp.minimum(step, n_tiles - 1) % tiles_per_seq
            ridx = lax.broadcasted_iota(jnp.int32, h.shape, 0)
            keep = ((ridx >= HALO) | (t != 0)) & ((ridx < HALO) | (t != tiles_per_seq - 1))
            h = jnp.where(keep, h, 0.0)
            for c in range(n_ct):
                h_ref[c, 0:HALO, :] = h[:HALO, c * LANES:(c + 1) * LANES]
                h_ref[c, h_rows - HALO:h_rows, :] = h[HALO:, c * LANES:(c + 1) * LANES]

    def head_permute():
        def group(rows):
            return jnp.concatenate([h_ref[c, rows, :] for c in range(n_ct)], axis=1)
        def put(row0, first, second):
            hp_ref[row0:row0 + 2 * HALO, :] = jnp.concatenate(
                [group(first), group(second)], axis=0).astype(BF16)
        for j in range(0, nj, 2):
            put(j * HALO, pl.ds(HALO + j, HALO, stride=seg_pitch),
                pl.ds(HALO + j + 1, HALO, stride=seg_pitch))
        put(tm, pl.ds(0, HALO), pl.ds(h_rows - HALO, HALO))

    def tail(interleaved):
        mod = mod_tail[0]

        def conv(u, cols):
            w = cw[:, cols]
            sub = lax.broadcasted_iota(jnp.int32, (HALO, u.shape[1]), 0)
            before = jnp.where(sub == 0, u[tm + HALO - 1:tm + HALO],
                               pltpu.roll(u[tm - HALO:tm], 1, 0))
            after = jnp.where(sub == HALO - 1, u[tm + HALO:tm + HALO + 1],
                              pltpu.roll(u[0:HALO], HALO - 1, 0))
            for s in range(1, HALO):
                if (s * nj) % seq_len == 0:
                    before = jnp.where(sub == s, 0.0, before)
                    after = jnp.where(sub == s - 1, 0.0, after)
            prev = jnp.concatenate([before, u[0:tm - HALO]], axis=0)
            nxt = jnp.concatenate([u[HALO:tm], after], axis=0)
            return prev * w[0:1] + u[0:tm] * w[1:2] + nxt * w[2:3] + cb[:, cols]

        def cols(ch):
            ca = slice(chunks[ch][0], chunks[ch][0] + chunks[ch][1])
            return ca, slice(D_FF + ca.start, D_FF + ca.stop)

        def up(ch):
            ca, cg = cols(ch)
            return _dot(hp_ref[...], wu[:, cg]), _dot(hp_ref[...], wu[:, ca])

        nxt_u = up(0)
        for ch in range(len(chunks)):
            ug, ua = nxt_u
            if ch + 1 < len(chunks):
                nxt_u = up(ch + 1)
            for blk in interleaved[ch]:
                head_block(blk)
            ca, cg = cols(ch)
            act_ref[:, ca] = (_silu(conv(ug, cg)) * conv(ua, ca)).astype(BF16)
        ffn = mod[5:6] * _rms(_dot(act_ref[...], wd[...]), g[2:3])
        for c in range(n_ct):
            acc_ref[c] = ffn[:, c * LANES:(c + 1) * LANES]

        for s in range(HALO):
            for jb in range(nj // HALO):
                t0 = s * nj + HALO * jb
                rows = pl.ds(HALO * HALO * jb + s, HALO, stride=HALO)
                back = jnp.concatenate([acc_ref[c, rows, :] for c in range(n_ct)], axis=1)
                o_ref[t0:t0 + HALO, :] = x1_ref[1 - par, t0:t0 + HALO, :] + back

    @pl.when(step == 0)
    def _():
        head_matmul()
        for blk in range(n_blocks + 1):
            head_block(blk)
        head_permute()

    @pl.when(step > 0)
    def _():
        head_matmul()
        tail([[blk for blk in range(n_blocks + 1) if blk * len(chunks) // (n_blocks + 1) == ch]
              for ch in range(len(chunks))])
        head_permute()


def _post(x, mix_a, mix_b, mods, gains, wo, wu, cw, cb, wd, layer, seq_len, tm, latent):
    n = x.shape[0]
    nt = n // tm
    (mix_a, col_a), (mix_b, col_b) = mix_a, mix_b
    tiles_per_seq = max(seq_len // tm, 1)
    cond_row = (lambda i: 1 + i // tiles_per_seq) if latent else (lambda i: 0)

    head_tile = lambda i: jnp.minimum(i, nt - 1)
    tail_tile = lambda i: jnp.maximum(i - 1, 0)

    def triple(width, halo_rows, col=0):
        per = tm // halo_rows
        last = n // halo_rows - 1
        return [
            pl.BlockSpec((tm, width), lambda i: (head_tile(i), col)),
            pl.BlockSpec((halo_rows, width),
                         lambda i: (jnp.maximum(head_tile(i) * per - 1, 0), col)),
            pl.BlockSpec((halo_rows, width),
                         lambda i: (jnp.minimum((head_tile(i) + 1) * per, last), col)),
        ]

    const = lambda *shape: pl.BlockSpec((None,) + shape, lambda i: (layer,) + (0,) * len(shape),
                                        pipeline_mode=pl.Buffered(1))
    in_specs = (triple(D_MODEL, HALO) + triple(WIDTH, 2 * HALO, col_a)
                + triple(WIDTH, 2 * HALO, col_b) + [
        pl.BlockSpec((None, 1, 6, D_MODEL), lambda i: (layer, cond_row(head_tile(i)), 0, 0)),
        pl.BlockSpec((None, 1, 6, D_MODEL), lambda i: (layer, cond_row(tail_tile(i)), 0, 0)),
        const(3, D_MODEL),
        const(2 * WIDTH, D_MODEL),
        const(D_MODEL, 2 * D_FF),
        const(3, 2 * D_FF),
        const(1, 2 * D_FF),
        const(D_FF, D_MODEL),
    ])
    return pl.pallas_call(
        functools.partial(_post_kernel, seq_len=seq_len, n_tiles=nt),
        grid=(nt + 1,),
        in_specs=in_specs,
        out_specs=pl.BlockSpec((tm, D_MODEL), lambda i: (tail_tile(i), 0)),
        out_shape=jax.ShapeDtypeStruct((n, D_MODEL), F32),
        scratch_shapes=[pltpu.VMEM((D_MODEL // LANES, tm, LANES), F32),
                        pltpu.VMEM((D_MODEL // LANES, 2 * HALO + HALO * (tm // HALO + HALO), LANES),
                                   F32),
                        pltpu.VMEM((tm + 2 * HALO, D_MODEL), BF16),
                        pltpu.VMEM((2, tm, D_MODEL), F32),
                        pltpu.VMEM((tm + 2 * HALO, D_MODEL), F32),
                        pltpu.VMEM((tm, D_FF), BF16)],
        compiler_params=_cparams(1),
        name="post_latent" if latent else "post_context",
    )(x, x, x, mix_a, mix_a, mix_a, mix_b, mix_b, mix_b, mods, mods, gains, wo, wu, cw, cb, wd)


def kernel(x_prompt, x_sample, c, cache_na_k, cache_na_v, state_ret_fwd, state_ret_bwd, c_ctx,
           ada_w, ada_b, g_pre_mix, g_post_mix, g_pre_ffn, g_post_ffn, w_in,
           ret_decay_fwd, ret_decay_bwd, na_rpb, w_out, w_up, conv_w, conv_b, w_down):
    depth = w_in.shape[0]
    batch, seq, _ = x_prompt.shape
    dec_batch, dec_seq, _ = x_sample.shape
    past = cache_na_k.shape[2]
    tm = 512

    cond = jnp.concatenate(
        [c_ctx[None, :], c, jnp.zeros((8 - 1 - dec_batch, D_MODEL), F32)], axis=0)
    mods, bias, lo, hi = _modulation_and_na_bias(cond, ada_w, ada_b, na_rpb, dec_seq // GRID_W)
    mods = mods.reshape(depth, 8, 6, D_MODEL)
    rope_tabs = _rope_tables(dec_seq)

    w_in_b = w_in.astype(BF16)
    g_pre = g_pre_mix.reshape(depth, 1, D_MODEL)
    gains = jnp.stack([g_post_mix, g_pre_ffn, g_post_ffn], axis=1)
    cb = conv_b.reshape(depth, 1, 2 * D_FF)

    y_p = x_prompt.reshape(batch * seq, D_MODEL)
    y_s = x_sample.reshape(dec_batch * dec_seq, D_MODEL)
    new_k = new_v = None
    sfs, sbs = [], []
    for l in range(depth):
        dec = (jnp.repeat(ret_decay_fwd[l], HEAD_DIM)[None, :],
               jnp.repeat(ret_decay_bwd[l], HEAD_DIM)[None, :],
               jnp.repeat(ret_decay_fwd[l], LANES)[None, :],
               jnp.repeat(ret_decay_bwd[l], LANES)[None, :])

        p_c, new_k, new_v, *cast_c = _inproj_context(
            y_p, mods, g_pre, w_in_b, l, seq, tm, (new_k, new_v), cast=(w_up,) if l == 0 else ())
        p_s, mix_c, st_c, *cast_s = _inproj_latent_and_ctx_mixer(
            y_s, mods, g_pre, w_in_b, l, dec_seq, tm, rope_tabs, p_c, dec, batch, seq,
            cast=(w_down, w_out) if l == 0 else ())
        if l == 0:
            (wu_b,), (wd_b, wo_b) = cast_c, cast_s
            post_params = (mods, gains, wo_b, wu_b, conv_w, cb, wd_b)
        y_p = _post(y_p, (mix_c, 0), (mix_c, 1), *post_params, l, seq, tm, latent=False)
        sfs.append(_diag_states(st_c[:, :, 0]))
        sbs.append(_diag_states(st_c[:, :, 1]))

        s0 = _block_diag_states(state_ret_fwd[:, l], state_ret_bwd[:, l])
        ret_s = _lat_retention(p_s, dec, s0, dec_batch, dec_seq)
        ck = cache_na_k[:, l].reshape(dec_batch, past, WIDTH).astype(BF16)
        cv = cache_na_v[:, l].reshape(dec_batch, past, WIDTH).astype(BF16)
        na_s = _neighbourhood_attention(p_s, ck, cv, bias, l, lo, hi, dec_batch, dec_seq)
        y_s = _post(y_s, (ret_s, 0), (na_s, 0), *post_params, l, dec_seq, tm, latent=True)

    return (y_p.reshape(batch, seq, D_MODEL),
            y_s.reshape(dec_batch, dec_seq, D_MODEL),
            new_k.reshape(batch, depth, seq, N_HEADS, HEAD_DIM),
            new_v.reshape(batch, depth, seq, N_HEADS, HEAD_DIM),
            jnp.stack(sfs, axis=1), jnp.stack(sbs, axis=1))
```

```python
import functools

import numpy as np
import jax
import jax.numpy as jnp
from jax import lax
from jax.experimental import pallas as pl
from jax.experimental.pallas import tpu as pltpu

F32 = jnp.float32
BF16 = jnp.bfloat16

D_MODEL = 1024
HEAD_DIM = 64
N_HEADS = 8
HEAD_PAIRS = N_HEADS // 2
LANES = 128
WIDTH = N_HEADS * HEAD_DIM
N_GROUPS = 7
IN_WIDTH = N_GROUPS * WIDTH
D_FF = 2816
FF_CHUNK = 768
HEAD_ROWS = 32
CHUNK = 128
RET_UNROLL = 8
GRID_W = 64
NA_KH = 8
NA_KW = 16
NA_STAGE_ROWS = 8
NA_ROWS = 8
ROPE_BASE = 10000.0
EPS = 1e-6
NEG_INF = -1e9
HALO = 8
VMEM_LIMIT = 56 * 1024 * 1024


def _cparams(n_grid):
    return pltpu.CompilerParams(
        dimension_semantics=("arbitrary",) * n_grid, vmem_limit_bytes=VMEM_LIMIT)


def _rms(x, g):
    ms = jnp.mean(x * x, axis=-1, keepdims=True)
    return x * lax.rsqrt(ms + EPS) * g


def _silu(x):
    return x * jax.nn.sigmoid(x)


def _log_sigmoid(x):
    return jnp.minimum(x, 0.0) - jnp.log1p(jnp.exp(-jnp.abs(x)))


def _dot(a, b):
    return jnp.dot(a, b, preferred_element_type=F32)


def _dot_nt(a, b):
    return lax.dot_general(a, b, (((1,), (1,)), ((), ())), preferred_element_type=F32)


def _dot_tn(a, b):
    return lax.dot_general(a, b, (((0,), (0,)), ((), ())), preferred_element_type=F32)


def _cast_streams(weights, n_steps):
    arrays = [w.reshape(-1, w.shape[-1]) for w in weights]
    in_specs, out_specs, out_shapes = [], [], []
    for a in arrays:
        rows = a.shape[0] // n_steps
        assert rows * n_steps == a.shape[0] and rows % (2 * HALO) == 0
        in_specs.append(pl.BlockSpec((rows, a.shape[1]), lambda i: (i, 0)))
        out_specs.append(pl.BlockSpec((rows, a.shape[1]), lambda i: (i, 0)))
        out_shapes.append(jax.ShapeDtypeStruct(a.shape, BF16))
    return arrays, in_specs, out_specs, out_shapes


def _cast_blocks(in_refs, out_refs):
    for src, dst in zip(in_refs, out_refs):
        dst[...] = src[...].astype(dst.dtype)


def _inproj_kernel(*refs, n_alias, layer):
    outs = refs[4 + n_alias:]
    for start, finish in _inproj_groups(*refs[:4], None, None, None, outs[0], outs[1:], n_alias,
                                        layer):
        finish(start())


def _inproj_groups(x_ref, mod_ref, g_ref, w_ref, cos_ref, sin_up_ref, sin_dn_ref, p_ref, kv_refs,
                   n_alias, layer, width=WIDTH):
    rope = cos_ref is not None
    emit_kv = len(kv_refs) > 0
    assert WIDTH % width == 0 and (width == WIDTH or not emit_kv)
    mod = mod_ref[0]
    h = (_rms(x_ref[...], g_ref[...]) * (1.0 + mod[1:2]) + mod[0:1]).astype(BF16)

    def start(n):
        return _dot(h, w_ref[:, n * width:(n + 1) * width])

    def finish(n, pg):
        cols = slice(n * width, (n + 1) * width)
        g = n * width // WIDTH
        if rope and g < 2:
            parts = []
            for j in range(width // LANES):
                xj = pg[:, j * LANES:(j + 1) * LANES]
                parts.append(xj * cos_ref[...]
                             + pltpu.roll(xj, 16, 1) * sin_up_ref[...]
                             + pltpu.roll(xj, LANES - 16, 1) * sin_dn_ref[...])
            pg = jnp.concatenate(parts, axis=1)
        if g in (0, 4):
            pg = pg * (HEAD_DIM ** -0.5)
        p_ref[:, cols] = pg.astype(BF16)
        if emit_kv and g >= 5:
            kv_ref = kv_refs[g - 5]
            seq = kv_ref.shape[-2]
            for j in range(kv_ref.shape[0]):
                if n_alias:
                    kv_ref[j] = pg[j * seq:(j + 1) * seq]
                else:
                    for l in range(kv_ref.shape[1]):
                        kv_ref[j, l] = (pg[j * seq:(j + 1) * seq] if l == layer
                                        else jnp.zeros((seq, WIDTH), F32))

    return [(functools.partial(start, n), functools.partial(finish, n))
            for n in range(IN_WIDTH // width)]


def _inproj_context(x, mods, g_pre, w, layer, seq_len, tm, kv_out):
    n = x.shape[0]
    depth = w.shape[0]
    in_specs = [
        pl.BlockSpec((tm, D_MODEL), lambda i: (i, 0)),
        pl.BlockSpec((None, 1, 6, D_MODEL), lambda i: (layer, 0, 0, 0)),
        pl.BlockSpec((None, 1, D_MODEL), lambda i: (layer, 0, 0)),
        pl.BlockSpec((None, D_MODEL, IN_WIDTH), lambda i: (layer, 0, 0),
                     pipeline_mode=pl.Buffered(1)),
    ]
    args = [x, mods, g_pre, w]
    out_specs = [pl.BlockSpec((tm, IN_WIDTH), lambda i: (i, 0))]
    out_shape = [jax.ShapeDtypeStruct((n, IN_WIDTH), BF16)]
    aliases = {}
    seqs_per_tile = tm // seq_len
    kv_shape = jax.ShapeDtypeStruct((n // seq_len, depth, seq_len, WIDTH), F32)
    assert (kv_out[0] is None) == (kv_out[1] is None)
    for j, prev in enumerate(kv_out):
        if prev is not None:
            aliases[len(args)] = 1 + j
            in_specs.append(pl.BlockSpec(memory_space=pl.ANY))
            args.append(prev)
            out_specs.append(pl.BlockSpec((seqs_per_tile, None, seq_len, WIDTH),
                                          lambda i: (i, layer, 0, 0)))
        else:
            out_specs.append(pl.BlockSpec((seqs_per_tile, depth, seq_len, WIDTH),
                                          lambda i: (i, 0, 0, 0)))
        out_shape.append(kv_shape)
    return pl.pallas_call(
        functools.partial(_inproj_kernel, n_alias=len(aliases), layer=layer),
        grid=(n // tm,),
        in_specs=in_specs,
        out_specs=out_specs,
        out_shape=out_shape,
        input_output_aliases=aliases,
        compiler_params=_cparams(1),
        name="inproj_context",
    )(*args)


def _rope_tables(seq_len):
    t = np.arange(seq_len)
    lane = np.arange(LANES)
    d = lane % HEAD_DIM
    pos = np.where(d[None, :] < HEAD_DIM // 2, (t // GRID_W)[:, None], (t % GRID_W)[:, None])
    pos = pos.astype(np.float32)
    half = HEAD_DIM // 2
    inv = np.power(np.float32(ROPE_BASE), -np.arange(0, half, 2, dtype=np.float32) / half)
    ang = pos * inv[d % (half // 2)][None, :]
    cos, sin = np.cos(ang), np.sin(ang)
    upper = (d % half) >= half // 2
    sin_up = np.where(upper[None, :], sin, 0.0)
    sin_dn = np.where(upper[None, :], 0.0, -sin)
    return (jnp.asarray(cos, F32), jnp.asarray(sin_up, F32), jnp.asarray(sin_dn, F32))


def _retention_tables(dec_f, dec_b, dec_f2, dec_b2):
    lgf, lgb = _log_sigmoid(dec_f), _log_sigmoid(dec_b)
    pos = lax.broadcasted_iota(jnp.int32, (CHUNK, LANES), 0).astype(F32)
    tabs = dict(
        qdf=jnp.exp(lgf * (pos + 1.0)), kdf=jnp.exp(lgf * (CHUNK - 1.0 - pos)),
        cdf=jnp.exp(lgf * float(CHUNK)),
        qdb=jnp.exp(lgb * (CHUNK - pos)), kdb=jnp.exp(lgb * pos),
        cdb=jnp.exp(lgb * float(CHUNK)),
    )
    lgf2, lgb2 = _log_sigmoid(dec_f2), _log_sigmoid(dec_b2)
    i = lax.broadcasted_iota(jnp.int32, (CHUNK, 2 * CHUNK), 0)
    j = lax.broadcasted_iota(jnp.int32, (CHUNK, 2 * CHUNK), 1) & (CHUNK - 1)
    diff = (i - j).astype(F32)
    tabs["decay"] = (jnp.where(diff >= 0, jnp.exp(lgf2 * jnp.maximum(diff, 0.0)), 0.0)
                     + jnp.where(diff <= 0, jnp.exp(lgb2 * jnp.maximum(-diff, 0.0)), 0.0))
    lane = lax.broadcasted_iota(jnp.int32, (1, LANES), 1)
    tabs["head_a"] = lane < HEAD_DIM
    r = lax.broadcasted_iota(jnp.int32, (2 * LANES, LANES), 0) & (LANES - 1)
    c = lax.broadcasted_iota(jnp.int32, (2 * LANES, LANES), 1)
    tabs["same_head"] = (r < HEAD_DIM) == (c < HEAD_DIM)
    return tabs


def _split_heads(x, head_a, axis):
    zero = jnp.zeros_like(x)
    return jnp.concatenate([jnp.where(head_a, x, zero), jnp.where(head_a, zero, x)], axis=axis)


def _chunk_kv(k2, v2, tabs):
    kf = k2.astype(F32)
    kk = jnp.concatenate([kf * tabs["kdf"], kf * tabs["kdb"]], axis=1).astype(BF16)
    return jnp.where(tabs["same_head"], _dot_tn(kk, v2), 0.0)


def _chunk_mix(q2, k2, v2, state, tabs):
    head_a = tabs["head_a"]
    s = _dot_nt(q2, _split_heads(k2, head_a, 0))
    p = (s * tabs["decay"]).astype(BF16)
    qf = q2.astype(F32)
    lhs = jnp.concatenate(
        [p, (qf * tabs["qdf"]).astype(BF16), (qf * tabs["qdb"]).astype(BF16)], axis=1)
    rhs = jnp.concatenate([_split_heads(v2, head_a, 0), state], axis=0)
    return _dot(lhs, rhs)


def _chunk_norm_gate(o, g2, tabs):
    head_a = tabs["head_a"]
    inv = 1.0 / HEAD_DIM
    sum_a = jnp.sum(jnp.where(head_a, o, 0.0), axis=-1, keepdims=True)
    sum_b = jnp.sum(jnp.where(head_a, 0.0, o), axis=-1, keepdims=True)
    d = o - jnp.where(head_a, sum_a, sum_b) * inv
    d2 = d * d
    var_a = jnp.sum(jnp.where(head_a, d2, 0.0), axis=-1, keepdims=True)
    var_b = jnp.sum(jnp.where(head_a, 0.0, d2), axis=-1, keepdims=True)
    o = d * lax.rsqrt(jnp.where(head_a, var_a, var_b) * inv + EPS)
    return o * _silu(g2.astype(F32))


def _ctx_mixer_units(rq, rk, rv, rg, nq, nk, nv, df, db, df2, db2, mix_ref, st_ref):
    n_seq = st_ref.shape[0]
    seq = rq.shape[0] // n_seq
    nc = seq // CHUNK

    thunks = []
    for hp in range(HEAD_PAIRS):
        cols = slice(hp * LANES, (hp + 1) * LANES)
        cols2 = slice(hp * 2 * LANES, (hp + 1) * 2 * LANES)
        shared = {}

        def scan(hp=hp, cols=cols, cols2=cols2, shared=shared):
            tabs = _retention_tables(df[:, cols], db[:, cols], df2[:, cols2], db2[:, cols2])
            shared["tabs"] = tabs
            for b in range(n_seq):
                rows = [slice(b * seq + c * CHUNK, b * seq + (c + 1) * CHUNK) for c in range(nc)]
                kv = [_chunk_kv(rk[r, cols], rv[r, cols], tabs) for r in rows]
                sf = [jnp.zeros((LANES, LANES), F32)]
                for c in range(nc):
                    sf.append(sf[-1] * tabs["cdf"] + kv[c][:LANES])
                sb = [jnp.zeros((LANES, LANES), F32)]
                for c in reversed(range(nc)):
                    sb.append(sb[-1] * tabs["cdb"] + kv[c][LANES:])
                st_ref[b, hp, 0] = sf[nc]
                st_ref[b, hp, 1] = sb[nc]
                shared[b] = [jnp.concatenate([sf[c], sb[nc - 1 - c]], axis=0).astype(BF16)
                             for c in range(nc)]

        def chunk_out(b, c, cols=cols, shared=shared):
            rows = slice(b * seq + c * CHUNK, b * seq + (c + 1) * CHUNK)
            tabs = shared["tabs"]
            o = _chunk_mix(rq[rows, cols], rk[rows, cols], rv[rows, cols], shared[b][c], tabs)
            mix_ref[rows, cols] = _chunk_norm_gate(o, rg[rows, cols], tabs).astype(BF16)

        def attention(b, hp=hp, cols=cols, shared=shared):
            head_a = shared["tabs"]["head_a"]
            tok = slice(b * seq, (b + 1) * seq)
            s = _dot_nt(nq[tok, cols], _split_heads(nk[tok, cols], head_a, 0))
            es, rinv = [], []
            for h in range(2):
                sh = s[:, h * seq:(h + 1) * seq]
                e = jnp.exp(sh - jnp.max(sh, axis=-1, keepdims=True))
                rinv.append(1.0 / jnp.sum(e, axis=-1, keepdims=True))
                es.append(e.astype(BF16))
            o = _dot(jnp.concatenate(es, axis=1), _split_heads(nv[tok, cols], head_a, 0))
            o = o * jnp.where(head_a, rinv[0], rinv[1])
            mix_ref[tok, WIDTH + hp * LANES:WIDTH + (hp + 1) * LANES] = o.astype(BF16)

        thunks.append(scan)
        for b in range(n_seq):
            thunks += [functools.partial(chunk_out, b, c) for c in range(nc)]
            thunks.append(functools.partial(attention, b))
    return thunks


def _inproj_mix_kernel(*refs, n_cast):
    n_proj, n_mix = 7, 11
    outs = refs[n_proj + n_mix + n_cast:]
    _cast_blocks(refs[n_proj + n_mix:n_proj + n_mix + n_cast], outs[3:])
    groups = _inproj_groups(*refs[:n_proj], outs[0], (), n_alias=0, layer=0, width=2 * LANES)
    units = _ctx_mixer_units(*refs[n_proj:n_proj + n_mix], outs[1], outs[2])
    done = 0
    for u, unit in enumerate(units):
        upto = (u + 1) * len(groups) // len(units)
        started = [(finish, start()) for start, finish in groups[done:upto]]
        done = upto
        unit()
        for finish, result in started:
            finish(result)


def _inproj_latent_and_ctx_mixer(x, mods, g_pre, w, layer, seq_len, tm, rope_tabs,
                                 p_ctx, dec, batch, seq, cast=()):
    n = x.shape[0]
    n_tiles = n // tm
    cast_arrays, cast_in, cast_out, cast_shapes = _cast_streams(cast, n_tiles)
    n_seq = batch // n_tiles
    assert n_seq * n_tiles == batch and seq_len % tm == 0
    tiles_per_seq = seq_len // tm
    group = lambda g: pl.BlockSpec((n_seq * seq, WIDTH), lambda i, g=g: (i, g))
    vec = lambda width: pl.BlockSpec((1, width), lambda i: (0, 0))
    in_specs = [
        pl.BlockSpec((tm, D_MODEL), lambda i: (i, 0)),
        pl.BlockSpec((None, 1, 6, D_MODEL), lambda i: (layer, 1 + i // tiles_per_seq, 0, 0)),
        pl.BlockSpec((None, 1, D_MODEL), lambda i: (layer, 0, 0)),
        pl.BlockSpec((None, D_MODEL, IN_WIDTH), lambda i: (layer, 0, 0),
                     pipeline_mode=pl.Buffered(1)),
    ] + [pl.BlockSpec((tm, LANES), lambda i: (i % tiles_per_seq, 0))] * 3 + [
        group(g) for g in range(N_GROUPS)] + [vec(WIDTH), vec(WIDTH), vec(2 * WIDTH),
                                              vec(2 * WIDTH)]
    results = pl.pallas_call(
        functools.partial(_inproj_mix_kernel, n_cast=len(cast)),
        grid=(n_tiles,),
        in_specs=in_specs + cast_in,
        out_specs=[
            pl.BlockSpec((tm, IN_WIDTH), lambda i: (i, 0)),
            pl.BlockSpec((n_seq * seq, 2 * WIDTH), lambda i: (i, 0)),
            pl.BlockSpec((n_seq, HEAD_PAIRS, 2, LANES, LANES), lambda i: (i, 0, 0, 0, 0)),
        ] + cast_out,
        out_shape=[
            jax.ShapeDtypeStruct((n, IN_WIDTH), BF16),
            jax.ShapeDtypeStruct((batch * seq, 2 * WIDTH), BF16),
            jax.ShapeDtypeStruct((batch, HEAD_PAIRS, 2, LANES, LANES), F32),
        ] + cast_shapes,
        compiler_params=_cparams(1),
        name="inproj_latent_ctx_mixer",
    )(x, mods, g_pre, w, *rope_tabs, *([p_ctx] * N_GROUPS), *dec, *cast_arrays)
    return list(results[:3]) + [r.reshape(w.shape) for r, w in zip(results[3:], cast)]


def _lat_retention_kernel(q_ref, k_ref, v_ref, g_ref, df, db, df2, db2, s0_ref, o_ref,
                          kv_ref, st_ref):
    nc = q_ref.shape[0] // CHUNK
    tabs = _retention_tables(df[...], db[...], df2[...], db2[...])

    def rows(c):
        return pl.ds(pl.multiple_of(c * CHUNK, CHUNK), CHUNK)

    def kv_body(c, carry):
        kv_ref[c] = _chunk_kv(k_ref[rows(c), :], v_ref[rows(c), :], tabs)
        return carry
    lax.fori_loop(0, nc, kv_body, 0, unroll=RET_UNROLL)

    def fwd_body(c, s):
        st_ref[c, :LANES, :] = s.astype(BF16)
        return s * tabs["cdf"] + kv_ref[c, :LANES, :]
    lax.fori_loop(0, nc, fwd_body, s0_ref[0, 0, 0])

    def bwd_body(i, s):
        c = nc - 1 - i
        st_ref[c, LANES:, :] = s.astype(BF16)
        return s * tabs["cdb"] + kv_ref[c, LANES:, :]
    lax.fori_loop(0, nc, bwd_body, s0_ref[0, 0, 1])

    def out_body(c, carry):
        r = rows(c)
        o = _chunk_mix(q_ref[r, :], k_ref[r, :], v_ref[r, :], st_ref[c], tabs)
        o_ref[r, :] = _chunk_norm_gate(o, g_ref[r, :], tabs).astype(BF16)
        return carry
    lax.fori_loop(0, nc, out_body, 0, unroll=RET_UNROLL)


def _lat_retention(p, dec, s0, batch, seq):
    nc = seq // CHUNK
    group = lambda g: pl.BlockSpec((seq, LANES), lambda b, h, g=g: (b, g * HEAD_PAIRS + h))
    vec = lambda w: pl.BlockSpec((1, w), lambda b, h: (0, h))
    return pl.pallas_call(
        _lat_retention_kernel,
        grid=(batch, HEAD_PAIRS),
        in_specs=[group(g) for g in range(4)] + [vec(LANES), vec(LANES), vec(2 * LANES),
                                                 vec(2 * LANES)]
        + [pl.BlockSpec((1, 1, 2, LANES, LANES), lambda b, h: (b, h, 0, 0, 0))],
        out_specs=pl.BlockSpec((seq, LANES), lambda b, h: (b, h)),
        out_shape=jax.ShapeDtypeStruct((batch * seq, WIDTH), BF16),
        scratch_shapes=[pltpu.VMEM((nc, 2 * LANES, LANES), F32),
                        pltpu.VMEM((nc, 2 * LANES, LANES), BF16)],
        compiler_params=_cparams(2),
        name="latent_retention",
    )(*([p] * 4), *dec, s0)


def _block_diag_states(s_f, s_b):
    def bd(s):
        b = s.shape[0]
        s = s.reshape(b, HEAD_PAIRS, 2, HEAD_DIM, HEAD_DIM)
        z = jnp.zeros_like(s[:, :, 0])
        top = jnp.concatenate([s[:, :, 0], z], axis=-1)
        bot = jnp.concatenate([z, s[:, :, 1]], axis=-1)
        return jnp.concatenate([top, bot], axis=-2)
    return jnp.stack([bd(s_f), bd(s_b)], axis=2)


def _diag_states(st):
    b = st.shape[0]
    a = st[:, :, :HEAD_DIM, :HEAD_DIM]
    c = st[:, :, HEAD_DIM:, HEAD_DIM:]
    return jnp.stack([a, c], axis=2).reshape(b, N_HEADS, HEAD_DIM, HEAD_DIM)


def _na_kernel(q_ref, k_ref, v_ref, ck_ref, cv_ref, bias_ref, o_ref, *, lo, hi):
    rows_total = k_ref.shape[0] // GRID_W
    lane = lax.broadcasted_iota(jnp.int32, (1, LANES), 1)
    head_a = lane < HEAD_DIM
    wins, cls = [], []
    for i in range(NA_ROWS):
        r = pl.program_id(1) * NA_ROWS + i
        rs = jnp.clip(r - NA_KH // 2, 0, rows_total - NA_KH)
        wins.append(pl.ds(pl.multiple_of(rs * GRID_W, GRID_W), NA_KH * GRID_W))
        cls.append(jnp.where(r < lo, r, jnp.where(r > hi, r - hi + lo, lo)))
    stages = [(hp, r0) for hp in range(HEAD_PAIRS) for r0 in range(0, NA_ROWS, NA_STAGE_ROWS)]

    def scores(stage):
        hp, r0 = stage
        cols = slice(hp * LANES, (hp + 1) * LANES)
        qq = jnp.concatenate(
            [_split_heads(q_ref[i * GRID_W:(i + 1) * GRID_W, cols], head_a, 0)
             for i in range(r0, r0 + NA_STAGE_ROWS)], axis=0)
        s_ctx = _dot_nt(qq, ck_ref[0, :, cols])
        s_loc = [_dot_nt(qq[j * 2 * GRID_W:(j + 1) * 2 * GRID_W], k_ref[wins[r0 + j], cols])
                 for j in range(NA_STAGE_ROWS)]
        return s_ctx, s_loc

    def softmax(stage, s_ctx, s_loc):
        hp, r0 = stage
        m_ctx = jnp.max(s_ctx, axis=-1, keepdims=True)
        m = []
        for j in range(NA_STAGE_ROWS):
            blk = slice(j * 2 * GRID_W, (j + 1) * 2 * GRID_W)
            s_loc[j] = s_loc[j] + bias_ref[cls[r0 + j], hp]
            m.append(jnp.maximum(jnp.max(s_loc[j], axis=-1, keepdims=True), m_ctx[blk]))
        e_ctx = jnp.exp(s_ctx - jnp.concatenate(m, axis=0))
        den_ctx = jnp.sum(e_ctx, axis=-1, keepdims=True)
        e_loc, den = [], []
        for j in range(NA_STAGE_ROWS):
            blk = slice(j * 2 * GRID_W, (j + 1) * 2 * GRID_W)
            e = jnp.exp(s_loc[j] - m[j])
            den.append(jnp.sum(e, axis=-1, keepdims=True) + den_ctx[blk])
            e_loc.append(e.astype(BF16))
        return e_ctx.astype(BF16), e_loc, den

    def weighted_values(stage, e_ctx, e_loc, den):
        hp, r0 = stage
        cols = slice(hp * LANES, (hp + 1) * LANES)
        o_ctx = _dot(e_ctx, cv_ref[0, :, cols])
        for j in range(NA_STAGE_ROWS):
            i = r0 + j
            blk = slice(j * 2 * GRID_W, (j + 1) * 2 * GRID_W)
            o = (_dot(e_loc[j], v_ref[wins[i], cols]) + o_ctx[blk]) / den[j]
            o_ref[i * GRID_W:(i + 1) * GRID_W, cols] = jnp.where(
                head_a, o[:GRID_W], o[GRID_W:]).astype(BF16)

    nxt = scores(stages[0])
    pending = None
    for n, stage in enumerate(stages):
        s_ctx, s_loc = nxt
        if n + 1 < len(stages):
            nxt = scores(stages[n + 1])
        probs = softmax(stage, s_ctx, s_loc)
        if pending is not None:
            weighted_values(*pending)
        pending = (stage,) + probs
    weighted_values(*pending)


def _mod_bias_kernel(cond_ref, w_ref, b_ref, rpb_ref, mod_ref, bias_ref, *, dr_first, n_dr):
    s = _silu(cond_ref[...]).astype(BF16)
    mod_ref[...] = _dot(s, w_ref[...].astype(BF16)) + b_ref[...]

    qc = lax.broadcasted_iota(jnp.int32, (GRID_W, LANES), 0)
    lane = lax.broadcasted_iota(jnp.int32, (GRID_W, LANES), 1)
    kc = lane & (GRID_W - 1)
    cs = jnp.clip(qc - NA_KW // 2, 0, GRID_W - NA_KW)
    inside = (kc >= cs) & (kc < cs + NA_KW)
    first = lane < GRID_W
    lo_half, hi_half = [], []
    for dr in range(n_dr):
        line = jnp.broadcast_to(rpb_ref[pl.ds(dr, 1), :], (GRID_W, LANES))
        lo_half.append(pltpu.roll(line, LANES - (NA_KW - 1), 1, stride=1, stride_axis=0))
        hi_half.append(pltpu.roll(line, GRID_W - (NA_KW - 1), 1, stride=1, stride_axis=0))
    for cls, dr0 in enumerate(dr_first):
        for jp in range(NA_KH // 2):
            dr = dr0 + 2 * jp
            tile = jnp.where(inside, jnp.where(first, lo_half[dr], hi_half[dr + 1]), NEG_INF)
            bias_ref[cls, :, jp * LANES:(jp + 1) * LANES] = tile


def _modulation_and_na_bias(cond, ada_w, ada_b, na_rpb, rows_total):
    depth, heads, n_dr, n_dc = na_rpb.shape
    kh = NA_KH
    lo, hi = kh // 2, rows_total - kh // 2 - 1
    reps = list(range(lo)) + [lo] + list(range(hi + 1, rows_total))
    dr_first = tuple(int(np.clip(r - kh // 2, 0, rows_total - kh)) - r + kh - 1 for r in reps)
    n_dr_pad = -(-n_dr // HALO) * HALO
    lines = jnp.pad(na_rpb, ((0, 0), (0, 0), (0, n_dr_pad - n_dr), (0, LANES - n_dc)))
    mod_cols = 6 * D_MODEL // heads
    assert mod_cols % LANES == 0
    mod, bias = pl.pallas_call(
        functools.partial(_mod_bias_kernel, dr_first=dr_first, n_dr=n_dr),
        grid=(depth, heads),
        in_specs=[
            pl.BlockSpec((8, D_MODEL), lambda l, h: (0, 0)),
            pl.BlockSpec((None, D_MODEL, mod_cols), lambda l, h: (l, 0, h)),
            pl.BlockSpec((None, 1, mod_cols), lambda l, h: (l, 0, h)),
            pl.BlockSpec((None, None, n_dr_pad, LANES), lambda l, h: (l, h, 0, 0)),
        ],
        out_specs=[
            pl.BlockSpec((None, 8, mod_cols), lambda l, h: (l, 0, h)),
            pl.BlockSpec((None, len(reps), None, GRID_W, kh * GRID_W),
                         lambda l, h: (l, 0, h // 2, h % 2, 0)),
        ],
        out_shape=[
            jax.ShapeDtypeStruct((depth, 8, 6 * D_MODEL), F32),
            jax.ShapeDtypeStruct((depth, len(reps), HEAD_PAIRS, 2 * GRID_W, kh * GRID_W), F32),
        ],
        compiler_params=_cparams(2),
        name="modulation_and_na_bias",
    )(cond, ada_w, ada_b.reshape(depth, 1, 6 * D_MODEL), lines)
    return mod, bias, lo, hi


def _neighbourhood_attention(p, ctx_k, ctx_v, bias, layer, lo, hi, batch, seq):
    rows_total = seq // GRID_W
    past = ctx_k.shape[1]

    groups = rows_total // NA_ROWS
    return pl.pallas_call(
        functools.partial(_na_kernel, lo=lo, hi=hi),
        grid=(batch, groups),
        in_specs=[
            pl.BlockSpec((NA_ROWS * GRID_W, WIDTH), lambda b, g: (b * groups + g, 4)),
            pl.BlockSpec((seq, WIDTH), lambda b, g: (b, 5), pipeline_mode=pl.Buffered(1)),
            pl.BlockSpec((seq, WIDTH), lambda b, g: (b, 6), pipeline_mode=pl.Buffered(1)),
            pl.BlockSpec((1, past, WIDTH), lambda b, g: (b, 0, 0), pipeline_mode=pl.Buffered(1)),
            pl.BlockSpec((1, past, WIDTH), lambda b, g: (b, 0, 0), pipeline_mode=pl.Buffered(1)),
            pl.BlockSpec((None,) + bias.shape[1:], lambda b, g: (layer, 0, 0, 0, 0),
                         pipeline_mode=pl.Buffered(1)),
        ],
        out_specs=pl.BlockSpec((NA_ROWS * GRID_W, WIDTH), lambda b, g: (b * groups + g, 0)),
        out_shape=jax.ShapeDtypeStruct((batch * seq, WIDTH), BF16),
        compiler_params=_cparams(2),
        name="neighbourhood_attention",
    )(p, p, p, ctx_k, ctx_v, bias)


def _post_kernel(xm, xp, xn, am, ap, an, bm, bp, bn, mod_head, mod_tail, g_ref, wo, wu, cw, cb, wd,
                 o_ref, acc_ref, h_ref, hp_ref, x1_ref, y_ref, act_ref, *, seq_len, n_tiles):
    step = pl.program_id(0)
    tm = xm.shape[0]
    nj = tm // HALO
    tiles_per_seq = max(seq_len // tm, 1)
    assert (tm % seq_len == 0 and seq_len % nj == 0) or seq_len % tm == 0
    h_rows = h_ref.shape[1]
    seg_pitch = nj + HALO
    assert h_rows == 2 * HALO + HALO * seg_pitch and nj % HEAD_ROWS == 0
    n_ct = D_MODEL // LANES
    g = g_ref[...]

    def ext(main, prev, nxt):
        lo = prev[...].astype(F32)[prev.shape[0] - HALO:]
        hi = nxt[...].astype(F32)[:HALO]
        return jnp.concatenate([lo, main[...].astype(F32), hi], axis=0)

    chunks = [(c0, min(FF_CHUNK, D_FF - c0)) for c0 in range(0, D_FF, FF_CHUNK)]
    par = step % 2
    mod_h = mod_head[0]
    n_blocks = tm // HEAD_ROWS

    def head_matmul():
        mixed = jnp.concatenate([ext(am, ap, an), ext(bm, bp, bn)], axis=1).astype(BF16)
        y_ref[...] = _dot(mixed, wo[...])

    def head_rows(x, y):
        x1 = x + mod_h[2:3] * _rms(y, g[0:1])
        return x1, _rms(x1, g[1:2]) * (1.0 + mod_h[4:5]) + mod_h[3:4]

    def head_block(blk):
        if blk < n_blocks:
            rows = slice(blk * HEAD_ROWS, (blk + 1) * HEAD_ROWS)
            erows = slice(HALO + blk * HEAD_ROWS, HALO + (blk + 1) * HEAD_ROWS)
            x1, h = head_rows(xm[rows, :], y_ref[erows, :])
            x1_ref[par, rows, :] = x1
            t0 = blk * HEAD_ROWS
            hrow = HALO + (t0 // nj) * seg_pitch + t0 % nj
            for c in range(n_ct):
                h_ref[c, hrow:hrow + HEAD_ROWS, :] = h[:, c * LANES:(c + 1) * LANES]
        else:
            x = jnp.concatenate([xp[...], xn[...]], axis=0)
            y = jnp.concatenate([y_ref[0:HALO, :], y_ref[HALO + tm:2 * HALO + tm, :]], axis=0)
            _, h = head_rows(x, y)
            t = jnp.minimum(step, n_tiles - 1) % tiles_per_seq
            ridx = lax.broadcasted_iota(jnp.int32, h.shape, 0)
            keep = ((ridx >= HALO) | (t != 0)) & ((ridx < HALO) | (t != tiles_per_seq - 1))
            h = jnp.where(keep, h, 0.0)
            for c in range(n_ct):
                h_ref[c, 0:HALO, :] = h[:HALO, c * LANES:(c + 1) * LANES]
                h_ref[c, h_rows - HALO:h_rows, :] = h[HALO:, c * LANES:(c + 1) * LANES]

    def head_permute():
        def group(rows):
            return jnp.concatenate([h_ref[c, rows, :] for c in range(n_ct)], axis=1)
        def put(row0, first, second):
            hp_ref[row0:row0 + 2 * HALO, :] = jnp.concatenate(
                [group(first), group(second)], axis=0).astype(BF16)
        for j in range(0, nj, 2):
            put(j * HALO, pl.ds(HALO + j, HALO, stride=seg_pitch),
                pl.ds(HALO + j + 1, HALO, stride=seg_pitch))
        put(tm, pl.ds(0, HALO), pl.ds(h_rows - HALO, HALO))

    def tail(interleaved):
        mod = mod_tail[0]

        def conv(u, cols):
            w = cw[:, cols]
            sub = lax.broadcasted_iota(jnp.int32, (HALO, u.shape[1]), 0)
            before = jnp.where(sub == 0, u[tm + HALO - 1:tm + HALO],
                               pltpu.roll(u[tm - HALO:tm], 1, 0))
            after = jnp.where(sub == HALO - 1, u[tm + HALO:tm + HALO + 1],
                              pltpu.roll(u[0:HALO], HALO - 1, 0))
            for s in range(1, HALO):
                if (s * nj) % seq_len == 0:
                    before = jnp.where(sub == s, 0.0, before)
                    after = jnp.where(sub == s - 1, 0.0, after)
            prev = jnp.concatenate([before, u[0:tm - HALO]], axis=0)
            nxt = jnp.concatenate([u[HALO:tm], after], axis=0)
            return prev * w[0:1] + u[0:tm] * w[1:2] + nxt * w[2:3] + cb[:, cols]

        def cols(ch):
            ca = slice(chunks[ch][0], chunks[ch][0] + chunks[ch][1])
            return ca, slice(D_FF + ca.start, D_FF + ca.stop)

        def up(ch):
            ca, cg = cols(ch)
            return _dot(hp_ref[...], wu[:, cg]), _dot(hp_ref[...], wu[:, ca])

        nxt_u = up(0)
        for ch in range(len(chunks)):
            ug, ua = nxt_u
            if ch + 1 < len(chunks):
                nxt_u = up(ch + 1)
            for blk in interleaved[ch]:
                head_block(blk)
            ca, cg = cols(ch)
            act_ref[:, ca] = (_silu(conv(ug, cg)) * conv(ua, ca)).astype(BF16)
        ffn = mod[5:6] * _rms(_dot(act_ref[...], wd[...]), g[2:3])
        for c in range(n_ct):
            acc_ref[c] = ffn[:, c * LANES:(c + 1) * LANES]

        for s in range(HALO):
            for jb in range(nj // HALO):
                t0 = s * nj + HALO * jb
                rows = pl.ds(HALO * HALO * jb + s, HALO, stride=HALO)
                back = jnp.concatenate([acc_ref[c, rows, :] for c in range(n_ct)], axis=1)
                o_ref[t0:t0 + HALO, :] = x1_ref[1 - par, t0:t0 + HALO, :] + back

    @pl.when(step == 0)
    def _():
        head_matmul()
        for blk in range(n_blocks + 1):
            head_block(blk)
        head_permute()

    @pl.when(step > 0)
    def _():
        head_matmul()
        tail([[blk for blk in range(n_blocks + 1) if blk * len(chunks) // (n_blocks + 1) == ch]
              for ch in range(len(chunks))])
        head_permute()


def _post(x, mix_a, mix_b, mods, gains, wo, wu, cw, cb, wd, layer, seq_len, tm, latent):
    n = x.shape[0]
    nt = n // tm
    (mix_a, col_a), (mix_b, col_b) = mix_a, mix_b
    tiles_per_seq = max(seq_len // tm, 1)
    cond_row = (lambda i: 1 + i // tiles_per_seq) if latent else (lambda i: 0)

    head_tile = lambda i: jnp.minimum(i, nt - 1)
    tail_tile = lambda i: jnp.maximum(i - 1, 0)

    def triple(width, halo_rows, col=0):
        per = tm // halo_rows
        last = n // halo_rows - 1
        return [
            pl.BlockSpec((tm, width), lambda i: (head_tile(i), col)),
            pl.BlockSpec((halo_rows, width),
                         lambda i: (jnp.maximum(head_tile(i) * per - 1, 0), col)),
            pl.BlockSpec((halo_rows, width),
                         lambda i: (jnp.minimum((head_tile(i) + 1) * per, last), col)),
        ]

    const = lambda *shape: pl.BlockSpec((None,) + shape, lambda i: (layer,) + (0,) * len(shape),
                                        pipeline_mode=pl.Buffered(1))
    in_specs = (triple(D_MODEL, HALO) + triple(WIDTH, 2 * HALO, col_a)
                + triple(WIDTH, 2 * HALO, col_b) + [
        pl.BlockSpec((None, 1, 6, D_MODEL), lambda i: (layer, cond_row(head_tile(i)), 0, 0)),
        pl.BlockSpec((None, 1, 6, D_MODEL), lambda i: (layer, cond_row(tail_tile(i)), 0, 0)),
        const(3, D_MODEL),
        const(2 * WIDTH, D_MODEL),
        const(D_MODEL, 2 * D_FF),
        const(3, 2 * D_FF),
        const(1, 2 * D_FF),
        const(D_FF, D_MODEL),
    ])
    return pl.pallas_call(
        functools.partial(_post_kernel, seq_len=seq_len, n_tiles=nt),
        grid=(nt + 1,),
        in_specs=in_specs,
        out_specs=pl.BlockSpec((tm, D_MODEL), lambda i: (tail_tile(i), 0)),
        out_shape=jax.ShapeDtypeStruct((n, D_MODEL), F32),
        scratch_shapes=[pltpu.VMEM((D_MODEL // LANES, tm, LANES), F32),
                        pltpu.VMEM((D_MODEL // LANES, 2 * HALO + HALO * (tm // HALO + HALO), LANES),
                                   F32),
                        pltpu.VMEM((tm + 2 * HALO, D_MODEL), BF16),
                        pltpu.VMEM((2, tm, D_MODEL), F32),
                        pltpu.VMEM((tm + 2 * HALO, D_MODEL), F32),
                        pltpu.VMEM((tm, D_FF), BF16)],
        compiler_params=_cparams(1),
        name="post_latent" if latent else "post_context",
    )(x, x, x, mix_a, mix_a, mix_a, mix_b, mix_b, mix_b, mods, mods, gains, wo, wu, cw, cb, wd)


def kernel(x_prompt, x_sample, c, cache_na_k, cache_na_v, state_ret_fwd, state_ret_bwd, c_ctx,
           ada_w, ada_b, g_pre_mix, g_post_mix, g_pre_ffn, g_post_ffn, w_in,
           ret_decay_fwd, ret_decay_bwd, na_rpb, w_out, w_up, conv_w, conv_b, w_down):
    depth = w_in.shape[0]
    batch, seq, _ = x_prompt.shape
    dec_batch, dec_seq, _ = x_sample.shape
    past = cache_na_k.shape[2]
    tm = 512

    cond = jnp.concatenate(
        [c_ctx[None, :], c, jnp.zeros((8 - 1 - dec_batch, D_MODEL), F32)], axis=0)
    mods, bias, lo, hi = _modulation_and_na_bias(cond, ada_w, ada_b, na_rpb, dec_seq // GRID_W)
    mods = mods.reshape(depth, 8, 6, D_MODEL)
    rope_tabs = _rope_tables(dec_seq)

    w_in_b = w_in.astype(BF16)
    g_pre = g_pre_mix.reshape(depth, 1, D_MODEL)
    gains = jnp.stack([g_post_mix, g_pre_ffn, g_post_ffn], axis=1)
    cb = conv_b.reshape(depth, 1, 2 * D_FF)

    y_p = x_prompt.reshape(batch * seq, D_MODEL)
    y_s = x_sample.reshape(dec_batch * dec_seq, D_MODEL)
    new_k = new_v = None
    sfs, sbs = [], []
    for l in range(depth):
        dec = (jnp.repeat(ret_decay_fwd[l], HEAD_DIM)[None, :],
               jnp.repeat(ret_decay_bwd[l], HEAD_DIM)[None, :],
               jnp.repeat(ret_decay_fwd[l], LANES)[None, :],
               jnp.repeat(ret_decay_bwd[l], LANES)[None, :])

        p_c, new_k, new_v = _inproj_context(y_p, mods, g_pre, w_in_b, l, seq, tm, (new_k, new_v))
        p_s, mix_c, st_c, *cast = _inproj_latent_and_ctx_mixer(
            y_s, mods, g_pre, w_in_b, l, dec_seq, tm, rope_tabs, p_c, dec, batch, seq,
            cast=(w_out, w_up, w_down) if l == 0 else ())
        if l == 0:
            post_params = (mods, gains, cast[0], cast[1], conv_w, cb, cast[2])
        y_p = _post(y_p, (mix_c, 0), (mix_c, 1), *post_params, l, seq, tm, latent=False)
        sfs.append(_diag_states(st_c[:, :, 0]))
        sbs.append(_diag_states(st_c[:, :, 1]))

        s0 = _block_diag_states(state_ret_fwd[:, l], state_ret_bwd[:, l])
        ret_s = _lat_retention(p_s, dec, s0, dec_batch, dec_seq)
        ck = cache_na_k[:, l].reshape(dec_batch, past, WIDTH).astype(BF16)
        cv = cache_na_v[:, l].reshape(dec_batch, past, WIDTH).astype(BF16)
        na_s = _neighbourhood_attention(p_s, ck, cv, bias, l, lo, hi, dec_batch, dec_seq)
        y_s = _post(y_s, (ret_s, 0), (na_s, 0), *post_params, l, dec_seq, tm, latent=True)

    return (y_p.reshape(batch, seq, D_MODEL),
            y_s.reshape(dec_batch, dec_seq, D_MODEL),
            new_k.reshape(batch, depth, seq, N_HEADS, HEAD_DIM),
            new_v.reshape(batch, depth, seq, N_HEADS, HEAD_DIM),
            jnp.stack(sfs, axis=1), jnp.stack(sbs, axis=1))
```

```python
import functools

import numpy as np
import jax
import jax.numpy as jnp
from jax import lax
from jax.experimental import pallas as pl
from jax.experimental.pallas import tpu as pltpu

F32 = jnp.float32
BF16 = jnp.bfloat16

D_MODEL = 1024
HEAD_DIM = 64
N_HEADS = 8
HEAD_PAIRS = N_HEADS // 2
LANES = 128
WIDTH = N_HEADS * HEAD_DIM
N_GROUPS = 7
IN_WIDTH = N_GROUPS * WIDTH
D_FF = 2816
FF_CHUNK = 768
HEAD_ROWS = 32
CHUNK = 128
RET_UNROLL = 8
GRID_W = 64
NA_KH = 8
NA_KW = 16
NA_STAGE_ROWS = 8
NA_ROWS = 8
ROPE_BASE = 10000.0
EPS = 1e-6
NEG_INF = -1e9
HALO = 8
VMEM_LIMIT = 56 * 1024 * 1024


def _cparams(n_grid):
    return pltpu.CompilerParams(
        dimension_semantics=("arbitrary",) * n_grid, vmem_limit_bytes=VMEM_LIMIT)


def _rms(x, g):
    ms = jnp.mean(x * x, axis=-1, keepdims=True)
    return x * lax.rsqrt(ms + EPS) * g


def _silu(x):
    return x * jax.nn.sigmoid(x)


def _log_sigmoid(x):
    return jnp.minimum(x, 0.0) - jnp.log1p(jnp.exp(-jnp.abs(x)))


def _dot(a, b):
    return jnp.dot(a, b, preferred_element_type=F32)


def _dot_nt(a, b):
    return lax.dot_general(a, b, (((1,), (1,)), ((), ())), preferred_element_type=F32)


def _dot_tn(a, b):
    return lax.dot_general(a, b, (((0,), (0,)), ((), ())), preferred_element_type=F32)


def _cast_streams(weights, n_steps, flat_step=lambda i: i):
    arrays = [w.reshape(-1, w.shape[-1]) for w in weights]
    in_specs, out_specs, out_shapes = [], [], []
    for a in arrays:
        rows = a.shape[0] // n_steps
        assert rows * n_steps == a.shape[0] and rows % (2 * HALO) == 0
        in_specs.append(pl.BlockSpec((rows, a.shape[1]), lambda *idx: (flat_step(*idx), 0)))
        out_specs.append(pl.BlockSpec((rows, a.shape[1]), lambda *idx: (flat_step(*idx), 0)))
        out_shapes.append(jax.ShapeDtypeStruct(a.shape, BF16))
    return arrays, in_specs, out_specs, out_shapes


def _cast_blocks(in_refs, out_refs):
    for src, dst in zip(in_refs, out_refs):
        dst[...] = src[...].astype(dst.dtype)


def _inproj_kernel(*refs, n_alias, layer):
    outs = refs[4 + n_alias:]
    for start, finish in _inproj_groups(*refs[:4], None, None, None, outs[0], outs[1:], n_alias,
                                        layer):
        finish(start())


def _inproj_groups(x_ref, mod_ref, g_ref, w_ref, cos_ref, sin_up_ref, sin_dn_ref, p_ref, kv_refs,
                   n_alias, layer, width=WIDTH):
    rope = cos_ref is not None
    emit_kv = len(kv_refs) > 0
    assert WIDTH % width == 0 and (width == WIDTH or not emit_kv)
    mod = mod_ref[0]
    h = (_rms(x_ref[...], g_ref[...]) * (1.0 + mod[1:2]) + mod[0:1]).astype(BF16)

    def start(n):
        return _dot(h, w_ref[:, n * width:(n + 1) * width])

    def finish(n, pg):
        cols = slice(n * width, (n + 1) * width)
        g = n * width // WIDTH
        if rope and g < 2:
            parts = []
            for j in range(width // LANES):
                xj = pg[:, j * LANES:(j + 1) * LANES]
                parts.append(xj * cos_ref[...]
                             + pltpu.roll(xj, 16, 1) * sin_up_ref[...]
                             + pltpu.roll(xj, LANES - 16, 1) * sin_dn_ref[...])
            pg = jnp.concatenate(parts, axis=1)
        if g in (0, 4):
            pg = pg * (HEAD_DIM ** -0.5)
        p_ref[:, cols] = pg.astype(BF16)
        if emit_kv and g >= 5:
            kv_ref = kv_refs[g - 5]
            seq = kv_ref.shape[-2]
            for j in range(kv_ref.shape[0]):
                if n_alias:
                    kv_ref[j] = pg[j * seq:(j + 1) * seq]
                else:
                    for l in range(kv_ref.shape[1]):
                        kv_ref[j, l] = (pg[j * seq:(j + 1) * seq] if l == layer
                                        else jnp.zeros((seq, WIDTH), F32))

    return [(functools.partial(start, n), functools.partial(finish, n))
            for n in range(IN_WIDTH // width)]


def _inproj_context(x, mods, g_pre, w, layer, seq_len, tm, kv_out):
    n = x.shape[0]
    depth = w.shape[0]
    in_specs = [
        pl.BlockSpec((tm, D_MODEL), lambda i: (i, 0)),
        pl.BlockSpec((None, 1, 6, D_MODEL), lambda i: (layer, 0, 0, 0)),
        pl.BlockSpec((None, 1, D_MODEL), lambda i: (layer, 0, 0)),
        pl.BlockSpec((None, D_MODEL, IN_WIDTH), lambda i: (layer, 0, 0),
                     pipeline_mode=pl.Buffered(1)),
    ]
    args = [x, mods, g_pre, w]
    out_specs = [pl.BlockSpec((tm, IN_WIDTH), lambda i: (i, 0))]
    out_shape = [jax.ShapeDtypeStruct((n, IN_WIDTH), BF16)]
    aliases = {}
    seqs_per_tile = tm // seq_len
    kv_shape = jax.ShapeDtypeStruct((n // seq_len, depth, seq_len, WIDTH), F32)
    assert (kv_out[0] is None) == (kv_out[1] is None)
    for j, prev in enumerate(kv_out):
        if prev is not None:
            aliases[len(args)] = 1 + j
            in_specs.append(pl.BlockSpec(memory_space=pl.ANY))
            args.append(prev)
            out_specs.append(pl.BlockSpec((seqs_per_tile, None, seq_len, WIDTH),
                                          lambda i: (i, layer, 0, 0)))
        else:
            out_specs.append(pl.BlockSpec((seqs_per_tile, depth, seq_len, WIDTH),
                                          lambda i: (i, 0, 0, 0)))
        out_shape.append(kv_shape)
    return pl.pallas_call(
        functools.partial(_inproj_kernel, n_alias=len(aliases), layer=layer),
        grid=(n // tm,),
        in_specs=in_specs,
        out_specs=out_specs,
        out_shape=out_shape,
        input_output_aliases=aliases,
        compiler_params=_cparams(1),
        name="inproj_context",
    )(*args)


def _rope_tables(seq_len):
    t = np.arange(seq_len)
    lane = np.arange(LANES)
    d = lane % HEAD_DIM
    pos = np.where(d[None, :] < HEAD_DIM // 2, (t // GRID_W)[:, None], (t % GRID_W)[:, None])
    pos = pos.astype(np.float32)
    half = HEAD_DIM // 2
    inv = np.power(np.float32(ROPE_BASE), -np.arange(0, half, 2, dtype=np.float32) / half)
    ang = pos * inv[d % (half // 2)][None, :]
    cos, sin = np.cos(ang), np.sin(ang)
    upper = (d % half) >= half // 2
    sin_up = np.where(upper[None, :], sin, 0.0)
    sin_dn = np.where(upper[None, :], 0.0, -sin)
    return (jnp.asarray(cos, F32), jnp.asarray(sin_up, F32), jnp.asarray(sin_dn, F32))


def _retention_tables(dec_f, dec_b, dec_f2, dec_b2):
    lgf, lgb = _log_sigmoid(dec_f), _log_sigmoid(dec_b)
    pos = lax.broadcasted_iota(jnp.int32, (CHUNK, LANES), 0).astype(F32)
    tabs = dict(
        qdf=jnp.exp(lgf * (pos + 1.0)), kdf=jnp.exp(lgf * (CHUNK - 1.0 - pos)),
        cdf=jnp.exp(lgf * float(CHUNK)),
        qdb=jnp.exp(lgb * (CHUNK - pos)), kdb=jnp.exp(lgb * pos),
        cdb=jnp.exp(lgb * float(CHUNK)),
    )
    lgf2, lgb2 = _log_sigmoid(dec_f2), _log_sigmoid(dec_b2)
    i = lax.broadcasted_iota(jnp.int32, (CHUNK, 2 * CHUNK), 0)
    j = lax.broadcasted_iota(jnp.int32, (CHUNK, 2 * CHUNK), 1) & (CHUNK - 1)
    diff = (i - j).astype(F32)
    tabs["decay"] = (jnp.where(diff >= 0, jnp.exp(lgf2 * jnp.maximum(diff, 0.0)), 0.0)
                     + jnp.where(diff <= 0, jnp.exp(lgb2 * jnp.maximum(-diff, 0.0)), 0.0))
    lane = lax.broadcasted_iota(jnp.int32, (1, LANES), 1)
    tabs["head_a"] = lane < HEAD_DIM
    r = lax.broadcasted_iota(jnp.int32, (2 * LANES, LANES), 0) & (LANES - 1)
    c = lax.broadcasted_iota(jnp.int32, (2 * LANES, LANES), 1)
    tabs["same_head"] = (r < HEAD_DIM) == (c < HEAD_DIM)
    return tabs


def _split_heads(x, head_a, axis):
    zero = jnp.zeros_like(x)
    return jnp.concatenate([jnp.where(head_a, x, zero), jnp.where(head_a, zero, x)], axis=axis)


def _chunk_kv(k2, v2, tabs):
    kf = k2.astype(F32)
    kk = jnp.concatenate([kf * tabs["kdf"], kf * tabs["kdb"]], axis=1).astype(BF16)
    return jnp.where(tabs["same_head"], _dot_tn(kk, v2), 0.0)


def _chunk_mix(q2, k2, v2, state, tabs):
    head_a = tabs["head_a"]
    s = _dot_nt(q2, _split_heads(k2, head_a, 0))
    p = (s * tabs["decay"]).astype(BF16)
    qf = q2.astype(F32)
    lhs = jnp.concatenate(
        [p, (qf * tabs["qdf"]).astype(BF16), (qf * tabs["qdb"]).astype(BF16)], axis=1)
    rhs = jnp.concatenate([_split_heads(v2, head_a, 0), state], axis=0)
    return _dot(lhs, rhs)


def _chunk_norm_gate(o, g2, tabs):
    head_a = tabs["head_a"]
    inv = 1.0 / HEAD_DIM
    sum_a = jnp.sum(jnp.where(head_a, o, 0.0), axis=-1, keepdims=True)
    sum_b = jnp.sum(jnp.where(head_a, 0.0, o), axis=-1, keepdims=True)
    d = o - jnp.where(head_a, sum_a, sum_b) * inv
    d2 = d * d
    var_a = jnp.sum(jnp.where(head_a, d2, 0.0), axis=-1, keepdims=True)
    var_b = jnp.sum(jnp.where(head_a, 0.0, d2), axis=-1, keepdims=True)
    o = d * lax.rsqrt(jnp.where(head_a, var_a, var_b) * inv + EPS)
    return o * _silu(g2.astype(F32))


def _ctx_mixer_units(rq, rk, rv, rg, nq, nk, nv, df, db, df2, db2, mix_ref, fwd_ref, bwd_ref,
                     layer, state_layers):
    n_seq = fwd_ref.shape[0]
    seq = rq.shape[0] // n_seq
    nc = seq // CHUNK

    thunks = []
    for hp in range(HEAD_PAIRS):
        cols = slice(hp * LANES, (hp + 1) * LANES)
        cols2 = slice(hp * 2 * LANES, (hp + 1) * 2 * LANES)
        shared = {}

        def scan(hp=hp, cols=cols, cols2=cols2, shared=shared):
            tabs = _retention_tables(df[:, cols], db[:, cols], df2[:, cols2], db2[:, cols2])
            shared["tabs"] = tabs
            for b in range(n_seq):
                rows = [slice(b * seq + c * CHUNK, b * seq + (c + 1) * CHUNK) for c in range(nc)]
                kv = [_chunk_kv(rk[r, cols], rv[r, cols], tabs) for r in rows]
                sf = [jnp.zeros((LANES, LANES), F32)]
                for c in range(nc):
                    sf.append(sf[-1] * tabs["cdf"] + kv[c][:LANES])
                sb = [jnp.zeros((LANES, LANES), F32)]
                for c in reversed(range(nc)):
                    sb.append(sb[-1] * tabs["cdb"] + kv[c][LANES:])
                for ref, s in ((fwd_ref, sf[nc]), (bwd_ref, sb[nc])):
                    blocks = (s[:HEAD_DIM, :HEAD_DIM],
                              pltpu.roll(s[HEAD_DIM:], HEAD_DIM, 1)[:, :HEAD_DIM])
                    for h, blk in enumerate(blocks):
                        if state_layers is None:
                            ref[b, 2 * hp + h] = blk
                        else:
                            for l in range(state_layers):
                                ref[b, l, 2 * hp + h] = blk if l == layer else jnp.zeros_like(blk)
                shared[b] = [jnp.concatenate([sf[c], sb[nc - 1 - c]], axis=0).astype(BF16)
                             for c in range(nc)]

        def chunk_out(b, c, cols=cols, shared=shared):
            rows = slice(b * seq + c * CHUNK, b * seq + (c + 1) * CHUNK)
            tabs = shared["tabs"]
            o = _chunk_mix(rq[rows, cols], rk[rows, cols], rv[rows, cols], shared[b][c], tabs)
            mix_ref[rows, cols] = _chunk_norm_gate(o, rg[rows, cols], tabs).astype(BF16)

        def attention(b, hp=hp, cols=cols, shared=shared):
            head_a = shared["tabs"]["head_a"]
            tok = slice(b * seq, (b + 1) * seq)
            s = _dot_nt(nq[tok, cols], _split_heads(nk[tok, cols], head_a, 0))
            es, rinv = [], []
            for h in range(2):
                sh = s[:, h * seq:(h + 1) * seq]
                e = jnp.exp(sh - jnp.max(sh, axis=-1, keepdims=True))
                rinv.append(1.0 / jnp.sum(e, axis=-1, keepdims=True))
                es.append(e.astype(BF16))
            o = _dot(jnp.concatenate(es, axis=1), _split_heads(nv[tok, cols], head_a, 0))
            o = o * jnp.where(head_a, rinv[0], rinv[1])
            mix_ref[tok, WIDTH + hp * LANES:WIDTH + (hp + 1) * LANES] = o.astype(BF16)

        thunks.append(scan)
        for b in range(n_seq):
            thunks += [functools.partial(chunk_out, b, c) for c in range(nc)]
            thunks.append(functools.partial(attention, b))
    return thunks


def _inproj_mix_kernel(*refs, n_cast, n_alias, layer, state_layers):
    n_proj, n_mix = 7, 11
    outs = refs[n_proj + n_mix + n_cast + n_alias:]
    _cast_blocks(refs[n_proj + n_mix:n_proj + n_mix + n_cast], outs[4:])
    groups = _inproj_groups(*refs[:n_proj], outs[0], (), n_alias=0, layer=0, width=2 * LANES)
    units = _ctx_mixer_units(*refs[n_proj:n_proj + n_mix], outs[1], outs[2], outs[3],
                             layer, state_layers)
    done = 0
    for u, unit in enumerate(units):
        upto = (u + 1) * len(groups) // len(units)
        started = [(finish, start()) for start, finish in groups[done:upto]]
        done = upto
        unit()
        for finish, result in started:
            finish(result)


def _inproj_latent_and_ctx_mixer(x, mods, g_pre, w, layer, seq_len, tm, rope_tabs,
                                 p_ctx, dec, batch, seq, states, cast=()):
    n = x.shape[0]
    depth = w.shape[0]
    n_tiles = n // tm
    cast_arrays, cast_in, cast_out, cast_shapes = _cast_streams(cast, n_tiles)
    n_seq = batch // n_tiles
    assert n_seq * n_tiles == batch and seq_len % tm == 0
    assert (states[0] is None) == (states[1] is None)
    creates = states[0] is None
    tiles_per_seq = seq_len // tm
    group = lambda g: pl.BlockSpec((n_seq * seq, WIDTH), lambda i, g=g: (i, g))
    vec = lambda width: pl.BlockSpec((1, width), lambda i: (0, 0))
    in_specs = [
        pl.BlockSpec((tm, D_MODEL), lambda i: (i, 0)),
        pl.BlockSpec((None, 1, 6, D_MODEL), lambda i: (layer, 1 + i // tiles_per_seq, 0, 0)),
        pl.BlockSpec((None, 1, D_MODEL), lambda i: (layer, 0, 0)),
        pl.BlockSpec((None, D_MODEL, IN_WIDTH), lambda i: (layer, 0, 0),
                     pipeline_mode=pl.Buffered(1)),
    ] + [pl.BlockSpec((tm, LANES), lambda i: (i % tiles_per_seq, 0))] * 3 + [
        group(g) for g in range(N_GROUPS)] + [vec(WIDTH), vec(WIDTH), vec(2 * WIDTH),
                                              vec(2 * WIDTH)]
    args = [x, mods, g_pre, w, *rope_tabs, *([p_ctx] * N_GROUPS), *dec, *cast_arrays]
    in_specs = in_specs + cast_in
    aliases = {}
    if creates:
        state_spec = pl.BlockSpec((n_seq, depth, N_HEADS, HEAD_DIM, HEAD_DIM),
                                  lambda i: (i, 0, 0, 0, 0))
    else:
        state_spec = pl.BlockSpec((n_seq, None, N_HEADS, HEAD_DIM, HEAD_DIM),
                                  lambda i: (i, layer, 0, 0, 0))
        for j, prev in enumerate(states):
            aliases[len(args)] = 2 + j
            in_specs.append(pl.BlockSpec(memory_space=pl.ANY))
            args.append(prev)
    state_shape = jax.ShapeDtypeStruct((batch, depth, N_HEADS, HEAD_DIM, HEAD_DIM), F32)
    results = pl.pallas_call(
        functools.partial(_inproj_mix_kernel, n_cast=len(cast), n_alias=len(aliases),
                          layer=layer, state_layers=depth if creates else None),
        grid=(n_tiles,),
        in_specs=in_specs,
        out_specs=[
            pl.BlockSpec((tm, IN_WIDTH), lambda i: (i, 0)),
            pl.BlockSpec((n_seq * seq, 2 * WIDTH), lambda i: (i, 0)),
            state_spec, state_spec,
        ] + cast_out,
        out_shape=[
            jax.ShapeDtypeStruct((n, IN_WIDTH), BF16),
            jax.ShapeDtypeStruct((batch * seq, 2 * WIDTH), BF16),
            state_shape, state_shape,
        ] + cast_shapes,
        input_output_aliases=aliases,
        compiler_params=_cparams(1),
        name="inproj_latent_ctx_mixer",
    )(*args)
    return list(results[:4]) + [r.reshape(w.shape) for r, w in zip(results[4:], cast)]


def _lat_retention_kernel(q_ref, k_ref, v_ref, g_ref, df, db, df2, db2, s0_ref, o_ref,
                          kv_ref, st_ref):
    nc = q_ref.shape[0] // CHUNK
    tabs = _retention_tables(df[...], db[...], df2[...], db2[...])

    def rows(c):
        return pl.ds(pl.multiple_of(c * CHUNK, CHUNK), CHUNK)

    def kv_body(c, carry):
        kv_ref[c] = _chunk_kv(k_ref[rows(c), :], v_ref[rows(c), :], tabs)
        return carry
    lax.fori_loop(0, nc, kv_body, 0, unroll=RET_UNROLL)

    def fwd_body(c, s):
        st_ref[c, :LANES, :] = s.astype(BF16)
        return s * tabs["cdf"] + kv_ref[c, :LANES, :]
    lax.fori_loop(0, nc, fwd_body, s0_ref[0, 0, 0])

    def bwd_body(i, s):
        c = nc - 1 - i
        st_ref[c, LANES:, :] = s.astype(BF16)
        return s * tabs["cdb"] + kv_ref[c, LANES:, :]
    lax.fori_loop(0, nc, bwd_body, s0_ref[0, 0, 1])

    def out_body(c, carry):
        r = rows(c)
        o = _chunk_mix(q_ref[r, :], k_ref[r, :], v_ref[r, :], st_ref[c], tabs)
        o_ref[r, :] = _chunk_norm_gate(o, g_ref[r, :], tabs).astype(BF16)
        return carry
    lax.fori_loop(0, nc, out_body, 0, unroll=RET_UNROLL)


def _lat_retention(p, dec, s0, batch, seq):
    nc = seq // CHUNK
    group = lambda g: pl.BlockSpec((seq, LANES), lambda b, h, g=g: (b, g * HEAD_PAIRS + h))
    vec = lambda w: pl.BlockSpec((1, w), lambda b, h: (0, h))
    return pl.pallas_call(
        _lat_retention_kernel,
        grid=(batch, HEAD_PAIRS),
        in_specs=[group(g) for g in range(4)] + [vec(LANES), vec(LANES), vec(2 * LANES),
                                                 vec(2 * LANES)]
        + [pl.BlockSpec((1, 1, 2, LANES, LANES), lambda b, h: (b, h, 0, 0, 0))],
        out_specs=pl.BlockSpec((seq, LANES), lambda b, h: (b, h)),
        out_shape=jax.ShapeDtypeStruct((batch * seq, WIDTH), BF16),
        scratch_shapes=[pltpu.VMEM((nc, 2 * LANES, LANES), F32),
                        pltpu.VMEM((nc, 2 * LANES, LANES), BF16)],
        compiler_params=_cparams(2),
        name="latent_retention",
    )(*([p] * 4), *dec, s0)


def _block_diag_states(s_f, s_b):
    def bd(s):
        b = s.shape[0]
        s = s.reshape(b, HEAD_PAIRS, 2, HEAD_DIM, HEAD_DIM)
        z = jnp.zeros_like(s[:, :, 0])
        top = jnp.concatenate([s[:, :, 0], z], axis=-1)
        bot = jnp.concatenate([z, s[:, :, 1]], axis=-1)
        return jnp.concatenate([top, bot], axis=-2)
    return jnp.stack([bd(s_f), bd(s_b)], axis=2)


def _na_kernel(q_ref, k_ref, v_ref, ck_ref, cv_ref, bias_ref, o_ref, *, lo, hi):
    rows_total = k_ref.shape[0] // GRID_W
    lane = lax.broadcasted_iota(jnp.int32, (1, LANES), 1)
    head_a = lane < HEAD_DIM
    wins, cls = [], []
    for i in range(NA_ROWS):
        r = pl.program_id(1) * NA_ROWS + i
        rs = jnp.clip(r - NA_KH // 2, 0, rows_total - NA_KH)
        wins.append(pl.ds(pl.multiple_of(rs * GRID_W, GRID_W), NA_KH * GRID_W))
        cls.append(jnp.where(r < lo, r, jnp.where(r > hi, r - hi + lo, lo)))
    stages = [(hp, r0) for hp in range(HEAD_PAIRS) for r0 in range(0, NA_ROWS, NA_STAGE_ROWS)]

    def scores(stage):
        hp, r0 = stage
        cols = slice(hp * LANES, (hp + 1) * LANES)
        qq = jnp.concatenate(
            [_split_heads(q_ref[i * GRID_W:(i + 1) * GRID_W, cols], head_a, 0)
             for i in range(r0, r0 + NA_STAGE_ROWS)], axis=0)
        s_ctx = _dot_nt(qq, ck_ref[0, :, cols])
        s_loc = [_dot_nt(qq[j * 2 * GRID_W:(j + 1) * 2 * GRID_W], k_ref[wins[r0 + j], cols])
                 for j in range(NA_STAGE_ROWS)]
        return s_ctx, s_loc

    def softmax(stage, s_ctx, s_loc):
        hp, r0 = stage
        m_ctx = jnp.max(s_ctx, axis=-1, keepdims=True)
        m = []
        for j in range(NA_STAGE_ROWS):
            blk = slice(j * 2 * GRID_W, (j + 1) * 2 * GRID_W)
            s_loc[j] = s_loc[j] + bias_ref[cls[r0 + j], hp]
            m.append(jnp.maximum(jnp.max(s_loc[j], axis=-1, keepdims=True), m_ctx[blk]))
        e_ctx = jnp.exp(s_ctx - jnp.concatenate(m, axis=0))
        den_ctx = jnp.sum(e_ctx, axis=-1, keepdims=True)
        e_loc, den = [], []
        for j in range(NA_STAGE_ROWS):
            blk = slice(j * 2 * GRID_W, (j + 1) * 2 * GRID_W)
            e = jnp.exp(s_loc[j] - m[j])
            den.append(jnp.sum(e, axis=-1, keepdims=True) + den_ctx[blk])
            e_loc.append(e.astype(BF16))
        return e_ctx.astype(BF16), e_loc, den

    def weighted_values(stage, e_ctx, e_loc, den):
        hp, r0 = stage
        cols = slice(hp * LANES, (hp + 1) * LANES)
        o_ctx = _dot(e_ctx, cv_ref[0, :, cols])
        for j in range(NA_STAGE_ROWS):
            i = r0 + j
            blk = slice(j * 2 * GRID_W, (j + 1) * 2 * GRID_W)
            o = (_dot(e_loc[j], v_ref[wins[i], cols]) + o_ctx[blk]) / den[j]
            o_ref[i * GRID_W:(i + 1) * GRID_W, cols] = jnp.where(
                head_a, o[:GRID_W], o[GRID_W:]).astype(BF16)

    nxt = scores(stages[0])
    pending = None
    for n, stage in enumerate(stages):
        s_ctx, s_loc = nxt
        if n + 1 < len(stages):
            nxt = scores(stages[n + 1])
        probs = softmax(stage, s_ctx, s_loc)
        if pending is not None:
            weighted_values(*pending)
        pending = (stage,) + probs
    weighted_values(*pending)


def _mod_bias_kernel(cond_ref, w_ref, b_ref, rpb_ref, win_ref, mod_ref, bias_ref, win_bf_ref, *,
                     dr_first, n_dr):
    s = _silu(cond_ref[...]).astype(BF16)
    mod_ref[...] = _dot(s, w_ref[...].astype(BF16)) + b_ref[...]
    _cast_blocks([win_ref], [win_bf_ref])

    qc = lax.broadcasted_iota(jnp.int32, (GRID_W, LANES), 0)
    lane = lax.broadcasted_iota(jnp.int32, (GRID_W, LANES), 1)
    kc = lane & (GRID_W - 1)
    cs = jnp.clip(qc - NA_KW // 2, 0, GRID_W - NA_KW)
    inside = (kc >= cs) & (kc < cs + NA_KW)
    first = lane < GRID_W
    lo_half, hi_half = [], []
    for dr in range(n_dr):
        line = jnp.broadcast_to(rpb_ref[pl.ds(dr, 1), :], (GRID_W, LANES))
        lo_half.append(pltpu.roll(line, LANES - (NA_KW - 1), 1, stride=1, stride_axis=0))
        hi_half.append(pltpu.roll(line, GRID_W - (NA_KW - 1), 1, stride=1, stride_axis=0))
    for cls, dr0 in enumerate(dr_first):
        for jp in range(NA_KH // 2):
            dr = dr0 + 2 * jp
            tile = jnp.where(inside, jnp.where(first, lo_half[dr], hi_half[dr + 1]), NEG_INF)
            bias_ref[cls, :, jp * LANES:(jp + 1) * LANES] = tile


def _modulation_and_na_bias(cond, ada_w, ada_b, na_rpb, rows_total, w_in):
    depth, heads, n_dr, n_dc = na_rpb.shape
    kh = NA_KH
    lo, hi = kh // 2, rows_total - kh // 2 - 1
    reps = list(range(lo)) + [lo] + list(range(hi + 1, rows_total))
    dr_first = tuple(int(np.clip(r - kh // 2, 0, rows_total - kh)) - r + kh - 1 for r in reps)
    n_dr_pad = -(-n_dr // HALO) * HALO
    lines = jnp.pad(na_rpb, ((0, 0), (0, 0), (0, n_dr_pad - n_dr), (0, LANES - n_dc)))
    mod_cols = 6 * D_MODEL // heads
    assert mod_cols % LANES == 0
    cast_arrays, cast_in, cast_out, cast_shapes = _cast_streams(
        (w_in,), depth * heads, lambda l, h: l * heads + h)
    mod, bias, w_in_b = pl.pallas_call(
        functools.partial(_mod_bias_kernel, dr_first=dr_first, n_dr=n_dr),
        grid=(depth, heads),
        in_specs=[
            pl.BlockSpec((8, D_MODEL), lambda l, h: (0, 0)),
            pl.BlockSpec((None, D_MODEL, mod_cols), lambda l, h: (l, 0, h)),
            pl.BlockSpec((None, 1, mod_cols), lambda l, h: (l, 0, h)),
            pl.BlockSpec((None, None, n_dr_pad, LANES), lambda l, h: (l, h, 0, 0)),
        ] + cast_in,
        out_specs=[
            pl.BlockSpec((None, 8, mod_cols), lambda l, h: (l, 0, h)),
            pl.BlockSpec((None, len(reps), None, GRID_W, kh * GRID_W),
                         lambda l, h: (l, 0, h // 2, h % 2, 0)),
        ] + cast_out,
        out_shape=[
            jax.ShapeDtypeStruct((depth, 8, 6 * D_MODEL), F32),
            jax.ShapeDtypeStruct((depth, len(reps), HEAD_PAIRS, 2 * GRID_W, kh * GRID_W), F32),
        ] + cast_shapes,
        compiler_params=_cparams(2),
        name="modulation_and_na_bias",
    )(cond, ada_w, ada_b.reshape(depth, 1, 6 * D_MODEL), lines, *cast_arrays)
    return mod, bias, lo, hi, w_in_b.reshape(w_in.shape)


def _neighbourhood_attention(p, ctx_k, ctx_v, bias, layer, lo, hi, batch, seq):
    rows_total = seq // GRID_W
    past = ctx_k.shape[1]

    groups = rows_total // NA_ROWS
    return pl.pallas_call(
        functools.partial(_na_kernel, lo=lo, hi=hi),
        grid=(batch, groups),
        in_specs=[
            pl.BlockSpec((NA_ROWS * GRID_W, WIDTH), lambda b, g: (b * groups + g, 4)),
            pl.BlockSpec((seq, WIDTH), lambda b, g: (b, 5), pipeline_mode=pl.Buffered(1)),
            pl.BlockSpec((seq, WIDTH), lambda b, g: (b, 6), pipeline_mode=pl.Buffered(1)),
            pl.BlockSpec((1, past, WIDTH), lambda b, g: (b, 0, 0), pipeline_mode=pl.Buffered(1)),
            pl.BlockSpec((1, past, WIDTH), lambda b, g: (b, 0, 0), pipeline_mode=pl.Buffered(1)),
            pl.BlockSpec((None,) + bias.shape[1:], lambda b, g: (layer, 0, 0, 0, 0),
                         pipeline_mode=pl.Buffered(1)),
        ],
        out_specs=pl.BlockSpec((NA_ROWS * GRID_W, WIDTH), lambda b, g: (b * groups + g, 0)),
        out_shape=jax.ShapeDtypeStruct((batch * seq, WIDTH), BF16),
        compiler_params=_cparams(2),
        name="neighbourhood_attention",
    )(p, p, p, ctx_k, ctx_v, bias)


def _post_kernel(xm, xp, xn, am, ap, an, bm, bp, bn, mod_head, mod_tail, g_ref, wo, wu, cw, cb, wd,
                 o_ref, acc_ref, h_ref, hp_ref, x1_ref, y_ref, act_ref, *, seq_len, n_tiles):
    step = pl.program_id(0)
    tm = xm.shape[0]
    nj = tm // HALO
    tiles_per_seq = max(seq_len // tm, 1)
    assert (tm % seq_len == 0 and seq_len % nj == 0) or seq_len % tm == 0
    h_rows = h_ref.shape[1]
    seg_pitch = nj + HALO
    assert h_rows == 2 * HALO + HALO * seg_pitch and nj % HEAD_ROWS == 0
    n_ct = D_MODEL // LANES
    g = g_ref[...]

    def ext(main, prev, nxt):
        lo = prev[...].astype(F32)[prev.shape[0] - HALO:]
        hi = nxt[...].astype(F32)[:HALO]
        return jnp.concatenate([lo, main[...].astype(F32), hi], axis=0)

    chunks = [(c0, min(FF_CHUNK, D_FF - c0)) for c0 in range(0, D_FF, FF_CHUNK)]
    par = step % 2
    mod_h = mod_head[0]
    n_blocks = tm // HEAD_ROWS

    def head_matmul():
        mixed = jnp.concatenate([ext(am, ap, an), ext(bm, bp, bn)], axis=1).astype(BF16)
        y_ref[...] = _dot(mixed, wo[...])

    def head_rows(x, y):
        x1 = x + mod_h[2:3] * _rms(y, g[0:1])
        return x1, _rms(x1, g[1:2]) * (1.0 + mod_h[4:5]) + mod_h[3:4]

    def head_block(blk):
        if blk < n_blocks:
            rows = slice(blk * HEAD_ROWS, (blk + 1) * HEAD_ROWS)
            erows = slice(HALO + blk * HEAD_ROWS, HALO + (blk + 1) * HEAD_ROWS)
            x1, h = head_rows(xm[rows, :], y_ref[erows, :])
            x1_ref[par, rows, :] = x1
            t0 = blk * HEAD_ROWS
            hrow = HALO + (t0 // nj) * seg_pitch + t0 % nj
            for c in range(n_ct):
                h_ref[c, hrow:hrow + HEAD_ROWS, :] = h[:, c * LANES:(c + 1) * LANES]
        else:
            x = jnp.concatenate([xp[...], xn[...]], axis=0)
            y = jnp.concatenate([y_ref[0:HALO, :], y_ref[HALO + tm:2 * HALO + tm, :]], axis=0)
            _, h = head_rows(x, y)
            t = jnp.minimum(step, n_tiles - 1) % tiles_per_seq
            ridx = lax.broadcasted_iota(jnp.int32, h.shape, 0)
            keep = ((ridx >= HALO) | (t != 0)) & ((ridx < HALO) | (t != tiles_per_seq - 1))
            h = jnp.where(keep, h, 0.0)
            for c in range(n_ct):
                h_ref[c, 0:HALO, :] = h[:HALO, c * LANES:(c + 1) * LANES]
                h_ref[c, h_rows - HALO:h_rows, :] = h[HALO:, c * LANES:(c + 1) * LANES]

    def head_permute():
        def group(rows):
            return jnp.concatenate([h_ref[c, rows, :] for c in range(n_ct)], axis=1)
        def put(row0, first, second):
            hp_ref[row0:row0 + 2 * HALO, :] = jnp.concatenate(
                [group(first), group(second)], axis=0).astype(BF16)
        for j in range(0, nj, 2):
            put(j * HALO, pl.ds(HALO + j, HALO, stride=seg_pitch),
                pl.ds(HALO + j + 1, HALO, stride=seg_pitch))
        put(tm, pl.ds(0, HALO), pl.ds(h_rows - HALO, HALO))

    def tail(interleaved):
        mod = mod_tail[0]

        def conv(u, cols):
            w = cw[:, cols]
            sub = lax.broadcasted_iota(jnp.int32, (HALO, u.shape[1]), 0)
            before = jnp.where(sub == 0, u[tm + HALO - 1:tm + HALO],
                               pltpu.roll(u[tm - HALO:tm], 1, 0))
            after = jnp.where(sub == HALO - 1, u[tm + HALO:tm + HALO + 1],
                              pltpu.roll(u[0:HALO], HALO - 1, 0))
            for s in range(1, HALO):
                if (s * nj) % seq_len == 0:
                    before = jnp.where(sub == s, 0.0, before)
                    after = jnp.where(sub == s - 1, 0.0, after)
            prev = jnp.concatenate([before, u[0:tm - HALO]], axis=0)
            nxt = jnp.concatenate([u[HALO:tm], after], axis=0)
            return prev * w[0:1] + u[0:tm] * w[1:2] + nxt * w[2:3] + cb[:, cols]

        def cols(ch):
            ca = slice(chunks[ch][0], chunks[ch][0] + chunks[ch][1])
            return ca, slice(D_FF + ca.start, D_FF + ca.stop)

        def up(ch):
            ca, cg = cols(ch)
            return _dot(hp_ref[...], wu[:, cg]), _dot(hp_ref[...], wu[:, ca])

        nxt_u = up(0)
        for ch in range(len(chunks)):
            ug, ua = nxt_u
            if ch + 1 < len(chunks):
                nxt_u = up(ch + 1)
            for blk in interleaved[ch]:
                head_block(blk)
            ca, cg = cols(ch)
            act_ref[:, ca] = (_silu(conv(ug, cg)) * conv(ua, ca)).astype(BF16)
        ffn = mod[5:6] * _rms(_dot(act_ref[...], wd[...]), g[2:3])
        for c in range(n_ct):
            acc_ref[c] = ffn[:, c * LANES:(c + 1) * LANES]

        for s in range(HALO):
            for jb in range(nj // HALO):
                t0 = s * nj + HALO * jb
                rows = pl.ds(HALO * HALO * jb + s, HALO, stride=HALO)
                back = jnp.concatenate([acc_ref[c, rows, :] for c in range(n_ct)], axis=1)
                o_ref[t0:t0 + HALO, :] = x1_ref[1 - par, t0:t0 + HALO, :] + back

    @pl.when(step == 0)
    def _():
        head_matmul()
        for blk in range(n_blocks + 1):
            head_block(blk)
        head_permute()

    @pl.when(step > 0)
    def _():
        head_matmul()
        tail([[blk for blk in range(n_blocks + 1) if blk * len(chunks) // (n_blocks + 1) == ch]
              for ch in range(len(chunks))])
        head_permute()


def _post(x, mix_a, mix_b, mods, gains, wo, wu, cw, cb, wd, layer, seq_len, tm, latent):
    n = x.shape[0]
    nt = n // tm
    (mix_a, col_a), (mix_b, col_b) = mix_a, mix_b
    tiles_per_seq = max(seq_len // tm, 1)
    cond_row = (lambda i: 1 + i // tiles_per_seq) if latent else (lambda i: 0)

    head_tile = lambda i: jnp.minimum(i, nt - 1)
    tail_tile = lambda i: jnp.maximum(i - 1, 0)

    def triple(width, halo_rows, col=0):
        per = tm // halo_rows
        last = n // halo_rows - 1
        return [
            pl.BlockSpec((tm, width), lambda i: (head_tile(i), col)),
            pl.BlockSpec((halo_rows, width),
                         lambda i: (jnp.maximum(head_tile(i) * per - 1, 0), col)),
            pl.BlockSpec((halo_rows, width),
                         lambda i: (jnp.minimum((head_tile(i) + 1) * per, last), col)),
        ]

    const = lambda *shape: pl.BlockSpec((None,) + shape, lambda i: (layer,) + (0,) * len(shape),
                                        pipeline_mode=pl.Buffered(1))
    in_specs = (triple(D_MODEL, HALO) + triple(WIDTH, 2 * HALO, col_a)
                + triple(WIDTH, 2 * HALO, col_b) + [
        pl.BlockSpec((None, 1, 6, D_MODEL), lambda i: (layer, cond_row(head_tile(i)), 0, 0)),
        pl.BlockSpec((None, 1, 6, D_MODEL), lambda i: (layer, cond_row(tail_tile(i)), 0, 0)),
        const(3, D_MODEL),
        const(2 * WIDTH, D_MODEL),
        const(D_MODEL, 2 * D_FF),
        const(3, 2 * D_FF),
        const(1, 2 * D_FF),
        const(D_FF, D_MODEL),
    ])
    return pl.pallas_call(
        functools.partial(_post_kernel, seq_len=seq_len, n_tiles=nt),
        grid=(nt + 1,),
        in_specs=in_specs,
        out_specs=pl.BlockSpec((tm, D_MODEL), lambda i: (tail_tile(i), 0)),
        out_shape=jax.ShapeDtypeStruct((n, D_MODEL), F32),
        scratch_shapes=[pltpu.VMEM((D_MODEL // LANES, tm, LANES), F32),
                        pltpu.VMEM((D_MODEL // LANES, 2 * HALO + HALO * (tm // HALO + HALO), LANES),
                                   F32),
                        pltpu.VMEM((tm + 2 * HALO, D_MODEL), BF16),
                        pltpu.VMEM((2, tm, D_MODEL), F32),
                        pltpu.VMEM((tm + 2 * HALO, D_MODEL), F32),
                        pltpu.VMEM((tm, D_FF), BF16)],
        compiler_params=_cparams(1),
        name="post_latent" if latent else "post_context",
    )(x, x, x, mix_a, mix_a, mix_a, mix_b, mix_b, mix_b, mods, mods, gains, wo, wu, cw, cb, wd)


def kernel(x_prompt, x_sample, c, cache_na_k, cache_na_v, state_ret_fwd, state_ret_bwd, c_ctx,
           ada_w, ada_b, g_pre_mix, g_post_mix, g_pre_ffn, g_post_ffn, w_in,
           ret_decay_fwd, ret_decay_bwd, na_rpb, w_out, w_up, conv_w, conv_b, w_down):
    depth = w_in.shape[0]
    batch, seq, _ = x_prompt.shape
    dec_batch, dec_seq, _ = x_sample.shape
    past = cache_na_k.shape[2]
    tm = 512

    cond = jnp.concatenate(
        [c_ctx[None, :], c, jnp.zeros((8 - 1 - dec_batch, D_MODEL), F32)], axis=0)
    mods, bias, lo, hi, w_in_b = _modulation_and_na_bias(
        cond, ada_w, ada_b, na_rpb, dec_seq // GRID_W, w_in)
    mods = mods.reshape(depth, 8, 6, D_MODEL)
    rope_tabs = _rope_tables(dec_seq)

    g_pre = g_pre_mix.reshape(depth, 1, D_MODEL)
    gains = jnp.stack([g_post_mix, g_pre_ffn, g_post_ffn], axis=1)
    cb = conv_b.reshape(depth, 1, 2 * D_FF)

    y_p = x_prompt.reshape(batch * seq, D_MODEL)
    y_s = x_sample.reshape(dec_batch * dec_seq, D_MODEL)
    new_k = new_v = new_sf = new_sb = None
    for l in range(depth):
        dec = (jnp.repeat(ret_decay_fwd[l], HEAD_DIM)[None, :],
               jnp.repeat(ret_decay_bwd[l], HEAD_DIM)[None, :],
               jnp.repeat(ret_decay_fwd[l], LANES)[None, :],
               jnp.repeat(ret_decay_bwd[l], LANES)[None, :])

        p_c, new_k, new_v = _inproj_context(y_p, mods, g_pre, w_in_b, l, seq, tm, (new_k, new_v))
        p_s, mix_c, new_sf, new_sb, *cast = _inproj_latent_and_ctx_mixer(
            y_s, mods, g_pre, w_in_b, l, dec_seq, tm, rope_tabs, p_c, dec, batch, seq,
            (new_sf, new_sb), cast=(w_out, w_up, w_down) if l == 0 else ())
        if l == 0:
            post_params = (mods, gains, cast[0], cast[1], conv_w, cb, cast[2])
        y_p = _post(y_p, (mix_c, 0), (mix_c, 1), *post_params, l, seq, tm, latent=False)

        s0 = _block_diag_states(state_ret_fwd[:, l], state_ret_bwd[:, l])
        ret_s = _lat_retention(p_s, dec, s0, dec_batch, dec_seq)
        ck = cache_na_k[:, l].reshape(dec_batch, past, WIDTH).astype(BF16)
        cv = cache_na_v[:, l].reshape(dec_batch, past, WIDTH).astype(BF16)
        na_s = _neighbourhood_attention(p_s, ck, cv, bias, l, lo, hi, dec_batch, dec_seq)
        y_s = _post(y_s, (ret_s, 0), (na_s, 0), *post_params, l, dec_seq, tm, latent=True)

    return (y_p.reshape(batch, seq, D_MODEL),
            y_s.reshape(dec_batch, dec_seq, D_MODEL),
            new_k.reshape(batch, depth, seq, N_HEADS, HEAD_DIM),
            new_v.reshape(batch, depth, seq, N_HEADS, HEAD_DIM),
            new_sf, new_sb)
```

```python
import functools

import numpy as np
import jax
import jax.numpy as jnp
from jax import lax
from jax.experimental import pallas as pl
from jax.experimental.pallas import tpu as pltpu

F32 = jnp.float32
BF16 = jnp.bfloat16

D_MODEL = 1024
HEAD_DIM = 64
N_HEADS = 8
HEAD_PAIRS = N_HEADS // 2
LANES = 128
WIDTH = N_HEADS * HEAD_DIM
N_GROUPS = 7
IN_WIDTH = N_GROUPS * WIDTH
D_FF = 2816
TOKEN_TILE = 512
FF_CHUNK = 768
HEAD_ROWS = 32
CHUNK = 128
RET_UNROLL = 16
GRID_W = 64
NA_KH = 8
NA_KW = 16
NA_STAGE_ROWS = 8
NA_ROWS = 8
ROPE_BASE = 10000.0
EPS = 1e-6
NEG_INF = -1e9
SUBLANES = 8
HALO = SUBLANES
COND_ROWS = SUBLANES
VMEM_LIMIT = 56 * 1024 * 1024


def _cparams(n_grid):
    return pltpu.CompilerParams(
        dimension_semantics=("arbitrary",) * n_grid, vmem_limit_bytes=VMEM_LIMIT)


def _rms(x, g):
    ms = jnp.mean(x * x, axis=-1, keepdims=True)
    return x * lax.rsqrt(ms + EPS) * g


def _silu(x):
    return x * jax.nn.sigmoid(x)


def _log_sigmoid(x):
    return jnp.minimum(x, 0.0) - jnp.log1p(jnp.exp(-jnp.abs(x)))


def _dot(a, b):
    return jnp.dot(a, b, preferred_element_type=F32)


def _dot_nt(a, b):
    return lax.dot_general(a, b, (((1,), (1,)), ((), ())), preferred_element_type=F32)


def _dot_tn(a, b):
    return lax.dot_general(a, b, (((0,), (0,)), ((), ())), preferred_element_type=F32)


def _cast_streams(weights, n_steps, flat_step=lambda i: i):
    arrays = [w.reshape(-1, w.shape[-1]) for w in weights]
    in_specs, out_specs, out_shapes = [], [], []
    for a in arrays:
        rows = a.shape[0] // n_steps
        assert rows * n_steps == a.shape[0] and rows % (2 * HALO) == 0
        in_specs.append(pl.BlockSpec((rows, a.shape[1]), lambda *idx: (flat_step(*idx), 0)))
        out_specs.append(pl.BlockSpec((rows, a.shape[1]), lambda *idx: (flat_step(*idx), 0)))
        out_shapes.append(jax.ShapeDtypeStruct(a.shape, BF16))
    return arrays, in_specs, out_specs, out_shapes


def _cast_blocks(in_refs, out_refs):
    for src, dst in zip(in_refs, out_refs):
        dst[...] = src[...].astype(dst.dtype)


def _inproj_kernel(*refs, n_alias, layer):
    outs = refs[4 + n_alias:]
    for start, finish in _inproj_groups(*refs[:4], None, None, None, outs[0], outs[1:], n_alias,
                                        layer):
        finish(start())


def _inproj_groups(x_ref, mod_ref, g_ref, w_ref, cos_ref, sin_up_ref, sin_dn_ref, p_ref, kv_refs,
                   n_alias, layer, width=WIDTH):
    rope = cos_ref is not None
    emit_kv = len(kv_refs) > 0
    assert WIDTH % width == 0 and (width == WIDTH or not emit_kv)
    mod = mod_ref[0]
    h = (_rms(x_ref[...], g_ref[...]) * (1.0 + mod[1:2]) + mod[0:1]).astype(BF16)

    def start(n):
        return _dot(h, w_ref[:, n * width:(n + 1) * width])

    def finish(n, pg):
        cols = slice(n * width, (n + 1) * width)
        g = n * width // WIDTH
        if rope and g < 2:
            parts = []
            for j in range(width // LANES):
                xj = pg[:, j * LANES:(j + 1) * LANES]
                parts.append(xj * cos_ref[...]
                             + pltpu.roll(xj, 16, 1) * sin_up_ref[...]
                             + pltpu.roll(xj, LANES - 16, 1) * sin_dn_ref[...])
            pg = jnp.concatenate(parts, axis=1)
        if g in (0, 4):
            pg = pg * (HEAD_DIM ** -0.5)
        p_ref[:, cols] = pg.astype(BF16)
        if emit_kv and g >= 5:
            kv_ref = kv_refs[g - 5]
            seq = kv_ref.shape[-2]
            for j in range(kv_ref.shape[0]):
                if n_alias:
                    kv_ref[j] = pg[j * seq:(j + 1) * seq]
                else:
                    for l in range(kv_ref.shape[1]):
                        kv_ref[j, l] = (pg[j * seq:(j + 1) * seq] if l == layer
                                        else jnp.zeros((seq, WIDTH), F32))

    return [(functools.partial(start, n), functools.partial(finish, n))
            for n in range(IN_WIDTH // width)]


def _inproj_context(x, mods, g_pre, w, layer, seq_len, tm, kv_out):
    n = x.shape[0]
    depth = w.shape[0]
    in_specs = [
        pl.BlockSpec((tm, D_MODEL), lambda i: (i, 0)),
        pl.BlockSpec((None, 1, 6, D_MODEL), lambda i: (layer, 0, 0, 0)),
        pl.BlockSpec((None, 1, D_MODEL), lambda i: (layer, 0, 0)),
        pl.BlockSpec((None, D_MODEL, IN_WIDTH), lambda i: (layer, 0, 0),
                     pipeline_mode=pl.Buffered(1)),
    ]
    args = [x, mods, g_pre, w]
    out_specs = [pl.BlockSpec((tm, IN_WIDTH), lambda i: (i, 0))]
    out_shape = [jax.ShapeDtypeStruct((n, IN_WIDTH), BF16)]
    aliases = {}
    seqs_per_tile = tm // seq_len
    kv_shape = jax.ShapeDtypeStruct((n // seq_len, depth, seq_len, WIDTH), F32)
    assert (kv_out[0] is None) == (kv_out[1] is None)
    for j, prev in enumerate(kv_out):
        if prev is not None:
            aliases[len(args)] = 1 + j
            in_specs.append(pl.BlockSpec(memory_space=pl.ANY))
            args.append(prev)
            out_specs.append(pl.BlockSpec((seqs_per_tile, None, seq_len, WIDTH),
                                          lambda i: (i, layer, 0, 0)))
        else:
            out_specs.append(pl.BlockSpec((seqs_per_tile, depth, seq_len, WIDTH),
                                          lambda i: (i, 0, 0, 0)))
        out_shape.append(kv_shape)
    return pl.pallas_call(
        functools.partial(_inproj_kernel, n_alias=len(aliases), layer=layer),
        grid=(n // tm,),
        in_specs=in_specs,
        out_specs=out_specs,
        out_shape=out_shape,
        input_output_aliases=aliases,
        compiler_params=_cparams(1),
        name="inproj_context",
    )(*args)


def _rope_tables(seq_len):
    t = np.arange(seq_len)
    lane = np.arange(LANES)
    d = lane % HEAD_DIM
    pos = np.where(d[None, :] < HEAD_DIM // 2, (t // GRID_W)[:, None], (t % GRID_W)[:, None])
    pos = pos.astype(np.float32)
    half = HEAD_DIM // 2
    inv = np.power(np.float32(ROPE_BASE), -np.arange(0, half, 2, dtype=np.float32) / half)
    ang = pos * inv[d % (half // 2)][None, :]
    cos, sin = np.cos(ang), np.sin(ang)
    upper = (d % half) >= half // 2
    sin_up = np.where(upper[None, :], sin, 0.0)
    sin_dn = np.where(upper[None, :], 0.0, -sin)
    return (jnp.asarray(cos, F32), jnp.asarray(sin_up, F32), jnp.asarray(sin_dn, F32))


def _retention_tables(dec_f, dec_b, dec_f2, dec_b2):
    lgf, lgb = _log_sigmoid(dec_f), _log_sigmoid(dec_b)
    pos = lax.broadcasted_iota(jnp.int32, (CHUNK, LANES), 0).astype(F32)
    tabs = dict(
        qdf=jnp.exp(lgf * (pos + 1.0)), kdf=jnp.exp(lgf * (CHUNK - 1.0 - pos)),
        cdf=jnp.exp(lgf * float(CHUNK)),
        qdb=jnp.exp(lgb * (CHUNK - pos)), kdb=jnp.exp(lgb * pos),
        cdb=jnp.exp(lgb * float(CHUNK)),
    )
    lgf2, lgb2 = _log_sigmoid(dec_f2), _log_sigmoid(dec_b2)
    i = lax.broadcasted_iota(jnp.int32, (CHUNK, 2 * CHUNK), 0)
    j = lax.broadcasted_iota(jnp.int32, (CHUNK, 2 * CHUNK), 1) & (CHUNK - 1)
    diff = (i - j).astype(F32)
    tabs["decay"] = (jnp.where(diff >= 0, jnp.exp(lgf2 * jnp.maximum(diff, 0.0)), 0.0)
                     + jnp.where(diff <= 0, jnp.exp(lgb2 * jnp.maximum(-diff, 0.0)), 0.0))
    lane = lax.broadcasted_iota(jnp.int32, (1, LANES), 1)
    tabs["head_a"] = lane < HEAD_DIM
    r = lax.broadcasted_iota(jnp.int32, (2 * LANES, LANES), 0) & (LANES - 1)
    c = lax.broadcasted_iota(jnp.int32, (2 * LANES, LANES), 1)
    tabs["same_head"] = (r < HEAD_DIM) == (c < HEAD_DIM)
    return tabs


def _split_heads(x, head_a, axis):
    zero = jnp.zeros_like(x)
    return jnp.concatenate([jnp.where(head_a, x, zero), jnp.where(head_a, zero, x)], axis=axis)


def _chunk_kv(k2, v2, tabs):
    kf = k2.astype(F32)
    kk = jnp.concatenate([kf * tabs["kdf"], kf * tabs["kdb"]], axis=1).astype(BF16)
    return jnp.where(tabs["same_head"], _dot_tn(kk, v2), 0.0)


def _chunk_mix(q2, k2, v2, state, tabs):
    head_a = tabs["head_a"]
    s = _dot_nt(q2, _split_heads(k2, head_a, 0))
    p = (s * tabs["decay"]).astype(BF16)
    qf = q2.astype(F32)
    lhs = jnp.concatenate(
        [p, (qf * tabs["qdf"]).astype(BF16), (qf * tabs["qdb"]).astype(BF16)], axis=1)
    rhs = jnp.concatenate([_split_heads(v2, head_a, 0), state], axis=0)
    return _dot(lhs, rhs)


def _chunk_norm_gate(o, g2, tabs):
    head_a = tabs["head_a"]
    inv = 1.0 / HEAD_DIM
    sum_a = jnp.sum(jnp.where(head_a, o, 0.0), axis=-1, keepdims=True)
    sum_b = jnp.sum(jnp.where(head_a, 0.0, o), axis=-1, keepdims=True)
    d = o - jnp.where(head_a, sum_a, sum_b) * inv
    d2 = d * d
    var_a = jnp.sum(jnp.where(head_a, d2, 0.0), axis=-1, keepdims=True)
    var_b = jnp.sum(jnp.where(head_a, 0.0, d2), axis=-1, keepdims=True)
    o = d * lax.rsqrt(jnp.where(head_a, var_a, var_b) * inv + EPS)
    return o * _silu(g2.astype(F32))


def _ctx_mixer_units(rq, rk, rv, rg, nq, nk, nv, df, db, df2, db2, mix_ref, fwd_ref, bwd_ref,
                     layer, state_layers):
    n_seq = fwd_ref.shape[0]
    seq = rq.shape[0] // n_seq
    nc = seq // CHUNK

    thunks = []
    for hp in range(HEAD_PAIRS):
        cols = slice(hp * LANES, (hp + 1) * LANES)
        cols2 = slice(hp * 2 * LANES, (hp + 1) * 2 * LANES)
        shared = {}

        def scan(hp=hp, cols=cols, cols2=cols2, shared=shared):
            tabs = _retention_tables(df[:, cols], db[:, cols], df2[:, cols2], db2[:, cols2])
            shared["tabs"] = tabs
            for b in range(n_seq):
                rows = [slice(b * seq + c * CHUNK, b * seq + (c + 1) * CHUNK) for c in range(nc)]
                kv = [_chunk_kv(rk[r, cols], rv[r, cols], tabs) for r in rows]
                sf = [jnp.zeros((LANES, LANES), F32)]
                for c in range(nc):
                    sf.append(sf[-1] * tabs["cdf"] + kv[c][:LANES])
                sb = [jnp.zeros((LANES, LANES), F32)]
                for c in reversed(range(nc)):
                    sb.append(sb[-1] * tabs["cdb"] + kv[c][LANES:])
                for ref, s in ((fwd_ref, sf[nc]), (bwd_ref, sb[nc])):
                    blocks = (s[:HEAD_DIM, :HEAD_DIM],
                              pltpu.roll(s[HEAD_DIM:], HEAD_DIM, 1)[:, :HEAD_DIM])
                    for h, blk in enumerate(blocks):
                        if state_layers is None:
                            ref[b, 2 * hp + h] = blk
                        else:
                            for l in range(state_layers):
                                ref[b, l, 2 * hp + h] = blk if l == layer else jnp.zeros_like(blk)
                shared[b] = [jnp.concatenate([sf[c], sb[nc - 1 - c]], axis=0).astype(BF16)
                             for c in range(nc)]

        def chunk_out(b, c, cols=cols, shared=shared):
            rows = slice(b * seq + c * CHUNK, b * seq + (c + 1) * CHUNK)
            tabs = shared["tabs"]
            o = _chunk_mix(rq[rows, cols], rk[rows, cols], rv[rows, cols], shared[b][c], tabs)
            mix_ref[rows, cols] = _chunk_norm_gate(o, rg[rows, cols], tabs).astype(BF16)

        def attention(b, hp=hp, cols=cols, shared=shared):
            head_a = shared["tabs"]["head_a"]
            tok = slice(b * seq, (b + 1) * seq)
            s = _dot_nt(nq[tok, cols], _split_heads(nk[tok, cols], head_a, 0))
            es, rinv = [], []
            for h in range(2):
                sh = s[:, h * seq:(h + 1) * seq]
                e = jnp.exp(sh - jnp.max(sh, axis=-1, keepdims=True))
                rinv.append(1.0 / jnp.sum(e, axis=-1, keepdims=True))
                es.append(e.astype(BF16))
            o = _dot(jnp.concatenate(es, axis=1), _split_heads(nv[tok, cols], head_a, 0))
            o = o * jnp.where(head_a, rinv[0], rinv[1])
            mix_ref[tok, WIDTH + hp * LANES:WIDTH + (hp + 1) * LANES] = o.astype(BF16)

        thunks.append(scan)
        for b in range(n_seq):
            thunks += [functools.partial(chunk_out, b, c) for c in range(nc)]
            thunks.append(functools.partial(attention, b))
    return thunks


def _inproj_mix_kernel(*refs, n_cast, n_alias, layer, state_layers):
    n_proj, n_mix = 7, 11
    outs = refs[n_proj + n_mix + n_cast + n_alias:]
    _cast_blocks(refs[n_proj + n_mix:n_proj + n_mix + n_cast], outs[4:])
    groups = _inproj_groups(*refs[:n_proj], outs[0], (), n_alias=0, layer=0, width=2 * LANES)
    units = _ctx_mixer_units(*refs[n_proj:n_proj + n_mix], outs[1], outs[2], outs[3],
                             layer, state_layers)
    done = 0
    for u, unit in enumerate(units):
        upto = (u + 1) * len(groups) // len(units)
        started = [(finish, start()) for start, finish in groups[done:upto]]
        done = upto
        unit()
        for finish, result in started:
            finish(result)


def _inproj_latent_and_ctx_mixer(x, mods, g_pre, w, layer, seq_len, tm, rope_tabs,
                                 p_ctx, dec, batch, seq, states, cast=()):
    n = x.shape[0]
    depth = w.shape[0]
    n_tiles = n // tm
    cast_arrays, cast_in, cast_out, cast_shapes = _cast_streams(cast, n_tiles)
    n_seq = batch // n_tiles
    assert n_seq * n_tiles == batch and seq_len % tm == 0
    assert (states[0] is None) == (states[1] is None)
    creates = states[0] is None
    tiles_per_seq = seq_len // tm
    group = lambda g: pl.BlockSpec((n_seq * seq, WIDTH), lambda i, g=g: (i, g))
    vec = lambda width: pl.BlockSpec((1, width), lambda i: (0, 0))
    in_specs = [
        pl.BlockSpec((tm, D_MODEL), lambda i: (i, 0)),
        pl.BlockSpec((None, 1, 6, D_MODEL), lambda i: (layer, 1 + i // tiles_per_seq, 0, 0)),
        pl.BlockSpec((None, 1, D_MODEL), lambda i: (layer, 0, 0)),
        pl.BlockSpec((None, D_MODEL, IN_WIDTH), lambda i: (layer, 0, 0),
                     pipeline_mode=pl.Buffered(1)),
    ] + [pl.BlockSpec((tm, LANES), lambda i: (i % tiles_per_seq, 0))] * 3 + [
        group(g) for g in range(N_GROUPS)] + [vec(WIDTH), vec(WIDTH), vec(2 * WIDTH),
                                              vec(2 * WIDTH)]
    args = [x, mods, g_pre, w, *rope_tabs, *([p_ctx] * N_GROUPS), *dec, *cast_arrays]
    in_specs = in_specs + cast_in
    aliases = {}
    if creates:
        state_spec = pl.BlockSpec((n_seq, depth, N_HEADS, HEAD_DIM, HEAD_DIM),
                                  lambda i: (i, 0, 0, 0, 0))
    else:
        state_spec = pl.BlockSpec((n_seq, None, N_HEADS, HEAD_DIM, HEAD_DIM),
                                  lambda i: (i, layer, 0, 0, 0))
        for j, prev in enumerate(states):
            aliases[len(args)] = 2 + j
            in_specs.append(pl.BlockSpec(memory_space=pl.ANY))
            args.append(prev)
    state_shape = jax.ShapeDtypeStruct((batch, depth, N_HEADS, HEAD_DIM, HEAD_DIM), F32)
    results = pl.pallas_call(
        functools.partial(_inproj_mix_kernel, n_cast=len(cast), n_alias=len(aliases),
                          layer=layer, state_layers=depth if creates else None),
        grid=(n_tiles,),
        in_specs=in_specs,
        out_specs=[
            pl.BlockSpec((tm, IN_WIDTH), lambda i: (i, 0)),
            pl.BlockSpec((n_seq * seq, 2 * WIDTH), lambda i: (i, 0)),
            state_spec, state_spec,
        ] + cast_out,
        out_shape=[
            jax.ShapeDtypeStruct((n, IN_WIDTH), BF16),
            jax.ShapeDtypeStruct((batch * seq, 2 * WIDTH), BF16),
            state_shape, state_shape,
        ] + cast_shapes,
        input_output_aliases=aliases,
        compiler_params=_cparams(1),
        name="inproj_latent_ctx_mixer",
    )(*args)
    return list(results[:4]) + [r.reshape(w.shape) for r, w in zip(results[4:], cast)]


def _lat_retention_kernel(q_ref, k_ref, v_ref, g_ref, df, db, df2, db2, s0_ref, o_ref,
                          kv_ref, st_ref):
    nc = q_ref.shape[0] // CHUNK
    tabs = _retention_tables(df[...], db[...], df2[...], db2[...])

    def rows(c):
        return pl.ds(pl.multiple_of(c * CHUNK, CHUNK), CHUNK)

    def kv_body(c, carry):
        kv_ref[c] = _chunk_kv(k_ref[rows(c), :], v_ref[rows(c), :], tabs)
        return carry
    lax.fori_loop(0, nc, kv_body, 0, unroll=RET_UNROLL)

    def fwd_body(c, s):
        st_ref[c, :LANES, :] = s.astype(BF16)
        return s * tabs["cdf"] + kv_ref[c, :LANES, :]
    lax.fori_loop(0, nc, fwd_body, s0_ref[0, 0, 0])

    def bwd_body(i, s):
        c = nc - 1 - i
        st_ref[c, LANES:, :] = s.astype(BF16)
        return s * tabs["cdb"] + kv_ref[c, LANES:, :]
    lax.fori_loop(0, nc, bwd_body, s0_ref[0, 0, 1])

    def out_body(c, carry):
        r = rows(c)
        o = _chunk_mix(q_ref[r, :], k_ref[r, :], v_ref[r, :], st_ref[c], tabs)
        o_ref[r, :] = _chunk_norm_gate(o, g_ref[r, :], tabs).astype(BF16)
        return carry
    lax.fori_loop(0, nc, out_body, 0, unroll=RET_UNROLL)


def _lat_retention(p, dec, s0, batch, seq):
    nc = seq // CHUNK
    group = lambda g: pl.BlockSpec((seq, LANES), lambda b, h, g=g: (b, g * HEAD_PAIRS + h))
    vec = lambda w: pl.BlockSpec((1, w), lambda b, h: (0, h))
    return pl.pallas_call(
        _lat_retention_kernel,
        grid=(batch, HEAD_PAIRS),
        in_specs=[group(g) for g in range(4)] + [vec(LANES), vec(LANES), vec(2 * LANES),
                                                 vec(2 * LANES)]
        + [pl.BlockSpec((1, 1, 2, LANES, LANES), lambda b, h: (b, h, 0, 0, 0))],
        out_specs=pl.BlockSpec((seq, LANES), lambda b, h: (b, h)),
        out_shape=jax.ShapeDtypeStruct((batch * seq, WIDTH), BF16),
        scratch_shapes=[pltpu.VMEM((nc, 2 * LANES, LANES), F32),
                        pltpu.VMEM((nc, 2 * LANES, LANES), BF16)],
        compiler_params=_cparams(2),
        name="latent_retention",
    )(*([p] * 4), *dec, s0)


def _block_diag_states(s_f, s_b):
    def bd(s):
        b = s.shape[0]
        s = s.reshape(b, HEAD_PAIRS, 2, HEAD_DIM, HEAD_DIM)
        z = jnp.zeros_like(s[:, :, 0])
        top = jnp.concatenate([s[:, :, 0], z], axis=-1)
        bot = jnp.concatenate([z, s[:, :, 1]], axis=-1)
        return jnp.concatenate([top, bot], axis=-2)
    return jnp.stack([bd(s_f), bd(s_b)], axis=2)


def _na_kernel(q_ref, k_ref, v_ref, ck_ref, cv_ref, bias_ref, o_ref, *, lo, hi):
    rows_total = k_ref.shape[0] // GRID_W
    lane = lax.broadcasted_iota(jnp.int32, (1, LANES), 1)
    head_a = lane < HEAD_DIM
    wins, cls = [], []
    for i in range(NA_ROWS):
        r = pl.program_id(1) * NA_ROWS + i
        rs = jnp.clip(r - NA_KH // 2, 0, rows_total - NA_KH)
        wins.append(pl.ds(pl.multiple_of(rs * GRID_W, GRID_W), NA_KH * GRID_W))
        cls.append(jnp.where(r < lo, r, jnp.where(r > hi, r - hi + lo, lo)))
    stages = [(hp, r0) for hp in range(HEAD_PAIRS) for r0 in range(0, NA_ROWS, NA_STAGE_ROWS)]

    def scores(stage):
        hp, r0 = stage
        cols = slice(hp * LANES, (hp + 1) * LANES)
        qq = jnp.concatenate(
            [_split_heads(q_ref[i * GRID_W:(i + 1) * GRID_W, cols], head_a, 0)
             for i in range(r0, r0 + NA_STAGE_ROWS)], axis=0)
        s_ctx = _dot_nt(qq, ck_ref[0, :, cols])
        s_loc = [_dot_nt(qq[j * 2 * GRID_W:(j + 1) * 2 * GRID_W], k_ref[wins[r0 + j], cols])
                 for j in range(NA_STAGE_ROWS)]
        return s_ctx, s_loc

    def softmax(stage, s_ctx, s_loc):
        hp, r0 = stage
        m_ctx = jnp.max(s_ctx, axis=-1, keepdims=True)
        m = []
        for j in range(NA_STAGE_ROWS):
            blk = slice(j * 2 * GRID_W, (j + 1) * 2 * GRID_W)
            s_loc[j] = s_loc[j] + bias_ref[cls[r0 + j], hp]
            m.append(jnp.maximum(jnp.max(s_loc[j], axis=-1, keepdims=True), m_ctx[blk]))
        e_ctx = jnp.exp(s_ctx - jnp.concatenate(m, axis=0))
        den_ctx = jnp.sum(e_ctx, axis=-1, keepdims=True)
        e_loc, den = [], []
        for j in range(NA_STAGE_ROWS):
            blk = slice(j * 2 * GRID_W, (j + 1) * 2 * GRID_W)
            e = jnp.exp(s_loc[j] - m[j])
            den.append(jnp.sum(e, axis=-1, keepdims=True) + den_ctx[blk])
            e_loc.append(e.astype(BF16))
        return e_ctx.astype(BF16), e_loc, den

    def weighted_values(stage, e_ctx, e_loc, den):
        hp, r0 = stage
        cols = slice(hp * LANES, (hp + 1) * LANES)
        o_ctx = _dot(e_ctx, cv_ref[0, :, cols])
        for j in range(NA_STAGE_ROWS):
            i = r0 + j
            blk = slice(j * 2 * GRID_W, (j + 1) * 2 * GRID_W)
            o = (_dot(e_loc[j], v_ref[wins[i], cols]) + o_ctx[blk]) / den[j]
            o_ref[i * GRID_W:(i + 1) * GRID_W, cols] = jnp.where(
                head_a, o[:GRID_W], o[GRID_W:]).astype(BF16)

    nxt = scores(stages[0])
    pending = None
    for n, stage in enumerate(stages):
        s_ctx, s_loc = nxt
        if n + 1 < len(stages):
            nxt = scores(stages[n + 1])
        probs = softmax(stage, s_ctx, s_loc)
        if pending is not None:
            weighted_values(*pending)
        pending = (stage,) + probs
    weighted_values(*pending)


def _mod_bias_kernel(cond_ref, w_ref, b_ref, rpb_ref, win_ref, mod_ref, bias_ref, win_bf_ref, *,
                     dr_first, n_dr):
    s = _silu(cond_ref[...]).astype(BF16)
    mod_ref[...] = _dot(s, w_ref[...].astype(BF16)) + b_ref[...]
    _cast_blocks([win_ref], [win_bf_ref])

    qc = lax.broadcasted_iota(jnp.int32, (GRID_W, LANES), 0)
    lane = lax.broadcasted_iota(jnp.int32, (GRID_W, LANES), 1)
    kc = lane & (GRID_W - 1)
    cs = jnp.clip(qc - NA_KW // 2, 0, GRID_W - NA_KW)
    inside = (kc >= cs) & (kc < cs + NA_KW)
    first = lane < GRID_W
    lo_half, hi_half = [], []
    for dr in range(n_dr):
        line = jnp.broadcast_to(rpb_ref[pl.ds(dr, 1), :], (GRID_W, LANES))
        lo_half.append(pltpu.roll(line, LANES - (NA_KW - 1), 1, stride=1, stride_axis=0))
        hi_half.append(pltpu.roll(line, GRID_W - (NA_KW - 1), 1, stride=1, stride_axis=0))
    for cls, dr0 in enumerate(dr_first):
        for jp in range(NA_KH // 2):
            dr = dr0 + 2 * jp
            tile = jnp.where(inside, jnp.where(first, lo_half[dr], hi_half[dr + 1]), NEG_INF)
            bias_ref[cls, :, jp * LANES:(jp + 1) * LANES] = tile


def _modulation_and_na_bias(cond, ada_w, ada_b, na_rpb, rows_total, w_in):
    depth, heads, n_dr, n_dc = na_rpb.shape
    kh = NA_KH
    lo, hi = kh // 2, rows_total - kh // 2 - 1
    reps = list(range(lo)) + [lo] + list(range(hi + 1, rows_total))
    dr_first = tuple(int(np.clip(r - kh // 2, 0, rows_total - kh)) - r + kh - 1 for r in reps)
    n_dr_pad = -(-n_dr // HALO) * HALO
    lines = jnp.pad(na_rpb, ((0, 0), (0, 0), (0, n_dr_pad - n_dr), (0, LANES - n_dc)))
    mod_cols = 6 * D_MODEL // heads
    assert mod_cols % LANES == 0
    cast_arrays, cast_in, cast_out, cast_shapes = _cast_streams(
        (w_in,), depth * heads, lambda l, h: l * heads + h)
    mod, bias, w_in_b = pl.pallas_call(
        functools.partial(_mod_bias_kernel, dr_first=dr_first, n_dr=n_dr),
        grid=(depth, heads),
        in_specs=[
            pl.BlockSpec((COND_ROWS, D_MODEL), lambda l, h: (0, 0)),
            pl.BlockSpec((None, D_MODEL, mod_cols), lambda l, h: (l, 0, h)),
            pl.BlockSpec((None, 1, mod_cols), lambda l, h: (l, 0, h)),
            pl.BlockSpec((None, None, n_dr_pad, LANES), lambda l, h: (l, h, 0, 0)),
        ] + cast_in,
        out_specs=[
            pl.BlockSpec((None, COND_ROWS, mod_cols), lambda l, h: (l, 0, h)),
            pl.BlockSpec((None, len(reps), None, GRID_W, kh * GRID_W),
                         lambda l, h: (l, 0, h // 2, h % 2, 0)),
        ] + cast_out,
        out_shape=[
            jax.ShapeDtypeStruct((depth, COND_ROWS, 6 * D_MODEL), F32),
            jax.ShapeDtypeStruct((depth, len(reps), HEAD_PAIRS, 2 * GRID_W, kh * GRID_W), F32),
        ] + cast_shapes,
        compiler_params=_cparams(2),
        name="modulation_and_na_bias",
    )(cond, ada_w, ada_b.reshape(depth, 1, 6 * D_MODEL), lines, *cast_arrays)
    return mod, bias, lo, hi, w_in_b.reshape(w_in.shape)


def _neighbourhood_attention(p, ctx_k, ctx_v, bias, layer, lo, hi, batch, seq):
    rows_total = seq // GRID_W
    past = ctx_k.shape[1]

    groups = rows_total // NA_ROWS
    return pl.pallas_call(
        functools.partial(_na_kernel, lo=lo, hi=hi),
        grid=(batch, groups),
        in_specs=[
            pl.BlockSpec((NA_ROWS * GRID_W, WIDTH), lambda b, g: (b * groups + g, 4)),
            pl.BlockSpec((seq, WIDTH), lambda b, g: (b, 5), pipeline_mode=pl.Buffered(1)),
            pl.BlockSpec((seq, WIDTH), lambda b, g: (b, 6), pipeline_mode=pl.Buffered(1)),
            pl.BlockSpec((1, past, WIDTH), lambda b, g: (b, 0, 0), pipeline_mode=pl.Buffered(1)),
            pl.BlockSpec((1, past, WIDTH), lambda b, g: (b, 0, 0), pipeline_mode=pl.Buffered(1)),
            pl.BlockSpec((None,) + bias.shape[1:], lambda b, g: (layer, 0, 0, 0, 0),
                         pipeline_mode=pl.Buffered(1)),
        ],
        out_specs=pl.BlockSpec((NA_ROWS * GRID_W, WIDTH), lambda b, g: (b * groups + g, 0)),
        out_shape=jax.ShapeDtypeStruct((batch * seq, WIDTH), BF16),
        compiler_params=_cparams(2),
        name="neighbourhood_attention",
    )(p, p, p, ctx_k, ctx_v, bias)


def _post_kernel(xm, xp, xn, am, ap, an, bm, bp, bn, mod_head, mod_tail, g_ref, wo, wu, cw, cb, wd,
                 o_ref, acc_ref, h_ref, hp_ref, x1_ref, y_ref, act_ref, *, seq_len, n_tiles):
    step = pl.program_id(0)
    tm = xm.shape[0]
    nj = tm // HALO
    tiles_per_seq = max(seq_len // tm, 1)
    assert (tm % seq_len == 0 and seq_len % nj == 0) or seq_len % tm == 0
    h_rows = h_ref.shape[1]
    seg_pitch = nj + HALO
    assert h_rows == 2 * HALO + HALO * seg_pitch and nj % HEAD_ROWS == 0
    n_ct = D_MODEL // LANES
    g = g_ref[...]

    def ext(main, prev, nxt):
        lo = prev[...].astype(F32)[prev.shape[0] - HALO:]
        hi = nxt[...].astype(F32)[:HALO]
        return jnp.concatenate([lo, main[...].astype(F32), hi], axis=0)

    chunks = [(c0, min(FF_CHUNK, D_FF - c0)) for c0 in range(0, D_FF, FF_CHUNK)]
    par = step % 2
    mod_h = mod_head[0]
    n_blocks = tm // HEAD_ROWS

    def head_matmul():
        mixed = jnp.concatenate([ext(am, ap, an), ext(bm, bp, bn)], axis=1).astype(BF16)
        y_ref[...] = _dot(mixed, wo[...])

    def head_rows(x, y):
        x1 = x + mod_h[2:3] * _rms(y, g[0:1])
        return x1, _rms(x1, g[1:2]) * (1.0 + mod_h[4:5]) + mod_h[3:4]

    def head_block(blk):
        if blk < n_blocks:
            rows = slice(blk * HEAD_ROWS, (blk + 1) * HEAD_ROWS)
            erows = slice(HALO + blk * HEAD_ROWS, HALO + (blk + 1) * HEAD_ROWS)
            x1, h = head_rows(xm[rows, :], y_ref[erows, :])
            x1_ref[par, rows, :] = x1
            t0 = blk * HEAD_ROWS
            hrow = HALO + (t0 // nj) * seg_pitch + t0 % nj
            for c in range(n_ct):
                h_ref[c, hrow:hrow + HEAD_ROWS, :] = h[:, c * LANES:(c + 1) * LANES]
        else:
            x = jnp.concatenate([xp[...], xn[...]], axis=0)
            y = jnp.concatenate([y_ref[0:HALO, :], y_ref[HALO + tm:2 * HALO + tm, :]], axis=0)
            _, h = head_rows(x, y)
            t = jnp.minimum(step, n_tiles - 1) % tiles_per_seq
            ridx = lax.broadcasted_iota(jnp.int32, h.shape, 0)
            keep = ((ridx >= HALO) | (t != 0)) & ((ridx < HALO) | (t != tiles_per_seq - 1))
            h = jnp.where(keep, h, 0.0)
            for c in range(n_ct):
                h_ref[c, 0:HALO, :] = h[:HALO, c * LANES:(c + 1) * LANES]
                h_ref[c, h_rows - HALO:h_rows, :] = h[HALO:, c * LANES:(c + 1) * LANES]

    def head_permute():
        def group(rows):
            return jnp.concatenate([h_ref[c, rows, :] for c in range(n_ct)], axis=1)
        def put(row0, first, second):
            hp_ref[row0:row0 + 2 * HALO, :] = jnp.concatenate(
                [group(first), group(second)], axis=0).astype(BF16)
        for j in range(0, nj, 2):
            put(j * HALO, pl.ds(HALO + j, HALO, stride=seg_pitch),
                pl.ds(HALO + j + 1, HALO, stride=seg_pitch))
        put(tm, pl.ds(0, HALO), pl.ds(h_rows - HALO, HALO))

    def tail(interleaved):
        mod = mod_tail[0]

        def conv(u, cols):
            w = cw[:, cols]
            sub = lax.broadcasted_iota(jnp.int32, (HALO, u.shape[1]), 0)
            before = jnp.where(sub == 0, u[tm + HALO - 1:tm + HALO],
                               pltpu.roll(u[tm - HALO:tm], 1, 0))
            after = jnp.where(sub == HALO - 1, u[tm + HALO:tm + HALO + 1],
                              pltpu.roll(u[0:HALO], HALO - 1, 0))
            for s in range(1, HALO):
                if (s * nj) % seq_len == 0:
                    before = jnp.where(sub == s, 0.0, before)
                    after = jnp.where(sub == s - 1, 0.0, after)
            prev = jnp.concatenate([before, u[0:tm - HALO]], axis=0)
            nxt = jnp.concatenate([u[HALO:tm], after], axis=0)
            return prev * w[0:1] + u[0:tm] * w[1:2] + nxt * w[2:3] + cb[:, cols]

        def cols(ch):
            ca = slice(chunks[ch][0], chunks[ch][0] + chunks[ch][1])
            return ca, slice(D_FF + ca.start, D_FF + ca.stop)

        def up(ch):
            ca, cg = cols(ch)
            return _dot(hp_ref[...], wu[:, cg]), _dot(hp_ref[...], wu[:, ca])

        nxt_u = up(0)
        for ch in range(len(chunks)):
            ug, ua = nxt_u
            if ch + 1 < len(chunks):
                nxt_u = up(ch + 1)
            for blk in interleaved[ch]:
                head_block(blk)
            ca, cg = cols(ch)
            act_ref[:, ca] = (_silu(conv(ug, cg)) * conv(ua, ca)).astype(BF16)
        ffn = mod[5:6] * _rms(_dot(act_ref[...], wd[...]), g[2:3])
        for c in range(n_ct):
            acc_ref[c] = ffn[:, c * LANES:(c + 1) * LANES]

        for s in range(HALO):
            for jb in range(nj // HALO):
                t0 = s * nj + HALO * jb
                rows = pl.ds(HALO * HALO * jb + s, HALO, stride=HALO)
                back = jnp.concatenate([acc_ref[c, rows, :] for c in range(n_ct)], axis=1)
                o_ref[t0:t0 + HALO, :] = x1_ref[1 - par, t0:t0 + HALO, :] + back

    @pl.when(step == 0)
    def _():
        head_matmul()
        for blk in range(n_blocks + 1):
            head_block(blk)
        head_permute()

    @pl.when(step > 0)
    def _():
        head_matmul()
        tail([[blk for blk in range(n_blocks + 1) if blk * len(chunks) // (n_blocks + 1) == ch]
              for ch in range(len(chunks))])
        head_permute()


def _post(x, mix_a, mix_b, mods, gains, wo, wu, cw, cb, wd, layer, seq_len, tm, latent):
    n = x.shape[0]
    nt = n // tm
    (mix_a, col_a), (mix_b, col_b) = mix_a, mix_b
    tiles_per_seq = max(seq_len // tm, 1)
    cond_row = (lambda i: 1 + i // tiles_per_seq) if latent else (lambda i: 0)

    head_tile = lambda i: jnp.minimum(i, nt - 1)
    tail_tile = lambda i: jnp.maximum(i - 1, 0)

    def triple(width, halo_rows, col=0):
        per = tm // halo_rows
        last = n // halo_rows - 1
        return [
            pl.BlockSpec((tm, width), lambda i: (head_tile(i), col)),
            pl.BlockSpec((halo_rows, width),
                         lambda i: (jnp.maximum(head_tile(i) * per - 1, 0), col)),
            pl.BlockSpec((halo_rows, width),
                         lambda i: (jnp.minimum((head_tile(i) + 1) * per, last), col)),
        ]

    const = lambda *shape: pl.BlockSpec((None,) + shape, lambda i: (layer,) + (0,) * len(shape),
                                        pipeline_mode=pl.Buffered(1))
    in_specs = (triple(D_MODEL, HALO) + triple(WIDTH, 2 * HALO, col_a)
                + triple(WIDTH, 2 * HALO, col_b) + [
        pl.BlockSpec((None, 1, 6, D_MODEL), lambda i: (layer, cond_row(head_tile(i)), 0, 0)),
        pl.BlockSpec((None, 1, 6, D_MODEL), lambda i: (layer, cond_row(tail_tile(i)), 0, 0)),
        const(3, D_MODEL),
        const(2 * WIDTH, D_MODEL),
        const(D_MODEL, 2 * D_FF),
        const(3, 2 * D_FF),
        const(1, 2 * D_FF),
        const(D_FF, D_MODEL),
    ])
    return pl.pallas_call(
        functools.partial(_post_kernel, seq_len=seq_len, n_tiles=nt),
        grid=(nt + 1,),
        in_specs=in_specs,
        out_specs=pl.BlockSpec((tm, D_MODEL), lambda i: (tail_tile(i), 0)),
        out_shape=jax.ShapeDtypeStruct((n, D_MODEL), F32),
        scratch_shapes=[pltpu.VMEM((D_MODEL // LANES, tm, LANES), F32),
                        pltpu.VMEM((D_MODEL // LANES, 2 * HALO + HALO * (tm // HALO + HALO), LANES),
                                   F32),
                        pltpu.VMEM((tm + 2 * HALO, D_MODEL), BF16),
                        pltpu.VMEM((2, tm, D_MODEL), F32),
                        pltpu.VMEM((tm + 2 * HALO, D_MODEL), F32),
                        pltpu.VMEM((tm, D_FF), BF16)],
        compiler_params=_cparams(1),
        name="post_latent" if latent else "post_context",
    )(x, x, x, mix_a, mix_a, mix_a, mix_b, mix_b, mix_b, mods, mods, gains, wo, wu, cw, cb, wd)


def kernel(x_prompt, x_sample, c, cache_na_k, cache_na_v, state_ret_fwd, state_ret_bwd, c_ctx,
           ada_w, ada_b, g_pre_mix, g_post_mix, g_pre_ffn, g_post_ffn, w_in,
           ret_decay_fwd, ret_decay_bwd, na_rpb, w_out, w_up, conv_w, conv_b, w_down):
    depth = w_in.shape[0]
    batch, seq, _ = x_prompt.shape
    dec_batch, dec_seq, _ = x_sample.shape
    past = cache_na_k.shape[2]
    tm = TOKEN_TILE
    assert 1 + dec_batch <= COND_ROWS

    cond = jnp.concatenate(
        [c_ctx[None, :], c, jnp.zeros((COND_ROWS - 1 - dec_batch, D_MODEL), F32)], axis=0)
    mods, bias, lo, hi, w_in_b = _modulation_and_na_bias(
        cond, ada_w, ada_b, na_rpb, dec_seq // GRID_W, w_in)
    mods = mods.reshape(depth, COND_ROWS, 6, D_MODEL)
    rope_tabs = _rope_tables(dec_seq)

    g_pre = g_pre_mix.reshape(depth, 1, D_MODEL)
    gains = jnp.stack([g_post_mix, g_pre_ffn, g_post_ffn], axis=1)
    cb = conv_b.reshape(depth, 1, 2 * D_FF)

    y_p = x_prompt.reshape(batch * seq, D_MODEL)
    y_s = x_sample.reshape(dec_batch * dec_seq, D_MODEL)
    new_k = new_v = new_sf = new_sb = None
    for l in range(depth):
        dec = (jnp.repeat(ret_decay_fwd[l], HEAD_DIM)[None, :],
               jnp.repeat(ret_decay_bwd[l], HEAD_DIM)[None, :],
               jnp.repeat(ret_decay_fwd[l], LANES)[None, :],
               jnp.repeat(ret_decay_bwd[l], LANES)[None, :])

        p_c, new_k, new_v = _inproj_context(y_p, mods, g_pre, w_in_b, l, seq, tm, (new_k, new_v))
        p_s, mix_c, new_sf, new_sb, *cast = _inproj_latent_and_ctx_mixer(
            y_s, mods, g_pre, w_in_b, l, dec_seq, tm, rope_tabs, p_c, dec, batch, seq,
            (new_sf, new_sb), cast=(w_out, w_up, w_down) if l == 0 else ())
        if l == 0:
            post_params = (mods, gains, cast[0], cast[1], conv_w, cb, cast[2])
        y_p = _post(y_p, (mix_c, 0), (mix_c, 1), *post_params, l, seq, tm, latent=False)

        s0 = _block_diag_states(state_ret_fwd[:, l], state_ret_bwd[:, l])
        ret_s = _lat_retention(p_s, dec, s0, dec_batch, dec_seq)
        ck = cache_na_k[:, l].reshape(dec_batch, past, WIDTH).astype(BF16)
        cv = cache_na_v[:, l].reshape(dec_batch, past, WIDTH).astype(BF16)
        na_s = _neighbourhood_attention(p_s, ck, cv, bias, l, lo, hi, dec_batch, dec_seq)
        y_s = _post(y_s, (ret_s, 0), (na_s, 0), *post_params, l, dec_seq, tm, latent=True)

    return (y_p.reshape(batch, seq, D_MODEL),
            y_s.reshape(dec_batch, dec_seq, D_MODEL),
            new_k.reshape(batch, depth, seq, N_HEADS, HEAD_DIM),
            new_v.reshape(batch, depth, seq, N_HEADS, HEAD_DIM),
            new_sf, new_sb)
```

```python
import functools

import numpy as np
import jax
import jax.numpy as jnp
from jax import lax
from jax.experimental import pallas as pl
from jax.experimental.pallas import tpu as pltpu

F32 = jnp.float32
BF16 = jnp.bfloat16

D_MODEL = 1024
HEAD_DIM = 64
N_HEADS = 8
HEAD_PAIRS = N_HEADS // 2
LANES = 128
WIDTH = N_HEADS * HEAD_DIM
N_GROUPS = 7
IN_WIDTH = N_GROUPS * WIDTH
D_FF = 2816
TOKEN_TILE = 512
FF_CHUNK = 768
HEAD_ROWS = 32
CHUNK = 128
RET_UNROLL = 32
GRID_W = 64
NA_KH = 8
NA_KW = 16
NA_STAGE_ROWS = 8
NA_ROWS = 8
ROPE_BASE = 10000.0
EPS = 1e-6
NEG_INF = -1e9
SUBLANES = 8
HALO = SUBLANES
COND_ROWS = SUBLANES
VMEM_LIMIT = 56 * 1024 * 1024


def _cparams(n_grid):
    return pltpu.CompilerParams(
        dimension_semantics=("arbitrary",) * n_grid, vmem_limit_bytes=VMEM_LIMIT)


def _rms(x, g):
    ms = jnp.mean(x * x, axis=-1, keepdims=True)
    return x * lax.rsqrt(ms + EPS) * g


def _silu(x):
    return x * jax.nn.sigmoid(x)


def _log_sigmoid(x):
    return jnp.minimum(x, 0.0) - jnp.log1p(jnp.exp(-jnp.abs(x)))


def _dot(a, b):
    return jnp.dot(a, b, preferred_element_type=F32)


def _dot_nt(a, b):
    return lax.dot_general(a, b, (((1,), (1,)), ((), ())), preferred_element_type=F32)


def _dot_tn(a, b):
    return lax.dot_general(a, b, (((0,), (0,)), ((), ())), preferred_element_type=F32)


def _cast_streams(weights, n_steps, flat_step=lambda i: i):
    arrays = [w.reshape(-1, w.shape[-1]) for w in weights]
    in_specs, out_specs, out_shapes = [], [], []
    for a in arrays:
        rows = a.shape[0] // n_steps
        assert rows * n_steps == a.shape[0] and rows % (2 * HALO) == 0
        in_specs.append(pl.BlockSpec((rows, a.shape[1]), lambda *idx: (flat_step(*idx), 0)))
        out_specs.append(pl.BlockSpec((rows, a.shape[1]), lambda *idx: (flat_step(*idx), 0)))
        out_shapes.append(jax.ShapeDtypeStruct(a.shape, BF16))
    return arrays, in_specs, out_specs, out_shapes


def _cast_blocks(in_refs, out_refs):
    for src, dst in zip(in_refs, out_refs):
        dst[...] = src[...].astype(dst.dtype)


def _inproj_kernel(*refs, n_alias, layer):
    outs = refs[4 + n_alias:]
    for start, finish in _inproj_groups(*refs[:4], None, None, None, outs[0], outs[1:], n_alias,
                                        layer):
        finish(start())


def _inproj_groups(x_ref, mod_ref, g_ref, w_ref, cos_ref, sin_up_ref, sin_dn_ref, p_ref, kv_refs,
                   n_alias, layer, width=WIDTH):
    rope = cos_ref is not None
    emit_kv = len(kv_refs) > 0
    assert WIDTH % width == 0 and (width == WIDTH or not emit_kv)
    mod = mod_ref[0]
    h = (_rms(x_ref[...], g_ref[...]) * (1.0 + mod[1:2]) + mod[0:1]).astype(BF16)

    def start(n):
        return _dot(h, w_ref[:, n * width:(n + 1) * width])

    def finish(n, pg):
        cols = slice(n * width, (n + 1) * width)
        g = n * width // WIDTH
        if rope and g < 2:
            parts = []
            for j in range(width // LANES):
                xj = pg[:, j * LANES:(j + 1) * LANES]
                parts.append(xj * cos_ref[...]
                             + pltpu.roll(xj, 16, 1) * sin_up_ref[...]
                             + pltpu.roll(xj, LANES - 16, 1) * sin_dn_ref[...])
            pg = jnp.concatenate(parts, axis=1)
        if g in (0, 4):
            pg = pg * (HEAD_DIM ** -0.5)
        p_ref[:, cols] = pg.astype(BF16)
        if emit_kv and g >= 5:
            kv_ref = kv_refs[g - 5]
            seq = kv_ref.shape[-2]
            for j in range(kv_ref.shape[0]):
                if n_alias:
                    kv_ref[j] = pg[j * seq:(j + 1) * seq]
                else:
                    for l in range(kv_ref.shape[1]):
                        kv_ref[j, l] = (pg[j * seq:(j + 1) * seq] if l == layer
                                        else jnp.zeros((seq, WIDTH), F32))

    return [(functools.partial(start, n), functools.partial(finish, n))
            for n in range(IN_WIDTH // width)]


def _inproj_context(x, mods, g_pre, w, layer, seq_len, tm, kv_out):
    n = x.shape[0]
    depth = w.shape[0]
    in_specs = [
        pl.BlockSpec((tm, D_MODEL), lambda i: (i, 0)),
        pl.BlockSpec((None, 1, 6, D_MODEL), lambda i: (layer, 0, 0, 0)),
        pl.BlockSpec((None, 1, D_MODEL), lambda i: (layer, 0, 0)),
        pl.BlockSpec((None, D_MODEL, IN_WIDTH), lambda i: (layer, 0, 0),
                     pipeline_mode=pl.Buffered(1)),
    ]
    args = [x, mods, g_pre, w]
    out_specs = [pl.BlockSpec((tm, IN_WIDTH), lambda i: (i, 0))]
    out_shape = [jax.ShapeDtypeStruct((n, IN_WIDTH), BF16)]
    aliases = {}
    seqs_per_tile = tm // seq_len
    kv_shape = jax.ShapeDtypeStruct((n // seq_len, depth, seq_len, WIDTH), F32)
    assert (kv_out[0] is None) == (kv_out[1] is None)
    for j, prev in enumerate(kv_out):
        if prev is not None:
            aliases[len(args)] = 1 + j
            in_specs.append(pl.BlockSpec(memory_space=pl.ANY))
            args.append(prev)
            out_specs.append(pl.BlockSpec((seqs_per_tile, None, seq_len, WIDTH),
                                          lambda i: (i, layer, 0, 0)))
        else:
            out_specs.append(pl.BlockSpec((seqs_per_tile, depth, seq_len, WIDTH),
                                          lambda i: (i, 0, 0, 0)))
        out_shape.append(kv_shape)
    return pl.pallas_call(
        functools.partial(_inproj_kernel, n_alias=len(aliases), layer=layer),
        grid=(n // tm,),
        in_specs=in_specs,
        out_specs=out_specs,
        out_shape=out_shape,
        input_output_aliases=aliases,
        compiler_params=_cparams(1),
        name="inproj_context",
    )(*args)


def _rope_tables(seq_len):
    t = np.arange(seq_len)
    lane = np.arange(LANES)
    d = lane % HEAD_DIM
    pos = np.where(d[None, :] < HEAD_DIM // 2, (t // GRID_W)[:, None], (t % GRID_W)[:, None])
    pos = pos.astype(np.float32)
    half = HEAD_DIM // 2
    inv = np.power(np.float32(ROPE_BASE), -np.arange(0, half, 2, dtype=np.float32) / half)
    ang = pos * inv[d % (half // 2)][None, :]
    cos, sin = np.cos(ang), np.sin(ang)
    upper = (d % half) >= half // 2
    sin_up = np.where(upper[None, :], sin, 0.0)
    sin_dn = np.where(upper[None, :], 0.0, -sin)
    return (jnp.asarray(cos, F32), jnp.asarray(sin_up, F32), jnp.asarray(sin_dn, F32))


def _retention_tables(dec_f, dec_b, dec_f2, dec_b2):
    lgf, lgb = _log_sigmoid(dec_f), _log_sigmoid(dec_b)
    pos = lax.broadcasted_iota(jnp.int32, (CHUNK, LANES), 0).astype(F32)
    tabs = dict(
        qdf=jnp.exp(lgf * (pos + 1.0)), kdf=jnp.exp(lgf * (CHUNK - 1.0 - pos)),
        cdf=jnp.exp(lgf * float(CHUNK)),
        qdb=jnp.exp(lgb * (CHUNK - pos)), kdb=jnp.exp(lgb * pos),
        cdb=jnp.exp(lgb * float(CHUNK)),
    )
    lgf2, lgb2 = _log_sigmoid(dec_f2), _log_sigmoid(dec_b2)
    i = lax.broadcasted_iota(jnp.int32, (CHUNK, 2 * CHUNK), 0)
    j = lax.broadcasted_iota(jnp.int32, (CHUNK, 2 * CHUNK), 1) & (CHUNK - 1)
    diff = (i - j).astype(F32)
    tabs["decay"] = (jnp.where(diff >= 0, jnp.exp(lgf2 * jnp.maximum(diff, 0.0)), 0.0)
                     + jnp.where(diff <= 0, jnp.exp(lgb2 * jnp.maximum(-diff, 0.0)), 0.0))
    lane = lax.broadcasted_iota(jnp.int32, (1, LANES), 1)
    tabs["head_a"] = lane < HEAD_DIM
    r = lax.broadcasted_iota(jnp.int32, (2 * LANES, LANES), 0) & (LANES - 1)
    c = lax.broadcasted_iota(jnp.int32, (2 * LANES, LANES), 1)
    tabs["same_head"] = (r < HEAD_DIM) == (c < HEAD_DIM)
    return tabs


def _split_heads(x, head_a, axis):
    zero = jnp.zeros_like(x)
    return jnp.concatenate([jnp.where(head_a, x, zero), jnp.where(head_a, zero, x)], axis=axis)


def _chunk_kv(k2, v2, tabs):
    kf = k2.astype(F32)
    kk = jnp.concatenate([kf * tabs["kdf"], kf * tabs["kdb"]], axis=1).astype(BF16)
    return jnp.where(tabs["same_head"], _dot_tn(kk, v2), 0.0)


def _chunk_mix(q2, k2, v2, state, tabs):
    head_a = tabs["head_a"]
    s = _dot_nt(q2, _split_heads(k2, head_a, 0))
    p = (s * tabs["decay"]).astype(BF16)
    qf = q2.astype(F32)
    lhs = jnp.concatenate(
        [p, (qf * tabs["qdf"]).astype(BF16), (qf * tabs["qdb"]).astype(BF16)], axis=1)
    rhs = jnp.concatenate([_split_heads(v2, head_a, 0), state], axis=0)
    return _dot(lhs, rhs)


def _chunk_norm_gate(o, g2, tabs):
    head_a = tabs["head_a"]
    inv = 1.0 / HEAD_DIM
    sum_a = jnp.sum(jnp.where(head_a, o, 0.0), axis=-1, keepdims=True)
    sum_b = jnp.sum(jnp.where(head_a, 0.0, o), axis=-1, keepdims=True)
    d = o - jnp.where(head_a, sum_a, sum_b) * inv
    d2 = d * d
    var_a = jnp.sum(jnp.where(head_a, d2, 0.0), axis=-1, keepdims=True)
    var_b = jnp.sum(jnp.where(head_a, 0.0, d2), axis=-1, keepdims=True)
    o = d * lax.rsqrt(jnp.where(head_a, var_a, var_b) * inv + EPS)
    return o * _silu(g2.astype(F32))


def _ctx_mixer_units(rq, rk, rv, rg, nq, nk, nv, df, db, df2, db2, mix_ref, fwd_ref, bwd_ref,
                     layer, state_layers):
    n_seq = fwd_ref.shape[0]
    seq = rq.shape[0] // n_seq
    nc = seq // CHUNK

    thunks = []
    for hp in range(HEAD_PAIRS):
        cols = slice(hp * LANES, (hp + 1) * LANES)
        cols2 = slice(hp * 2 * LANES, (hp + 1) * 2 * LANES)
        shared = {}

        def scan(hp=hp, cols=cols, cols2=cols2, shared=shared):
            tabs = _retention_tables(df[:, cols], db[:, cols], df2[:, cols2], db2[:, cols2])
            shared["tabs"] = tabs
            for b in range(n_seq):
                rows = [slice(b * seq + c * CHUNK, b * seq + (c + 1) * CHUNK) for c in range(nc)]
                kv = [_chunk_kv(rk[r, cols], rv[r, cols], tabs) for r in rows]
                sf = [jnp.zeros((LANES, LANES), F32)]
                for c in range(nc):
                    sf.append(sf[-1] * tabs["cdf"] + kv[c][:LANES])
                sb = [jnp.zeros((LANES, LANES), F32)]
                for c in reversed(range(nc)):
                    sb.append(sb[-1] * tabs["cdb"] + kv[c][LANES:])
                for ref, s in ((fwd_ref, sf[nc]), (bwd_ref, sb[nc])):
                    blocks = (s[:HEAD_DIM, :HEAD_DIM],
                              pltpu.roll(s[HEAD_DIM:], HEAD_DIM, 1)[:, :HEAD_DIM])
                    for h, blk in enumerate(blocks):
                        if state_layers is None:
                            ref[b, 2 * hp + h] = blk
                        else:
                            for l in range(state_layers):
                                ref[b, l, 2 * hp + h] = blk if l == layer else jnp.zeros_like(blk)
                shared[b] = [jnp.concatenate([sf[c], sb[nc - 1 - c]], axis=0).astype(BF16)
                             for c in range(nc)]

        def chunk_out(b, c, cols=cols, shared=shared):
            rows = slice(b * seq + c * CHUNK, b * seq + (c + 1) * CHUNK)
            tabs = shared["tabs"]
            o = _chunk_mix(rq[rows, cols], rk[rows, cols], rv[rows, cols], shared[b][c], tabs)
            mix_ref[rows, cols] = _chunk_norm_gate(o, rg[rows, cols], tabs).astype(BF16)

        def attention(b, hp=hp, cols=cols, shared=shared):
            head_a = shared["tabs"]["head_a"]
            tok = slice(b * seq, (b + 1) * seq)
            s = _dot_nt(nq[tok, cols], _split_heads(nk[tok, cols], head_a, 0))
            es, rinv = [], []
            for h in range(2):
                sh = s[:, h * seq:(h + 1) * seq]
                e = jnp.exp(sh - jnp.max(sh, axis=-1, keepdims=True))
                rinv.append(1.0 / jnp.sum(e, axis=-1, keepdims=True))
                es.append(e.astype(BF16))
            o = _dot(jnp.concatenate(es, axis=1), _split_heads(nv[tok, cols], head_a, 0))
            o = o * jnp.where(head_a, rinv[0], rinv[1])
            mix_ref[tok, WIDTH + hp * LANES:WIDTH + (hp + 1) * LANES] = o.astype(BF16)

        thunks.append(scan)
        for b in range(n_seq):
            thunks += [functools.partial(chunk_out, b, c) for c in range(nc)]
            thunks.append(functools.partial(attention, b))
    return thunks


def _inproj_mix_kernel(*refs, n_cast, n_alias, layer, state_layers):
    n_proj, n_mix = 7, 11
    outs = refs[n_proj + n_mix + n_cast + n_alias:]
    _cast_blocks(refs[n_proj + n_mix:n_proj + n_mix + n_cast], outs[4:])
    groups = _inproj_groups(*refs[:n_proj], outs[0], (), n_alias=0, layer=0, width=2 * LANES)
    units = _ctx_mixer_units(*refs[n_proj:n_proj + n_mix], outs[1], outs[2], outs[3],
                             layer, state_layers)
    done = 0
    for u, unit in enumerate(units):
        upto = (u + 1) * len(groups) // len(units)
        started = [(finish, start()) for start, finish in groups[done:upto]]
        done = upto
        unit()
        for finish, result in started:
            finish(result)


def _inproj_latent_and_ctx_mixer(x, mods, g_pre, w, layer, seq_len, tm, rope_tabs,
                                 p_ctx, dec, batch, seq, states, cast=()):
    n = x.shape[0]
    depth = w.shape[0]
    n_tiles = n // tm
    cast_arrays, cast_in, cast_out, cast_shapes = _cast_streams(cast, n_tiles)
    n_seq = batch // n_tiles
    assert n_seq * n_tiles == batch and seq_len % tm == 0
    assert (states[0] is None) == (states[1] is None)
    creates = states[0] is None
    tiles_per_seq = seq_len // tm
    group = lambda g: pl.BlockSpec((n_seq * seq, WIDTH), lambda i, g=g: (i, g))
    vec = lambda width: pl.BlockSpec((1, width), lambda i: (0, 0))
    in_specs = [
        pl.BlockSpec((tm, D_MODEL), lambda i: (i, 0)),
        pl.BlockSpec((None, 1, 6, D_MODEL), lambda i: (layer, 1 + i // tiles_per_seq, 0, 0)),
        pl.BlockSpec((None, 1, D_MODEL), lambda i: (layer, 0, 0)),
        pl.BlockSpec((None, D_MODEL, IN_WIDTH), lambda i: (layer, 0, 0),
                     pipeline_mode=pl.Buffered(1)),
    ] + [pl.BlockSpec((tm, LANES), lambda i: (i % tiles_per_seq, 0))] * 3 + [
        group(g) for g in range(N_GROUPS)] + [vec(WIDTH), vec(WIDTH), vec(2 * WIDTH),
                                              vec(2 * WIDTH)]
    args = [x, mods, g_pre, w, *rope_tabs, *([p_ctx] * N_GROUPS), *dec, *cast_arrays]
    in_specs = in_specs + cast_in
    aliases = {}
    if creates:
        state_spec = pl.BlockSpec((n_seq, depth, N_HEADS, HEAD_DIM, HEAD_DIM),
                                  lambda i: (i, 0, 0, 0, 0))
    else:
        state_spec = pl.BlockSpec((n_seq, None, N_HEADS, HEAD_DIM, HEAD_DIM),
                                  lambda i: (i, layer, 0, 0, 0))
        for j, prev in enumerate(states):
            aliases[len(args)] = 2 + j
            in_specs.append(pl.BlockSpec(memory_space=pl.ANY))
            args.append(prev)
    state_shape = jax.ShapeDtypeStruct((batch, depth, N_HEADS, HEAD_DIM, HEAD_DIM), F32)
    results = pl.pallas_call(
        functools.partial(_inproj_mix_kernel, n_cast=len(cast), n_alias=len(aliases),
                          layer=layer, state_layers=depth if creates else None),
        grid=(n_tiles,),
        in_specs=in_specs,
        out_specs=[
            pl.BlockSpec((tm, IN_WIDTH), lambda i: (i, 0)),
            pl.BlockSpec((n_seq * seq, 2 * WIDTH), lambda i: (i, 0)),
            state_spec, state_spec,
        ] + cast_out,
        out_shape=[
            jax.ShapeDtypeStruct((n, IN_WIDTH), BF16),
            jax.ShapeDtypeStruct((batch * seq, 2 * WIDTH), BF16),
            state_shape, state_shape,
        ] + cast_shapes,
        input_output_aliases=aliases,
        compiler_params=_cparams(1),
        name="inproj_latent_ctx_mixer",
    )(*args)
    return list(results[:4]) + [r.reshape(w.shape) for r, w in zip(results[4:], cast)]


def _lat_retention_kernel(q_ref, k_ref, v_ref, g_ref, df, db, df2, db2, s0_ref, o_ref,
                          kv_ref, st_ref):
    nc = q_ref.shape[0] // CHUNK
    tabs = _retention_tables(df[...], db[...], df2[...], db2[...])

    def rows(c):
        return pl.ds(pl.multiple_of(c * CHUNK, CHUNK), CHUNK)

    def kv_body(c, carry):
        kv_ref[c] = _chunk_kv(k_ref[rows(c), :], v_ref[rows(c), :], tabs)
        return carry
    lax.fori_loop(0, nc, kv_body, 0, unroll=RET_UNROLL)

    def fwd_body(c, s):
        st_ref[c, :LANES, :] = s.astype(BF16)
        return s * tabs["cdf"] + kv_ref[c, :LANES, :]
    lax.fori_loop(0, nc, fwd_body, s0_ref[0, 0, 0])

    def bwd_body(i, s):
        c = nc - 1 - i
        st_ref[c, LANES:, :] = s.astype(BF16)
        return s * tabs["cdb"] + kv_ref[c, LANES:, :]
    lax.fori_loop(0, nc, bwd_body, s0_ref[0, 0, 1])

    def out_body(c, carry):
        r = rows(c)
        o = _chunk_mix(q_ref[r, :], k_ref[r, :], v_ref[r, :], st_ref[c], tabs)
        o_ref[r, :] = _chunk_norm_gate(o, g_ref[r, :], tabs).astype(BF16)
        return carry
    lax.fori_loop(0, nc, out_body, 0, unroll=RET_UNROLL)


def _lat_retention(p, dec, s0, batch, seq):
    nc = seq // CHUNK
    group = lambda g: pl.BlockSpec((seq, LANES), lambda b, h, g=g: (b, g * HEAD_PAIRS + h))
    vec = lambda w: pl.BlockSpec((1, w), lambda b, h: (0, h))
    return pl.pallas_call(
        _lat_retention_kernel,
        grid=(batch, HEAD_PAIRS),
        in_specs=[group(g) for g in range(4)] + [vec(LANES), vec(LANES), vec(2 * LANES),
                                                 vec(2 * LANES)]
        + [pl.BlockSpec((1, 1, 2, LANES, LANES), lambda b, h: (b, h, 0, 0, 0))],
        out_specs=pl.BlockSpec((seq, LANES), lambda b, h: (b, h)),
        out_shape=jax.ShapeDtypeStruct((batch * seq, WIDTH), BF16),
        scratch_shapes=[pltpu.VMEM((nc, 2 * LANES, LANES), F32),
                        pltpu.VMEM((nc, 2 * LANES, LANES), BF16)],
        compiler_params=_cparams(2),
        name="latent_retention",
    )(*([p] * 4), *dec, s0)


def _block_diag_states(s_f, s_b):
    def bd(s):
        b = s.shape[0]
        s = s.reshape(b, HEAD_PAIRS, 2, HEAD_DIM, HEAD_DIM)
        z = jnp.zeros_like(s[:, :, 0])
        top = jnp.concatenate([s[:, :, 0], z], axis=-1)
        bot = jnp.concatenate([z, s[:, :, 1]], axis=-1)
        return jnp.concatenate([top, bot], axis=-2)
    return jnp.stack([bd(s_f), bd(s_b)], axis=2)


def _na_kernel(q_ref, k_ref, v_ref, ck_ref, cv_ref, bias_ref, o_ref, *, lo, hi):
    rows_total = k_ref.shape[0] // GRID_W
    lane = lax.broadcasted_iota(jnp.int32, (1, LANES), 1)
    head_a = lane < HEAD_DIM
    wins, cls = [], []
    for i in range(NA_ROWS):
        r = pl.program_id(1) * NA_ROWS + i
        rs = jnp.clip(r - NA_KH // 2, 0, rows_total - NA_KH)
        wins.append(pl.ds(pl.multiple_of(rs * GRID_W, GRID_W), NA_KH * GRID_W))
        cls.append(jnp.where(r < lo, r, jnp.where(r > hi, r - hi + lo, lo)))
    stages = [(hp, r0) for hp in range(HEAD_PAIRS) for r0 in range(0, NA_ROWS, NA_STAGE_ROWS)]

    def scores(stage):
        hp, r0 = stage
        cols = slice(hp * LANES, (hp + 1) * LANES)
        qq = jnp.concatenate(
            [_split_heads(q_ref[i * GRID_W:(i + 1) * GRID_W, cols], head_a, 0)
             for i in range(r0, r0 + NA_STAGE_ROWS)], axis=0)
        s_ctx = _dot_nt(qq, ck_ref[0, :, cols])
        s_loc = [_dot_nt(qq[j * 2 * GRID_W:(j + 1) * 2 * GRID_W], k_ref[wins[r0 + j], cols])
                 for j in range(NA_STAGE_ROWS)]
        return s_ctx, s_loc

    def softmax(stage, s_ctx, s_loc):
        hp, r0 = stage
        m_ctx = jnp.max(s_ctx, axis=-1, keepdims=True)
        m = []
        for j in range(NA_STAGE_ROWS):
            blk = slice(j * 2 * GRID_W, (j + 1) * 2 * GRID_W)
            s_loc[j] = s_loc[j] + bias_ref[cls[r0 + j], hp]
            m.append(jnp.maximum(jnp.max(s_loc[j], axis=-1, keepdims=True), m_ctx[blk]))
        e_ctx = jnp.exp(s_ctx - jnp.concatenate(m, axis=0))
        den_ctx = jnp.sum(e_ctx, axis=-1, keepdims=True)
        e_loc, den = [], []
        for j in range(NA_STAGE_ROWS):
            blk = slice(j * 2 * GRID_W, (j + 1) * 2 * GRID_W)
            e = jnp.exp(s_loc[j] - m[j])
            den.append(jnp.sum(e, axis=-1, keepdims=True) + den_ctx[blk])
            e_loc.append(e.astype(BF16))
        return e_ctx.astype(BF16), e_loc, den

    def weighted_values(stage, e_ctx, e_loc, den):
        hp, r0 = stage
        cols = slice(hp * LANES, (hp + 1) * LANES)
        o_ctx = _dot(e_ctx, cv_ref[0, :, cols])
        for j in range(NA_STAGE_ROWS):
            i = r0 + j
            blk = slice(j * 2 * GRID_W, (j + 1) * 2 * GRID_W)
            o = (_dot(e_loc[j], v_ref[wins[i], cols]) + o_ctx[blk]) / den[j]
            o_ref[i * GRID_W:(i + 1) * GRID_W, cols] = jnp.where(
                head_a, o[:GRID_W], o[GRID_W:]).astype(BF16)

    nxt = scores(stages[0])
    pending = None
    for n, stage in enumerate(stages):
        s_ctx, s_loc = nxt
        if n + 1 < len(stages):
            nxt = scores(stages[n + 1])
        probs = softmax(stage, s_ctx, s_loc)
        if pending is not None:
            weighted_values(*pending)
        pending = (stage,) + probs
    weighted_values(*pending)


def _mod_bias_kernel(cond_ref, w_ref, b_ref, rpb_ref, win_ref, mod_ref, bias_ref, win_bf_ref, *,
                     dr_first, n_dr):
    s = _silu(cond_ref[...]).astype(BF16)
    mod_ref[...] = _dot(s, w_ref[...].astype(BF16)) + b_ref[...]
    _cast_blocks([win_ref], [win_bf_ref])

    qc = lax.broadcasted_iota(jnp.int32, (GRID_W, LANES), 0)
    lane = lax.broadcasted_iota(jnp.int32, (GRID_W, LANES), 1)
    kc = lane & (GRID_W - 1)
    cs = jnp.clip(qc - NA_KW // 2, 0, GRID_W - NA_KW)
    inside = (kc >= cs) & (kc < cs + NA_KW)
    first = lane < GRID_W
    lo_half, hi_half = [], []
    for dr in range(n_dr):
        line = jnp.broadcast_to(rpb_ref[pl.ds(dr, 1), :], (GRID_W, LANES))
        lo_half.append(pltpu.roll(line, LANES - (NA_KW - 1), 1, stride=1, stride_axis=0))
        hi_half.append(pltpu.roll(line, GRID_W - (NA_KW - 1), 1, stride=1, stride_axis=0))
    for cls, dr0 in enumerate(dr_first):
        for jp in range(NA_KH // 2):
            dr = dr0 + 2 * jp
            tile = jnp.where(inside, jnp.where(first, lo_half[dr], hi_half[dr + 1]), NEG_INF)
            bias_ref[cls, :, jp * LANES:(jp + 1) * LANES] = tile


def _modulation_and_na_bias(cond, ada_w, ada_b, na_rpb, rows_total, w_in):
    depth, heads, n_dr, n_dc = na_rpb.shape
    kh = NA_KH
    lo, hi = kh // 2, rows_total - kh // 2 - 1
    reps = list(range(lo)) + [lo] + list(range(hi + 1, rows_total))
    dr_first = tuple(int(np.clip(r - kh // 2, 0, rows_total - kh)) - r + kh - 1 for r in reps)
    n_dr_pad = -(-n_dr // HALO) * HALO
    lines = jnp.pad(na_rpb, ((0, 0), (0, 0), (0, n_dr_pad - n_dr), (0, LANES - n_dc)))
    mod_cols = 6 * D_MODEL // heads
    assert mod_cols % LANES == 0
    cast_arrays, cast_in, cast_out, cast_shapes = _cast_streams(
        (w_in,), depth * heads, lambda l, h: l * heads + h)
    mod, bias, w_in_b = pl.pallas_call(
        functools.partial(_mod_bias_kernel, dr_first=dr_first, n_dr=n_dr),
        grid=(depth, heads),
        in_specs=[
            pl.BlockSpec((COND_ROWS, D_MODEL), lambda l, h: (0, 0)),
            pl.BlockSpec((None, D_MODEL, mod_cols), lambda l, h: (l, 0, h)),
            pl.BlockSpec((None, 1, mod_cols), lambda l, h: (l, 0, h)),
            pl.BlockSpec((None, None, n_dr_pad, LANES), lambda l, h: (l, h, 0, 0)),
        ] + cast_in,
        out_specs=[
            pl.BlockSpec((None, COND_ROWS, mod_cols), lambda l, h: (l, 0, h)),
            pl.BlockSpec((None, len(reps), None, GRID_W, kh * GRID_W),
                         lambda l, h: (l, 0, h // 2, h % 2, 0)),
        ] + cast_out,
        out_shape=[
            jax.ShapeDtypeStruct((depth, COND_ROWS, 6 * D_MODEL), F32),
            jax.ShapeDtypeStruct((depth, len(reps), HEAD_PAIRS, 2 * GRID_W, kh * GRID_W), F32),
        ] + cast_shapes,
        compiler_params=_cparams(2),
        name="modulation_and_na_bias",
    )(cond, ada_w, ada_b.reshape(depth, 1, 6 * D_MODEL), lines, *cast_arrays)
    return mod, bias, lo, hi, w_in_b.reshape(w_in.shape)


def _neighbourhood_attention(p, ctx_k, ctx_v, bias, layer, lo, hi, batch, seq):
    rows_total = seq // GRID_W
    past = ctx_k.shape[1]

    groups = rows_total // NA_ROWS
    return pl.pallas_call(
        functools.partial(_na_kernel, lo=lo, hi=hi),
        grid=(batch, groups),
        in_specs=[
            pl.BlockSpec((NA_ROWS * GRID_W, WIDTH), lambda b, g: (b * groups + g, 4)),
            pl.BlockSpec((seq, WIDTH), lambda b, g: (b, 5), pipeline_mode=pl.Buffered(1)),
            pl.BlockSpec((seq, WIDTH), lambda b, g: (b, 6), pipeline_mode=pl.Buffered(1)),
            pl.BlockSpec((1, past, WIDTH), lambda b, g: (b, 0, 0), pipeline_mode=pl.Buffered(1)),
            pl.BlockSpec((1, past, WIDTH), lambda b, g: (b, 0, 0), pipeline_mode=pl.Buffered(1)),
            pl.BlockSpec((None,) + bias.shape[1:], lambda b, g: (layer, 0, 0, 0, 0),
                         pipeline_mode=pl.Buffered(1)),
        ],
        out_specs=pl.BlockSpec((NA_ROWS * GRID_W, WIDTH), lambda b, g: (b * groups + g, 0)),
        out_shape=jax.ShapeDtypeStruct((batch * seq, WIDTH), BF16),
        compiler_params=_cparams(2),
        name="neighbourhood_attention",
    )(p, p, p, ctx_k, ctx_v, bias)


def _post_kernel(xm, xp, xn, am, ap, an, bm, bp, bn, mod_head, mod_tail, g_ref, wo, wu, cw, cb, wd,
                 o_ref, acc_ref, h_ref, hp_ref, x1_ref, y_ref, act_ref, *, seq_len, n_tiles):
    step = pl.program_id(0)
    tm = xm.shape[0]
    nj = tm // HALO
    tiles_per_seq = max(seq_len // tm, 1)
    assert (tm % seq_len == 0 and seq_len % nj == 0) or seq_len % tm == 0
    h_rows = h_ref.shape[1]
    seg_pitch = nj + HALO
    assert h_rows == 2 * HALO + HALO * seg_pitch and nj % HEAD_ROWS == 0
    n_ct = D_MODEL // LANES
    g = g_ref[...]

    def ext(main, prev, nxt):
        lo = prev[...].astype(F32)[prev.shape[0] - HALO:]
        hi = nxt[...].astype(F32)[:HALO]
        return jnp.concatenate([lo, main[...].astype(F32), hi], axis=0)

    chunks = [(c0, min(FF_CHUNK, D_FF - c0)) for c0 in range(0, D_FF, FF_CHUNK)]
    par = step % 2
    mod_h = mod_head[0]
    n_blocks = tm // HEAD_ROWS

    def head_matmul():
        mixed = jnp.concatenate([ext(am, ap, an), ext(bm, bp, bn)], axis=1).astype(BF16)
        y_ref[...] = _dot(mixed, wo[...])

    def head_rows(x, y):
        x1 = x + mod_h[2:3] * _rms(y, g[0:1])
        return x1, _rms(x1, g[1:2]) * (1.0 + mod_h[4:5]) + mod_h[3:4]

    def head_block(blk):
        if blk < n_blocks:
            rows = slice(blk * HEAD_ROWS, (blk + 1) * HEAD_ROWS)
            erows = slice(HALO + blk * HEAD_ROWS, HALO + (blk + 1) * HEAD_ROWS)
            x1, h = head_rows(xm[rows, :], y_ref[erows, :])
            x1_ref[par, rows, :] = x1
            t0 = blk * HEAD_ROWS
            hrow = HALO + (t0 // nj) * seg_pitch + t0 % nj
            for c in range(n_ct):
                h_ref[c, hrow:hrow + HEAD_ROWS, :] = h[:, c * LANES:(c + 1) * LANES]
        else:
            x = jnp.concatenate([xp[...], xn[...]], axis=0)
            y = jnp.concatenate([y_ref[0:HALO, :], y_ref[HALO + tm:2 * HALO + tm, :]], axis=0)
            _, h = head_rows(x, y)
            t = jnp.minimum(step, n_tiles - 1) % tiles_per_seq
            ridx = lax.broadcasted_iota(jnp.int32, h.shape, 0)
            keep = ((ridx >= HALO) | (t != 0)) & ((ridx < HALO) | (t != tiles_per_seq - 1))
            h = jnp.where(keep, h, 0.0)
            for c in range(n_ct):
                h_ref[c, 0:HALO, :] = h[:HALO, c * LANES:(c + 1) * LANES]
                h_ref[c, h_rows - HALO:h_rows, :] = h[HALO:, c * LANES:(c + 1) * LANES]

    def head_permute():
        def group(rows):
            return jnp.concatenate([h_ref[c, rows, :] for c in range(n_ct)], axis=1)
        def put(row0, first, second):
            hp_ref[row0:row0 + 2 * HALO, :] = jnp.concatenate(
                [group(first), group(second)], axis=0).astype(BF16)
        for j in range(0, nj, 2):
            put(j * HALO, pl.ds(HALO + j, HALO, stride=seg_pitch),
                pl.ds(HALO + j + 1, HALO, stride=seg_pitch))
        put(tm, pl.ds(0, HALO), pl.ds(h_rows - HALO, HALO))

    def tail(interleaved):
        mod = mod_tail[0]

        def conv(u, cols):
            w = cw[:, cols]
            sub = lax.broadcasted_iota(jnp.int32, (HALO, u.shape[1]), 0)
            before = jnp.where(sub == 0, u[tm + HALO - 1:tm + HALO],
                               pltpu.roll(u[tm - HALO:tm], 1, 0))
            after = jnp.where(sub == HALO - 1, u[tm + HALO:tm + HALO + 1],
                              pltpu.roll(u[0:HALO], HALO - 1, 0))
            for s in range(1, HALO):
                if (s * nj) % seq_len == 0:
                    before = jnp.where(sub == s, 0.0, before)
                    after = jnp.where(sub == s - 1, 0.0, after)
            prev = jnp.concatenate([before, u[0:tm - HALO]], axis=0)
            nxt = jnp.concatenate([u[HALO:tm], after], axis=0)
            return prev * w[0:1] + u[0:tm] * w[1:2] + nxt * w[2:3] + cb[:, cols]

        def cols(ch):
            ca = slice(chunks[ch][0], chunks[ch][0] + chunks[ch][1])
            return ca, slice(D_FF + ca.start, D_FF + ca.stop)

        def up(ch):
            ca, cg = cols(ch)
            return _dot(hp_ref[...], wu[:, cg]), _dot(hp_ref[...], wu[:, ca])

        nxt_u = up(0)
        for ch in range(len(chunks)):
            ug, ua = nxt_u
            if ch + 1 < len(chunks):
                nxt_u = up(ch + 1)
            for blk in interleaved[ch]:
                head_block(blk)
            ca, cg = cols(ch)
            act_ref[:, ca] = (_silu(conv(ug, cg)) * conv(ua, ca)).astype(BF16)
        ffn = mod[5:6] * _rms(_dot(act_ref[...], wd[...]), g[2:3])
        for c in range(n_ct):
            acc_ref[c] = ffn[:, c * LANES:(c + 1) * LANES]

        for s in range(HALO):
            for jb in range(nj // HALO):
                t0 = s * nj + HALO * jb
                rows = pl.ds(HALO * HALO * jb + s, HALO, stride=HALO)
                back = jnp.concatenate([acc_ref[c, rows, :] for c in range(n_ct)], axis=1)
                o_ref[t0:t0 + HALO, :] = x1_ref[1 - par, t0:t0 + HALO, :] + back

    @pl.when(step == 0)
    def _():
        head_matmul()
        for blk in range(n_blocks + 1):
            head_block(blk)
        head_permute()

    @pl.when(step > 0)
    def _():
        head_matmul()
        tail([[blk for blk in range(n_blocks + 1) if blk * len(chunks) // (n_blocks + 1) == ch]
              for ch in range(len(chunks))])
        head_permute()


def _post(x, mix_a, mix_b, mods, gains, wo, wu, cw, cb, wd, layer, seq_len, tm, latent):
    n = x.shape[0]
    nt = n // tm
    (mix_a, col_a), (mix_b, col_b) = mix_a, mix_b
    tiles_per_seq = max(seq_len // tm, 1)
    cond_row = (lambda i: 1 + i // tiles_per_seq) if latent else (lambda i: 0)

    head_tile = lambda i: jnp.minimum(i, nt - 1)
    tail_tile = lambda i: jnp.maximum(i - 1, 0)

    def triple(width, halo_rows, col=0):
        per = tm // halo_rows
        last = n // halo_rows - 1
        return [
            pl.BlockSpec((tm, width), lambda i: (head_tile(i), col)),
            pl.BlockSpec((halo_rows, width),
                         lambda i: (jnp.maximum(head_tile(i) * per - 1, 0), col)),
            pl.BlockSpec((halo_rows, width),
                         lambda i: (jnp.minimum((head_tile(i) + 1) * per, last), col)),
        ]

    const = lambda *shape: pl.BlockSpec((None,) + shape, lambda i: (layer,) + (0,) * len(shape),
                                        pipeline_mode=pl.Buffered(1))
    in_specs = (triple(D_MODEL, HALO) + triple(WIDTH, 2 * HALO, col_a)
                + triple(WIDTH, 2 * HALO, col_b) + [
        pl.BlockSpec((None, 1, 6, D_MODEL), lambda i: (layer, cond_row(head_tile(i)), 0, 0)),
        pl.BlockSpec((None, 1, 6, D_MODEL), lambda i: (layer, cond_row(tail_tile(i)), 0, 0)),
        const(3, D_MODEL),
        const(2 * WIDTH, D_MODEL),
        const(D_MODEL, 2 * D_FF),
        const(3, 2 * D_FF),
        const(1, 2 * D_FF),
        const(D_FF, D_MODEL),
    ])
    return pl.pallas_call(
        functools.partial(_post_kernel, seq_len=seq_len, n_tiles=nt),
        grid=(nt + 1,),
        in_specs=in_specs,
        out_specs=pl.BlockSpec((tm, D_MODEL), lambda i: (tail_tile(i), 0)),
        out_shape=jax.ShapeDtypeStruct((n, D_MODEL), F32),
        scratch_shapes=[pltpu.VMEM((D_MODEL // LANES, tm, LANES), F32),
                        pltpu.VMEM((D_MODEL // LANES, 2 * HALO + HALO * (tm // HALO + HALO), LANES),
                                   F32),
                        pltpu.VMEM((tm + 2 * HALO, D_MODEL), BF16),
                        pltpu.VMEM((2, tm, D_MODEL), F32),
                        pltpu.VMEM((tm + 2 * HALO, D_MODEL), F32),
                        pltpu.VMEM((tm, D_FF), BF16)],
        compiler_params=_cparams(1),
        name="post_latent" if latent else "post_context",
    )(x, x, x, mix_a, mix_a, mix_a, mix_b, mix_b, mix_b, mods, mods, gains, wo, wu, cw, cb, wd)


def kernel(x_prompt, x_sample, c, cache_na_k, cache_na_v, state_ret_fwd, state_ret_bwd, c_ctx,
           ada_w, ada_b, g_pre_mix, g_post_mix, g_pre_ffn, g_post_ffn, w_in,
           ret_decay_fwd, ret_decay_bwd, na_rpb, w_out, w_up, conv_w, conv_b, w_down):
    depth = w_in.shape[0]
    batch, seq, _ = x_prompt.shape
    dec_batch, dec_seq, _ = x_sample.shape
    past = cache_na_k.shape[2]
    tm = TOKEN_TILE
    assert 1 + dec_batch <= COND_ROWS

    cond = jnp.concatenate(
        [c_ctx[None, :], c, jnp.zeros((COND_ROWS - 1 - dec_batch, D_MODEL), F32)], axis=0)
    mods, bias, lo, hi, w_in_b = _modulation_and_na_bias(
        cond, ada_w, ada_b, na_rpb, dec_seq // GRID_W, w_in)
    mods = mods.reshape(depth, COND_ROWS, 6, D_MODEL)
    rope_tabs = _rope_tables(dec_seq)

    g_pre = g_pre_mix.reshape(depth, 1, D_MODEL)
    gains = jnp.stack([g_post_mix, g_pre_ffn, g_post_ffn], axis=1)
    cb = conv_b.reshape(depth, 1, 2 * D_FF)

    y_p = x_prompt.reshape(batch * seq, D_MODEL)
    y_s = x_sample.reshape(dec_batch * dec_seq, D_MODEL)
    new_k = new_v = new_sf = new_sb = None
    for l in range(depth):
        dec = (jnp.repeat(ret_decay_fwd[l], HEAD_DIM)[None, :],
               jnp.repeat(ret_decay_bwd[l], HEAD_DIM)[None, :],
               jnp.repeat(ret_decay_fwd[l], LANES)[None, :],
               jnp.repeat(ret_decay_bwd[l], LANES)[None, :])

        p_c, new_k, new_v = _inproj_context(y_p, mods, g_pre, w_in_b, l, seq, tm, (new_k, new_v))
        p_s, mix_c, new_sf, new_sb, *cast = _inproj_latent_and_ctx_mixer(
            y_s, mods, g_pre, w_in_b, l, dec_seq, tm, rope_tabs, p_c, dec, batch, seq,
            (new_sf, new_sb), cast=(w_out, w_up, w_down) if l == 0 else ())
        if l == 0:
            post_params = (mods, gains, cast[0], cast[1], conv_w, cb, cast[2])
        y_p = _post(y_p, (mix_c, 0), (mix_c, 1), *post_params, l, seq, tm, latent=False)

        s0 = _block_diag_states(state_ret_fwd[:, l], state_ret_bwd[:, l])
        ret_s = _lat_retention(p_s, dec, s0, dec_batch, dec_seq)
        ck = cache_na_k[:, l].reshape(dec_batch, past, WIDTH).astype(BF16)
        cv = cache_na_v[:, l].reshape(dec_batch, past, WIDTH).astype(BF16)
        na_s = _neighbourhood_attention(p_s, ck, cv, bias, l, lo, hi, dec_batch, dec_seq)
        y_s = _post(y_s, (ret_s, 0), (na_s, 0), *post_params, l, dec_seq, tm, latent=True)

    return (y_p.reshape(batch, seq, D_MODEL),
            y_s.reshape(dec_batch, dec_seq, D_MODEL),
            new_k.reshape(batch, depth, seq, N_HEADS, HEAD_DIM),
            new_v.reshape(batch, depth, seq, N_HEADS, HEAD_DIM),
            new_sf, new_sb)
```

```python
import functools

import numpy as np
import jax
import jax.numpy as jnp
from jax import lax
from jax.experimental import pallas as pl
from jax.experimental.pallas import tpu as pltpu

F32 = jnp.float32
BF16 = jnp.bfloat16

D_MODEL = 1024
HEAD_DIM = 64
N_HEADS = 8
HEAD_PAIRS = N_HEADS // 2
LANES = 128
WIDTH = N_HEADS * HEAD_DIM
N_GROUPS = 7
IN_WIDTH = N_GROUPS * WIDTH
D_FF = 2816
TOKEN_TILE = 512
FF_CHUNK = 768
HEAD_ROWS = 32
CHUNK = 128
RET_UNROLL = 32
GRID_W = 64
NA_KH = 8
NA_KW = 16
NA_STAGE_ROWS = 8
NA_ROWS = 8
ROPE_BASE = 10000.0
EPS = 1e-6
NEG_INF = -1e9
SUBLANES = 8
HALO = SUBLANES
COND_ROWS = SUBLANES
VMEM_LIMIT = 56 * 1024 * 1024


def _cparams(n_grid):
    return pltpu.CompilerParams(
        dimension_semantics=("arbitrary",) * n_grid, vmem_limit_bytes=VMEM_LIMIT)


def _rms(x, g):
    ms = jnp.mean(x * x, axis=-1, keepdims=True)
    return x * lax.rsqrt(ms + EPS) * g


def _silu(x):
    return x * jax.nn.sigmoid(x)


def _log_sigmoid(x):
    return jnp.minimum(x, 0.0) - jnp.log1p(jnp.exp(-jnp.abs(x)))


def _dot(a, b):
    return jnp.dot(a, b, preferred_element_type=F32)


def _dot_nt(a, b):
    return lax.dot_general(a, b, (((1,), (1,)), ((), ())), preferred_element_type=F32)


def _dot_tn(a, b):
    return lax.dot_general(a, b, (((0,), (0,)), ((), ())), preferred_element_type=F32)


def _cast_streams(weights, n_steps, flat_step=lambda i: i):
    arrays = [w.reshape(-1, w.shape[-1]) for w in weights]
    in_specs, out_specs, out_shapes = [], [], []
    for a in arrays:
        rows = a.shape[0] // n_steps
        assert rows * n_steps == a.shape[0] and rows % (2 * HALO) == 0
        in_specs.append(pl.BlockSpec((rows, a.shape[1]), lambda *idx: (flat_step(*idx), 0)))
        out_specs.append(pl.BlockSpec((rows, a.shape[1]), lambda *idx: (flat_step(*idx), 0)))
        out_shapes.append(jax.ShapeDtypeStruct(a.shape, BF16))
    return arrays, in_specs, out_specs, out_shapes


def _cast_blocks(in_refs, out_refs):
    for src, dst in zip(in_refs, out_refs):
        dst[...] = src[...].astype(dst.dtype)


def _inproj_kernel(*refs, n_alias, layer):
    outs = refs[4 + n_alias:]
    for start, finish in _inproj_groups(*refs[:4], None, None, None, outs[0], outs[1:], n_alias,
                                        layer):
        finish(start())


def _inproj_groups(x_ref, mod_ref, g_ref, w_ref, cos_ref, sin_up_ref, sin_dn_ref, p_ref, kv_refs,
                   n_alias, layer, width=WIDTH):
    rope = cos_ref is not None
    emit_kv = len(kv_refs) > 0
    assert WIDTH % width == 0 and (width == WIDTH or not emit_kv)
    mod = mod_ref[0]
    h = (_rms(x_ref[...], g_ref[...]) * (1.0 + mod[1:2]) + mod[0:1]).astype(BF16)

    def start(n):
        return _dot(h, w_ref[:, n * width:(n + 1) * width])

    def finish(n, pg):
        cols = slice(n * width, (n + 1) * width)
        g = n * width // WIDTH
        if rope and g < 2:
            parts = []
            for j in range(width // LANES):
                xj = pg[:, j * LANES:(j + 1) * LANES]
                parts.append(xj * cos_ref[...]
                             + pltpu.roll(xj, 16, 1) * sin_up_ref[...]
                             + pltpu.roll(xj, LANES - 16, 1) * sin_dn_ref[...])
            pg = jnp.concatenate(parts, axis=1)
        if g in (0, 4):
            pg = pg * (HEAD_DIM ** -0.5)
        p_ref[:, cols] = pg.astype(BF16)
        if emit_kv and g >= 5:
            kv_ref = kv_refs[g - 5]
            seq = kv_ref.shape[-2]
            for j in range(kv_ref.shape[0]):
                if n_alias:
                    kv_ref[j] = pg[j * seq:(j + 1) * seq]
                else:
                    for l in range(kv_ref.shape[1]):
                        kv_ref[j, l] = (pg[j * seq:(j + 1) * seq] if l == layer
                                        else jnp.zeros((seq, WIDTH), F32))

    return [(functools.partial(start, n), functools.partial(finish, n))
            for n in range(IN_WIDTH // width)]


def _inproj_context(x, mods, g_pre, w, layer, seq_len, tm, kv_out):
    n = x.shape[0]
    depth = w.shape[0]
    in_specs = [
        pl.BlockSpec((tm, D_MODEL), lambda i: (i, 0)),
        pl.BlockSpec((None, 1, 6, D_MODEL), lambda i: (layer, 0, 0, 0)),
        pl.BlockSpec((None, 1, D_MODEL), lambda i: (layer, 0, 0)),
        pl.BlockSpec((None, D_MODEL, IN_WIDTH), lambda i: (layer, 0, 0),
                     pipeline_mode=pl.Buffered(1)),
    ]
    args = [x, mods, g_pre, w]
    out_specs = [pl.BlockSpec((tm, IN_WIDTH), lambda i: (i, 0))]
    out_shape = [jax.ShapeDtypeStruct((n, IN_WIDTH), BF16)]
    aliases = {}
    seqs_per_tile = tm // seq_len
    kv_shape = jax.ShapeDtypeStruct((n // seq_len, depth, seq_len, WIDTH), F32)
    assert (kv_out[0] is None) == (kv_out[1] is None)
    for j, prev in enumerate(kv_out):
        if prev is not None:
            aliases[len(args)] = 1 + j
            in_specs.append(pl.BlockSpec(memory_space=pl.ANY))
            args.append(prev)
            out_specs.append(pl.BlockSpec((seqs_per_tile, None, seq_len, WIDTH),
                                          lambda i: (i, layer, 0, 0)))
        else:
            out_specs.append(pl.BlockSpec((seqs_per_tile, depth, seq_len, WIDTH),
                                          lambda i: (i, 0, 0, 0)))
        out_shape.append(kv_shape)
    return pl.pallas_call(
        functools.partial(_inproj_kernel, n_alias=len(aliases), layer=layer),
        grid=(n // tm,),
        in_specs=in_specs,
        out_specs=out_specs,
        out_shape=out_shape,
        input_output_aliases=aliases,
        compiler_params=_cparams(1),
        name="inproj_context",
    )(*args)


def _rope_tables(seq_len):
    t = np.arange(seq_len)
    lane = np.arange(LANES)
    d = lane % HEAD_DIM
    pos = np.where(d[None, :] < HEAD_DIM // 2, (t // GRID_W)[:, None], (t % GRID_W)[:, None])
    pos = pos.astype(np.float32)
    half = HEAD_DIM // 2
    inv = np.power(np.float32(ROPE_BASE), -np.arange(0, half, 2, dtype=np.float32) / half)
    ang = pos * inv[d % (half // 2)][None, :]
    cos, sin = np.cos(ang), np.sin(ang)
    upper = (d % half) >= half // 2
    sin_up = np.where(upper[None, :], sin, 0.0)
    sin_dn = np.where(upper[None, :], 0.0, -sin)
    return (jnp.asarray(cos, F32), jnp.asarray(sin_up, F32), jnp.asarray(sin_dn, F32))


def _retention_tables(dec_f, dec_b, dec_f2, dec_b2):
    lgf, lgb = _log_sigmoid(dec_f), _log_sigmoid(dec_b)
    pos = lax.broadcasted_iota(jnp.int32, (CHUNK, LANES), 0).astype(F32)
    tabs = dict(
        qdf=jnp.exp(lgf * (pos + 1.0)), kdf=jnp.exp(lgf * (CHUNK - 1.0 - pos)),
        cdf=jnp.exp(lgf * float(CHUNK)),
        qdb=jnp.exp(lgb * (CHUNK - pos)), kdb=jnp.exp(lgb * pos),
        cdb=jnp.exp(lgb * float(CHUNK)),
    )
    lgf2, lgb2 = _log_sigmoid(dec_f2), _log_sigmoid(dec_b2)
    i = lax.broadcasted_iota(jnp.int32, (CHUNK, 2 * CHUNK), 0)
    j = lax.broadcasted_iota(jnp.int32, (CHUNK, 2 * CHUNK), 1) & (CHUNK - 1)
    diff = (i - j).astype(F32)
    tabs["decay"] = (jnp.where(diff >= 0, jnp.exp(lgf2 * jnp.maximum(diff, 0.0)), 0.0)
                     + jnp.where(diff <= 0, jnp.exp(lgb2 * jnp.maximum(-diff, 0.0)), 0.0))
    lane = lax.broadcasted_iota(jnp.int32, (1, LANES), 1)
    tabs["head_a"] = lane < HEAD_DIM
    r = lax.broadcasted_iota(jnp.int32, (2 * LANES, LANES), 0) & (LANES - 1)
    c = lax.broadcasted_iota(jnp.int32, (2 * LANES, LANES), 1)
    tabs["same_head"] = (r < HEAD_DIM) == (c < HEAD_DIM)
    return tabs


def _split_heads(x, head_a, axis):
    zero = jnp.zeros_like(x)
    return jnp.concatenate([jnp.where(head_a, x, zero), jnp.where(head_a, zero, x)], axis=axis)


def _chunk_kv(k2, v2, tabs):
    kf = k2.astype(F32)
    kk = jnp.concatenate([kf * tabs["kdf"], kf * tabs["kdb"]], axis=1).astype(BF16)
    return jnp.where(tabs["same_head"], _dot_tn(kk, v2), 0.0)


def _chunk_mix(q2, k2, v2, state, tabs):
    head_a = tabs["head_a"]
    s = _dot_nt(q2, _split_heads(k2, head_a, 0))
    p = (s * tabs["decay"]).astype(BF16)
    qf = q2.astype(F32)
    lhs = jnp.concatenate(
        [p, (qf * tabs["qdf"]).astype(BF16), (qf * tabs["qdb"]).astype(BF16)], axis=1)
    rhs = jnp.concatenate([_split_heads(v2, head_a, 0), state], axis=0)
    return _dot(lhs, rhs)


def _chunk_norm_gate(o, g2, tabs):
    head_a = tabs["head_a"]
    inv = 1.0 / HEAD_DIM
    sum_a = jnp.sum(jnp.where(head_a, o, 0.0), axis=-1, keepdims=True)
    sum_b = jnp.sum(jnp.where(head_a, 0.0, o), axis=-1, keepdims=True)
    d = o - jnp.where(head_a, sum_a, sum_b) * inv
    d2 = d * d
    var_a = jnp.sum(jnp.where(head_a, d2, 0.0), axis=-1, keepdims=True)
    var_b = jnp.sum(jnp.where(head_a, 0.0, d2), axis=-1, keepdims=True)
    o = d * lax.rsqrt(jnp.where(head_a, var_a, var_b) * inv + EPS)
    return o * _silu(g2.astype(F32))


def _ctx_mixer_units(rq, rk, rv, rg, nq, nk, nv, df, db, df2, db2, mix_ref, fwd_ref, bwd_ref,
                     layer, state_layers):
    n_seq = fwd_ref.shape[0]
    seq = rq.shape[0] // n_seq
    nc = seq // CHUNK

    thunks = []
    for hp in range(HEAD_PAIRS):
        cols = slice(hp * LANES, (hp + 1) * LANES)
        cols2 = slice(hp * 2 * LANES, (hp + 1) * 2 * LANES)
        shared = {}

        def scan(hp=hp, cols=cols, cols2=cols2, shared=shared):
            tabs = _retention_tables(df[:, cols], db[:, cols], df2[:, cols2], db2[:, cols2])
            shared["tabs"] = tabs
            for b in range(n_seq):
                rows = [slice(b * seq + c * CHUNK, b * seq + (c + 1) * CHUNK) for c in range(nc)]
                kv = [_chunk_kv(rk[r, cols], rv[r, cols], tabs) for r in rows]
                sf = [jnp.zeros((LANES, LANES), F32)]
                for c in range(nc):
                    sf.append(sf[-1] * tabs["cdf"] + kv[c][:LANES])
                sb = [jnp.zeros((LANES, LANES), F32)]
                for c in reversed(range(nc)):
                    sb.append(sb[-1] * tabs["cdb"] + kv[c][LANES:])
                for ref, s in ((fwd_ref, sf[nc]), (bwd_ref, sb[nc])):
                    blocks = (s[:HEAD_DIM, :HEAD_DIM],
                              pltpu.roll(s[HEAD_DIM:], HEAD_DIM, 1)[:, :HEAD_DIM])
                    for h, blk in enumerate(blocks):
                        if state_layers is None:
                            ref[b, 2 * hp + h] = blk
                        else:
                            for l in range(state_layers):
                                ref[b, l, 2 * hp + h] = blk if l == layer else jnp.zeros_like(blk)
                shared[b] = [jnp.concatenate([sf[c], sb[nc - 1 - c]], axis=0).astype(BF16)
                             for c in range(nc)]

        def chunk_out(b, c, cols=cols, shared=shared):
            rows = slice(b * seq + c * CHUNK, b * seq + (c + 1) * CHUNK)
            tabs = shared["tabs"]
            o = _chunk_mix(rq[rows, cols], rk[rows, cols], rv[rows, cols], shared[b][c], tabs)
            mix_ref[rows, cols] = _chunk_norm_gate(o, rg[rows, cols], tabs).astype(BF16)

        def attention(b, hp=hp, cols=cols, shared=shared):
            head_a = shared["tabs"]["head_a"]
            tok = slice(b * seq, (b + 1) * seq)
            s = _dot_nt(nq[tok, cols], _split_heads(nk[tok, cols], head_a, 0))
            es, rinv = [], []
            for h in range(2):
                sh = s[:, h * seq:(h + 1) * seq]
                e = jnp.exp(sh - jnp.max(sh, axis=-1, keepdims=True))
                rinv.append(1.0 / jnp.sum(e, axis=-1, keepdims=True))
                es.append(e.astype(BF16))
            o = _dot(jnp.concatenate(es, axis=1), _split_heads(nv[tok, cols], head_a, 0))
            o = o * jnp.where(head_a, rinv[0], rinv[1])
            mix_ref[tok, WIDTH + hp * LANES:WIDTH + (hp + 1) * LANES] = o.astype(BF16)

        thunks.append(scan)
        for b in range(n_seq):
            thunks += [functools.partial(chunk_out, b, c) for c in range(nc)]
            thunks.append(functools.partial(attention, b))
    return thunks


def _inproj_mix_kernel(*refs, n_cast, n_alias, layer, state_layers):
    n_proj, n_mix = 7, 11
    outs = refs[n_proj + n_mix + n_cast + n_alias:]
    _cast_blocks(refs[n_proj + n_mix:n_proj + n_mix + n_cast], outs[4:])
    groups = _inproj_groups(*refs[:n_proj], outs[0], (), n_alias=0, layer=0, width=2 * LANES)
    units = _ctx_mixer_units(*refs[n_proj:n_proj + n_mix], outs[1], outs[2], outs[3],
                             layer, state_layers)
    done = 0
    for u, unit in enumerate(units):
        upto = (u + 1) * len(groups) // len(units)
        started = [(finish, start()) for start, finish in groups[done:upto]]
        done = upto
        unit()
        for finish, result in started:
            finish(result)


def _inproj_latent_and_ctx_mixer(x, mods, g_pre, w, layer, seq_len, tm, rope_tabs,
                                 p_ctx, dec, batch, seq, states, cast=()):
    n = x.shape[0]
    depth = w.shape[0]
    n_tiles = n // tm
    cast_arrays, cast_in, cast_out, cast_shapes = _cast_streams(cast, n_tiles)
    n_seq = batch // n_tiles
    assert n_seq * n_tiles == batch and seq_len % tm == 0
    assert (states[0] is None) == (states[1] is None)
    creates = states[0] is None
    tiles_per_seq = seq_len // tm
    group = lambda g: pl.BlockSpec((n_seq * seq, WIDTH), lambda i, g=g: (i, g))
    vec = lambda width: pl.BlockSpec((1, width), lambda i: (0, 0))
    in_specs = [
        pl.BlockSpec((tm, D_MODEL), lambda i: (i, 0)),
        pl.BlockSpec((None, 1, 6, D_MODEL), lambda i: (layer, 1 + i // tiles_per_seq, 0, 0)),
        pl.BlockSpec((None, 1, D_MODEL), lambda i: (layer, 0, 0)),
        pl.BlockSpec((None, D_MODEL, IN_WIDTH), lambda i: (layer, 0, 0),
                     pipeline_mode=pl.Buffered(1)),
    ] + [pl.BlockSpec((tm, LANES), lambda i: (i % tiles_per_seq, 0))] * 3 + [
        group(g) for g in range(N_GROUPS)] + [vec(WIDTH), vec(WIDTH), vec(2 * WIDTH),
                                              vec(2 * WIDTH)]
    args = [x, mods, g_pre, w, *rope_tabs, *([p_ctx] * N_GROUPS), *dec, *cast_arrays]
    in_specs = in_specs + cast_in
    aliases = {}
    if creates:
        state_spec = pl.BlockSpec((n_seq, depth, N_HEADS, HEAD_DIM, HEAD_DIM),
                                  lambda i: (i, 0, 0, 0, 0))
    else:
        state_spec = pl.BlockSpec((n_seq, None, N_HEADS, HEAD_DIM, HEAD_DIM),
                                  lambda i: (i, layer, 0, 0, 0))
        for j, prev in enumerate(states):
            aliases[len(args)] = 2 + j
            in_specs.append(pl.BlockSpec(memory_space=pl.ANY))
            args.append(prev)
    state_shape = jax.ShapeDtypeStruct((batch, depth, N_HEADS, HEAD_DIM, HEAD_DIM), F32)
    results = pl.pallas_call(
        functools.partial(_inproj_mix_kernel, n_cast=len(cast), n_alias=len(aliases),
                          layer=layer, state_layers=depth if creates else None),
        grid=(n_tiles,),
        in_specs=in_specs,
        out_specs=[
            pl.BlockSpec((tm, IN_WIDTH), lambda i: (i, 0)),
            pl.BlockSpec((n_seq * seq, 2 * WIDTH), lambda i: (i, 0)),
            state_spec, state_spec,
        ] + cast_out,
        out_shape=[
            jax.ShapeDtypeStruct((n, IN_WIDTH), BF16),
            jax.ShapeDtypeStruct((batch * seq, 2 * WIDTH), BF16),
            state_shape, state_shape,
        ] + cast_shapes,
        input_output_aliases=aliases,
        compiler_params=_cparams(1),
        name="inproj_latent_ctx_mixer",
    )(*args)
    return list(results[:4]) + [r.reshape(w.shape) for r, w in zip(results[4:], cast)]


def _lat_retention_kernel(q_ref, k_ref, v_ref, g_ref, df, db, df2, db2, s0_ref, o_ref,
                          kv_ref, st_ref):
    nc = q_ref.shape[0] // CHUNK
    tabs = _retention_tables(df[...], db[...], df2[...], db2[...])

    def rows(c):
        return pl.ds(pl.multiple_of(c * CHUNK, CHUNK), CHUNK)

    def kv_body(c, carry):
        kv_ref[c] = _chunk_kv(k_ref[rows(c), :], v_ref[rows(c), :], tabs)
        return carry
    lax.fori_loop(0, nc, kv_body, 0, unroll=RET_UNROLL)

    def fwd_body(c, s):
        st_ref[c, :LANES, :] = s.astype(BF16)
        return s * tabs["cdf"] + kv_ref[c, :LANES, :]
    lax.fori_loop(0, nc, fwd_body, s0_ref[0, 0, 0], unroll=RET_UNROLL)

    def bwd_body(i, s):
        c = nc - 1 - i
        st_ref[c, LANES:, :] = s.astype(BF16)
        return s * tabs["cdb"] + kv_ref[c, LANES:, :]
    lax.fori_loop(0, nc, bwd_body, s0_ref[0, 0, 1], unroll=RET_UNROLL)

    def out_body(c, carry):
        r = rows(c)
        o = _chunk_mix(q_ref[r, :], k_ref[r, :], v_ref[r, :], st_ref[c], tabs)
        o_ref[r, :] = _chunk_norm_gate(o, g_ref[r, :], tabs).astype(BF16)
        return carry
    lax.fori_loop(0, nc, out_body, 0, unroll=RET_UNROLL)


def _lat_retention(p, dec, s0, batch, seq):
    nc = seq // CHUNK
    group = lambda g: pl.BlockSpec((seq, LANES), lambda b, h, g=g: (b, g * HEAD_PAIRS + h))
    vec = lambda w: pl.BlockSpec((1, w), lambda b, h: (0, h))
    return pl.pallas_call(
        _lat_retention_kernel,
        grid=(batch, HEAD_PAIRS),
        in_specs=[group(g) for g in range(4)] + [vec(LANES), vec(LANES), vec(2 * LANES),
                                                 vec(2 * LANES)]
        + [pl.BlockSpec((1, 1, 2, LANES, LANES), lambda b, h: (b, h, 0, 0, 0))],
        out_specs=pl.BlockSpec((seq, LANES), lambda b, h: (b, h)),
        out_shape=jax.ShapeDtypeStruct((batch * seq, WIDTH), BF16),
        scratch_shapes=[pltpu.VMEM((nc, 2 * LANES, LANES), F32),
                        pltpu.VMEM((nc, 2 * LANES, LANES), BF16)],
        compiler_params=_cparams(2),
        name="latent_retention",
    )(*([p] * 4), *dec, s0)


def _block_diag_states(s_f, s_b):
    def bd(s):
        b = s.shape[0]
        s = s.reshape(b, HEAD_PAIRS, 2, HEAD_DIM, HEAD_DIM)
        z = jnp.zeros_like(s[:, :, 0])
        top = jnp.concatenate([s[:, :, 0], z], axis=-1)
        bot = jnp.concatenate([z, s[:, :, 1]], axis=-1)
        return jnp.concatenate([top, bot], axis=-2)
    return jnp.stack([bd(s_f), bd(s_b)], axis=2)


def _na_kernel(q_ref, k_ref, v_ref, ck_ref, cv_ref, bias_ref, o_ref, *, lo, hi):
    rows_total = k_ref.shape[0] // GRID_W
    lane = lax.broadcasted_iota(jnp.int32, (1, LANES), 1)
    head_a = lane < HEAD_DIM
    wins, cls = [], []
    for i in range(NA_ROWS):
        r = pl.program_id(1) * NA_ROWS + i
        rs = jnp.clip(r - NA_KH // 2, 0, rows_total - NA_KH)
        wins.append(pl.ds(pl.multiple_of(rs * GRID_W, GRID_W), NA_KH * GRID_W))
        cls.append(jnp.where(r < lo, r, jnp.where(r > hi, r - hi + lo, lo)))
    stages = [(hp, r0) for hp in range(HEAD_PAIRS) for r0 in range(0, NA_ROWS, NA_STAGE_ROWS)]

    def scores(stage):
        hp, r0 = stage
        cols = slice(hp * LANES, (hp + 1) * LANES)
        qq = jnp.concatenate(
            [_split_heads(q_ref[i * GRID_W:(i + 1) * GRID_W, cols], head_a, 0)
             for i in range(r0, r0 + NA_STAGE_ROWS)], axis=0)
        s_ctx = _dot_nt(qq, ck_ref[0, :, cols])
        s_loc = [_dot_nt(qq[j * 2 * GRID_W:(j + 1) * 2 * GRID_W], k_ref[wins[r0 + j], cols])
                 for j in range(NA_STAGE_ROWS)]
        return s_ctx, s_loc

    def softmax(stage, s_ctx, s_loc):
        hp, r0 = stage
        m_ctx = jnp.max(s_ctx, axis=-1, keepdims=True)
        m = []
        for j in range(NA_STAGE_ROWS):
            blk = slice(j * 2 * GRID_W, (j + 1) * 2 * GRID_W)
            s_loc[j] = s_loc[j] + bias_ref[cls[r0 + j], hp]
            m.append(jnp.maximum(jnp.max(s_loc[j], axis=-1, keepdims=True), m_ctx[blk]))
        e_ctx = jnp.exp(s_ctx - jnp.concatenate(m, axis=0))
        den_ctx = jnp.sum(e_ctx, axis=-1, keepdims=True)
        e_loc, den = [], []
        for j in range(NA_STAGE_ROWS):
            blk = slice(j * 2 * GRID_W, (j + 1) * 2 * GRID_W)
            e = jnp.exp(s_loc[j] - m[j])
            den.append(jnp.sum(e, axis=-1, keepdims=True) + den_ctx[blk])
            e_loc.append(e.astype(BF16))
        return e_ctx.astype(BF16), e_loc, den

    def weighted_values(stage, e_ctx, e_loc, den):
        hp, r0 = stage
        cols = slice(hp * LANES, (hp + 1) * LANES)
        o_ctx = _dot(e_ctx, cv_ref[0, :, cols])
        for j in range(NA_STAGE_ROWS):
            i = r0 + j
            blk = slice(j * 2 * GRID_W, (j + 1) * 2 * GRID_W)
            o = (_dot(e_loc[j], v_ref[wins[i], cols]) + o_ctx[blk]) / den[j]
            o_ref[i * GRID_W:(i + 1) * GRID_W, cols] = jnp.where(
                head_a, o[:GRID_W], o[GRID_W:]).astype(BF16)

    nxt = scores(stages[0])
    pending = None
    for n, stage in enumerate(stages):
        s_ctx, s_loc = nxt
        if n + 1 < len(stages):
            nxt = scores(stages[n + 1])
        probs = softmax(stage, s_ctx, s_loc)
        if pending is not None:
            weighted_values(*pending)
        pending = (stage,) + probs
    weighted_values(*pending)


def _mod_bias_kernel(cond_ref, w_ref, b_ref, rpb_ref, win_ref, mod_ref, bias_ref, win_bf_ref, *,
                     dr_first, n_dr):
    s = _silu(cond_ref[...]).astype(BF16)
    mod_ref[...] = _dot(s, w_ref[...].astype(BF16)) + b_ref[...]
    _cast_blocks([win_ref], [win_bf_ref])

    qc = lax.broadcasted_iota(jnp.int32, (GRID_W, LANES), 0)
    lane = lax.broadcasted_iota(jnp.int32, (GRID_W, LANES), 1)
    kc = lane & (GRID_W - 1)
    cs = jnp.clip(qc - NA_KW // 2, 0, GRID_W - NA_KW)
    inside = (kc >= cs) & (kc < cs + NA_KW)
    first = lane < GRID_W
    lo_half, hi_half = [], []
    for dr in range(n_dr):
        line = jnp.broadcast_to(rpb_ref[pl.ds(dr, 1), :], (GRID_W, LANES))
        lo_half.append(pltpu.roll(line, LANES - (NA_KW - 1), 1, stride=1, stride_axis=0))
        hi_half.append(pltpu.roll(line, GRID_W - (NA_KW - 1), 1, stride=1, stride_axis=0))
    for cls, dr0 in enumerate(dr_first):
        for jp in range(NA_KH // 2):
            dr = dr0 + 2 * jp
            tile = jnp.where(inside, jnp.where(first, lo_half[dr], hi_half[dr + 1]), NEG_INF)
            bias_ref[cls, :, jp * LANES:(jp + 1) * LANES] = tile


def _modulation_and_na_bias(cond, ada_w, ada_b, na_rpb, rows_total, w_in):
    depth, heads, n_dr, n_dc = na_rpb.shape
    kh = NA_KH
    lo, hi = kh // 2, rows_total - kh // 2 - 1
    reps = list(range(lo)) + [lo] + list(range(hi + 1, rows_total))
    dr_first = tuple(int(np.clip(r - kh // 2, 0, rows_total - kh)) - r + kh - 1 for r in reps)
    n_dr_pad = -(-n_dr // HALO) * HALO
    lines = jnp.pad(na_rpb, ((0, 0), (0, 0), (0, n_dr_pad - n_dr), (0, LANES - n_dc)))
    mod_cols = 6 * D_MODEL // heads
    assert mod_cols % LANES == 0
    cast_arrays, cast_in, cast_out, cast_shapes = _cast_streams(
        (w_in,), depth * heads, lambda l, h: l * heads + h)
    mod, bias, w_in_b = pl.pallas_call(
        functools.partial(_mod_bias_kernel, dr_first=dr_first, n_dr=n_dr),
        grid=(depth, heads),
        in_specs=[
            pl.BlockSpec((COND_ROWS, D_MODEL), lambda l, h: (0, 0)),
            pl.BlockSpec((None, D_MODEL, mod_cols), lambda l, h: (l, 0, h)),
            pl.BlockSpec((None, 1, mod_cols), lambda l, h: (l, 0, h)),
            pl.BlockSpec((None, None, n_dr_pad, LANES), lambda l, h: (l, h, 0, 0)),
        ] + cast_in,
        out_specs=[
            pl.BlockSpec((None, COND_ROWS, mod_cols), lambda l, h: (l, 0, h)),
            pl.BlockSpec((None, len(reps), None, GRID_W, kh * GRID_W),
                         lambda l, h: (l, 0, h // 2, h % 2, 0)),
        ] + cast_out,
        out_shape=[
            jax.ShapeDtypeStruct((depth, COND_ROWS, 6 * D_MODEL), F32),
            jax.ShapeDtypeStruct((depth, len(reps), HEAD_PAIRS, 2 * GRID_W, kh * GRID_W), F32),
        ] + cast_shapes,
        compiler_params=_cparams(2),
        name="modulation_and_na_bias",
    )(cond, ada_w, ada_b.reshape(depth, 1, 6 * D_MODEL), lines, *cast_arrays)
    return mod, bias, lo, hi, w_in_b.reshape(w_in.shape)


def _neighbourhood_attention(p, ctx_k, ctx_v, bias, layer, lo, hi, batch, seq):
    rows_total = seq // GRID_W
    past = ctx_k.shape[1]

    groups = rows_total // NA_ROWS
    return pl.pallas_call(
        functools.partial(_na_kernel, lo=lo, hi=hi),
        grid=(batch, groups),
        in_specs=[
            pl.BlockSpec((NA_ROWS * GRID_W, WIDTH), lambda b, g: (b * groups + g, 4)),
            pl.BlockSpec((seq, WIDTH), lambda b, g: (b, 5), pipeline_mode=pl.Buffered(1)),
            pl.BlockSpec((seq, WIDTH), lambda b, g: (b, 6), pipeline_mode=pl.Buffered(1)),
            pl.BlockSpec((1, past, WIDTH), lambda b, g: (b, 0, 0), pipeline_mode=pl.Buffered(1)),
            pl.BlockSpec((1, past, WIDTH), lambda b, g: (b, 0, 0), pipeline_mode=pl.Buffered(1)),
            pl.BlockSpec((None,) + bias.shape[1:], lambda b, g: (layer, 0, 0, 0, 0),
                         pipeline_mode=pl.Buffered(1)),
        ],
        out_specs=pl.BlockSpec((NA_ROWS * GRID_W, WIDTH), lambda b, g: (b * groups + g, 0)),
        out_shape=jax.ShapeDtypeStruct((batch * seq, WIDTH), BF16),
        compiler_params=_cparams(2),
        name="neighbourhood_attention",
    )(p, p, p, ctx_k, ctx_v, bias)


def _post_kernel(xm, xp, xn, am, ap, an, bm, bp, bn, mod_head, mod_tail, g_ref, wo, wu, cw, cb, wd,
                 o_ref, acc_ref, h_ref, hp_ref, x1_ref, y_ref, act_ref, *, seq_len, n_tiles):
    step = pl.program_id(0)
    tm = xm.shape[0]
    nj = tm // HALO
    tiles_per_seq = max(seq_len // tm, 1)
    assert (tm % seq_len == 0 and seq_len % nj == 0) or seq_len % tm == 0
    h_rows = h_ref.shape[1]
    seg_pitch = nj + HALO
    assert h_rows == 2 * HALO + HALO * seg_pitch and nj % HEAD_ROWS == 0
    n_ct = D_MODEL // LANES
    g = g_ref[...]

    def ext(main, prev, nxt):
        lo = prev[...].astype(F32)[prev.shape[0] - HALO:]
        hi = nxt[...].astype(F32)[:HALO]
        return jnp.concatenate([lo, main[...].astype(F32), hi], axis=0)

    chunks = [(c0, min(FF_CHUNK, D_FF - c0)) for c0 in range(0, D_FF, FF_CHUNK)]
    par = step % 2
    mod_h = mod_head[0]
    n_blocks = tm // HEAD_ROWS

    def head_matmul():
        mixed = jnp.concatenate([ext(am, ap, an), ext(bm, bp, bn)], axis=1).astype(BF16)
        y_ref[...] = _dot(mixed, wo[...])

    def head_rows(x, y):
        x1 = x + mod_h[2:3] * _rms(y, g[0:1])
        return x1, _rms(x1, g[1:2]) * (1.0 + mod_h[4:5]) + mod_h[3:4]

    def head_block(blk):
        if blk < n_blocks:
            rows = slice(blk * HEAD_ROWS, (blk + 1) * HEAD_ROWS)
            erows = slice(HALO + blk * HEAD_ROWS, HALO + (blk + 1) * HEAD_ROWS)
            x1, h = head_rows(xm[rows, :], y_ref[erows, :])
            x1_ref[par, rows, :] = x1
            t0 = blk * HEAD_ROWS
            hrow = HALO + (t0 // nj) * seg_pitch + t0 % nj
            for c in range(n_ct):
                h_ref[c, hrow:hrow + HEAD_ROWS, :] = h[:, c * LANES:(c + 1) * LANES]
        else:
            x = jnp.concatenate([xp[...], xn[...]], axis=0)
            y = jnp.concatenate([y_ref[0:HALO, :], y_ref[HALO + tm:2 * HALO + tm, :]], axis=0)
            _, h = head_rows(x, y)
            t = jnp.minimum(step, n_tiles - 1) % tiles_per_seq
            ridx = lax.broadcasted_iota(jnp.int32, h.shape, 0)
            keep = ((ridx >= HALO) | (t != 0)) & ((ridx < HALO) | (t != tiles_per_seq - 1))
            h = jnp.where(keep, h, 0.0)
            for c in range(n_ct):
                h_ref[c, 0:HALO, :] = h[:HALO, c * LANES:(c + 1) * LANES]
                h_ref[c, h_rows - HALO:h_rows, :] = h[HALO:, c * LANES:(c + 1) * LANES]

    def head_permute():
        def group(rows):
            return jnp.concatenate([h_ref[c, rows, :] for c in range(n_ct)], axis=1)
        def put(row0, first, second):
            hp_ref[row0:row0 + 2 * HALO, :] = jnp.concatenate(
                [group(first), group(second)], axis=0).astype(BF16)
        for j in range(0, nj, 2):
            put(j * HALO, pl.ds(HALO + j, HALO, stride=seg_pitch),
                pl.ds(HALO + j + 1, HALO, stride=seg_pitch))
        put(tm, pl.ds(0, HALO), pl.ds(h_rows - HALO, HALO))

    def tail(interleaved):
        mod = mod_tail[0]

        def conv(u, cols):
            w = cw[:, cols]
            sub = lax.broadcasted_iota(jnp.int32, (HALO, u.shape[1]), 0)
            before = jnp.where(sub == 0, u[tm + HALO - 1:tm + HALO],
                               pltpu.roll(u[tm - HALO:tm], 1, 0))
            after = jnp.where(sub == HALO - 1, u[tm + HALO:tm + HALO + 1],
                              pltpu.roll(u[0:HALO], HALO - 1, 0))
            for s in range(1, HALO):
                if (s * nj) % seq_len == 0:
                    before = jnp.where(sub == s, 0.0, before)
                    after = jnp.where(sub == s - 1, 0.0, after)
            prev = jnp.concatenate([before, u[0:tm - HALO]], axis=0)
            nxt = jnp.concatenate([u[HALO:tm], after], axis=0)
            return prev * w[0:1] + u[0:tm] * w[1:2] + nxt * w[2:3] + cb[:, cols]

        def cols(ch):
            ca = slice(chunks[ch][0], chunks[ch][0] + chunks[ch][1])
            return ca, slice(D_FF + ca.start, D_FF + ca.stop)

        def up(ch):
            ca, cg = cols(ch)
            return _dot(hp_ref[...], wu[:, cg]), _dot(hp_ref[...], wu[:, ca])

        nxt_u = up(0)
        for ch in range(len(chunks)):
            ug, ua = nxt_u
            if ch + 1 < len(chunks):
                nxt_u = up(ch + 1)
            for blk in interleaved[ch]:
                head_block(blk)
            ca, cg = cols(ch)
            act_ref[:, ca] = (_silu(conv(ug, cg)) * conv(ua, ca)).astype(BF16)
        ffn = mod[5:6] * _rms(_dot(act_ref[...], wd[...]), g[2:3])
        for c in range(n_ct):
            acc_ref[c] = ffn[:, c * LANES:(c + 1) * LANES]

        for s in range(HALO):
            for jb in range(nj // HALO):
                t0 = s * nj + HALO * jb
                rows = pl.ds(HALO * HALO * jb + s, HALO, stride=HALO)
                back = jnp.concatenate([acc_ref[c, rows, :] for c in range(n_ct)], axis=1)
                o_ref[t0:t0 + HALO, :] = x1_ref[1 - par, t0:t0 + HALO, :] + back

    @pl.when(step == 0)
    def _():
        head_matmul()
        for blk in range(n_blocks + 1):
            head_block(blk)
        head_permute()

    @pl.when(step > 0)
    def _():
        head_matmul()
        tail([[blk for blk in range(n_blocks + 1) if blk * len(chunks) // (n_blocks + 1) == ch]
              for ch in range(len(chunks))])
        head_permute()


def _post(x, mix_a, mix_b, mods, gains, wo, wu, cw, cb, wd, layer, seq_len, tm, latent):
    n = x.shape[0]
    nt = n // tm
    (mix_a, col_a), (mix_b, col_b) = mix_a, mix_b
    tiles_per_seq = max(seq_len // tm, 1)
    cond_row = (lambda i: 1 + i // tiles_per_seq) if latent else (lambda i: 0)

    head_tile = lambda i: jnp.minimum(i, nt - 1)
    tail_tile = lambda i: jnp.maximum(i - 1, 0)

    def triple(width, halo_rows, col=0):
        per = tm // halo_rows
        last = n // halo_rows - 1
        return [
            pl.BlockSpec((tm, width), lambda i: (head_tile(i), col)),
            pl.BlockSpec((halo_rows, width),
                         lambda i: (jnp.maximum(head_tile(i) * per - 1, 0), col)),
            pl.BlockSpec((halo_rows, width),
                         lambda i: (jnp.minimum((head_tile(i) + 1) * per, last), col)),
        ]

    const = lambda *shape: pl.BlockSpec((None,) + shape, lambda i: (layer,) + (0,) * len(shape),
                                        pipeline_mode=pl.Buffered(1))
    in_specs = (triple(D_MODEL, HALO) + triple(WIDTH, 2 * HALO, col_a)
                + triple(WIDTH, 2 * HALO, col_b) + [
        pl.BlockSpec((None, 1, 6, D_MODEL), lambda i: (layer, cond_row(head_tile(i)), 0, 0)),
        pl.BlockSpec((None, 1, 6, D_MODEL), lambda i: (layer, cond_row(tail_tile(i)), 0, 0)),
        const(3, D_MODEL),
        const(2 * WIDTH, D_MODEL),
        const(D_MODEL, 2 * D_FF),
        const(3, 2 * D_FF),
        const(1, 2 * D_FF),
        const(D_FF, D_MODEL),
    ])
    return pl.pallas_call(
        functools.partial(_post_kernel, seq_len=seq_len, n_tiles=nt),
        grid=(nt + 1,),
        in_specs=in_specs,
        out_specs=pl.BlockSpec((tm, D_MODEL), lambda i: (tail_tile(i), 0)),
        out_shape=jax.ShapeDtypeStruct((n, D_MODEL), F32),
        scratch_shapes=[pltpu.VMEM((D_MODEL // LANES, tm, LANES), F32),
                        pltpu.VMEM((D_MODEL // LANES, 2 * HALO + HALO * (tm // HALO + HALO), LANES),
                                   F32),
                        pltpu.VMEM((tm + 2 * HALO, D_MODEL), BF16),
                        pltpu.VMEM((2, tm, D_MODEL), F32),
                        pltpu.VMEM((tm + 2 * HALO, D_MODEL), F32),
                        pltpu.VMEM((tm, D_FF), BF16)],
        compiler_params=_cparams(1),
        name="post_latent" if latent else "post_context",
    )(x, x, x, mix_a, mix_a, mix_a, mix_b, mix_b, mix_b, mods, mods, gains, wo, wu, cw, cb, wd)


def kernel(x_prompt, x_sample, c, cache_na_k, cache_na_v, state_ret_fwd, state_ret_bwd, c_ctx,
           ada_w, ada_b, g_pre_mix, g_post_mix, g_pre_ffn, g_post_ffn, w_in,
           ret_decay_fwd, ret_decay_bwd, na_rpb, w_out, w_up, conv_w, conv_b, w_down):
    depth = w_in.shape[0]
    batch, seq, _ = x_prompt.shape
    dec_batch, dec_seq, _ = x_sample.shape
    past = cache_na_k.shape[2]
    tm = TOKEN_TILE
    assert 1 + dec_batch <= COND_ROWS

    cond = jnp.concatenate(
        [c_ctx[None, :], c, jnp.zeros((COND_ROWS - 1 - dec_batch, D_MODEL), F32)], axis=0)
    mods, bias, lo, hi, w_in_b = _modulation_and_na_bias(
        cond, ada_w, ada_b, na_rpb, dec_seq // GRID_W, w_in)
    mods = mods.reshape(depth, COND_ROWS, 6, D_MODEL)
    rope_tabs = _rope_tables(dec_seq)

    g_pre = g_pre_mix.reshape(depth, 1, D_MODEL)
    gains = jnp.stack([g_post_mix, g_pre_ffn, g_post_ffn], axis=1)
    cb = conv_b.reshape(depth, 1, 2 * D_FF)

    y_p = x_prompt.reshape(batch * seq, D_MODEL)
    y_s = x_sample.reshape(dec_batch * dec_seq, D_MODEL)
    new_k = new_v = new_sf = new_sb = None
    for l in range(depth):
        dec = (jnp.repeat(ret_decay_fwd[l], HEAD_DIM)[None, :],
               jnp.repeat(ret_decay_bwd[l], HEAD_DIM)[None, :],
               jnp.repeat(ret_decay_fwd[l], LANES)[None, :],
               jnp.repeat(ret_decay_bwd[l], LANES)[None, :])

        p_c, new_k, new_v = _inproj_context(y_p, mods, g_pre, w_in_b, l, seq, tm, (new_k, new_v))
        p_s, mix_c, new_sf, new_sb, *cast = _inproj_latent_and_ctx_mixer(
            y_s, mods, g_pre, w_in_b, l, dec_seq, tm, rope_tabs, p_c, dec, batch, seq,
            (new_sf, new_sb), cast=(w_out, w_up, w_down) if l == 0 else ())
        if l == 0:
            post_params = (mods, gains, cast[0], cast[1], conv_w, cb, cast[2])
        y_p = _post(y_p, (mix_c, 0), (mix_c, 1), *post_params, l, seq, tm, latent=False)

        s0 = _block_diag_states(state_ret_fwd[:, l], state_ret_bwd[:, l])
        ret_s = _lat_retention(p_s, dec, s0, dec_batch, dec_seq)
        ck = cache_na_k[:, l].reshape(dec_batch, past, WIDTH).astype(BF16)
        cv = cache_na_v[:, l].reshape(dec_batch, past, WIDTH).astype(BF16)
        na_s = _neighbourhood_attention(p_s, ck, cv, bias, l, lo, hi, dec_batch, dec_seq)
        y_s = _post(y_s, (ret_s, 0), (na_s, 0), *post_params, l, dec_seq, tm, latent=True)

    return (y_p.reshape(batch, seq, D_MODEL),
            y_s.reshape(dec_batch, dec_seq, D_MODEL),
            new_k.reshape(batch, depth, seq, N_HEADS, HEAD_DIM),
            new_v.reshape(batch, depth, seq, N_HEADS, HEAD_DIM),
            new_sf, new_sb)
```
